```python
import jax
import jax.numpy as jnp
from jax import lax
import numpy as np

D_MODEL = 1024
BATCH = 8
SEQ = 2048
DEPTH = 2

F32 = jnp.float32
NORM_EPS = 1e-6
MIX_W = D_MODEL // 2

CONV_CH = MIX_W
CONV_WIDTH = 31
CONV_LN_EPS = 1e-5

GLA_HEADS = 4
GLA_DV = MIX_W // GLA_HEADS
GLA_DK = GLA_DV // 2
GLA_GATE_RANK = 16
GLA_TAU = 16.0
GLA_CHUNK = 64

RWKV_HEADS = 8
RWKV_HD = MIX_W // RWKV_HEADS
RWKV_W_RANK = 64
RWKV_A_RANK = 64
RWKV_G_RANK = 128
RWKV_DECAY_SCALE = 0.606531
RWKV_GN_EPS = 64e-5
RWKV_L2_EPS = 1e-12

FOX_HEADS = 8
FOX_HD = MIX_W // FOX_HEADS
FOX_BLOCK = 128

FFN_DENSE = 2816
N_EXPERTS = 8
TOP_K = 2
FFN_EXPERT = 3584

L0_SPLITS = (CONV_CH, CONV_CH, GLA_HEADS * GLA_DK, GLA_HEADS * GLA_DK, MIX_W, MIX_W, GLA_GATE_RANK)
RWKV_SPLITS = (MIX_W, MIX_W, MIX_W, RWKV_W_RANK, RWKV_A_RANK, RWKV_G_RANK)
FOX_SPLITS = (MIX_W, MIX_W, MIX_W, FOX_HEADS)
L0_COLS = sum(L0_SPLITS)
RWKV_COLS = sum(RWKV_SPLITS)
L1_COLS = RWKV_COLS + sum(FOX_SPLITS)

kernel_name = "hybrid_conv_gla_rwkv7_fox_moe_trunk"


def _split(p, sizes):
    idx = np.cumsum(sizes)[:-1].tolist()
    return jnp.split(p, idx, axis=-1)


def rmsnorm(x, g):
    xf = x.astype(F32)
    y = xf * lax.rsqrt(jnp.mean(xf * xf, axis=-1, keepdims=True) + NORM_EPS) * g.astype(F32)
    return y.astype(x.dtype)


def swiglu(h, w1, w3, w2):
    return (jax.nn.silu(h @ w1) * (h @ w3)) @ w2


def conformer_conv(val, gate, conv_w, conv_b, ln_g, ln_b):
    u = val * jax.nn.sigmoid(gate)
    u = lax.conv_general_dilated(
        u, conv_w[:, None, :], window_strides=(1,), padding=[(CONV_WIDTH - 1, 0)],
        dimension_numbers=("NWC", "WIO", "NWC"), feature_group_count=CONV_CH) + conv_b
    uf = u.astype(F32)
    mu = jnp.mean(uf, axis=-1, keepdims=True)
    var = jnp.mean(jnp.square(uf - mu), axis=-1, keepdims=True)
    uf = (uf - mu) * lax.rsqrt(var + CONV_LN_EPS) * ln_g + ln_b
    return jax.nn.silu(uf).astype(val.dtype)


def gla(q, k, v, g_out, a_lr, w_a2, b_a, norm_g):
    B, T, _ = q.shape
    H, C = GLA_HEADS, GLA_CHUNK
    N = T // C
    log_a = jax.nn.log_sigmoid((a_lr @ w_a2 + b_a).astype(F32)) / GLA_TAU

    def chunks(t, d):
        return t.astype(F32).reshape(B, N, C, H, d).transpose(0, 3, 1, 2, 4)

    qc = chunks(q, GLA_DK) * (GLA_DK ** -0.5)
    kc = chunks(k, GLA_DK)
    vc = chunks(v, GLA_DV)
    b = jnp.cumsum(chunks(log_a, GLA_DK), axis=3)
    b_last = b[:, :, :, -1:, :]
    q_dec = qc * jnp.exp(b)
    k_dec = kc * jnp.exp(-b)
    k_end = kc * jnp.exp(b_last - b)
    causal = jnp.tril(jnp.ones((C, C), dtype=bool))
    attn = jnp.where(causal, jnp.einsum("bhnik,bhnjk->bhnij", q_dec, k_dec), 0.0)
    o_intra = jnp.einsum("bhnij,bhnjv->bhniv", attn, vc)

    def step(S, inp):
        q_d, k_e, v_n, decay = inp
        o = jnp.einsum("bhck,bhkv->bhcv", q_d, S)
        S = S * decay[..., None] + jnp.einsum("bhck,bhcv->bhkv", k_e, v_n)
        return S, o

    xs = (jnp.moveaxis(q_dec, 2, 0), jnp.moveaxis(k_end, 2, 0), jnp.moveaxis(vc, 2, 0),
          jnp.moveaxis(jnp.exp(b_last[:, :, :, 0, :]), 2, 0))
    S0 = jnp.zeros((B, H, GLA_DK, GLA_DV), F32)
    _, o_inter = lax.scan(step, S0, xs)
    o = o_intra + jnp.moveaxis(o_inter, 0, 2)
    o = o * lax.rsqrt(jnp.mean(o * o, axis=-1, keepdims=True) + NORM_EPS)
    o = o.transpose(0, 2, 3, 1, 4).reshape(B, T, H * GLA_DV) * norm_g
    return (o * jax.nn.silu(g_out.astype(F32))).astype(q.dtype)


def rwkv7(p, mu, w_w2, w0, w_a2, a0, w_g2, k_k, k_a, r_k, lnx_g, lnx_b):
    B, T, _ = p.shape
    H, N = RWKV_HEADS, RWKV_HD
    prev = jnp.pad(p, ((0, 0), (1, 0), (0, 0)))[:, :-1]
    p = p + (prev - p) * mu
    r, k, v, xw, xa, xg = _split(p, RWKV_SPLITS)
    log_w = -RWKV_DECAY_SCALE * jax.nn.sigmoid((w0 + jnp.tanh(xw) @ w_w2).astype(F32))
    a = jax.nn.sigmoid((a0 + xa @ w_a2).astype(F32))
    g = (jax.nn.sigmoid(xg) @ w_g2).astype(F32)

    def heads(t):
        return t.astype(F32).reshape(B, T, H, N)

    kk = heads(k * k_k)
    kk = kk / jnp.maximum(jnp.sqrt(jnp.sum(kk * kk, axis=-1, keepdims=True)), RWKV_L2_EPS)
    k_mod = k.astype(F32) * (1.0 + (a - 1.0) * k_a)
    rh, kh, vh, ah, wh = heads(r), heads(k_mod), heads(v), heads(a), jnp.exp(heads(log_w))

    def step(S, inp):
        r_t, w_t, k_t, v_t, kk_t, a_t = inp
        sa = jnp.einsum("bhvk,bhk->bhv", S, kk_t)
        S = (S * w_t[:, :, None, :] - sa[..., None] * (kk_t * a_t)[:, :, None, :]
             + v_t[..., None] * k_t[:, :, None, :])
        return S, jnp.einsum("bhvk,bhk->bhv", S, r_t)

    xs = (jnp.moveaxis(rh, 1, 0), jnp.moveaxis(wh, 1, 0), jnp.moveaxis(kh, 1, 0),
          jnp.moveaxis(vh, 1, 0), jnp.moveaxis(kk, 1, 0), jnp.moveaxis(ah, 1, 0))
    S0 = jnp.zeros((B, H, N, N), F32)
    _, y = lax.scan(step, S0, xs)
    y = jnp.moveaxis(y, 0, 1)
    mean = jnp.mean(y, axis=-1, keepdims=True)
    var = jnp.mean(jnp.square(y - mean), axis=-1, keepdims=True)
    y = ((y - mean) * lax.rsqrt(var + RWKV_GN_EPS)).reshape(B, T, H * N) * lnx_g + lnx_b
    bonus = jnp.sum(rh * kh * r_k, axis=-1, keepdims=True) * vh
    y = y + bonus.reshape(B, T, H * N)
    return (y * g).astype(p.dtype)


def fox(q, k, v, f_logit, b_f):
    B, T, _ = q.shape
    H, d = FOX_HEADS, FOX_HD
    log_f = jax.nn.log_sigmoid((f_logit + b_f).astype(F32))
    c = jnp.cumsum(log_f, axis=1).transpose(0, 2, 1)

    def heads(t):
        return t.reshape(B, T, H, d).transpose(0, 2, 1, 3)

    qh, kh, vh = heads(q * (d ** -0.5)), heads(k), heads(v)
    outs = []
    for i in range(T // FOX_BLOCK):
        lo, hi = i * FOX_BLOCK, (i + 1) * FOX_BLOCK
        s = jnp.einsum("bhqd,bhkd->bhqk", qh[:, :, lo:hi], kh[:, :, :hi]).astype(F32)
        s = s + c[:, :, lo:hi, None] - c[:, :, None, :hi]
        mask = (lo + jnp.arange(FOX_BLOCK))[:, None] >= jnp.arange(hi)[None, :]
        pr = jax.nn.softmax(jnp.where(mask, s, -jnp.inf), axis=-1)
        outs.append(jnp.einsum("bhqk,bhkd->bhqd", pr.astype(vh.dtype), vh[:, :, :hi]))
    o = jnp.concatenate(outs, axis=2)
    return o.transpose(0, 2, 1, 3).reshape(B, T, H * d)


def moe(h, router, w1, w3, w2):
    logits = (h @ router).astype(F32)
    top_v, top_i = lax.top_k(logits, TOP_K)
    gates = jax.nn.softmax(top_v, axis=-1)
    dense_gate = jnp.sum(jax.nn.one_hot(top_i, N_EXPERTS, dtype=F32) * gates[..., None], axis=-2)
    out = jnp.zeros(h.shape, F32)
    for e in range(N_EXPERTS):
        out = out + dense_gate[..., e:e + 1] * swiglu(h, w1[e], w3[e], w2[e]).astype(F32)
    return out.astype(h.dtype)


def setup_inputs(seed: int = 0) -> dict:
    key = jax.random.key(seed)
    ks = iter(jax.random.split(key, 48))
    D = D_MODEL

    def nrm(shape, scale):
        return jax.random.normal(next(ks), shape, F32) * scale

    def gain(shape):
        return 1.0 + nrm(shape, 0.02)

    return {
        "x": nrm((BATCH, SEQ, D), 1.0),
        "norm_mix_g": gain((DEPTH, D)),
        "norm_ffn_g": gain((DEPTH, D)),
        "w_in0": nrm((D, L0_COLS), D ** -0.5),
        "conv_w": nrm((CONV_WIDTH, CONV_CH), CONV_WIDTH ** -0.5),
        "conv_b": nrm((CONV_CH,), 0.02),
        "conv_ln_g": gain((CONV_CH,)),
        "conv_ln_b": nrm((CONV_CH,), 0.02),
        "gla_w_a2": nrm((GLA_GATE_RANK, GLA_HEADS * GLA_DK), GLA_GATE_RANK ** -0.5),
        "gla_b_a": nrm((GLA_HEADS * GLA_DK,), 0.1),
        "gla_norm_g": gain((MIX_W,)),
        "w_out0": nrm((2 * MIX_W, D), (2 * MIX_W) ** -0.5),
        "ffn_w1": nrm((D, FFN_DENSE), D ** -0.5),
        "ffn_w3": nrm((D, FFN_DENSE), D ** -0.5),
        "ffn_w2": nrm((FFN_DENSE, D), FFN_DENSE ** -0.5),
        "w_in1": nrm((D, L1_COLS), D ** -0.5),
        "rwkv_mu": jax.random.uniform(next(ks), (RWKV_COLS,), F32),
        "rwkv_w2": nrm((RWKV_W_RANK, MIX_W), 0.5 * RWKV_W_RANK ** -0.5),
        "rwkv_w0": nrm((MIX_W,), 0.5),
        "rwkv_a2": nrm((RWKV_A_RANK, MIX_W), RWKV_A_RANK ** -0.5),
        "rwkv_a0": nrm((MIX_W,), 0.1),
        "rwkv_g2": nrm((RWKV_G_RANK, MIX_W), RWKV_G_RANK ** -0.5),
        "rwkv_k_k": 0.85 + nrm((MIX_W,), 0.02),
        "rwkv_k_a": gain((MIX_W,)),
        "rwkv_r_k": nrm((RWKV_HEADS, RWKV_HD), 0.1),
        "rwkv_lnx_g": gain((MIX_W,)),
        "rwkv_lnx_b": nrm((MIX_W,), 0.02),
        "fox_b_f": 2.0 + nrm((FOX_HEADS,), 0.5),
        "w_out1": nrm((2 * MIX_W, D), (2 * MIX_W) ** -0.5),
        "moe_router": nrm((D, N_EXPERTS), D ** -0.5),
        "moe_w1": nrm((N_EXPERTS, D, FFN_EXPERT), D ** -0.5),
        "moe_w3": nrm((N_EXPERTS, D, FFN_EXPERT), D ** -0.5),
        "moe_w2": nrm((N_EXPERTS, FFN_EXPERT, D), FFN_EXPERT ** -0.5),
        "final_norm_g": gain((D,)),
    }


def reference(x, norm_mix_g, norm_ffn_g, w_in0, conv_w, conv_b, conv_ln_g, conv_ln_b,
              gla_w_a2, gla_b_a, gla_norm_g, w_out0, ffn_w1, ffn_w3, ffn_w2,
              w_in1, rwkv_mu, rwkv_w2, rwkv_w0, rwkv_a2, rwkv_a0, rwkv_g2, rwkv_k_k, rwkv_k_a,
              rwkv_r_k, rwkv_lnx_g, rwkv_lnx_b, fox_b_f, w_out1,
              moe_router, moe_w1, moe_w3, moe_w2, final_norm_g):
    h = x
    for layer in range(DEPTH):
        u = rmsnorm(h, norm_mix_g[layer])
        if layer % 2 == 0:
            p = u @ w_in0
            c_val, c_gate, q, k, v, g_out, a_lr = _split(p, L0_SPLITS)
            y_conv = conformer_conv(c_val, c_gate, conv_w, conv_b, conv_ln_g, conv_ln_b)
            y_gla = gla(q, k, v, g_out, a_lr, gla_w_a2, gla_b_a, gla_norm_g)
            h = h + jnp.concatenate([y_conv, y_gla], axis=-1) @ w_out0
            h = h + swiglu(rmsnorm(h, norm_ffn_g[layer]), ffn_w1, ffn_w3, ffn_w2)
        else:
            p = u @ w_in1
            p_rwkv, p_fox = p[..., :RWKV_COLS], p[..., RWKV_COLS:]
            y_rwkv = rwkv7(p_rwkv, rwkv_mu, rwkv_w2, rwkv_w0, rwkv_a2, rwkv_a0, rwkv_g2,
                           rwkv_k_k, rwkv_k_a, rwkv_r_k, rwkv_lnx_g, rwkv_lnx_b)
            fq, fk, fv, f_logit = _split(p_fox, FOX_SPLITS)
            y_fox = fox(fq, fk, fv, f_logit, fox_b_f)
            h = h + jnp.concatenate([y_rwkv, y_fox], axis=-1) @ w_out1
            h = h + moe(rmsnorm(h, norm_ffn_g[layer]), moe_router, moe_w1, moe_w3, moe_w2)
    return rmsnorm(h, final_norm_g)
```

```python
import functools

import jax
import jax.numpy as jnp
from jax import lax
from jax.experimental import pallas as pl
from jax.experimental.pallas import tpu as pltpu

F32 = jnp.float32
BF16 = jnp.bfloat16
HI = lax.Precision.HIGHEST

NORM_EPS = 1e-6
CONV_WIDTH = 31
CONV_LN_EPS = 1e-5
GLA_HEADS = 4
GLA_DK = 64
GLA_DV = 128
GLA_TAU = 16.0
GLA_CHUNK = 64
RWKV_HEADS = 8
RWKV_HD = 64
RWKV_CHUNK = 64
RWKV_DECAY_SCALE = 0.606531
RWKV_GN_EPS = 64e-5
RWKV_L2_EPS = 1e-12
FOX_HEADS = 8
FOX_HD = 64
N_EXPERTS = 8
LANE = 128
VMEM_LIMIT = 56 * 1024 * 1024


def _cparams(*sem):
    return pltpu.CompilerParams(dimension_semantics=sem, vmem_limit_bytes=VMEM_LIMIT)


def _dot(a, b):
    return jnp.dot(a.astype(BF16), b.astype(BF16), preferred_element_type=F32)


def _dot_nt(a, b):
    return lax.dot_general(a.astype(BF16), b.astype(BF16), (((1,), (1,)), ((), ())),
                           preferred_element_type=F32)


def _dot_tn(a, b):
    return lax.dot_general(a.astype(BF16), b.astype(BF16), (((0,), (0,)), ((), ())),
                           preferred_element_type=F32)


def _dot_hi(a, b):
    return jnp.dot(a, b, precision=HI, preferred_element_type=F32)


def _dot_tn_hi(a, b):
    return lax.dot_general(a, b, (((0,), (0,)), ((), ())), precision=HI, preferred_element_type=F32)


def _sigmoid(x):
    return 1.0 / (1.0 + jnp.exp(-x))


def _silu(x):
    return x * _sigmoid(x)


def _log_sigmoid(x):
    return jnp.minimum(x, 0.0) - jnp.log(1.0 + jnp.exp(-jnp.abs(x)))


def _rmsnorm(x, g):
    return x * lax.rsqrt(jnp.mean(x * x, axis=-1, keepdims=True) + NORM_EPS) * g


def _tri(n, strict=False):
    r = lax.broadcasted_iota(jnp.int32, (n, n), 0)
    c = lax.broadcasted_iota(jnp.int32, (n, n), 1)
    return (r > c) if strict else (r >= c)


def _norm_proj_kernel(h_ref, g_ref, *refs):
    n = len(refs) // 2
    xn = _rmsnorm(h_ref[...], g_ref[...]).astype(BF16)
    for w_ref, o_ref in zip(refs[:n], refs[n:]):
        o_ref[...] = jnp.dot(xn, w_ref[...], preferred_element_type=F32)


def _norm_proj(h, g, ws, tm=512):
    m, d = h.shape
    return pl.pallas_call(
        _norm_proj_kernel,
        grid=(m // tm,),
        in_specs=[pl.BlockSpec((tm, d), lambda i: (i, 0)), pl.BlockSpec((1, d), lambda i: (0, 0))]
        + [pl.BlockSpec(w.shape, lambda i: (0, 0)) for w in ws],
        out_specs=[pl.BlockSpec((tm, w.shape[1]), lambda i: (i, 0)) for w in ws],
        out_shape=[jax.ShapeDtypeStruct((m, w.shape[1]), F32) for w in ws],
        compiler_params=_cparams("parallel"),
        name="norm_proj",
    )(h, g.reshape(1, d), *ws)


def _out_proj_kernel(h_ref, ya_ref, yb_ref, wa_ref, wb_ref, o_ref):
    o_ref[...] = h_ref[...] + _dot(ya_ref[...], wa_ref[...]) + _dot(yb_ref[...], wb_ref[...])


def _out_proj(h, ya, yb, wa, wb, tm=512):
    m, d = h.shape
    k = ya.shape[1]
    return pl.pallas_call(
        _out_proj_kernel,
        grid=(m // tm,),
        in_specs=[pl.BlockSpec((tm, d), lambda i: (i, 0)),
                  pl.BlockSpec((tm, k), lambda i: (i, 0)),
                  pl.BlockSpec((tm, k), lambda i: (i, 0)),
                  pl.BlockSpec((k, d), lambda i: (0, 0)),
                  pl.BlockSpec((k, d), lambda i: (0, 0))],
        out_specs=pl.BlockSpec((tm, d), lambda i: (i, 0)),
        out_shape=jax.ShapeDtypeStruct((m, d), F32),
        compiler_params=_cparams("parallel"),
        name="out_proj",
    )(h, ya, yb, wa, wb)


_CONV_HALO = 32


def _conv_kernel(val_ref, gate_ref, w_ref, b_ref, lg_ref, lb_ref, o_ref, u_ref, *, tt, rows):
    @pl.when(pl.program_id(1) == 0)
    def _():
        u_ref[0:_CONV_HALO, :] = jnp.zeros((_CONV_HALO, u_ref.shape[1]), F32)

    u_ref[_CONV_HALO:_CONV_HALO + tt, :] = val_ref[0] * _sigmoid(gate_ref[0])
    base = _CONV_HALO - (CONV_WIDTH - 1)
    for r0 in range(0, tt, rows):
        acc = jnp.zeros((rows, u_ref.shape[1]), F32) + b_ref[...]
        for j in range(CONV_WIDTH):
            acc = acc + u_ref[r0 + base + j:r0 + base + j + rows, :] * w_ref[j:j + 1, :]
        mu = jnp.mean(acc, axis=-1, keepdims=True)
        cen = acc - mu
        var = jnp.mean(cen * cen, axis=-1, keepdims=True)
        y = cen * lax.rsqrt(var + CONV_LN_EPS) * lg_ref[...] + lb_ref[...]
        o_ref[0, r0:r0 + rows, :] = _silu(y)
    u_ref[0:_CONV_HALO, :] = u_ref[tt:tt + _CONV_HALO, :]


def _conv(val, gate, conv_w, conv_b, ln_g, ln_b, tt=256, rows=32):
    b, t, c = val.shape
    wpad = jnp.zeros((32, c), F32).at[:CONV_WIDTH].set(conv_w)
    vec = lambda a: a.reshape(1, c)
    tile = pl.BlockSpec((1, tt, c), lambda i, j: (i, j, 0))
    full = lambda shape: pl.BlockSpec(shape, lambda i, j: (0, 0))
    return pl.pallas_call(
        functools.partial(_conv_kernel, tt=tt, rows=rows),
        grid=(b, t // tt),
        in_specs=[tile, tile, full((32, c)), full((1, c)), full((1, c)), full((1, c))],
        out_specs=tile,
        out_shape=jax.ShapeDtypeStruct((b, t, c), F32),
        scratch_shapes=[pltpu.VMEM((tt + _CONV_HALO, c), F32)],
        compiler_params=_cparams("parallel", "arbitrary"),
        name="conformer_conv",
    )(val, gate, wpad, vec(conv_b), vec(ln_g), vec(ln_b))


def _gla_kernel(q_ref, k_ref, v_ref, go_ref, alr_ref, wa2_ref, ba_ref, ng_ref, o_ref, s_ref, *, tt):
    c = GLA_CHUNK

    @pl.when(pl.program_id(1) == 0)
    def _():
        s_ref[...] = jnp.zeros(s_ref.shape, F32)

    tril = _tri(c)
    tril_f = tril.astype(F32)
    ones_cv = jnp.ones((c, GLA_DV), F32)
    for ci in range(tt // c):
        sl = slice(ci * c, (ci + 1) * c)
        z = _dot(alr_ref[0, sl, :], wa2_ref[...]) + ba_ref[...]
        log_a = _log_sigmoid(z) * (1.0 / GLA_TAU)
        bcum = _dot_hi(tril_f, log_a)
        b_last = bcum[c - 1:c, :]
        q = q_ref[0, sl, :] * (GLA_DK ** -0.5)
        k = k_ref[0, sl, :]
        q_dec = q * jnp.exp(bcum)
        k_dec = k * jnp.exp(-bcum)
        k_end = k * jnp.exp(b_last - bcum)
        outs = []
        for h in range(GLA_HEADS):
            ks = slice(h * GLA_DK, (h + 1) * GLA_DK)
            vs = slice(h * GLA_DV, (h + 1) * GLA_DV)
            v = v_ref[0, sl, vs]
            attn = jnp.where(tril, _dot_nt(q_dec[:, ks], k_dec[:, ks]), 0.0)
            state = s_ref[h]
            o = _dot(attn, v) + _dot(q_dec[:, ks], state)
            decay = jnp.exp(_dot_tn_hi(log_a[:, ks], ones_cv))
            s_ref[h] = state * decay + _dot_tn(k_end[:, ks], v)
            o = o * lax.rsqrt(jnp.mean(o * o, axis=-1, keepdims=True) + NORM_EPS)
            outs.append(o * ng_ref[:, vs] * _silu(go_ref[0, sl, vs]))
        o_ref[0, sl, :] = jnp.concatenate(outs, axis=-1)


def _gla(q, k, v, g_out, a_lr, w_a2, b_a, norm_g, tt=256):
    b, t, dq = q.shape
    dv = v.shape[-1]
    tile = lambda n: pl.BlockSpec((1, tt, n), lambda i, j: (i, j, 0))
    full = lambda shape: pl.BlockSpec(shape, lambda i, j: (0, 0))
    return pl.pallas_call(
        functools.partial(_gla_kernel, tt=tt),
        grid=(b, t // tt),
        in_specs=[tile(dq), tile(dq), tile(dv), tile(dv), tile(a_lr.shape[-1]),
                  full(w_a2.shape), full((1, dq)), full((1, dv))],
        out_specs=tile(dv),
        out_shape=jax.ShapeDtypeStruct((b, t, dv), F32),
        scratch_shapes=[pltpu.VMEM((GLA_HEADS, GLA_DK, GLA_DV), F32)],
        compiler_params=_cparams("parallel", "arbitrary"),
        name="gla",
    )(q, k, v, g_out, a_lr, w_a2, b_a.reshape(1, dq), norm_g.reshape(1, dv))


_RWKV_COLS = 3 * 512 + 64 + 64 + 128


def _rwkv_kernel(p_ref, mu_ref, w2_ref, w0_ref, a2_ref, a0_ref, g2_ref, kk_ref, ka_ref, rk_ref,
                 lg_ref, lb_ref, o_ref, xs_ref, st_ref):
    c = RWKV_CHUNK
    hd = RWKV_HD
    w = RWKV_HEADS * hd

    @pl.when(pl.program_id(1) == 0)
    def _():
        xs_ref[0:8, :] = jnp.zeros((8, xs_ref.shape[1]), F32)
        st_ref[...] = jnp.zeros(st_ref.shape, F32)

    x = p_ref[0]
    xs_ref[8:8 + c, :] = x
    prev = xs_ref[7:7 + c, :]
    xs_ref[7:8, :] = x[c - 1:c, :]
    x = x + (prev - x) * mu_ref[...]

    r = x[:, 0:w]
    k = x[:, w:2 * w]
    v = x[:, 2 * w:3 * w]
    xw = x[:, 3 * w:3 * w + 64]
    xa = x[:, 3 * w + 64:3 * w + 128]
    xg = x[:, 3 * w + 128:3 * w + 256]
    log_w = -RWKV_DECAY_SCALE * _sigmoid(w0_ref[...] + _dot(jnp.tanh(xw), w2_ref[...]))
    a = _sigmoid(a0_ref[...] + _dot(xa, a2_ref[...]))
    g = _dot(_sigmoid(xg), g2_ref[...])
    kk = k * kk_ref[...]
    k_mod = k * (1.0 + (a - 1.0) * ka_ref[...])

    tril = _tri(c)
    tril_s = _tri(c, strict=True)
    lcum = _dot_hi(tril.astype(F32), log_w)
    l_last = lcum[c - 1:c, :]
    e_pos = jnp.exp(lcum)
    e_neg = jnp.exp(-lcum)
    e_end = jnp.exp(l_last - lcum)
    e_prev = jnp.exp(lcum - log_w)
    ones_ck = jnp.ones((c, hd), F32)
    eye = (lax.broadcasted_iota(jnp.int32, (c, c), 0)
           == lax.broadcasted_iota(jnp.int32, (c, c), 1)).astype(F32)

    for h in range(RWKV_HEADS):
        hs = slice(h * hd, (h + 1) * hd)
        kk_h = kk[:, hs]
        nrm = jnp.sqrt(jnp.sum(kk_h * kk_h, axis=-1, keepdims=True))
        kk_h = kk_h / jnp.maximum(nrm, RWKV_L2_EPS)
        r_h, k_h, v_h, a_h = r[:, hs], k_mod[:, hs], v[:, hs], a[:, hs]
        aq = -kk_h * e_prev[:, hs]
        bk = kk_h * a_h * e_neg[:, hs]
        kd = k_h * e_neg[:, hs]
        rq = r_h * e_pos[:, hs]
        be = kk_h * a_h * e_end[:, hs]
        ke = k_h * e_end[:, hs]
        m1 = jnp.where(tril_s, _rdot_nt(aq, bk), 0.0)
        m2 = jnp.where(tril_s, _rdot_nt(aq, kd), 0.0)
        n1 = jnp.where(tril, _rdot_nt(rq, bk), 0.0)
        n2 = jnp.where(tril, _rdot_nt(rq, kd), 0.0)
        tinv = eye + m1
        mp = m1
        for _ in range(5):
            mp = _rdot(mp, mp)
            tinv = tinv + _rdot(tinv, mp)
        st = st_ref[h]
        u = _rdot(tinv, _rdot(aq, st) + _rdot(m2, v_h))
        y = _rdot(rq, st) + _rdot(n1, u) + _rdot(n2, v_h)
        decay = jnp.exp(_dot_tn_hi(log_w[:, hs], ones_ck))
        st_ref[h] = st * decay + _rdot_tn(be, u) + _rdot_tn(ke, v_h)
        mean = jnp.mean(y, axis=-1, keepdims=True)
        cen = y - mean
        var = jnp.mean(cen * cen, axis=-1, keepdims=True)
        yn = cen * lax.rsqrt(var + RWKV_GN_EPS) * lg_ref[:, hs] + lb_ref[:, hs]
        bonus = jnp.sum(r_h * k_h * rk_ref[:, hs], axis=-1, keepdims=True) * v_h
        o_ref[0, :, hs] = (yn + bonus) * g[:, hs]


def _rdot(a, b):
    return _dot_hi(a, b)


def _rdot_nt(a, b):
    return lax.dot_general(a, b, (((1,), (1,)), ((), ())), precision=HI, preferred_element_type=F32)


def _rdot_tn(a, b):
    return _dot_tn_hi(a, b)


def _rwkv(p, mu, w_w2, w0, w_a2, a0, w_g2, k_k, k_a, r_k, lnx_g, lnx_b):
    b, t, cols = p.shape
    w = RWKV_HEADS * RWKV_HD
    c = RWKV_CHUNK
    vec = lambda a_: a_.reshape(1, -1)
    full = lambda shape: pl.BlockSpec(shape, lambda i, j: (0, 0))
    return pl.pallas_call(
        _rwkv_kernel,
        grid=(b, t // c),
        in_specs=[pl.BlockSpec((1, c, cols), lambda i, j: (i, j, 0)), full((1, cols)),
                  full(w_w2.shape), full((1, w)), full(w_a2.shape), full((1, w)), full(w_g2.shape),
                  full((1, w)), full((1, w)), full((1, w)), full((1, w)), full((1, w))],
        out_specs=pl.BlockSpec((1, c, w), lambda i, j: (i, j, 0)),
        out_shape=jax.ShapeDtypeStruct((b, t, w), F32),
        scratch_shapes=[pltpu.VMEM((8 + c, cols), F32),
                        pltpu.VMEM((RWKV_HEADS, RWKV_HD, RWKV_HD), F32)],
        compiler_params=_cparams("parallel", "arbitrary"),
        name="rwkv7",
    )(p, vec(mu), w_w2, vec(w0), w_a2, vec(a0), w_g2, vec(k_k), vec(k_a), vec(r_k), vec(lnx_g), vec(lnx_b))


def _fox_cum_kernel(f_ref, bf_ref, col_ref, row_ref, carry_ref, *, tt):
    @pl.when(pl.program_id(1) == 0)
    def _():
        carry_ref[...] = jnp.zeros(carry_ref.shape, F32)

    log_f = _log_sigmoid(f_ref[0] + bf_ref[...])
    cum = _dot_hi(_tri(tt).astype(F32), log_f) + carry_ref[0:1, :]
    col_ref[0] = cum
    row_ref[0] = jnp.transpose(cum)
    carry_ref[...] = jnp.broadcast_to(cum[tt - 1:tt, :], carry_ref.shape)


def _fox_cum(f_logit, b_f, tt=256):
    b, t, n = f_logit.shape
    return pl.pallas_call(
        functools.partial(_fox_cum_kernel, tt=tt),
        grid=(b, t // tt),
        in_specs=[pl.BlockSpec((1, tt, n), lambda i, j: (i, j, 0)), pl.BlockSpec((1, n), lambda i, j: (0, 0))],
        out_specs=[pl.BlockSpec((1, tt, n), lambda i, j: (i, j, 0)),
                   pl.BlockSpec((1, n, tt), lambda i, j: (i, 0, j))],
        out_shape=[jax.ShapeDtypeStruct((b, t, n), F32), jax.ShapeDtypeStruct((b, n, t), F32)],
        scratch_shapes=[pltpu.VMEM((8, n), F32)],
        compiler_params=_cparams("parallel", "arbitrary"),
        name="fox_cumsum",
    )(f_logit, b_f)


def _fox_kernel(q_ref, k_ref, v_ref, ccol_ref, crow_ref, o_ref, *, tq):
    qi = pl.program_id(2)
    pair = pl.program_id(1)
    d = FOX_HD
    rows = qi * tq + lax.broadcasted_iota(jnp.int32, (tq, tq), 0)
    cols0 = lax.broadcasted_iota(jnp.int32, (tq, tq), 1)
    lane = lax.broadcasted_iota(jnp.int32, (tq, LANE), 1)
    sub = lax.broadcasted_iota(jnp.int32, (8, tq), 0)
    outs = []
    for hh in range(LANE // d):
        head = pair * (LANE // d) + hh
        hs = slice(hh * d, (hh + 1) * d)
        q = (q_ref[0, :, hs] * (d ** -0.5)).astype(BF16)
        c_i = jnp.sum(jnp.where(lane == head, ccol_ref[0], 0.0), axis=-1, keepdims=True)

        def body(j, carry):
            m, l, acc = carry
            start = pl.multiple_of(j * tq, tq)
            kb = k_ref[0, pl.ds(start, tq), hs]
            vb = v_ref[0, pl.ds(start, tq), hs]
            cr = crow_ref[0, :, pl.ds(start, tq)]
            c_j = jnp.sum(jnp.where(sub == head, cr, 0.0), axis=0, keepdims=True)
            s = _dot_nt(q, kb) + c_i - c_j
            s = jnp.where(rows >= start + cols0, s, -jnp.inf)
            m_new = jnp.maximum(m, jnp.max(s, axis=-1, keepdims=True))
            scale = jnp.exp(m - m_new)
            p = jnp.exp(s - m_new)
            l = l * scale + jnp.sum(p, axis=-1, keepdims=True)
            acc = acc * scale + _dot(p, vb)
            return m_new, l, acc

        init = (jnp.full((tq, 1), -jnp.inf, F32), jnp.zeros((tq, 1), F32), jnp.zeros((tq, d), F32))
        m, l, acc = lax.fori_loop(0, qi + 1, body, init)
        outs.append(acc / l)
    o_ref[0] = jnp.concatenate(outs, axis=-1)


def _fox(q, k, v, c_col, c_row, tq=256):
    b, t, w = q.shape
    npairs = w // LANE
    return pl.pallas_call(
        functools.partial(_fox_kernel, tq=tq),
        grid=(b, npairs, t // tq),
        in_specs=[pl.BlockSpec((1, tq, LANE), lambda i, p, j: (i, j, p)),
                  pl.BlockSpec((1, t, LANE), lambda i, p, j: (i, 0, p)),
                  pl.BlockSpec((1, t, LANE), lambda i, p, j: (i, 0, p)),
                  pl.BlockSpec((1, tq, LANE), lambda i, p, j: (i, j, 0)),
                  pl.BlockSpec((1, 8, t), lambda i, p, j: (i, 0, 0))],
        out_specs=pl.BlockSpec((1, tq, LANE), lambda i, p, j: (i, j, p)),
        out_shape=jax.ShapeDtypeStruct((b, t, w), F32),
        compiler_params=_cparams("parallel", "parallel", "arbitrary"),
        name="fox_attention",
    )(q, k, v, c_col, c_row)


def _ffn_kernel(h_ref, g_ref, w1_ref, w3_ref, w2_ref, o_ref, xn_ref, acc_ref):
    f = pl.program_id(1)

    @pl.when(f == 0)
    def _():
        xn_ref[...] = _rmsnorm(h_ref[...], g_ref[...]).astype(BF16)
        acc_ref[...] = jnp.zeros(acc_ref.shape, F32)

    xn = xn_ref[...]
    mid = _silu(jnp.dot(xn, w1_ref[...], preferred_element_type=F32)) * jnp.dot(
        xn, w3_ref[...], preferred_element_type=F32)
    acc_ref[...] += _dot(mid, w2_ref[...])

    @pl.when(f == pl.num_programs(1) - 1)
    def _():
        o_ref[...] = h_ref[...] + acc_ref[...]


def _ffn(h, g, w1, w3, w2, tm=1024, tf=256):
    m, d = h.shape
    nf = w1.shape[1]
    return pl.pallas_call(
        _ffn_kernel,
        grid=(m // tm, nf // tf),
        in_specs=[pl.BlockSpec((tm, d), lambda i, f: (i, 0)),
                  pl.BlockSpec((1, d), lambda i, f: (0, 0)),
                  pl.BlockSpec((d, tf), lambda i, f: (0, f)),
                  pl.BlockSpec((d, tf), lambda i, f: (0, f)),
                  pl.BlockSpec((tf, d), lambda i, f: (f, 0))],
        out_specs=pl.BlockSpec((tm, d), lambda i, f: (i, 0)),
        out_shape=jax.ShapeDtypeStruct((m, d), F32),
        scratch_shapes=[pltpu.VMEM((tm, d), BF16), pltpu.VMEM((tm, d), F32)],
        compiler_params=_cparams("parallel", "arbitrary"),
        name="ffn_dense",
    )(h, g.reshape(1, d), w1, w3, w2)


def _moe_kernel(h_ref, g_ref, r_ref, w1_ref, w3_ref, w2_ref, fg_ref, o_ref, xn_ref, gate_ref, acc_ref):
    e = pl.program_id(1)
    f = pl.program_id(2)

    @pl.when((e == 0) & (f == 0))
    def _():
        xn = _rmsnorm(h_ref[...], g_ref[...])
        xn_ref[...] = xn.astype(BF16)
        acc_ref[...] = jnp.zeros(acc_ref.shape, F32)
        lane = lax.broadcasted_iota(jnp.int32, (xn.shape[0], LANE), 1)
        logits = jnp.where(lane < N_EXPERTS, _dot_hi(xn, r_ref[...]), -jnp.inf)
        m1 = jnp.max(logits, axis=-1, keepdims=True)
        i1 = jnp.min(jnp.where(logits == m1, lane, LANE), axis=-1, keepdims=True)
        rest = jnp.where(lane == i1, -jnp.inf, logits)
        m2 = jnp.max(rest, axis=-1, keepdims=True)
        i2 = jnp.min(jnp.where(rest == m2, lane, LANE), axis=-1, keepdims=True)
        e2 = jnp.exp(m2 - m1)
        g1 = 1.0 / (1.0 + e2)
        g2 = e2 / (1.0 + e2)
        gate_ref[...] = jnp.where(lane == i1, g1, 0.0) + jnp.where(lane == i2, g2, 0.0)

    lane = lax.broadcasted_iota(jnp.int32, gate_ref.shape, 1)
    gate = jnp.sum(jnp.where(lane == e, gate_ref[...], 0.0), axis=-1, keepdims=True)
    xn = xn_ref[...]
    mid = _silu(jnp.dot(xn, w1_ref[0], preferred_element_type=F32)) * jnp.dot(
        xn, w3_ref[0], preferred_element_type=F32)
    acc_ref[...] += _dot(mid * gate, w2_ref[0])

    @pl.when((e == pl.num_programs(1) - 1) & (f == pl.num_programs(2) - 1))
    def _():
        o_ref[...] = _rmsnorm(h_ref[...] + acc_ref[...], fg_ref[...])


def _moe(h, g, router, w1, w3, w2, final_g, tm=1024, tf=512):
    m, d = h.shape
    ne, _, nf = w1.shape
    return pl.pallas_call(
        _moe_kernel,
        grid=(m // tm, ne, nf // tf),
        in_specs=[pl.BlockSpec((tm, d), lambda i, e, f: (i, 0)),
                  pl.BlockSpec((1, d), lambda i, e, f: (0, 0)),
                  pl.BlockSpec((d, LANE), lambda i, e, f: (0, 0)),
                  pl.BlockSpec((1, d, tf), lambda i, e, f: (e, 0, f)),
                  pl.BlockSpec((1, d, tf), lambda i, e, f: (e, 0, f)),
                  pl.BlockSpec((1, tf, d), lambda i, e, f: (e, f, 0)),
                  pl.BlockSpec((1, d), lambda i, e, f: (0, 0))],
        out_specs=pl.BlockSpec((tm, d), lambda i, e, f: (i, 0)),
        out_shape=jax.ShapeDtypeStruct((m, d), F32),
        scratch_shapes=[pltpu.VMEM((tm, d), BF16), pltpu.VMEM((tm, LANE), F32), pltpu.VMEM((tm, d), F32)],
        compiler_params=_cparams("parallel", "arbitrary", "arbitrary"),
        name="moe",
    )(h, g.reshape(1, d), router, w1, w3, w2, final_g.reshape(1, d))


def _pad_cols(w, n):
    return jnp.pad(w, ((0, 0), (0, n - w.shape[1])))


def kernel(x, norm_mix_g, norm_ffn_g, w_in0, conv_w, conv_b, conv_ln_g, conv_ln_b, gla_w_a2, gla_b_a, gla_norm_g, w_out0, ffn_w1, ffn_w3, ffn_w2, w_in1, rwkv_mu, rwkv_w2, rwkv_w0, rwkv_a2, rwkv_a0, rwkv_g2, rwkv_k_k, rwkv_k_a, rwkv_r_k, rwkv_lnx_g, rwkv_lnx_b, fox_b_f, w_out1, moe_router, moe_w1, moe_w3, moe_w2, final_norm_g):
    b, t, d = x.shape
    m = b * t
    half = d // 2
    bf = lambda a: a.astype(BF16)
    h = x.reshape(m, d)

    gk = GLA_HEADS * GLA_DK
    o = [0, half, 2 * half, 2 * half + gk, 2 * half + 2 * gk, 3 * half + 2 * gk, 4 * half + 2 * gk]
    ws0 = [bf(w_in0[:, o[i]:o[i + 1]]) for i in range(6)] + [bf(_pad_cols(w_in0[:, o[6]:], LANE))]
    c_val, c_gate, q, k, v, g_out, a_lr = [
        a.reshape(b, t, -1) for a in _norm_proj(h, norm_mix_g[0], ws0)]
    y_conv = _conv(c_val, c_gate, conv_w, conv_b, conv_ln_g, conv_ln_b)
    w_a2 = jnp.pad(gla_w_a2, ((0, LANE - gla_w_a2.shape[0]), (0, 0)))
    y_gla = _gla(q, k, v, g_out, a_lr, w_a2, gla_b_a, gla_norm_g)
    h = _out_proj(h, y_conv.reshape(m, half), y_gla.reshape(m, half), bf(w_out0[:half]), bf(w_out0[half:]))
    h = _ffn(h, norm_ffn_g[0], bf(ffn_w1), bf(ffn_w3), bf(ffn_w2))

    rc = _RWKV_COLS
    ws1 = [bf(w_in1[:, :rc]), bf(w_in1[:, rc:rc + half]), bf(w_in1[:, rc + half:rc + 2 * half]),
           bf(w_in1[:, rc + 2 * half:rc + 3 * half]), bf(_pad_cols(w_in1[:, rc + 3 * half:], LANE))]
    p_rwkv, fq, fk, fv, f_logit = [a.reshape(b, t, -1) for a in _norm_proj(h, norm_mix_g[1], ws1)]
    y_rwkv = _rwkv(p_rwkv, rwkv_mu, rwkv_w2, rwkv_w0, rwkv_a2, rwkv_a0, rwkv_g2, rwkv_k_k, rwkv_k_a,
                   rwkv_r_k, rwkv_lnx_g, rwkv_lnx_b)
    b_f = jnp.pad(fox_b_f, (0, LANE - FOX_HEADS)).reshape(1, LANE)
    c_col, c_row = _fox_cum(f_logit, b_f)
    y_fox = _fox(fq, fk, fv, c_col, c_row)
    h = _out_proj(h, y_rwkv.reshape(m, half), y_fox.reshape(m, half), bf(w_out1[:half]), bf(w_out1[half:]))
    out = _moe(h, norm_ffn_g[1], _pad_cols(moe_router, LANE), bf(moe_w1), bf(moe_w3), bf(moe_w2), final_norm_g)
    return out.reshape(b, t, d)
```

```python
import functools

import jax
import jax.numpy as jnp
from jax import lax
from jax.experimental import pallas as pl
from jax.experimental.pallas import tpu as pltpu

F32 = jnp.float32
BF16 = jnp.bfloat16
HI = lax.Precision.HIGHEST

NORM_EPS = 1e-6
CONV_WIDTH = 31
CONV_LN_EPS = 1e-5
GLA_HEADS = 4
GLA_DK = 64
GLA_DV = 128
GLA_TAU = 16.0
GLA_CHUNK = 64
RWKV_HEADS = 8
RWKV_HD = 64
RWKV_CHUNK = 64
RWKV_DECAY_SCALE = 0.606531
RWKV_GN_EPS = 64e-5
RWKV_L2_EPS = 1e-12
FOX_HEADS = 8
FOX_HD = 64
N_EXPERTS = 8
LANE = 128
VMEM_LIMIT = 56 * 1024 * 1024


def _cparams(*sem):
    return pltpu.CompilerParams(dimension_semantics=sem, vmem_limit_bytes=VMEM_LIMIT)


def _dot(a, b):
    return jnp.dot(a.astype(BF16), b.astype(BF16), preferred_element_type=F32)


def _dot_nt(a, b):
    return lax.dot_general(a.astype(BF16), b.astype(BF16), (((1,), (1,)), ((), ())),
                           preferred_element_type=F32)


def _dot_tn(a, b):
    return lax.dot_general(a.astype(BF16), b.astype(BF16), (((0,), (0,)), ((), ())),
                           preferred_element_type=F32)


def _dot_hi(a, b):
    return jnp.dot(a, b, precision=HI, preferred_element_type=F32)


def _dot_tn_hi(a, b):
    return lax.dot_general(a, b, (((0,), (0,)), ((), ())), precision=HI, preferred_element_type=F32)


def _sigmoid(x):
    return 1.0 / (1.0 + jnp.exp(-x))


def _silu(x):
    return x * _sigmoid(x)


def _log_sigmoid(x):
    return jnp.minimum(x, 0.0) - jnp.log(1.0 + jnp.exp(-jnp.abs(x)))


def _rmsnorm(x, g):
    return x * lax.rsqrt(jnp.mean(x * x, axis=-1, keepdims=True) + NORM_EPS) * g


def _tri(n, strict=False):
    r = lax.broadcasted_iota(jnp.int32, (n, n), 0)
    c = lax.broadcasted_iota(jnp.int32, (n, n), 1)
    return (r > c) if strict else (r >= c)


def _norm_proj_kernel(h_ref, g_ref, *refs):
    n = len(refs) // 2
    xn = _rmsnorm(h_ref[...], g_ref[...]).astype(BF16)
    for w_ref, o_ref in zip(refs[:n], refs[n:]):
        o_ref[...] = jnp.dot(xn, w_ref[...], preferred_element_type=F32)


def _norm_proj(h, g, ws, tm=512):
    m, d = h.shape
    return pl.pallas_call(
        _norm_proj_kernel,
        grid=(m // tm,),
        in_specs=[pl.BlockSpec((tm, d), lambda i: (i, 0)), pl.BlockSpec((1, d), lambda i: (0, 0))]
        + [pl.BlockSpec(w.shape, lambda i: (0, 0)) for w in ws],
        out_specs=[pl.BlockSpec((tm, w.shape[1]), lambda i: (i, 0)) for w in ws],
        out_shape=[jax.ShapeDtypeStruct((m, w.shape[1]), F32) for w in ws],
        compiler_params=_cparams("parallel"),
        name="norm_proj",
    )(h, g.reshape(1, d), *ws)


def _out_proj_kernel(h_ref, ya_ref, yb_ref, wa_ref, wb_ref, o_ref):
    o_ref[...] = h_ref[...] + _dot(ya_ref[...], wa_ref[...]) + _dot(yb_ref[...], wb_ref[...])


def _out_proj(h, ya, yb, wa, wb, tm=512):
    m, d = h.shape
    k = ya.shape[1]
    return pl.pallas_call(
        _out_proj_kernel,
        grid=(m // tm,),
        in_specs=[pl.BlockSpec((tm, d), lambda i: (i, 0)),
                  pl.BlockSpec((tm, k), lambda i: (i, 0)),
                  pl.BlockSpec((tm, k), lambda i: (i, 0)),
                  pl.BlockSpec((k, d), lambda i: (0, 0)),
                  pl.BlockSpec((k, d), lambda i: (0, 0))],
        out_specs=pl.BlockSpec((tm, d), lambda i: (i, 0)),
        out_shape=jax.ShapeDtypeStruct((m, d), F32),
        compiler_params=_cparams("parallel"),
        name="out_proj",
    )(h, ya, yb, wa, wb)


_CONV_HALO = 32


def _conv_kernel(val_ref, gate_ref, w_ref, b_ref, lg_ref, lb_ref, o_ref, u_ref, *, tt, rows):
    @pl.when(pl.program_id(1) == 0)
    def _():
        u_ref[0:_CONV_HALO, :] = jnp.zeros((_CONV_HALO, u_ref.shape[1]), F32)

    u_ref[_CONV_HALO:_CONV_HALO + tt, :] = val_ref[0] * _sigmoid(gate_ref[0])
    base = _CONV_HALO - (CONV_WIDTH - 1)
    for r0 in range(0, tt, rows):
        acc = jnp.zeros((rows, u_ref.shape[1]), F32) + b_ref[...]
        for j in range(CONV_WIDTH):
            acc = acc + u_ref[r0 + base + j:r0 + base + j + rows, :] * w_ref[j:j + 1, :]
        mu = jnp.mean(acc, axis=-1, keepdims=True)
        cen = acc - mu
        var = jnp.mean(cen * cen, axis=-1, keepdims=True)
        y = cen * lax.rsqrt(var + CONV_LN_EPS) * lg_ref[...] + lb_ref[...]
        o_ref[0, r0:r0 + rows, :] = _silu(y)
    u_ref[0:_CONV_HALO, :] = u_ref[tt:tt + _CONV_HALO, :]


def _conv(val, gate, conv_w, conv_b, ln_g, ln_b, tt=256, rows=32):
    b, t, c = val.shape
    wpad = jnp.zeros((32, c), F32).at[:CONV_WIDTH].set(conv_w)
    vec = lambda a: a.reshape(1, c)
    tile = pl.BlockSpec((1, tt, c), lambda i, j: (i, j, 0))
    full = lambda shape: pl.BlockSpec(shape, lambda i, j: (0, 0))
    return pl.pallas_call(
        functools.partial(_conv_kernel, tt=tt, rows=rows),
        grid=(b, t // tt),
        in_specs=[tile, tile, full((32, c)), full((1, c)), full((1, c)), full((1, c))],
        out_specs=tile,
        out_shape=jax.ShapeDtypeStruct((b, t, c), F32),
        scratch_shapes=[pltpu.VMEM((tt + _CONV_HALO, c), F32)],
        compiler_params=_cparams("parallel", "arbitrary"),
        name="conformer_conv",
    )(val, gate, wpad, vec(conv_b), vec(ln_g), vec(ln_b))


def _gla_kernel(q_ref, k_ref, v_ref, go_ref, alr_ref, wa2_ref, ba_ref, ng_ref, o_ref, s_ref, *, tt):
    c = GLA_CHUNK

    @pl.when(pl.program_id(1) == 0)
    def _():
        s_ref[...] = jnp.zeros(s_ref.shape, F32)

    tril = _tri(c)
    tril_f = tril.astype(F32)
    ones_cv = jnp.ones((c, GLA_DV), F32)
    for ci in range(tt // c):
        sl = slice(ci * c, (ci + 1) * c)
        z = _dot(alr_ref[0, sl, :], wa2_ref[...]) + ba_ref[...]
        log_a = _log_sigmoid(z) * (1.0 / GLA_TAU)
        bcum = _dot_hi(tril_f, log_a)
        b_last = bcum[c - 1:c, :]
        q = q_ref[0, sl, :] * (GLA_DK ** -0.5)
        k = k_ref[0, sl, :]
        q_dec = q * jnp.exp(bcum)
        k_dec = k * jnp.exp(-bcum)
        k_end = k * jnp.exp(b_last - bcum)
        outs = []
        for h in range(GLA_HEADS):
            ks = slice(h * GLA_DK, (h + 1) * GLA_DK)
            vs = slice(h * GLA_DV, (h + 1) * GLA_DV)
            v = v_ref[0, sl, vs]
            attn = jnp.where(tril, _dot_nt(q_dec[:, ks], k_dec[:, ks]), 0.0)
            state = s_ref[h]
            o = _dot(attn, v) + _dot(q_dec[:, ks], state)
            decay = jnp.exp(_dot_tn_hi(log_a[:, ks], ones_cv))
            s_ref[h] = state * decay + _dot_tn(k_end[:, ks], v)
            o = o * lax.rsqrt(jnp.mean(o * o, axis=-1, keepdims=True) + NORM_EPS)
            outs.append(o * ng_ref[:, vs] * _silu(go_ref[0, sl, vs]))
        o_ref[0, sl, :] = jnp.concatenate(outs, axis=-1)


def _gla(q, k, v, g_out, a_lr, w_a2, b_a, norm_g, tt=256):
    b, t, dq = q.shape
    dv = v.shape[-1]
    tile = lambda n: pl.BlockSpec((1, tt, n), lambda i, j: (i, j, 0))
    full = lambda shape: pl.BlockSpec(shape, lambda i, j: (0, 0))
    return pl.pallas_call(
        functools.partial(_gla_kernel, tt=tt),
        grid=(b, t // tt),
        in_specs=[tile(dq), tile(dq), tile(dv), tile(dv), tile(a_lr.shape[-1]),
                  full(w_a2.shape), full((1, dq)), full((1, dv))],
        out_specs=tile(dv),
        out_shape=jax.ShapeDtypeStruct((b, t, dv), F32),
        scratch_shapes=[pltpu.VMEM((GLA_HEADS, GLA_DK, GLA_DV), F32)],
        compiler_params=_cparams("parallel", "arbitrary"),
        name="gla",
    )(q, k, v, g_out, a_lr, w_a2, b_a.reshape(1, dq), norm_g.reshape(1, dv))


_RWKV_COLS = 3 * 512 + 64 + 64 + 128


def _rwkv_kernel(p_ref, mu_ref, w2_ref, w0_ref, a2_ref, a0_ref, g2_ref, kk_ref, ka_ref, rk_ref,
                 lg_ref, lb_ref, o_ref, xs_ref, st_ref, *, tt):
    c = RWKV_CHUNK
    hd = RWKV_HD
    w = RWKV_HEADS * hd

    @pl.when(pl.program_id(1) == 0)
    def _():
        xs_ref[0:8, :] = jnp.zeros((8, xs_ref.shape[1]), F32)
        st_ref[...] = jnp.zeros(st_ref.shape, F32)

    x = p_ref[0]
    xs_ref[8:8 + tt, :] = x
    prev = xs_ref[7:7 + tt, :]
    xs_ref[7:8, :] = x[tt - 1:tt, :]
    x = x + (prev - x) * mu_ref[...]

    r = x[:, 0:w]
    k = x[:, w:2 * w]
    v = x[:, 2 * w:3 * w]
    xw = x[:, 3 * w:3 * w + 64]
    xa = x[:, 3 * w + 64:3 * w + 128]
    xg = x[:, 3 * w + 128:3 * w + 256]
    log_w = -RWKV_DECAY_SCALE * _sigmoid(w0_ref[...] + _dot(jnp.tanh(xw), w2_ref[...]))
    a = _sigmoid(a0_ref[...] + _dot(xa, a2_ref[...]))
    g = _dot(_sigmoid(xg), g2_ref[...])
    kk = k * kk_ref[...]
    k_mod = k * (1.0 + (a - 1.0) * ka_ref[...])
    rkr = r * k_mod * rk_ref[...]

    tril = _tri(c)
    tril_s = _tri(c, strict=True)
    tril_f = tril.astype(F32)
    eye = (lax.broadcasted_iota(jnp.int32, (c, c), 0)
           == lax.broadcasted_iota(jnp.int32, (c, c), 1)).astype(F32)
    zeros_cc = jnp.zeros((c, hd), F32)

    nc = tt // c
    units = [(ci, h) for ci in range(nc) for h in range(RWKV_HEADS)]
    rows_of = lambda ci: slice(ci * c, (ci + 1) * c)
    lanes_of = lambda h: slice(h * hd, (h + 1) * hd)

    exps = []
    for ci in range(nc):
        lw_c = log_w[rows_of(ci)]
        lcum = _dot_hi(tril_f, lw_c)
        l_last = lcum[c - 1:c, :]
        exps.append((jnp.exp(lcum), jnp.exp(-lcum), jnp.exp(l_last - lcum), jnp.exp(lcum - lw_c),
                     jnp.exp(l_last)))

    aqs, rqs, vs, lhss, rhss, bkes = [], [], [], [], [], []
    for ci, h in units:
        rows, hs = rows_of(ci), lanes_of(h)
        e_pos, e_neg, e_end, e_prev, _ = exps[ci]
        kk_h = kk[rows, hs]
        nrm = jnp.sqrt(jnp.sum(kk_h * kk_h, axis=-1, keepdims=True))
        kk_h = kk_h / jnp.maximum(nrm, RWKV_L2_EPS)
        k_h = k_mod[rows, hs]
        kka = kk_h * a[rows, hs]
        aq = -kk_h * e_prev[:, hs]
        rq = r[rows, hs] * e_pos[:, hs]
        aqs.append(aq)
        rqs.append(rq)
        vs.append(v[rows, hs])
        lhss.append(jnp.concatenate([aq, rq], axis=0))
        rhss.append(jnp.concatenate([kka * e_neg[:, hs], k_h * e_neg[:, hs]], axis=0))
        bkes.append(jnp.concatenate([kka * e_end[:, hs], k_h * e_end[:, hs]], axis=0))

    xss = [_dot_nt(lhs, rhs) for lhs, rhs in zip(lhss, rhss)]
    m1s = [jnp.where(tril_s, xs[:c, :c], 0.0) for xs in xss]
    m2s = [jnp.where(tril_s, xs[:c, c:], 0.0) for xs in xss]
    n12s = [jnp.concatenate([jnp.where(tril, xs[c:, :c], 0.0), jnp.where(tril, xs[c:, c:], 0.0)], axis=1)
            for xs in xss]
    tinvs = [eye + m1 for m1 in m1s]
    mps = m1s
    for _ in range(5):
        mps = [_dot(mp, mp) for mp in mps]
        tinvs = [tinv + _dot(tinv, mp) for tinv, mp in zip(tinvs, mps)]
    mvs = [_dot(m2, v_h) for m2, v_h in zip(m2s, vs)]
    pqs = [_dot(tinv, jnp.concatenate([mv, aq], axis=1)) for tinv, mv, aq in zip(tinvs, mvs, aqs)]
    pqvs = [jnp.concatenate([pq, jnp.concatenate([v_h, zeros_cc], axis=1)], axis=0)
            for pq, v_h in zip(pqs, vs)]
    yys = [_dot(n12, pqv) for n12, pqv in zip(n12s, pqvs)]
    ghs = [_dot_tn(bke, pqv) for bke, pqv in zip(bkes, pqvs)]

    states = [st_ref[h] for h in range(RWKV_HEADS)]
    ys = []
    for u, (ci, h) in enumerate(units):
        yq = rqs[u] + yys[u][:, hd:]
        gmat = eye * exps[ci][4][:, lanes_of(h)] + ghs[u][:, hd:]
        res = _dot(jnp.concatenate([yq, gmat], axis=0), states[h])
        ys.append(res[:c] + yys[u][:, :hd])
        states[h] = res[c:] + ghs[u][:, :hd]
    for h in range(RWKV_HEADS):
        st_ref[h] = states[h]

    out_rows = []
    for ci in range(nc):
        yns, bonuses = [], []
        for h in range(RWKV_HEADS):
            u = ci * RWKV_HEADS + h
            y = ys[u]
            cen = y - jnp.mean(y, axis=-1, keepdims=True)
            var = jnp.mean(cen * cen, axis=-1, keepdims=True)
            yns.append(cen * lax.rsqrt(var + RWKV_GN_EPS))
            bonuses.append(jnp.sum(rkr[rows_of(ci), lanes_of(h)], axis=-1, keepdims=True) * vs[u])
        yn = jnp.concatenate(yns, axis=1)
        bonus = jnp.concatenate(bonuses, axis=1)
        out_rows.append((yn * lg_ref[...] + lb_ref[...] + bonus) * g[rows_of(ci)])
    o_ref[0] = jnp.concatenate(out_rows, axis=0)


def _rwkv(p, mu, w_w2, w0, w_a2, a0, w_g2, k_k, k_a, r_k, lnx_g, lnx_b, tt=128):
    b, t, cols = p.shape
    w = RWKV_HEADS * RWKV_HD
    vec = lambda a_: a_.reshape(1, -1)
    full = lambda shape: pl.BlockSpec(shape, lambda i, j: (0, 0))
    return pl.pallas_call(
        functools.partial(_rwkv_kernel, tt=tt),
        grid=(b, t // tt),
        in_specs=[pl.BlockSpec((1, tt, cols), lambda i, j: (i, j, 0)), full((1, cols)),
                  full(w_w2.shape), full((1, w)), full(w_a2.shape), full((1, w)), full(w_g2.shape),
                  full((1, w)), full((1, w)), full((1, w)), full((1, w)), full((1, w))],
        out_specs=pl.BlockSpec((1, tt, w), lambda i, j: (i, j, 0)),
        out_shape=jax.ShapeDtypeStruct((b, t, w), F32),
        scratch_shapes=[pltpu.VMEM((8 + tt, cols), F32),
                        pltpu.VMEM((RWKV_HEADS, RWKV_HD, RWKV_HD), F32)],
        compiler_params=_cparams("parallel", "arbitrary"),
        name="rwkv7",
    )(p, vec(mu), w_w2, vec(w0), w_a2, vec(a0), w_g2, vec(k_k), vec(k_a), vec(r_k), vec(lnx_g), vec(lnx_b))


def _fox_cum_kernel(f_ref, bf_ref, col_ref, carry_ref, *, tt):
    @pl.when(pl.program_id(1) == 0)
    def _():
        carry_ref[...] = jnp.zeros(carry_ref.shape, F32)

    log_f = _log_sigmoid(f_ref[0] + bf_ref[...])
    cum = _dot_hi(_tri(tt).astype(F32), log_f) + carry_ref[0:1, :]
    col_ref[0] = cum
    carry_ref[...] = jnp.broadcast_to(cum[tt - 1:tt, :], carry_ref.shape)


def _fox_cum(f_logit, b_f, tt=256):
    b, t, n = f_logit.shape
    return pl.pallas_call(
        functools.partial(_fox_cum_kernel, tt=tt),
        grid=(b, t // tt),
        in_specs=[pl.BlockSpec((1, tt, n), lambda i, j: (i, j, 0)), pl.BlockSpec((1, n), lambda i, j: (0, 0))],
        out_specs=pl.BlockSpec((1, tt, n), lambda i, j: (i, j, 0)),
        out_shape=jax.ShapeDtypeStruct((b, t, n), F32),
        scratch_shapes=[pltpu.VMEM((8, n), F32)],
        compiler_params=_cparams("parallel", "arbitrary"),
        name="fox_cumsum",
    )(f_logit, b_f)


def _split3(x):
    hi = x.astype(BF16).astype(F32)
    mid = (x - hi).astype(BF16).astype(F32)
    lo = (x - hi - mid).astype(BF16).astype(F32)
    return hi, mid, lo


def _fox_kernel(q_ref, k_ref, v_ref, c_ref, o_ref, kaug_ref, vbf_ref, *, tq):
    qi = pl.program_id(2)
    pair = pl.program_id(1)
    d = FOX_HD
    nh = LANE // d

    def augment(x_h, c, first):
        lane = lax.broadcasted_iota(jnp.int32, x_h.shape, 1)
        hi, mid, lo = _split3(c)
        parts = (hi, mid, lo, 1.0, 1.0, 1.0) if first else (1.0, 1.0, 1.0, -hi, -mid, -lo)
        out = jnp.where(lane < d, x_h, 0.0)
        for n, part in enumerate(parts):
            out = jnp.where(lane == d + n, part, out)
        return out.astype(BF16)

    def head_col(c_all, head):
        lane = lax.broadcasted_iota(jnp.int32, c_all.shape, 1)
        return jnp.sum(jnp.where(lane == head, c_all, 0.0), axis=-1, keepdims=True)

    @pl.when(qi == 0)
    def _():
        k = k_ref[0]
        vbf_ref[...] = v_ref[0].astype(BF16)
        for hh in range(nh):
            k_h = k if hh == 0 else pltpu.roll(k, LANE - hh * d, axis=1)
            kaug_ref[hh] = augment(k_h, head_col(c_ref[0], pair * nh + hh), first=False)

    row0 = pl.multiple_of(qi * tq, tq)
    q = q_ref[0] * (d ** -0.5)
    c_q = c_ref[0, pl.ds(row0, tq), :]
    qs = []
    for hh in range(nh):
        q_h = q if hh == 0 else pltpu.roll(q, LANE - hh * d, axis=1)
        qs.append(augment(q_h, head_col(c_q, pair * nh + hh), first=True))
    causal = _tri(tq)

    def step(j, carry, diagonal):
        start = pl.multiple_of(j * tq, tq)
        vb = vbf_ref[pl.ds(start, tq), :]
        new = []
        for hh in range(nh):
            m, l, acc = carry[hh]
            s = lax.dot_general(qs[hh], kaug_ref[hh, pl.ds(start, tq), :], (((1,), (1,)), ((), ())),
                                preferred_element_type=F32)
            if diagonal:
                s = jnp.where(causal, s, -jnp.inf)
            m_new = jnp.maximum(m, jnp.max(s, axis=-1, keepdims=True))
            scale = jnp.exp(m - m_new)
            p = jnp.exp(s - m_new)
            l = l * scale + jnp.sum(p, axis=-1, keepdims=True)
            acc = acc * scale + jnp.dot(p.astype(BF16), vb, preferred_element_type=F32)
            new.append((m_new, l, acc))
        return tuple(new)

    init = tuple((jnp.full((tq, 1), -jnp.inf, F32), jnp.zeros((tq, 1), F32), jnp.zeros((tq, LANE), F32))
                 for _ in range(nh))
    carry = lax.fori_loop(0, qi, lambda j, cr: step(j, cr, False), init)
    carry = step(qi, carry, True)
    lane = lax.broadcasted_iota(jnp.int32, (tq, LANE), 1)
    out = carry[0][2] / carry[0][1]
    for hh in range(1, nh):
        out = jnp.where(lane >= hh * d, carry[hh][2] / carry[hh][1], out)
    o_ref[0] = out


def _fox(q, k, v, c_col, tq=256):
    b, t, w = q.shape
    npairs = w // LANE
    return pl.pallas_call(
        functools.partial(_fox_kernel, tq=tq),
        grid=(b, npairs, t // tq),
        in_specs=[pl.BlockSpec((1, tq, LANE), lambda i, p, j: (i, j, p)),
                  pl.BlockSpec((1, t, LANE), lambda i, p, j: (i, 0, p)),
                  pl.BlockSpec((1, t, LANE), lambda i, p, j: (i, 0, p)),
                  pl.BlockSpec((1, t, LANE), lambda i, p, j: (i, 0, 0))],
        out_specs=pl.BlockSpec((1, tq, LANE), lambda i, p, j: (i, j, p)),
        out_shape=jax.ShapeDtypeStruct((b, t, w), F32),
        scratch_shapes=[pltpu.VMEM((LANE // FOX_HD, t, LANE), BF16), pltpu.VMEM((t, LANE), BF16)],
        compiler_params=_cparams("parallel", "arbitrary", "arbitrary"),
        name="fox_attention",
    )(q, k, v, c_col)


def _ffn_kernel(h_ref, g_ref, w1_ref, w3_ref, w2_ref, o_ref, xn_ref, acc_ref):
    f = pl.program_id(1)

    @pl.when(f == 0)
    def _():
        xn_ref[...] = _rmsnorm(h_ref[...], g_ref[...]).astype(BF16)
        acc_ref[...] = jnp.zeros(acc_ref.shape, F32)

    xn = xn_ref[...]
    mid = _silu(jnp.dot(xn, w1_ref[...], preferred_element_type=F32)) * jnp.dot(
        xn, w3_ref[...], preferred_element_type=F32)
    acc_ref[...] += _dot(mid, w2_ref[...])

    @pl.when(f == pl.num_programs(1) - 1)
    def _():
        o_ref[...] = h_ref[...] + acc_ref[...]


def _ffn(h, g, w1, w3, w2, tm=1024, tf=256):
    m, d = h.shape
    nf = w1.shape[1]
    return pl.pallas_call(
        _ffn_kernel,
        grid=(m // tm, nf // tf),
        in_specs=[pl.BlockSpec((tm, d), lambda i, f: (i, 0)),
                  pl.BlockSpec((1, d), lambda i, f: (0, 0)),
                  pl.BlockSpec((d, tf), lambda i, f: (0, f)),
                  pl.BlockSpec((d, tf), lambda i, f: (0, f)),
                  pl.BlockSpec((tf, d), lambda i, f: (f, 0))],
        out_specs=pl.BlockSpec((tm, d), lambda i, f: (i, 0)),
        out_shape=jax.ShapeDtypeStruct((m, d), F32),
        scratch_shapes=[pltpu.VMEM((tm, d), BF16), pltpu.VMEM((tm, d), F32)],
        compiler_params=_cparams("parallel", "arbitrary"),
        name="ffn_dense",
    )(h, g.reshape(1, d), w1, w3, w2)


def _moe_kernel(h_ref, g_ref, r_ref, w1_ref, w3_ref, w2_ref, fg_ref, o_ref, xn_ref, gate_ref, acc_ref):
    e = pl.program_id(1)
    f = pl.program_id(2)

    @pl.when((e == 0) & (f == 0))
    def _():
        xn = _rmsnorm(h_ref[...], g_ref[...])
        xn_ref[...] = xn.astype(BF16)
        acc_ref[...] = jnp.zeros(acc_ref.shape, F32)
        lane = lax.broadcasted_iota(jnp.int32, (xn.shape[0], LANE), 1)
        logits = jnp.where(lane < N_EXPERTS, _dot_hi(xn, r_ref[...]), -jnp.inf)
        m1 = jnp.max(logits, axis=-1, keepdims=True)
        i1 = jnp.min(jnp.where(logits == m1, lane, LANE), axis=-1, keepdims=True)
        rest = jnp.where(lane == i1, -jnp.inf, logits)
        m2 = jnp.max(rest, axis=-1, keepdims=True)
        i2 = jnp.min(jnp.where(rest == m2, lane, LANE), axis=-1, keepdims=True)
        e2 = jnp.exp(m2 - m1)
        g1 = 1.0 / (1.0 + e2)
        g2 = e2 / (1.0 + e2)
        gate_ref[...] = jnp.where(lane == i1, g1, 0.0) + jnp.where(lane == i2, g2, 0.0)

    lane = lax.broadcasted_iota(jnp.int32, gate_ref.shape, 1)
    gate = jnp.sum(jnp.where(lane == e, gate_ref[...], 0.0), axis=-1, keepdims=True)
    xn = xn_ref[...]
    mid = _silu(jnp.dot(xn, w1_ref[0], preferred_element_type=F32)) * jnp.dot(
        xn, w3_ref[0], preferred_element_type=F32)
    acc_ref[...] += _dot(mid * gate, w2_ref[0])

    @pl.when((e == pl.num_programs(1) - 1) & (f == pl.num_programs(2) - 1))
    def _():
        o_ref[...] = _rmsnorm(h_ref[...] + acc_ref[...], fg_ref[...])


def _moe(h, g, router, w1, w3, w2, final_g, tm=1024, tf=512):
    m, d = h.shape
    ne, _, nf = w1.shape
    return pl.pallas_call(
        _moe_kernel,
        grid=(m // tm, ne, nf // tf),
        in_specs=[pl.BlockSpec((tm, d), lambda i, e, f: (i, 0)),
                  pl.BlockSpec((1, d), lambda i, e, f: (0, 0)),
                  pl.BlockSpec((d, LANE), lambda i, e, f: (0, 0)),
                  pl.BlockSpec((1, d, tf), lambda i, e, f: (e, 0, f)),
                  pl.BlockSpec((1, d, tf), lambda i, e, f: (e, 0, f)),
                  pl.BlockSpec((1, tf, d), lambda i, e, f: (e, f, 0)),
                  pl.BlockSpec((1, d), lambda i, e, f: (0, 0))],
        out_specs=pl.BlockSpec((tm, d), lambda i, e, f: (i, 0)),
        out_shape=jax.ShapeDtypeStruct((m, d), F32),
        scratch_shapes=[pltpu.VMEM((tm, d), BF16), pltpu.VMEM((tm, LANE), F32), pltpu.VMEM((tm, d), F32)],
        compiler_params=_cparams("parallel", "arbitrary", "arbitrary"),
        name="moe",
    )(h, g.reshape(1, d), router, w1, w3, w2, final_g.reshape(1, d))


def _pad_cols(w, n):
    return jnp.pad(w, ((0, 0), (0, n - w.shape[1])))


def kernel(x, norm_mix_g, norm_ffn_g, w_in0, conv_w, conv_b, conv_ln_g, conv_ln_b, gla_w_a2, gla_b_a, gla_norm_g, w_out0, ffn_w1, ffn_w3, ffn_w2, w_in1, rwkv_mu, rwkv_w2, rwkv_w0, rwkv_a2, rwkv_a0, rwkv_g2, rwkv_k_k, rwkv_k_a, rwkv_r_k, rwkv_lnx_g, rwkv_lnx_b, fox_b_f, w_out1, moe_router, moe_w1, moe_w3, moe_w2, final_norm_g):
    b, t, d = x.shape
    m = b * t
    half = d // 2
    bf = lambda a: a.astype(BF16)
    h = x.reshape(m, d)

    gk = GLA_HEADS * GLA_DK
    o = [0, half, 2 * half, 2 * half + gk, 2 * half + 2 * gk, 3 * half + 2 * gk, 4 * half + 2 * gk]
    ws0 = [bf(w_in0[:, o[i]:o[i + 1]]) for i in range(6)] + [bf(_pad_cols(w_in0[:, o[6]:], LANE))]
    c_val, c_gate, q, k, v, g_out, a_lr = [
        a.reshape(b, t, -1) for a in _norm_proj(h, norm_mix_g[0], ws0)]
    y_conv = _conv(c_val, c_gate, conv_w, conv_b, conv_ln_g, conv_ln_b)
    w_a2 = jnp.pad(gla_w_a2, ((0, LANE - gla_w_a2.shape[0]), (0, 0)))
    y_gla = _gla(q, k, v, g_out, a_lr, w_a2, gla_b_a, gla_norm_g)
    h = _out_proj(h, y_conv.reshape(m, half), y_gla.reshape(m, half), bf(w_out0[:half]), bf(w_out0[half:]))
    h = _ffn(h, norm_ffn_g[0], bf(ffn_w1), bf(ffn_w3), bf(ffn_w2))

    rc = _RWKV_COLS
    ws1 = [bf(w_in1[:, :rc]), bf(w_in1[:, rc:rc + half]), bf(w_in1[:, rc + half:rc + 2 * half]),
           bf(w_in1[:, rc + 2 * half:rc + 3 * half]), bf(_pad_cols(w_in1[:, rc + 3 * half:], LANE))]
    p_rwkv, fq, fk, fv, f_logit = [a.reshape(b, t, -1) for a in _norm_proj(h, norm_mix_g[1], ws1)]
    y_rwkv = _rwkv(p_rwkv, rwkv_mu, rwkv_w2, rwkv_w0, rwkv_a2, rwkv_a0, rwkv_g2, rwkv_k_k, rwkv_k_a,
                   rwkv_r_k, rwkv_lnx_g, rwkv_lnx_b)
    b_f = jnp.pad(fox_b_f, (0, LANE - FOX_HEADS)).reshape(1, LANE)
    y_fox = _fox(fq, fk, fv, _fox_cum(f_logit, b_f))
    h = _out_proj(h, y_rwkv.reshape(m, half), y_fox.reshape(m, half), bf(w_out1[:half]), bf(w_out1[half:]))
    out = _moe(h, norm_ffn_g[1], _pad_cols(moe_router, LANE), bf(moe_w1), bf(moe_w3), bf(moe_w2), final_norm_g)
    return out.reshape(b, t, d)
```

```python
import functools

import jax
import jax.numpy as jnp
from jax import lax
from jax.experimental import pallas as pl
from jax.experimental.pallas import tpu as pltpu

F32 = jnp.float32
BF16 = jnp.bfloat16
HI = lax.Precision.HIGHEST

NORM_EPS = 1e-6
CONV_WIDTH = 31
CONV_LN_EPS = 1e-5
GLA_HEADS = 4
GLA_DK = 64
GLA_DV = 128
GLA_TAU = 16.0
GLA_CHUNK = 64
RWKV_HEADS = 8
RWKV_HD = 64
RWKV_CHUNK = 64
RWKV_DECAY_SCALE = 0.606531
RWKV_GN_EPS = 64e-5
RWKV_L2_EPS = 1e-12
FOX_HEADS = 8
FOX_HD = 64
N_EXPERTS = 8
LANE = 128
VMEM_LIMIT = 56 * 1024 * 1024


def _cparams(*sem):
    return pltpu.CompilerParams(dimension_semantics=sem, vmem_limit_bytes=VMEM_LIMIT)


def _dot(a, b):
    return jnp.dot(a.astype(BF16), b.astype(BF16), preferred_element_type=F32)


def _dot_nt(a, b):
    return lax.dot_general(a.astype(BF16), b.astype(BF16), (((1,), (1,)), ((), ())),
                           preferred_element_type=F32)


def _dot_tn(a, b):
    return lax.dot_general(a.astype(BF16), b.astype(BF16), (((0,), (0,)), ((), ())),
                           preferred_element_type=F32)


def _dot_hi(a, b):
    return jnp.dot(a, b, precision=HI, preferred_element_type=F32)


def _dot_tn_hi(a, b):
    return lax.dot_general(a, b, (((0,), (0,)), ((), ())), precision=HI, preferred_element_type=F32)


def _sigmoid(x):
    return 1.0 / (1.0 + jnp.exp(-x))


def _silu(x):
    return x * _sigmoid(x)


def _log_sigmoid(x):
    return jnp.minimum(x, 0.0) - jnp.log(1.0 + jnp.exp(-jnp.abs(x)))


def _rmsnorm(x, g):
    return x * lax.rsqrt(jnp.mean(x * x, axis=-1, keepdims=True) + NORM_EPS) * g


def _tri(n, strict=False):
    r = lax.broadcasted_iota(jnp.int32, (n, n), 0)
    c = lax.broadcasted_iota(jnp.int32, (n, n), 1)
    return (r > c) if strict else (r >= c)


def _norm_proj_kernel(h_ref, g_ref, *refs):
    n = len(refs) // 2
    xn = _rmsnorm(h_ref[...], g_ref[...]).astype(BF16)
    for w_ref, o_ref in zip(refs[:n], refs[n:]):
        o_ref[...] = jnp.dot(xn, w_ref[...], preferred_element_type=F32)


def _norm_proj(h, g, ws, tm=512):
    m, d = h.shape
    return pl.pallas_call(
        _norm_proj_kernel,
        grid=(m // tm,),
        in_specs=[pl.BlockSpec((tm, d), lambda i: (i, 0)), pl.BlockSpec((1, d), lambda i: (0, 0))]
        + [pl.BlockSpec(w.shape, lambda i: (0, 0)) for w in ws],
        out_specs=[pl.BlockSpec((tm, w.shape[1]), lambda i: (i, 0)) for w in ws],
        out_shape=[jax.ShapeDtypeStruct((m, w.shape[1]), F32) for w in ws],
        compiler_params=_cparams("parallel"),
        name="norm_proj",
    )(h, g.reshape(1, d), *ws)


def _out_proj_kernel(h_ref, ya_ref, yb_ref, wa_ref, wb_ref, o_ref):
    o_ref[...] = h_ref[...] + _dot(ya_ref[...], wa_ref[...]) + _dot(yb_ref[...], wb_ref[...])


def _out_proj(h, ya, yb, wa, wb, tm=512):
    m, d = h.shape
    k = ya.shape[1]
    return pl.pallas_call(
        _out_proj_kernel,
        grid=(m // tm,),
        in_specs=[pl.BlockSpec((tm, d), lambda i: (i, 0)),
                  pl.BlockSpec((tm, k), lambda i: (i, 0)),
                  pl.BlockSpec((tm, k), lambda i: (i, 0)),
                  pl.BlockSpec((k, d), lambda i: (0, 0)),
                  pl.BlockSpec((k, d), lambda i: (0, 0))],
        out_specs=pl.BlockSpec((tm, d), lambda i: (i, 0)),
        out_shape=jax.ShapeDtypeStruct((m, d), F32),
        compiler_params=_cparams("parallel"),
        name="out_proj",
    )(h, ya, yb, wa, wb)


_CONV_HALO = 32


def _conv_kernel(val_ref, gate_ref, w_ref, b_ref, lg_ref, lb_ref, o_ref, u_ref, *, tt, rows):
    @pl.when(pl.program_id(1) == 0)
    def _():
        u_ref[0:_CONV_HALO, :] = jnp.zeros((_CONV_HALO, u_ref.shape[1]), F32)

    u_ref[_CONV_HALO:_CONV_HALO + tt, :] = val_ref[0] * _sigmoid(gate_ref[0])
    base = _CONV_HALO - (CONV_WIDTH - 1)
    for r0 in range(0, tt, rows):
        acc = jnp.zeros((rows, u_ref.shape[1]), F32) + b_ref[...]
        for j in range(CONV_WIDTH):
            acc = acc + u_ref[r0 + base + j:r0 + base + j + rows, :] * w_ref[j:j + 1, :]
        mu = jnp.mean(acc, axis=-1, keepdims=True)
        cen = acc - mu
        var = jnp.mean(cen * cen, axis=-1, keepdims=True)
        y = cen * lax.rsqrt(var + CONV_LN_EPS) * lg_ref[...] + lb_ref[...]
        o_ref[0, r0:r0 + rows, :] = _silu(y)
    u_ref[0:_CONV_HALO, :] = u_ref[tt:tt + _CONV_HALO, :]


def _conv(val, gate, conv_w, conv_b, ln_g, ln_b, tt=256, rows=32):
    b, t, c = val.shape
    wpad = jnp.zeros((32, c), F32).at[:CONV_WIDTH].set(conv_w)
    vec = lambda a: a.reshape(1, c)
    tile = pl.BlockSpec((1, tt, c), lambda i, j: (i, j, 0))
    full = lambda shape: pl.BlockSpec(shape, lambda i, j: (0, 0))
    return pl.pallas_call(
        functools.partial(_conv_kernel, tt=tt, rows=rows),
        grid=(b, t // tt),
        in_specs=[tile, tile, full((32, c)), full((1, c)), full((1, c)), full((1, c))],
        out_specs=tile,
        out_shape=jax.ShapeDtypeStruct((b, t, c), F32),
        scratch_shapes=[pltpu.VMEM((tt + _CONV_HALO, c), F32)],
        compiler_params=_cparams("parallel", "arbitrary"),
        name="conformer_conv",
    )(val, gate, wpad, vec(conv_b), vec(ln_g), vec(ln_b))


def _gla_kernel(q_ref, k_ref, v_ref, go_ref, alr_ref, wa2_ref, ba_ref, ng_ref, o_ref, s_ref, *, tt):
    c = GLA_CHUNK

    @pl.when(pl.program_id(1) == 0)
    def _():
        s_ref[...] = jnp.zeros(s_ref.shape, F32)

    tril = _tri(c)
    tril_f = tril.astype(F32)
    ones_cv = jnp.ones((c, GLA_DV), F32)
    for ci in range(tt // c):
        sl = slice(ci * c, (ci + 1) * c)
        z = _dot(alr_ref[0, sl, :], wa2_ref[...]) + ba_ref[...]
        log_a = _log_sigmoid(z) * (1.0 / GLA_TAU)
        bcum = _dot_hi(tril_f, log_a)
        b_last = bcum[c - 1:c, :]
        q = q_ref[0, sl, :] * (GLA_DK ** -0.5)
        k = k_ref[0, sl, :]
        q_dec = q * jnp.exp(bcum)
        k_dec = k * jnp.exp(-bcum)
        k_end = k * jnp.exp(b_last - bcum)
        outs = []
        for h in range(GLA_HEADS):
            ks = slice(h * GLA_DK, (h + 1) * GLA_DK)
            vs = slice(h * GLA_DV, (h + 1) * GLA_DV)
            v = v_ref[0, sl, vs]
            attn = jnp.where(tril, _dot_nt(q_dec[:, ks], k_dec[:, ks]), 0.0)
            state = s_ref[h]
            o = _dot(attn, v) + _dot(q_dec[:, ks], state)
            decay = jnp.exp(_dot_tn_hi(log_a[:, ks], ones_cv))
            s_ref[h] = state * decay + _dot_tn(k_end[:, ks], v)
            o = o * lax.rsqrt(jnp.mean(o * o, axis=-1, keepdims=True) + NORM_EPS)
            outs.append(o * ng_ref[:, vs] * _silu(go_ref[0, sl, vs]))
        o_ref[0, sl, :] = jnp.concatenate(outs, axis=-1)


def _gla(q, k, v, g_out, a_lr, w_a2, b_a, norm_g, tt=256):
    b, t, dq = q.shape
    dv = v.shape[-1]
    tile = lambda n: pl.BlockSpec((1, tt, n), lambda i, j: (i, j, 0))
    full = lambda shape: pl.BlockSpec(shape, lambda i, j: (0, 0))
    return pl.pallas_call(
        functools.partial(_gla_kernel, tt=tt),
        grid=(b, t // tt),
        in_specs=[tile(dq), tile(dq), tile(dv), tile(dv), tile(a_lr.shape[-1]),
                  full(w_a2.shape), full((1, dq)), full((1, dv))],
        out_specs=tile(dv),
        out_shape=jax.ShapeDtypeStruct((b, t, dv), F32),
        scratch_shapes=[pltpu.VMEM((GLA_HEADS, GLA_DK, GLA_DV), F32)],
        compiler_params=_cparams("parallel", "arbitrary"),
        name="gla",
    )(q, k, v, g_out, a_lr, w_a2, b_a.reshape(1, dq), norm_g.reshape(1, dv))


_RWKV_COLS = 3 * 512 + 64 + 64 + 128


def _rwkv_kernel(p_ref, mu_ref, w2_ref, w0_ref, a2_ref, a0_ref, g2_ref, kk_ref, ka_ref, rk_ref,
                 lg_ref, lb_ref, o_ref, xs_ref, st_ref, *, tt):
    c = RWKV_CHUNK
    hd = RWKV_HD
    w = RWKV_HEADS * hd

    @pl.when(pl.program_id(1) == 0)
    def _():
        xs_ref[0:8, :] = jnp.zeros((8, xs_ref.shape[1]), F32)
        st_ref[...] = jnp.zeros(st_ref.shape, F32)

    x = p_ref[0]
    xs_ref[8:8 + tt, :] = x
    prev = xs_ref[7:7 + tt, :]
    xs_ref[7:8, :] = x[tt - 1:tt, :]
    x = x + (prev - x) * mu_ref[...]

    r = x[:, 0:w]
    k = x[:, w:2 * w]
    v = x[:, 2 * w:3 * w]
    xw = x[:, 3 * w:3 * w + 64]
    xa = x[:, 3 * w + 64:3 * w + 128]
    xg = x[:, 3 * w + 128:3 * w + 256]
    log_w = -RWKV_DECAY_SCALE * _sigmoid(w0_ref[...] + _dot(jnp.tanh(xw), w2_ref[...]))
    a = _sigmoid(a0_ref[...] + _dot(xa, a2_ref[...]))
    g = _dot(_sigmoid(xg), g2_ref[...])
    kk = k * kk_ref[...]
    k_mod = k * (1.0 + (a - 1.0) * ka_ref[...])
    rkr = r * k_mod * rk_ref[...]

    tril = _tri(c)
    tril_s = _tri(c, strict=True)
    tril_f = tril.astype(F32)
    eye = (lax.broadcasted_iota(jnp.int32, (c, c), 0)
           == lax.broadcasted_iota(jnp.int32, (c, c), 1)).astype(F32)
    zeros_cc = jnp.zeros((c, hd), F32)

    nc = tt // c
    units = [(ci, h) for ci in range(nc) for h in range(RWKV_HEADS)]
    rows_of = lambda ci: slice(ci * c, (ci + 1) * c)
    lanes_of = lambda h: slice(h * hd, (h + 1) * hd)

    exps = []
    for ci in range(nc):
        lw_c = log_w[rows_of(ci)]
        lcum = _dot_hi(tril_f, lw_c)
        l_last = lcum[c - 1:c, :]
        exps.append((jnp.exp(lcum), jnp.exp(-lcum), jnp.exp(l_last - lcum), jnp.exp(lcum - lw_c),
                     jnp.exp(l_last)))

    aqs, rqs, vs, lhss, rhss, bkes = [], [], [], [], [], []
    for ci, h in units:
        rows, hs = rows_of(ci), lanes_of(h)
        e_pos, e_neg, e_end, e_prev, _ = exps[ci]
        kk_h = kk[rows, hs]
        nrm = jnp.sqrt(jnp.sum(kk_h * kk_h, axis=-1, keepdims=True))
        kk_h = kk_h / jnp.maximum(nrm, RWKV_L2_EPS)
        k_h = k_mod[rows, hs]
        kka = kk_h * a[rows, hs]
        aq = -kk_h * e_prev[:, hs]
        rq = r[rows, hs] * e_pos[:, hs]
        aqs.append(aq)
        rqs.append(rq)
        vs.append(v[rows, hs])
        lhss.append(jnp.concatenate([aq, rq], axis=0))
        rhss.append(jnp.concatenate([kka * e_neg[:, hs], k_h * e_neg[:, hs]], axis=0))
        bkes.append(jnp.concatenate([kka * e_end[:, hs], k_h * e_end[:, hs]], axis=0))

    xss = [_dot_nt(lhs, rhs) for lhs, rhs in zip(lhss, rhss)]
    m1s = [jnp.where(tril_s, xs[:c, :c], 0.0) for xs in xss]
    m2s = [jnp.where(tril_s, xs[:c, c:], 0.0) for xs in xss]
    n12s = [jnp.concatenate([jnp.where(tril, xs[c:, :c], 0.0), jnp.where(tril, xs[c:, c:], 0.0)], axis=1)
            for xs in xss]
    tinvs = [eye + m1 for m1 in m1s]
    mps = m1s
    for _ in range(5):
        mps = [_dot(mp, mp) for mp in mps]
        tinvs = [tinv + _dot(tinv, mp) for tinv, mp in zip(tinvs, mps)]
    mvs = [_dot(m2, v_h) for m2, v_h in zip(m2s, vs)]
    pqs = [_dot(tinv, jnp.concatenate([mv, aq], axis=1)) for tinv, mv, aq in zip(tinvs, mvs, aqs)]
    pqvs = [jnp.concatenate([pq, jnp.concatenate([v_h, zeros_cc], axis=1)], axis=0)
            for pq, v_h in zip(pqs, vs)]
    yys = [_dot(n12, pqv) for n12, pqv in zip(n12s, pqvs)]
    ghs = [_dot_tn(bke, pqv) for bke, pqv in zip(bkes, pqvs)]

    states = [st_ref[h] for h in range(RWKV_HEADS)]
    ys = []
    for u, (ci, h) in enumerate(units):
        yq = rqs[u] + yys[u][:, hd:]
        gmat = eye * exps[ci][4][:, lanes_of(h)] + ghs[u][:, hd:]
        res = _dot(jnp.concatenate([yq, gmat], axis=0), states[h])
        ys.append(res[:c] + yys[u][:, :hd])
        states[h] = res[c:] + ghs[u][:, :hd]
    for h in range(RWKV_HEADS):
        st_ref[h] = states[h]

    out_rows = []
    for ci in range(nc):
        yns, bonuses = [], []
        for h in range(RWKV_HEADS):
            u = ci * RWKV_HEADS + h
            y = ys[u]
            cen = y - jnp.mean(y, axis=-1, keepdims=True)
            var = jnp.mean(cen * cen, axis=-1, keepdims=True)
            yns.append(cen * lax.rsqrt(var + RWKV_GN_EPS))
            bonuses.append(jnp.sum(rkr[rows_of(ci), lanes_of(h)], axis=-1, keepdims=True) * vs[u])
        yn = jnp.concatenate(yns, axis=1)
        bonus = jnp.concatenate(bonuses, axis=1)
        out_rows.append((yn * lg_ref[...] + lb_ref[...] + bonus) * g[rows_of(ci)])
    o_ref[0] = jnp.concatenate(out_rows, axis=0)


def _rwkv(p, mu, w_w2, w0, w_a2, a0, w_g2, k_k, k_a, r_k, lnx_g, lnx_b, tt=128):
    b, t, cols = p.shape
    w = RWKV_HEADS * RWKV_HD
    vec = lambda a_: a_.reshape(1, -1)
    full = lambda shape: pl.BlockSpec(shape, lambda i, j: (0, 0))
    return pl.pallas_call(
        functools.partial(_rwkv_kernel, tt=tt),
        grid=(b, t // tt),
        in_specs=[pl.BlockSpec((1, tt, cols), lambda i, j: (i, j, 0)), full((1, cols)),
                  full(w_w2.shape), full((1, w)), full(w_a2.shape), full((1, w)), full(w_g2.shape),
                  full((1, w)), full((1, w)), full((1, w)), full((1, w)), full((1, w))],
        out_specs=pl.BlockSpec((1, tt, w), lambda i, j: (i, j, 0)),
        out_shape=jax.ShapeDtypeStruct((b, t, w), F32),
        scratch_shapes=[pltpu.VMEM((8 + tt, cols), F32),
                        pltpu.VMEM((RWKV_HEADS, RWKV_HD, RWKV_HD), F32)],
        compiler_params=_cparams("parallel", "arbitrary"),
        name="rwkv7",
    )(p, vec(mu), w_w2, vec(w0), w_a2, vec(a0), w_g2, vec(k_k), vec(k_a), vec(r_k), vec(lnx_g), vec(lnx_b))


def _fox_cum_kernel(f_ref, bf_ref, col_ref, carry_ref, *, tt):
    @pl.when(pl.program_id(1) == 0)
    def _():
        carry_ref[...] = jnp.zeros(carry_ref.shape, F32)

    log_f = _log_sigmoid(f_ref[0] + bf_ref[...])
    cum = _dot_hi(_tri(tt).astype(F32), log_f) + carry_ref[0:1, :]
    col_ref[0] = cum
    carry_ref[...] = jnp.broadcast_to(cum[tt - 1:tt, :], carry_ref.shape)


def _fox_cum(f_logit, b_f, tt=256):
    b, t, n = f_logit.shape
    return pl.pallas_call(
        functools.partial(_fox_cum_kernel, tt=tt),
        grid=(b, t // tt),
        in_specs=[pl.BlockSpec((1, tt, n), lambda i, j: (i, j, 0)), pl.BlockSpec((1, n), lambda i, j: (0, 0))],
        out_specs=pl.BlockSpec((1, tt, n), lambda i, j: (i, j, 0)),
        out_shape=jax.ShapeDtypeStruct((b, t, n), F32),
        scratch_shapes=[pltpu.VMEM((8, n), F32)],
        compiler_params=_cparams("parallel", "arbitrary"),
        name="fox_cumsum",
    )(f_logit, b_f)


def _split3(x):
    hi = x.astype(BF16).astype(F32)
    mid = (x - hi).astype(BF16).astype(F32)
    lo = (x - hi - mid).astype(BF16).astype(F32)
    return hi, mid, lo


def _fox_kernel(q_ref, k_ref, v_ref, c_ref, o_ref, kaug_ref, vbf_ref, *, tq):
    qi = pl.program_id(2)
    pair = pl.program_id(1)
    d = FOX_HD
    nh = LANE // d

    def augment(x_h, c, first):
        lane = lax.broadcasted_iota(jnp.int32, x_h.shape, 1)
        hi, mid, lo = _split3(c)
        parts = (hi, mid, lo, 1.0, 1.0, 1.0) if first else (1.0, 1.0, 1.0, -hi, -mid, -lo)
        out = jnp.where(lane < d, x_h, 0.0)
        for n, part in enumerate(parts):
            out = jnp.where(lane == d + n, part, out)
        return out.astype(BF16)

    def head_col(c_all, head):
        lane = lax.broadcasted_iota(jnp.int32, c_all.shape, 1)
        return jnp.sum(jnp.where(lane == head, c_all, 0.0), axis=-1, keepdims=True)

    @pl.when(qi == 0)
    def _():
        k = k_ref[0]
        vbf_ref[...] = v_ref[0].astype(BF16)
        for hh in range(nh):
            k_h = k if hh == 0 else pltpu.roll(k, LANE - hh * d, axis=1)
            kaug_ref[hh] = augment(k_h, head_col(c_ref[0], pair * nh + hh), first=False)

    row0 = pl.multiple_of(qi * tq, tq)
    q = q_ref[0] * (d ** -0.5)
    c_q = c_ref[0, pl.ds(row0, tq), :]
    qs = []
    for hh in range(nh):
        q_h = q if hh == 0 else pltpu.roll(q, LANE - hh * d, axis=1)
        qs.append(augment(q_h, head_col(c_q, pair * nh + hh), first=True))
    causal = _tri(tq)

    def step(j, carry, diagonal):
        start = pl.multiple_of(j * tq, tq)
        vb = vbf_ref[pl.ds(start, tq), :]
        new = []
        for hh in range(nh):
            m, l, acc = carry[hh]
            s = lax.dot_general(qs[hh], kaug_ref[hh, pl.ds(start, tq), :], (((1,), (1,)), ((), ())),
                                preferred_element_type=F32)
            if diagonal:
                s = jnp.where(causal, s, -jnp.inf)
            m_new = jnp.maximum(m, jnp.max(s, axis=-1, keepdims=True))
            scale = jnp.exp(m - m_new)
            p = jnp.exp(s - m_new)
            l = l * scale + jnp.sum(p, axis=-1, keepdims=True)
            acc = acc * scale + jnp.dot(p.astype(BF16), vb, preferred_element_type=F32)
            new.append((m_new, l, acc))
        return tuple(new)

    init = tuple((jnp.full((tq, 1), -jnp.inf, F32), jnp.zeros((tq, 1), F32), jnp.zeros((tq, LANE), F32))
                 for _ in range(nh))
    carry = lax.fori_loop(0, qi, lambda j, cr: step(j, cr, False), init)
    carry = step(qi, carry, True)
    lane = lax.broadcasted_iota(jnp.int32, (tq, LANE), 1)
    out = carry[0][2] / carry[0][1]
    for hh in range(1, nh):
        out = jnp.where(lane >= hh * d, carry[hh][2] / carry[hh][1], out)
    o_ref[0] = out


def _fox(q, k, v, c_col, tq=256):
    b, t, w = q.shape
    npairs = w // LANE
    return pl.pallas_call(
        functools.partial(_fox_kernel, tq=tq),
        grid=(b, npairs, t // tq),
        in_specs=[pl.BlockSpec((1, tq, LANE), lambda i, p, j: (i, j, p)),
                  pl.BlockSpec((1, t, LANE), lambda i, p, j: (i, 0, p)),
                  pl.BlockSpec((1, t, LANE), lambda i, p, j: (i, 0, p)),
                  pl.BlockSpec((1, t, LANE), lambda i, p, j: (i, 0, 0))],
        out_specs=pl.BlockSpec((1, tq, LANE), lambda i, p, j: (i, j, p)),
        out_shape=jax.ShapeDtypeStruct((b, t, w), F32),
        scratch_shapes=[pltpu.VMEM((LANE // FOX_HD, t, LANE), BF16), pltpu.VMEM((t, LANE), BF16)],
        compiler_params=_cparams("parallel", "arbitrary", "arbitrary"),
        name="fox_attention",
    )(q, k, v, c_col)


def _ffn_kernel(h_ref, g_ref, w1_ref, w3_ref, w2_ref, o_ref, xn_ref, acc_ref):
    f = pl.program_id(1)

    @pl.when(f == 0)
    def _():
        xn_ref[...] = _rmsnorm(h_ref[...], g_ref[...]).astype(BF16)
        acc_ref[...] = jnp.zeros(acc_ref.shape, F32)

    xn = xn_ref[...]
    mid = _silu(jnp.dot(xn, w1_ref[...], preferred_element_type=F32)) * jnp.dot(
        xn, w3_ref[...], preferred_element_type=F32)
    acc_ref[...] += _dot(mid, w2_ref[...])

    @pl.when(f == pl.num_programs(1) - 1)
    def _():
        o_ref[...] = h_ref[...] + acc_ref[...]


def _ffn(h, g, w1, w3, w2, tm=1024, tf=256):
    m, d = h.shape
    nf = w1.shape[1]
    return pl.pallas_call(
        _ffn_kernel,
        grid=(m // tm, nf // tf),
        in_specs=[pl.BlockSpec((tm, d), lambda i, f: (i, 0)),
                  pl.BlockSpec((1, d), lambda i, f: (0, 0)),
                  pl.BlockSpec((d, tf), lambda i, f: (0, f)),
                  pl.BlockSpec((d, tf), lambda i, f: (0, f)),
                  pl.BlockSpec((tf, d), lambda i, f: (f, 0))],
        out_specs=pl.BlockSpec((tm, d), lambda i, f: (i, 0)),
        out_shape=jax.ShapeDtypeStruct((m, d), F32),
        scratch_shapes=[pltpu.VMEM((tm, d), BF16), pltpu.VMEM((tm, d), F32)],
        compiler_params=_cparams("parallel", "arbitrary"),
        name="ffn_dense",
    )(h, g.reshape(1, d), w1, w3, w2)


def _moe_route_kernel(h_ref, g_ref, r_ref, xn_ref, meta_ref, cnt_ref, carry_ref):
    @pl.when(pl.program_id(0) == 0)
    def _():
        carry_ref[...] = jnp.zeros(carry_ref.shape, F32)

    xn = _rmsnorm(h_ref[...], g_ref[...])
    xn_ref[...] = xn
    tm = xn.shape[0]
    lane = lax.broadcasted_iota(jnp.int32, (tm, LANE), 1)
    logits = jnp.where(lane < N_EXPERTS, _dot_hi(xn, r_ref[...]), -jnp.inf)
    m1 = jnp.max(logits, axis=-1, keepdims=True)
    i1 = jnp.min(jnp.where(logits == m1, lane, LANE), axis=-1, keepdims=True)
    rest = jnp.where(lane == i1, -jnp.inf, logits)
    m2 = jnp.max(rest, axis=-1, keepdims=True)
    i2 = jnp.min(jnp.where(rest == m2, lane, LANE), axis=-1, keepdims=True)
    e2 = jnp.exp(m2 - m1)
    g1 = 1.0 / (1.0 + e2)
    g2 = e2 / (1.0 + e2)
    onehot = jnp.where(lane == i1, 1.0, jnp.where(lane == i2, 1.0, 0.0))
    before = _dot(_tri(tm, strict=True).astype(F32), onehot) + carry_ref[0:1, :]
    r1 = jnp.sum(jnp.where(lane == i1, before, 0.0), axis=-1, keepdims=True)
    r2 = jnp.sum(jnp.where(lane == i2, before, 0.0), axis=-1, keepdims=True)
    meta = jnp.where(lane == 0, i1.astype(F32), jnp.where(lane == 1, i2.astype(F32), 0.0))
    meta = jnp.where(lane == 2, g1, jnp.where(lane == 3, g2, meta))
    meta_ref[...] = jnp.where(lane == 4, r1, jnp.where(lane == 5, r2, meta))
    total = carry_ref[0:1, :] + jnp.sum(onehot, axis=0, keepdims=True)
    carry_ref[...] = jnp.broadcast_to(total, carry_ref.shape)
    cnt_ref[...] = jnp.broadcast_to(total, cnt_ref.shape)


def _moe_route(h, g, router, tm=512):
    m, d = h.shape
    return pl.pallas_call(
        _moe_route_kernel,
        grid=(m // tm,),
        in_specs=[pl.BlockSpec((tm, d), lambda i: (i, 0)), pl.BlockSpec((1, d), lambda i: (0, 0)),
                  pl.BlockSpec((d, LANE), lambda i: (0, 0))],
        out_specs=[pl.BlockSpec((tm, d), lambda i: (i, 0)), pl.BlockSpec((tm, LANE), lambda i: (i, 0)),
                   pl.BlockSpec((8, LANE), lambda i: (0, 0))],
        out_shape=[jax.ShapeDtypeStruct((m, d), F32), jax.ShapeDtypeStruct((m, LANE), F32),
                   jax.ShapeDtypeStruct((8, LANE), F32)],
        scratch_shapes=[pltpu.VMEM((8, LANE), F32)],
        compiler_params=_cparams("arbitrary"),
        name="moe_route",
    )(h, g.reshape(1, d), router)


def _row_copy(src_ref, src_row, dst_ref, dst_row, sem):
    return pltpu.make_async_copy(src_ref.at[pl.ds(src_row, 1)], dst_ref.at[pl.ds(dst_row, 1)], sem)


def _moe_dispatch_kernel(pos_ref, xn_ref, xs_in_ref, xs_ref, sem, *, tm):
    del xs_in_ref
    base = pl.program_id(0) * tm

    def start(r, carry):
        for s in range(2):
            _row_copy(xn_ref, base + r, xs_ref, pos_ref[0, 0, 2 * r + s], sem).start()
        return carry

    def wait(r, carry):
        for s in range(2):
            _row_copy(xn_ref, 0, xs_ref, 0, sem).wait()
        return carry

    lax.fori_loop(0, tm, start, 0)
    lax.fori_loop(0, tm, wait, 0)


def _moe_dispatch(xn, pos, n_rows, tm=256):
    m, d = xn.shape
    return pl.pallas_call(
        functools.partial(_moe_dispatch_kernel, tm=tm),
        grid=(m // tm,),
        in_specs=[pl.BlockSpec((1, 1, 2 * tm), lambda i: (i, 0, 0), memory_space=pltpu.SMEM),
                  pl.BlockSpec(memory_space=pl.ANY), pl.BlockSpec(memory_space=pl.ANY)],
        out_specs=pl.BlockSpec(memory_space=pl.ANY),
        out_shape=jax.ShapeDtypeStruct((n_rows, d), F32),
        scratch_shapes=[pltpu.SemaphoreType.DMA(())],
        input_output_aliases={2: 0},
        compiler_params=_cparams("arbitrary"),
        name="moe_dispatch",
    )(pos.reshape(m // tm, 1, 2 * tm), xn, jnp.zeros((n_rows, d), F32))


def _moe_expert_kernel(te_ref, nu_ref, x_ref, w1_ref, w3_ref, w2_ref, y_ref, xbf_ref, acc_ref):
    del te_ref
    f = pl.program_id(1)

    @pl.when(f == 0)
    def _():
        xbf_ref[...] = x_ref[...].astype(BF16)
        acc_ref[...] = jnp.zeros(acc_ref.shape, F32)

    @pl.when(pl.program_id(0) < nu_ref[0])
    def _():
        xb = xbf_ref[...]
        mid = _silu(jnp.dot(xb, w1_ref[0], preferred_element_type=F32)) * jnp.dot(
            xb, w3_ref[0], preferred_element_type=F32)
        acc_ref[...] += _dot(mid, w2_ref[0])

    @pl.when(f == pl.num_programs(1) - 1)
    def _():
        y_ref[...] = acc_ref[...]


def _moe_experts(xs, tile_expert, n_used, w1, w3, w2, tmx, tf=512):
    n_rows, d = xs.shape
    nf = w1.shape[2] // tf
    fidx = lambda i, f, nu: jnp.where(i < nu[0], f, nf - 1)
    grid_spec = pltpu.PrefetchScalarGridSpec(
        num_scalar_prefetch=2,
        grid=(n_rows // tmx, nf),
        in_specs=[pl.BlockSpec((tmx, d), lambda i, f, te, nu: (i, 0)),
                  pl.BlockSpec((1, d, tf), lambda i, f, te, nu: (te[i], 0, fidx(i, f, nu))),
                  pl.BlockSpec((1, d, tf), lambda i, f, te, nu: (te[i], 0, fidx(i, f, nu))),
                  pl.BlockSpec((1, tf, d), lambda i, f, te, nu: (te[i], fidx(i, f, nu), 0))],
        out_specs=pl.BlockSpec((tmx, d), lambda i, f, te, nu: (i, 0)),
        scratch_shapes=[pltpu.VMEM((tmx, d), BF16), pltpu.VMEM((tmx, d), F32)],
    )
    return pl.pallas_call(
        _moe_expert_kernel,
        grid_spec=grid_spec,
        out_shape=jax.ShapeDtypeStruct((n_rows, d), F32),
        compiler_params=_cparams("arbitrary", "arbitrary"),
        name="moe_experts",
    )(tile_expert, n_used, xs, w1, w3, w2)


def _moe_combine_kernel(pos_ref, posn_ref, h_ref, meta_ref, fg_ref, ys_ref, o_ref, ybuf_ref, sem, *, tm):
    i = pl.program_id(0)
    slot = lax.rem(i, 2)

    def gather(p_ref, dst_slot):
        def start(r, carry):
            for s in range(2):
                _row_copy(ys_ref, p_ref[0, 0, 2 * r + s], ybuf_ref.at[dst_slot], s * tm + r,
                          sem.at[dst_slot]).start()
            return carry
        lax.fori_loop(0, tm, start, 0)

    @pl.when(i == 0)
    def _():
        gather(pos_ref, 0)

    @pl.when(i + 1 < pl.num_programs(0))
    def _():
        gather(posn_ref, 1 - slot)

    def wait(r, carry):
        for s in range(2):
            _row_copy(ys_ref, 0, ybuf_ref.at[slot], 0, sem.at[slot]).wait()
        return carry

    lax.fori_loop(0, tm, wait, 0)
    lane = lax.broadcasted_iota(jnp.int32, (tm, LANE), 1)
    meta = meta_ref[...]
    g1 = jnp.sum(jnp.where(lane == 2, meta, 0.0), axis=-1, keepdims=True)
    g2 = jnp.sum(jnp.where(lane == 3, meta, 0.0), axis=-1, keepdims=True)
    y = ybuf_ref[slot]
    o_ref[...] = _rmsnorm(h_ref[...] + g1 * y[:tm] + g2 * y[tm:], fg_ref[...])


def _moe_combine(h, meta, pos, ys, final_g, tm=256):
    m, d = h.shape
    nt = m // tm
    pos3 = pos.reshape(nt, 1, 2 * tm)
    smem = lambda imap: pl.BlockSpec((1, 1, 2 * tm), imap, memory_space=pltpu.SMEM)
    return pl.pallas_call(
        functools.partial(_moe_combine_kernel, tm=tm),
        grid=(nt,),
        in_specs=[smem(lambda i: (i, 0, 0)), smem(lambda i: (jnp.minimum(i + 1, nt - 1), 0, 0)),
                  pl.BlockSpec((tm, d), lambda i: (i, 0)), pl.BlockSpec((tm, LANE), lambda i: (i, 0)),
                  pl.BlockSpec((1, d), lambda i: (0, 0)), pl.BlockSpec(memory_space=pl.ANY)],
        out_specs=pl.BlockSpec((tm, d), lambda i: (i, 0)),
        out_shape=jax.ShapeDtypeStruct((m, d), F32),
        scratch_shapes=[pltpu.VMEM((2, 2 * tm, d), F32), pltpu.SemaphoreType.DMA((2,))],
        compiler_params=_cparams("arbitrary"),
        name="moe_combine",
    )(pos3, pos3, h, meta, final_g.reshape(1, d), ys)


def _moe_routed(h, g, router, w1, w3, w2, final_g, tmx=512):
    m, d = h.shape
    xn, meta, cnt = _moe_route(h, g, router)
    experts = meta[:, 0:2].astype(jnp.int32)
    rank = meta[:, 4:6].astype(jnp.int32)
    counts = cnt[0, :N_EXPERTS].astype(jnp.int32)
    padded = (counts + tmx - 1) // tmx * tmx
    ends = jnp.cumsum(padded)
    pos = (ends - padded)[experts] + rank
    n_tiles = (2 * m) // tmx + N_EXPERTS
    n_used = ends[-1] // tmx
    tile_start = jnp.minimum(jnp.arange(n_tiles, dtype=jnp.int32), n_used - 1) * tmx
    tile_expert = jnp.sum(tile_start[:, None] >= ends[None, :], axis=1).astype(jnp.int32)
    xs = _moe_dispatch(xn, pos, n_tiles * tmx)
    ys = _moe_experts(xs, tile_expert, n_used.reshape(1).astype(jnp.int32), w1, w3, w2, tmx)
    return _moe_combine(h, meta, pos, ys, final_g)


def _pad_cols(w, n):
    return jnp.pad(w, ((0, 0), (0, n - w.shape[1])))


def kernel(x, norm_mix_g, norm_ffn_g, w_in0, conv_w, conv_b, conv_ln_g, conv_ln_b, gla_w_a2, gla_b_a, gla_norm_g, w_out0, ffn_w1, ffn_w3, ffn_w2, w_in1, rwkv_mu, rwkv_w2, rwkv_w0, rwkv_a2, rwkv_a0, rwkv_g2, rwkv_k_k, rwkv_k_a, rwkv_r_k, rwkv_lnx_g, rwkv_lnx_b, fox_b_f, w_out1, moe_router, moe_w1, moe_w3, moe_w2, final_norm_g):
    b, t, d = x.shape
    m = b * t
    half = d // 2
    bf = lambda a: a.astype(BF16)
    h = x.reshape(m, d)

    gk = GLA_HEADS * GLA_DK
    o = [0, half, 2 * half, 2 * half + gk, 2 * half + 2 * gk, 3 * half + 2 * gk, 4 * half + 2 * gk]
    ws0 = [bf(w_in0[:, o[i]:o[i + 1]]) for i in range(6)] + [bf(_pad_cols(w_in0[:, o[6]:], LANE))]
    c_val, c_gate, q, k, v, g_out, a_lr = [
        a.reshape(b, t, -1) for a in _norm_proj(h, norm_mix_g[0], ws0)]
    y_conv = _conv(c_val, c_gate, conv_w, conv_b, conv_ln_g, conv_ln_b)
    w_a2 = jnp.pad(gla_w_a2, ((0, LANE - gla_w_a2.shape[0]), (0, 0)))
    y_gla = _gla(q, k, v, g_out, a_lr, w_a2, gla_b_a, gla_norm_g)
    h = _out_proj(h, y_conv.reshape(m, half), y_gla.reshape(m, half), bf(w_out0[:half]), bf(w_out0[half:]))
    h = _ffn(h, norm_ffn_g[0], bf(ffn_w1), bf(ffn_w3), bf(ffn_w2))

    rc = _RWKV_COLS
    ws1 = [bf(w_in1[:, :rc]), bf(w_in1[:, rc:rc + half]), bf(w_in1[:, rc + half:rc + 2 * half]),
           bf(w_in1[:, rc + 2 * half:rc + 3 * half]), bf(_pad_cols(w_in1[:, rc + 3 * half:], LANE))]
    p_rwkv, fq, fk, fv, f_logit = [a.reshape(b, t, -1) for a in _norm_proj(h, norm_mix_g[1], ws1)]
    y_rwkv = _rwkv(p_rwkv, rwkv_mu, rwkv_w2, rwkv_w0, rwkv_a2, rwkv_a0, rwkv_g2, rwkv_k_k, rwkv_k_a,
                   rwkv_r_k, rwkv_lnx_g, rwkv_lnx_b)
    b_f = jnp.pad(fox_b_f, (0, LANE - FOX_HEADS)).reshape(1, LANE)
    y_fox = _fox(fq, fk, fv, _fox_cum(f_logit, b_f))
    h = _out_proj(h, y_rwkv.reshape(m, half), y_fox.reshape(m, half), bf(w_out1[:half]), bf(w_out1[half:]))
    out = _moe_routed(h, norm_ffn_g[1], _pad_cols(moe_router, LANE), bf(moe_w1), bf(moe_w3), bf(moe_w2),
                      final_norm_g)
    return out.reshape(b, t, d)
```

```python
import functools

import jax
import jax.numpy as jnp
from jax import lax
from jax.experimental import pallas as pl
from jax.experimental.pallas import tpu as pltpu

F32 = jnp.float32
BF16 = jnp.bfloat16
HI = lax.Precision.HIGHEST

NORM_EPS = 1e-6
CONV_WIDTH = 31
CONV_LN_EPS = 1e-5
GLA_HEADS = 4
GLA_DK = 64
GLA_DV = 128
GLA_TAU = 16.0
GLA_CHUNK = 64
RWKV_HEADS = 8
RWKV_HD = 64
RWKV_CHUNK = 64
RWKV_DECAY_SCALE = 0.606531
RWKV_GN_EPS = 64e-5
RWKV_L2_EPS = 1e-12
FOX_HEADS = 8
FOX_HD = 64
N_EXPERTS = 8
LANE = 128
VMEM_LIMIT = 56 * 1024 * 1024


def _cparams(*sem):
    return pltpu.CompilerParams(dimension_semantics=sem, vmem_limit_bytes=VMEM_LIMIT)


def _dot(a, b):
    return jnp.dot(a.astype(BF16), b.astype(BF16), preferred_element_type=F32)


def _dot_nt(a, b):
    return lax.dot_general(a.astype(BF16), b.astype(BF16), (((1,), (1,)), ((), ())),
                           preferred_element_type=F32)


def _dot_tn(a, b):
    return lax.dot_general(a.astype(BF16), b.astype(BF16), (((0,), (0,)), ((), ())),
                           preferred_element_type=F32)


def _dot_hi(a, b):
    return jnp.dot(a, b, precision=HI, preferred_element_type=F32)


def _dot_tn_hi(a, b):
    return lax.dot_general(a, b, (((0,), (0,)), ((), ())), precision=HI, preferred_element_type=F32)


def _sigmoid(x):
    return 1.0 / (1.0 + jnp.exp(-x))


def _silu(x):
    return x * _sigmoid(x)


def _log_sigmoid(x):
    return jnp.minimum(x, 0.0) - jnp.log(1.0 + jnp.exp(-jnp.abs(x)))


def _rmsnorm(x, g):
    return x * lax.rsqrt(jnp.mean(x * x, axis=-1, keepdims=True) + NORM_EPS) * g


def _tri(n, strict=False):
    r = lax.broadcasted_iota(jnp.int32, (n, n), 0)
    c = lax.broadcasted_iota(jnp.int32, (n, n), 1)
    return (r > c) if strict else (r >= c)


def _norm_proj_kernel(h_ref, g_ref, *refs):
    n = len(refs) // 2
    xn = _rmsnorm(h_ref[...], g_ref[...]).astype(BF16)
    for w_ref, o_ref in zip(refs[:n], refs[n:]):
        o_ref[...] = jnp.dot(xn, w_ref[...], preferred_element_type=F32)


def _norm_proj(h, g, ws, tm=512):
    m, d = h.shape
    return pl.pallas_call(
        _norm_proj_kernel,
        grid=(m // tm,),
        in_specs=[pl.BlockSpec((tm, d), lambda i: (i, 0)), pl.BlockSpec((1, d), lambda i: (0, 0))]
        + [pl.BlockSpec(w.shape, lambda i: (0, 0)) for w in ws],
        out_specs=[pl.BlockSpec((tm, w.shape[1]), lambda i: (i, 0)) for w in ws],
        out_shape=[jax.ShapeDtypeStruct((m, w.shape[1]), F32) for w in ws],
        compiler_params=_cparams("parallel"),
        name="norm_proj",
    )(h, g.reshape(1, d), *ws)


def _out_proj_kernel(h_ref, ya_ref, yb_ref, wa_ref, wb_ref, o_ref):
    o_ref[...] = h_ref[...] + _dot(ya_ref[...], wa_ref[...]) + _dot(yb_ref[...], wb_ref[...])


def _out_proj(h, ya, yb, wa, wb, tm=512):
    m, d = h.shape
    k = ya.shape[1]
    return pl.pallas_call(
        _out_proj_kernel,
        grid=(m // tm,),
        in_specs=[pl.BlockSpec((tm, d), lambda i: (i, 0)),
                  pl.BlockSpec((tm, k), lambda i: (i, 0)),
                  pl.BlockSpec((tm, k), lambda i: (i, 0)),
                  pl.BlockSpec((k, d), lambda i: (0, 0)),
                  pl.BlockSpec((k, d), lambda i: (0, 0))],
        out_specs=pl.BlockSpec((tm, d), lambda i: (i, 0)),
        out_shape=jax.ShapeDtypeStruct((m, d), F32),
        compiler_params=_cparams("parallel"),
        name="out_proj",
    )(h, ya, yb, wa, wb)


_CONV_HALO = 32


def _conv_kernel(val_ref, gate_ref, w_ref, b_ref, lg_ref, lb_ref, o_ref, u_ref, *, tt, rows):
    @pl.when(pl.program_id(1) == 0)
    def _():
        u_ref[0:_CONV_HALO, :] = jnp.zeros((_CONV_HALO, u_ref.shape[1]), F32)

    u_ref[_CONV_HALO:_CONV_HALO + tt, :] = val_ref[0] * _sigmoid(gate_ref[0])
    base = _CONV_HALO - (CONV_WIDTH - 1)
    for r0 in range(0, tt, rows):
        acc = jnp.zeros((rows, u_ref.shape[1]), F32) + b_ref[...]
        for j in range(CONV_WIDTH):
            acc = acc + u_ref[r0 + base + j:r0 + base + j + rows, :] * w_ref[j:j + 1, :]
        mu = jnp.mean(acc, axis=-1, keepdims=True)
        cen = acc - mu
        var = jnp.mean(cen * cen, axis=-1, keepdims=True)
        y = cen * lax.rsqrt(var + CONV_LN_EPS) * lg_ref[...] + lb_ref[...]
        o_ref[0, r0:r0 + rows, :] = _silu(y)
    u_ref[0:_CONV_HALO, :] = u_ref[tt:tt + _CONV_HALO, :]


def _conv(val, gate, conv_w, conv_b, ln_g, ln_b, tt=256, rows=32):
    b, t, c = val.shape
    wpad = jnp.zeros((32, c), F32).at[:CONV_WIDTH].set(conv_w)
    vec = lambda a: a.reshape(1, c)
    tile = pl.BlockSpec((1, tt, c), lambda i, j: (i, j, 0))
    full = lambda shape: pl.BlockSpec(shape, lambda i, j: (0, 0))
    return pl.pallas_call(
        functools.partial(_conv_kernel, tt=tt, rows=rows),
        grid=(b, t // tt),
        in_specs=[tile, tile, full((32, c)), full((1, c)), full((1, c)), full((1, c))],
        out_specs=tile,
        out_shape=jax.ShapeDtypeStruct((b, t, c), F32),
        scratch_shapes=[pltpu.VMEM((tt + _CONV_HALO, c), F32)],
        compiler_params=_cparams("parallel", "arbitrary"),
        name="conformer_conv",
    )(val, gate, wpad, vec(conv_b), vec(ln_g), vec(ln_b))


def _gla_kernel(q_ref, k_ref, v_ref, go_ref, alr_ref, wa2_ref, ba_ref, ng_ref, o_ref, s_ref, *, tt):
    c = GLA_CHUNK

    @pl.when(pl.program_id(1) == 0)
    def _():
        s_ref[...] = jnp.zeros(s_ref.shape, F32)

    tril = _tri(c)
    tril_f = tril.astype(F32)
    ones_cv = jnp.ones((c, GLA_DV), F32)
    for ci in range(tt // c):
        sl = slice(ci * c, (ci + 1) * c)
        z = _dot(alr_ref[0, sl, :], wa2_ref[...]) + ba_ref[...]
        log_a = _log_sigmoid(z) * (1.0 / GLA_TAU)
        bcum = _dot_hi(tril_f, log_a)
        b_last = bcum[c - 1:c, :]
        q = q_ref[0, sl, :] * (GLA_DK ** -0.5)
        k = k_ref[0, sl, :]
        q_dec = q * jnp.exp(bcum)
        k_dec = k * jnp.exp(-bcum)
        k_end = k * jnp.exp(b_last - bcum)
        outs = []
        for h in range(GLA_HEADS):
            ks = slice(h * GLA_DK, (h + 1) * GLA_DK)
            vs = slice(h * GLA_DV, (h + 1) * GLA_DV)
            v = v_ref[0, sl, vs]
            attn = jnp.where(tril, _dot_nt(q_dec[:, ks], k_dec[:, ks]), 0.0)
            state = s_ref[h]
            o = _dot(attn, v) + _dot(q_dec[:, ks], state)
            decay = jnp.exp(_dot_tn_hi(log_a[:, ks], ones_cv))
            s_ref[h] = state * decay + _dot_tn(k_end[:, ks], v)
            o = o * lax.rsqrt(jnp.mean(o * o, axis=-1, keepdims=True) + NORM_EPS)
            outs.append(o * ng_ref[:, vs] * _silu(go_ref[0, sl, vs]))
        o_ref[0, sl, :] = jnp.concatenate(outs, axis=-1)


def _gla(q, k, v, g_out, a_lr, w_a2, b_a, norm_g, tt=256):
    b, t, dq = q.shape
    dv = v.shape[-1]
    tile = lambda n: pl.BlockSpec((1, tt, n), lambda i, j: (i, j, 0))
    full = lambda shape: pl.BlockSpec(shape, lambda i, j: (0, 0))
    return pl.pallas_call(
        functools.partial(_gla_kernel, tt=tt),
        grid=(b, t // tt),
        in_specs=[tile(dq), tile(dq), tile(dv), tile(dv), tile(a_lr.shape[-1]),
                  full(w_a2.shape), full((1, dq)), full((1, dv))],
        out_specs=tile(dv),
        out_shape=jax.ShapeDtypeStruct((b, t, dv), F32),
        scratch_shapes=[pltpu.VMEM((GLA_HEADS, GLA_DK, GLA_DV), F32)],
        compiler_params=_cparams("parallel", "arbitrary"),
        name="gla",
    )(q, k, v, g_out, a_lr, w_a2, b_a.reshape(1, dq), norm_g.reshape(1, dv))


_RWKV_COLS = 3 * 512 + 64 + 64 + 128


def _rwkv_kernel(p_ref, mu_ref, w2_ref, w0_ref, a2_ref, a0_ref, g2_ref, kk_ref, ka_ref, rk_ref,
                 lg_ref, lb_ref, o_ref, xs_ref, st_ref, *, tt):
    c = RWKV_CHUNK
    hd = RWKV_HD
    w = RWKV_HEADS * hd

    @pl.when(pl.program_id(1) == 0)
    def _():
        xs_ref[0:8, :] = jnp.zeros((8, xs_ref.shape[1]), F32)
        st_ref[...] = jnp.zeros(st_ref.shape, F32)

    x = p_ref[0]
    xs_ref[8:8 + tt, :] = x
    prev = xs_ref[7:7 + tt, :]
    xs_ref[7:8, :] = x[tt - 1:tt, :]
    x = x + (prev - x) * mu_ref[...]

    r = x[:, 0:w]
    k = x[:, w:2 * w]
    v = x[:, 2 * w:3 * w]
    xw = x[:, 3 * w:3 * w + 64]
    xa = x[:, 3 * w + 64:3 * w + 128]
    xg = x[:, 3 * w + 128:3 * w + 256]
    log_w = -RWKV_DECAY_SCALE * _sigmoid(w0_ref[...] + _dot(jnp.tanh(xw), w2_ref[...]))
    a = _sigmoid(a0_ref[...] + _dot(xa, a2_ref[...]))
    g = _dot(_sigmoid(xg), g2_ref[...])
    kk = k * kk_ref[...]
    k_mod = k * (1.0 + (a - 1.0) * ka_ref[...])
    rkr = r * k_mod * rk_ref[...]

    tril = _tri(c)
    tril_s = _tri(c, strict=True)
    tril_f = tril.astype(F32)
    eye = (lax.broadcasted_iota(jnp.int32, (c, c), 0)
           == lax.broadcasted_iota(jnp.int32, (c, c), 1)).astype(F32)
    zeros_cc = jnp.zeros((c, hd), F32)

    nc = tt // c
    units = [(ci, h) for ci in range(nc) for h in range(RWKV_HEADS)]
    rows_of = lambda ci: slice(ci * c, (ci + 1) * c)
    lanes_of = lambda h: slice(h * hd, (h + 1) * hd)

    exps = []
    for ci in range(nc):
        lw_c = log_w[rows_of(ci)]
        lcum = _dot_hi(tril_f, lw_c)
        l_last = lcum[c - 1:c, :]
        exps.append((jnp.exp(lcum), jnp.exp(-lcum), jnp.exp(l_last - lcum), jnp.exp(lcum - lw_c),
                     jnp.exp(l_last)))

    aqs, rqs, vs, lhss, rhss, bkes = [], [], [], [], [], []
    for ci, h in units:
        rows, hs = rows_of(ci), lanes_of(h)
        e_pos, e_neg, e_end, e_prev, _ = exps[ci]
        kk_h = kk[rows, hs]
        nrm = jnp.sqrt(jnp.sum(kk_h * kk_h, axis=-1, keepdims=True))
        kk_h = kk_h / jnp.maximum(nrm, RWKV_L2_EPS)
        k_h = k_mod[rows, hs]
        kka = kk_h * a[rows, hs]
        aq = -kk_h * e_prev[:, hs]
        rq = r[rows, hs] * e_pos[:, hs]
        aqs.append(aq)
        rqs.append(rq)
        vs.append(v[rows, hs])
        lhss.append(jnp.concatenate([aq, rq], axis=0))
        rhss.append(jnp.concatenate([kka * e_neg[:, hs], k_h * e_neg[:, hs]], axis=0))
        bkes.append(jnp.concatenate([kka * e_end[:, hs], k_h * e_end[:, hs]], axis=0))

    xss = [_dot_nt(lhs, rhs) for lhs, rhs in zip(lhss, rhss)]
    m1s = [jnp.where(tril_s, xs[:c, :c], 0.0) for xs in xss]
    m2s = [jnp.where(tril_s, xs[:c, c:], 0.0) for xs in xss]
    n12s = [jnp.concatenate([jnp.where(tril, xs[c:, :c], 0.0), jnp.where(tril, xs[c:, c:], 0.0)], axis=1)
            for xs in xss]
    tinvs = [eye + m1 for m1 in m1s]
    mps = m1s
    for _ in range(5):
        mps = [_dot(mp, mp) for mp in mps]
        tinvs = [tinv + _dot(tinv, mp) for tinv, mp in zip(tinvs, mps)]
    mvs = [_dot(m2, v_h) for m2, v_h in zip(m2s, vs)]
    pqs = [_dot(tinv, jnp.concatenate([mv, aq], axis=1)) for tinv, mv, aq in zip(tinvs, mvs, aqs)]
    pqvs = [jnp.concatenate([pq, jnp.concatenate([v_h, zeros_cc], axis=1)], axis=0)
            for pq, v_h in zip(pqs, vs)]
    yys = [_dot(n12, pqv) for n12, pqv in zip(n12s, pqvs)]
    ghs = [_dot_tn(bke, pqv) for bke, pqv in zip(bkes, pqvs)]

    states = [st_ref[h] for h in range(RWKV_HEADS)]
    ys = []
    for u, (ci, h) in enumerate(units):
        yq = rqs[u] + yys[u][:, hd:]
        gmat = eye * exps[ci][4][:, lanes_of(h)] + ghs[u][:, hd:]
        res = _dot(jnp.concatenate([yq, gmat], axis=0), states[h])
        ys.append(res[:c] + yys[u][:, :hd])
        states[h] = res[c:] + ghs[u][:, :hd]
    for h in range(RWKV_HEADS):
        st_ref[h] = states[h]

    out_rows = []
    for ci in range(nc):
        yns, bonuses = [], []
        for h in range(RWKV_HEADS):
            u = ci * RWKV_HEADS + h
            y = ys[u]
            cen = y - jnp.mean(y, axis=-1, keepdims=True)
            var = jnp.mean(cen * cen, axis=-1, keepdims=True)
            yns.append(cen * lax.rsqrt(var + RWKV_GN_EPS))
            bonuses.append(jnp.sum(rkr[rows_of(ci), lanes_of(h)], axis=-1, keepdims=True) * vs[u])
        yn = jnp.concatenate(yns, axis=1)
        bonus = jnp.concatenate(bonuses, axis=1)
        out_rows.append((yn * lg_ref[...] + lb_ref[...] + bonus) * g[rows_of(ci)])
    o_ref[0] = jnp.concatenate(out_rows, axis=0)


def _rwkv(p, mu, w_w2, w0, w_a2, a0, w_g2, k_k, k_a, r_k, lnx_g, lnx_b, tt=128):
    b, t, cols = p.shape
    w = RWKV_HEADS * RWKV_HD
    vec = lambda a_: a_.reshape(1, -1)
    full = lambda shape: pl.BlockSpec(shape, lambda i, j: (0, 0))
    return pl.pallas_call(
        functools.partial(_rwkv_kernel, tt=tt),
        grid=(b, t // tt),
        in_specs=[pl.BlockSpec((1, tt, cols), lambda i, j: (i, j, 0)), full((1, cols)),
                  full(w_w2.shape), full((1, w)), full(w_a2.shape), full((1, w)), full(w_g2.shape),
                  full((1, w)), full((1, w)), full((1, w)), full((1, w)), full((1, w))],
        out_specs=pl.BlockSpec((1, tt, w), lambda i, j: (i, j, 0)),
        out_shape=jax.ShapeDtypeStruct((b, t, w), F32),
        scratch_shapes=[pltpu.VMEM((8 + tt, cols), F32),
                        pltpu.VMEM((RWKV_HEADS, RWKV_HD, RWKV_HD), F32)],
        compiler_params=_cparams("parallel", "arbitrary"),
        name="rwkv7",
    )(p, vec(mu), w_w2, vec(w0), w_a2, vec(a0), w_g2, vec(k_k), vec(k_a), vec(r_k), vec(lnx_g), vec(lnx_b))


def _fox_cum_kernel(f_ref, bf_ref, col_ref, carry_ref, *, tt):
    @pl.when(pl.program_id(1) == 0)
    def _():
        carry_ref[...] = jnp.zeros(carry_ref.shape, F32)

    log_f = _log_sigmoid(f_ref[0] + bf_ref[...])
    cum = _dot_hi(_tri(tt).astype(F32), log_f) + carry_ref[0:1, :]
    col_ref[0] = cum
    carry_ref[...] = jnp.broadcast_to(cum[tt - 1:tt, :], carry_ref.shape)


def _fox_cum(f_logit, b_f, tt=256):
    b, t, n = f_logit.shape
    return pl.pallas_call(
        functools.partial(_fox_cum_kernel, tt=tt),
        grid=(b, t // tt),
        in_specs=[pl.BlockSpec((1, tt, n), lambda i, j: (i, j, 0)), pl.BlockSpec((1, n), lambda i, j: (0, 0))],
        out_specs=pl.BlockSpec((1, tt, n), lambda i, j: (i, j, 0)),
        out_shape=jax.ShapeDtypeStruct((b, t, n), F32),
        scratch_shapes=[pltpu.VMEM((8, n), F32)],
        compiler_params=_cparams("parallel", "arbitrary"),
        name="fox_cumsum",
    )(f_logit, b_f)


def _split3(x):
    hi = x.astype(BF16).astype(F32)
    mid = (x - hi).astype(BF16).astype(F32)
    lo = (x - hi - mid).astype(BF16).astype(F32)
    return hi, mid, lo


def _fox_kernel(q_ref, k_ref, v_ref, c_ref, o_ref, kaug_ref, vbf_ref, *, tq):
    qi = pl.program_id(2)
    pair = pl.program_id(1)
    d = FOX_HD
    nh = LANE // d

    def augment(x_h, c, first):
        lane = lax.broadcasted_iota(jnp.int32, x_h.shape, 1)
        hi, mid, lo = _split3(c)
        parts = (hi, mid, lo, 1.0, 1.0, 1.0) if first else (1.0, 1.0, 1.0, -hi, -mid, -lo)
        out = jnp.where(lane < d, x_h, 0.0)
        for n, part in enumerate(parts):
            out = jnp.where(lane == d + n, part, out)
        return out.astype(BF16)

    def head_col(c_all, head):
        lane = lax.broadcasted_iota(jnp.int32, c_all.shape, 1)
        return jnp.sum(jnp.where(lane == head, c_all, 0.0), axis=-1, keepdims=True)

    @pl.when(qi == 0)
    def _():
        k = k_ref[0]
        vbf_ref[...] = v_ref[0].astype(BF16)
        for hh in range(nh):
            k_h = k if hh == 0 else pltpu.roll(k, LANE - hh * d, axis=1)
            kaug_ref[hh] = augment(k_h, head_col(c_ref[0], pair * nh + hh), first=False)

    row0 = pl.multiple_of(qi * tq, tq)
    q = q_ref[0] * (d ** -0.5)
    c_q = c_ref[0, pl.ds(row0, tq), :]
    qs = []
    for hh in range(nh):
        q_h = q if hh == 0 else pltpu.roll(q, LANE - hh * d, axis=1)
        qs.append(augment(q_h, head_col(c_q, pair * nh + hh), first=True))
    causal = _tri(tq)

    def step(j, carry, diagonal):
        start = pl.multiple_of(j * tq, tq)
        vb = vbf_ref[pl.ds(start, tq), :]
        new = []
        for hh in range(nh):
            m, l, acc = carry[hh]
            s = lax.dot_general(qs[hh], kaug_ref[hh, pl.ds(start, tq), :], (((1,), (1,)), ((), ())),
                                preferred_element_type=F32)
            if diagonal:
                s = jnp.where(causal, s, -jnp.inf)
            m_new = jnp.maximum(m, jnp.max(s, axis=-1, keepdims=True))
            scale = jnp.exp(m - m_new)
            p = jnp.exp(s - m_new)
            l = l * scale + jnp.sum(p, axis=-1, keepdims=True)
            acc = acc * scale + jnp.dot(p.astype(BF16), vb, preferred_element_type=F32)
            new.append((m_new, l, acc))
        return tuple(new)

    init = tuple((jnp.full((tq, 1), -jnp.inf, F32), jnp.zeros((tq, 1), F32), jnp.zeros((tq, LANE), F32))
                 for _ in range(nh))
    carry = lax.fori_loop(0, qi, lambda j, cr: step(j, cr, False), init)
    carry = step(qi, carry, True)
    lane = lax.broadcasted_iota(jnp.int32, (tq, LANE), 1)
    out = carry[0][2] / carry[0][1]
    for hh in range(1, nh):
        out = jnp.where(lane >= hh * d, carry[hh][2] / carry[hh][1], out)
    o_ref[0] = out


def _fox(q, k, v, c_col, tq=256):
    b, t, w = q.shape
    npairs = w // LANE
    return pl.pallas_call(
        functools.partial(_fox_kernel, tq=tq),
        grid=(b, npairs, t // tq),
        in_specs=[pl.BlockSpec((1, tq, LANE), lambda i, p, j: (i, j, p)),
                  pl.BlockSpec((1, t, LANE), lambda i, p, j: (i, 0, p)),
                  pl.BlockSpec((1, t, LANE), lambda i, p, j: (i, 0, p)),
                  pl.BlockSpec((1, t, LANE), lambda i, p, j: (i, 0, 0))],
        out_specs=pl.BlockSpec((1, tq, LANE), lambda i, p, j: (i, j, p)),
        out_shape=jax.ShapeDtypeStruct((b, t, w), F32),
        scratch_shapes=[pltpu.VMEM((LANE // FOX_HD, t, LANE), BF16), pltpu.VMEM((t, LANE), BF16)],
        compiler_params=_cparams("parallel", "arbitrary", "arbitrary"),
        name="fox_attention",
    )(q, k, v, c_col)


def _ffn_kernel(h_ref, g_ref, w1_ref, w3_ref, w2_ref, o_ref, xn_ref, acc_ref):
    f = pl.program_id(1)

    @pl.when(f == 0)
    def _():
        xn_ref[...] = _rmsnorm(h_ref[...], g_ref[...]).astype(BF16)
        acc_ref[...] = jnp.zeros(acc_ref.shape, F32)

    xn = xn_ref[...]
    mid = _silu(jnp.dot(xn, w1_ref[...], preferred_element_type=F32)) * jnp.dot(
        xn, w3_ref[...], preferred_element_type=F32)
    acc_ref[...] += _dot(mid, w2_ref[...])

    @pl.when(f == pl.num_programs(1) - 1)
    def _():
        o_ref[...] = h_ref[...] + acc_ref[...]


def _ffn(h, g, w1, w3, w2, tm=1024, tf=256):
    m, d = h.shape
    nf = w1.shape[1]
    return pl.pallas_call(
        _ffn_kernel,
        grid=(m // tm, nf // tf),
        in_specs=[pl.BlockSpec((tm, d), lambda i, f: (i, 0)),
                  pl.BlockSpec((1, d), lambda i, f: (0, 0)),
                  pl.BlockSpec((d, tf), lambda i, f: (0, f)),
                  pl.BlockSpec((d, tf), lambda i, f: (0, f)),
                  pl.BlockSpec((tf, d), lambda i, f: (f, 0))],
        out_specs=pl.BlockSpec((tm, d), lambda i, f: (i, 0)),
        out_shape=jax.ShapeDtypeStruct((m, d), F32),
        scratch_shapes=[pltpu.VMEM((tm, d), BF16), pltpu.VMEM((tm, d), F32)],
        compiler_params=_cparams("parallel", "arbitrary"),
        name="ffn_dense",
    )(h, g.reshape(1, d), w1, w3, w2)


def _moe_route_kernel(h_ref, g_ref, r_ref, xn_ref, meta_ref, cnt_ref, carry_ref):
    @pl.when(pl.program_id(0) == 0)
    def _():
        carry_ref[...] = jnp.zeros(carry_ref.shape, F32)

    xn = _rmsnorm(h_ref[...], g_ref[...])
    xn_ref[...] = xn
    tm = xn.shape[0]
    lane = lax.broadcasted_iota(jnp.int32, (tm, LANE), 1)
    logits = jnp.where(lane < N_EXPERTS, _dot_hi(xn, r_ref[...]), -jnp.inf)
    m1 = jnp.max(logits, axis=-1, keepdims=True)
    i1 = jnp.min(jnp.where(logits == m1, lane, LANE), axis=-1, keepdims=True)
    rest = jnp.where(lane == i1, -jnp.inf, logits)
    m2 = jnp.max(rest, axis=-1, keepdims=True)
    i2 = jnp.min(jnp.where(rest == m2, lane, LANE), axis=-1, keepdims=True)
    e2 = jnp.exp(m2 - m1)
    g1 = 1.0 / (1.0 + e2)
    g2 = e2 / (1.0 + e2)
    onehot = jnp.where(lane == i1, 1.0, jnp.where(lane == i2, 1.0, 0.0))
    before = _dot(_tri(tm, strict=True).astype(F32), onehot) + carry_ref[0:1, :]
    r1 = jnp.sum(jnp.where(lane == i1, before, 0.0), axis=-1, keepdims=True)
    r2 = jnp.sum(jnp.where(lane == i2, before, 0.0), axis=-1, keepdims=True)
    meta = jnp.where(lane == 0, i1.astype(F32), jnp.where(lane == 1, i2.astype(F32), 0.0))
    meta = jnp.where(lane == 2, g1, jnp.where(lane == 3, g2, meta))
    meta_ref[...] = jnp.where(lane == 4, r1, jnp.where(lane == 5, r2, meta))
    total = carry_ref[0:1, :] + jnp.sum(onehot, axis=0, keepdims=True)
    carry_ref[...] = jnp.broadcast_to(total, carry_ref.shape)
    cnt_ref[...] = jnp.broadcast_to(total, cnt_ref.shape)


def _moe_route(h, g, router, tm=512):
    m, d = h.shape
    return pl.pallas_call(
        _moe_route_kernel,
        grid=(m // tm,),
        in_specs=[pl.BlockSpec((tm, d), lambda i: (i, 0)), pl.BlockSpec((1, d), lambda i: (0, 0)),
                  pl.BlockSpec((d, LANE), lambda i: (0, 0))],
        out_specs=[pl.BlockSpec((tm, d), lambda i: (i, 0)), pl.BlockSpec((tm, LANE), lambda i: (i, 0)),
                   pl.BlockSpec((8, LANE), lambda i: (0, 0))],
        out_shape=[jax.ShapeDtypeStruct((m, d), F32), jax.ShapeDtypeStruct((m, LANE), F32),
                   jax.ShapeDtypeStruct((8, LANE), F32)],
        scratch_shapes=[pltpu.VMEM((8, LANE), F32)],
        compiler_params=_cparams("arbitrary"),
        name="moe_route",
    )(h, g.reshape(1, d), router)


def _row_copy(src_ref, src_row, dst_ref, dst_row, sem):
    return pltpu.make_async_copy(src_ref.at[pl.ds(src_row, 1)], dst_ref.at[pl.ds(dst_row, 1)], sem)


_ROW_UNROLL = 8


def _moe_dispatch_kernel(pos_ref, xn_ref, xs_in_ref, xs_ref, sem, *, tm):
    del xs_in_ref

    def start(blk, carry):
        for u in range(_ROW_UNROLL):
            r = blk * _ROW_UNROLL + u
            for s in range(2):
                _row_copy(xn_ref, r, xs_ref, pos_ref[0, 0, 2 * r + s], sem).start(priority=s)
        return carry

    def wait(blk, carry):
        for _ in range(2 * _ROW_UNROLL):
            _row_copy(xn_ref, 0, xs_ref, 0, sem).wait()
        return carry

    lax.fori_loop(0, tm // _ROW_UNROLL, start, 0)
    lax.fori_loop(0, tm // _ROW_UNROLL, wait, 0)


def _moe_dispatch(xn, pos, n_rows, tm=256):
    m, d = xn.shape
    return pl.pallas_call(
        functools.partial(_moe_dispatch_kernel, tm=tm),
        grid=(m // tm,),
        in_specs=[pl.BlockSpec((1, 1, 2 * tm), lambda i: (i, 0, 0), memory_space=pltpu.SMEM),
                  pl.BlockSpec((tm, d), lambda i: (i, 0)), pl.BlockSpec(memory_space=pl.ANY)],
        out_specs=pl.BlockSpec(memory_space=pl.ANY),
        out_shape=jax.ShapeDtypeStruct((n_rows, d), F32),
        scratch_shapes=[pltpu.SemaphoreType.DMA(())],
        input_output_aliases={2: 0},
        compiler_params=_cparams("arbitrary"),
        name="moe_dispatch",
    )(pos.reshape(m // tm, 1, 2 * tm), xn, jnp.zeros((n_rows, d), F32))


def _moe_expert_kernel(te_ref, nu_ref, x_ref, w1_ref, w3_ref, w2_ref, y_ref, xbf_ref, acc_ref):
    del te_ref
    f = pl.program_id(1)

    @pl.when(f == 0)
    def _():
        xbf_ref[...] = x_ref[...].astype(BF16)
        acc_ref[...] = jnp.zeros(acc_ref.shape, F32)

    @pl.when(pl.program_id(0) < nu_ref[0])
    def _():
        xb = xbf_ref[...]
        mid = _silu(jnp.dot(xb, w1_ref[0], preferred_element_type=F32)) * jnp.dot(
            xb, w3_ref[0], preferred_element_type=F32)
        acc_ref[...] += _dot(mid, w2_ref[0])

    @pl.when(f == pl.num_programs(1) - 1)
    def _():
        y_ref[...] = acc_ref[...]


def _moe_experts(xs, tile_expert, n_used, w1, w3, w2, tmx, tf=512):
    n_rows, d = xs.shape
    nf = w1.shape[2] // tf
    fidx = lambda i, f, nu: jnp.where(i < nu[0], f, nf - 1)
    grid_spec = pltpu.PrefetchScalarGridSpec(
        num_scalar_prefetch=2,
        grid=(n_rows // tmx, nf),
        in_specs=[pl.BlockSpec((tmx, d), lambda i, f, te, nu: (i, 0)),
                  pl.BlockSpec((1, d, tf), lambda i, f, te, nu: (te[i], 0, fidx(i, f, nu))),
                  pl.BlockSpec((1, d, tf), lambda i, f, te, nu: (te[i], 0, fidx(i, f, nu))),
                  pl.BlockSpec((1, tf, d), lambda i, f, te, nu: (te[i], fidx(i, f, nu), 0))],
        out_specs=pl.BlockSpec((tmx, d), lambda i, f, te, nu: (i, 0)),
        scratch_shapes=[pltpu.VMEM((tmx, d), BF16), pltpu.VMEM((tmx, d), F32)],
    )
    return pl.pallas_call(
        _moe_expert_kernel,
        grid_spec=grid_spec,
        out_shape=jax.ShapeDtypeStruct((n_rows, d), F32),
        compiler_params=_cparams("arbitrary", "arbitrary"),
        name="moe_experts",
    )(tile_expert, n_used, xs, w1, w3, w2)


def _moe_combine_kernel(pos_ref, posn_ref, h_ref, meta_ref, fg_ref, ys_ref, o_ref, ybuf_ref, sem, *, tm):
    i = pl.program_id(0)
    slot = lax.rem(i, 2)

    def gather(p_ref, dst_slot):
        def start(blk, carry):
            for u in range(_ROW_UNROLL):
                r = blk * _ROW_UNROLL + u
                for s in range(2):
                    _row_copy(ys_ref, p_ref[0, 0, 2 * r + s], ybuf_ref.at[dst_slot], s * tm + r,
                              sem.at[dst_slot]).start(priority=s)
            return carry
        lax.fori_loop(0, tm // _ROW_UNROLL, start, 0)

    @pl.when(i == 0)
    def _():
        gather(pos_ref, 0)

    @pl.when(i + 1 < pl.num_programs(0))
    def _():
        gather(posn_ref, 1 - slot)

    def wait(blk, carry):
        for _ in range(2 * _ROW_UNROLL):
            _row_copy(ys_ref, 0, ybuf_ref.at[slot], 0, sem.at[slot]).wait()
        return carry

    lax.fori_loop(0, tm // _ROW_UNROLL, wait, 0)
    lane = lax.broadcasted_iota(jnp.int32, (tm, LANE), 1)
    meta = meta_ref[...]
    g1 = jnp.sum(jnp.where(lane == 2, meta, 0.0), axis=-1, keepdims=True)
    g2 = jnp.sum(jnp.where(lane == 3, meta, 0.0), axis=-1, keepdims=True)
    y = ybuf_ref[slot]
    o_ref[...] = _rmsnorm(h_ref[...] + g1 * y[:tm] + g2 * y[tm:], fg_ref[...])


def _moe_combine(h, meta, pos, ys, final_g, tm=256):
    m, d = h.shape
    nt = m // tm
    pos3 = pos.reshape(nt, 1, 2 * tm)
    smem = lambda imap: pl.BlockSpec((1, 1, 2 * tm), imap, memory_space=pltpu.SMEM)
    return pl.pallas_call(
        functools.partial(_moe_combine_kernel, tm=tm),
        grid=(nt,),
        in_specs=[smem(lambda i: (i, 0, 0)), smem(lambda i: (jnp.minimum(i + 1, nt - 1), 0, 0)),
                  pl.BlockSpec((tm, d), lambda i: (i, 0)), pl.BlockSpec((tm, LANE), lambda i: (i, 0)),
                  pl.BlockSpec((1, d), lambda i: (0, 0)), pl.BlockSpec(memory_space=pl.ANY)],
        out_specs=pl.BlockSpec((tm, d), lambda i: (i, 0)),
        out_shape=jax.ShapeDtypeStruct((m, d), F32),
        scratch_shapes=[pltpu.VMEM((2, 2 * tm, d), F32), pltpu.SemaphoreType.DMA((2,))],
        compiler_params=_cparams("arbitrary"),
        name="moe_combine",
    )(pos3, pos3, h, meta, final_g.reshape(1, d), ys)


def _moe_routed(h, g, router, w1, w3, w2, final_g, tmx=512):
    m, d = h.shape
    xn, meta, cnt = _moe_route(h, g, router)
    experts = meta[:, 0:2].astype(jnp.int32)
    rank = meta[:, 4:6].astype(jnp.int32)
    counts = cnt[0, :N_EXPERTS].astype(jnp.int32)
    padded = (counts + tmx - 1) // tmx * tmx
    ends = jnp.cumsum(padded)
    pos = (ends - padded)[experts] + rank
    n_tiles = (2 * m) // tmx + N_EXPERTS
    n_used = ends[-1] // tmx
    tile_start = jnp.minimum(jnp.arange(n_tiles, dtype=jnp.int32), n_used - 1) * tmx
    tile_expert = jnp.sum(tile_start[:, None] >= ends[None, :], axis=1).astype(jnp.int32)
    xs = _moe_dispatch(xn, pos, n_tiles * tmx)
    ys = _moe_experts(xs, tile_expert, n_used.reshape(1).astype(jnp.int32), w1, w3, w2, tmx)
    return _moe_combine(h, meta, pos, ys, final_g)


def _pad_cols(w, n):
    return jnp.pad(w, ((0, 0), (0, n - w.shape[1])))


def kernel(x, norm_mix_g, norm_ffn_g, w_in0, conv_w, conv_b, conv_ln_g, conv_ln_b, gla_w_a2, gla_b_a, gla_norm_g, w_out0, ffn_w1, ffn_w3, ffn_w2, w_in1, rwkv_mu, rwkv_w2, rwkv_w0, rwkv_a2, rwkv_a0, rwkv_g2, rwkv_k_k, rwkv_k_a, rwkv_r_k, rwkv_lnx_g, rwkv_lnx_b, fox_b_f, w_out1, moe_router, moe_w1, moe_w3, moe_w2, final_norm_g):
    b, t, d = x.shape
    m = b * t
    half = d // 2
    bf = lambda a: a.astype(BF16)
    h = x.reshape(m, d)

    gk = GLA_HEADS * GLA_DK
    o = [0, half, 2 * half, 2 * half + gk, 2 * half + 2 * gk, 3 * half + 2 * gk, 4 * half + 2 * gk]
    ws0 = [bf(w_in0[:, o[i]:o[i + 1]]) for i in range(6)] + [bf(_pad_cols(w_in0[:, o[6]:], LANE))]
    c_val, c_gate, q, k, v, g_out, a_lr = [
        a.reshape(b, t, -1) for a in _norm_proj(h, norm_mix_g[0], ws0)]
    y_conv = _conv(c_val, c_gate, conv_w, conv_b, conv_ln_g, conv_ln_b)
    w_a2 = jnp.pad(gla_w_a2, ((0, LANE - gla_w_a2.shape[0]), (0, 0)))
    y_gla = _gla(q, k, v, g_out, a_lr, w_a2, gla_b_a, gla_norm_g)
    h = _out_proj(h, y_conv.reshape(m, half), y_gla.reshape(m, half), bf(w_out0[:half]), bf(w_out0[half:]))
    h = _ffn(h, norm_ffn_g[0], bf(ffn_w1), bf(ffn_w3), bf(ffn_w2))

    rc = _RWKV_COLS
    ws1 = [bf(w_in1[:, :rc]), bf(w_in1[:, rc:rc + half]), bf(w_in1[:, rc + half:rc + 2 * half]),
           bf(w_in1[:, rc + 2 * half:rc + 3 * half]), bf(_pad_cols(w_in1[:, rc + 3 * half:], LANE))]
    p_rwkv, fq, fk, fv, f_logit = [a.reshape(b, t, -1) for a in _norm_proj(h, norm_mix_g[1], ws1)]
    y_rwkv = _rwkv(p_rwkv, rwkv_mu, rwkv_w2, rwkv_w0, rwkv_a2, rwkv_a0, rwkv_g2, rwkv_k_k, rwkv_k_a,
                   rwkv_r_k, rwkv_lnx_g, rwkv_lnx_b)
    b_f = jnp.pad(fox_b_f, (0, LANE - FOX_HEADS)).reshape(1, LANE)
    y_fox = _fox(fq, fk, fv, _fox_cum(f_logit, b_f))
    h = _out_proj(h, y_rwkv.reshape(m, half), y_fox.reshape(m, half), bf(w_out1[:half]), bf(w_out1[half:]))
    out = _moe_routed(h, norm_ffn_g[1], _pad_cols(moe_router, LANE), bf(moe_w1), bf(moe_w3), bf(moe_w2),
                      final_norm_g)
    return out.reshape(b, t, d)
```

```python
import functools

import jax
import jax.numpy as jnp
from jax import lax
from jax.experimental import pallas as pl
from jax.experimental.pallas import tpu as pltpu

F32 = jnp.float32
BF16 = jnp.bfloat16
HI = lax.Precision.HIGHEST

NORM_EPS = 1e-6
CONV_WIDTH = 31
CONV_LN_EPS = 1e-5
GLA_HEADS = 4
GLA_DK = 64
GLA_DV = 128
GLA_TAU = 16.0
GLA_CHUNK = 64
RWKV_HEADS = 8
RWKV_HD = 64
RWKV_CHUNK = 64
RWKV_DECAY_SCALE = 0.606531
RWKV_GN_EPS = 64e-5
RWKV_L2_EPS = 1e-12
FOX_HEADS = 8
FOX_HD = 64
N_EXPERTS = 8
LANE = 128
_SUBLANES = 8
VMEM_LIMIT = 56 * 1024 * 1024


def _cparams(*sem):
    return pltpu.CompilerParams(dimension_semantics=sem, vmem_limit_bytes=VMEM_LIMIT)


def _dot(a, b):
    return jnp.dot(a.astype(BF16), b.astype(BF16), preferred_element_type=F32)


def _dot_nt(a, b):
    return lax.dot_general(a.astype(BF16), b.astype(BF16), (((1,), (1,)), ((), ())),
                           preferred_element_type=F32)


def _dot_tn(a, b):
    return lax.dot_general(a.astype(BF16), b.astype(BF16), (((0,), (0,)), ((), ())),
                           preferred_element_type=F32)


def _dot_hi(a, b):
    return jnp.dot(a, b, precision=HI, preferred_element_type=F32)


def _dot_tn_hi(a, b):
    return lax.dot_general(a, b, (((0,), (0,)), ((), ())), precision=HI, preferred_element_type=F32)


def _sigmoid(x):
    return 1.0 / (1.0 + jnp.exp(-x))


def _silu(x):
    return x * _sigmoid(x)


def _log_sigmoid(x):
    return jnp.minimum(x, 0.0) - jnp.log(1.0 + jnp.exp(-jnp.abs(x)))


def _rmsnorm(x, g):
    return x * lax.rsqrt(jnp.mean(x * x, axis=-1, keepdims=True) + NORM_EPS) * g


def _tri(n, strict=False):
    r = lax.broadcasted_iota(jnp.int32, (n, n), 0)
    c = lax.broadcasted_iota(jnp.int32, (n, n), 1)
    return (r > c) if strict else (r >= c)


def _norm_proj_kernel(h_ref, g_ref, *refs):
    n = len(refs) // 2
    xn = _rmsnorm(h_ref[...], g_ref[...]).astype(BF16)
    for w_ref, o_ref in zip(refs[:n], refs[n:]):
        o_ref[...] = jnp.dot(xn, w_ref[...], preferred_element_type=F32)


def _norm_proj(h, g, ws, tm=512):
    m, d = h.shape
    return pl.pallas_call(
        _norm_proj_kernel,
        grid=(m // tm,),
        in_specs=[pl.BlockSpec((tm, d), lambda i: (i, 0)), pl.BlockSpec((1, d), lambda i: (0, 0))]
        + [pl.BlockSpec(w.shape, lambda i: (0, 0)) for w in ws],
        out_specs=[pl.BlockSpec((tm, w.shape[1]), lambda i: (i, 0)) for w in ws],
        out_shape=[jax.ShapeDtypeStruct((m, w.shape[1]), F32) for w in ws],
        compiler_params=_cparams("parallel"),
        name="norm_proj",
    )(h, g.reshape(1, d), *ws)


def _out_proj_kernel(h_ref, ya_ref, yb_ref, wa_ref, wb_ref, o_ref):
    o_ref[...] = h_ref[...] + _dot(ya_ref[...], wa_ref[...]) + _dot(yb_ref[...], wb_ref[...])


def _out_proj(h, ya, yb, wa, wb, tm=512):
    m, d = h.shape
    k = ya.shape[1]
    return pl.pallas_call(
        _out_proj_kernel,
        grid=(m // tm,),
        in_specs=[pl.BlockSpec((tm, d), lambda i: (i, 0)),
                  pl.BlockSpec((tm, k), lambda i: (i, 0)),
                  pl.BlockSpec((tm, k), lambda i: (i, 0)),
                  pl.BlockSpec((k, d), lambda i: (0, 0)),
                  pl.BlockSpec((k, d), lambda i: (0, 0))],
        out_specs=pl.BlockSpec((tm, d), lambda i: (i, 0)),
        out_shape=jax.ShapeDtypeStruct((m, d), F32),
        compiler_params=_cparams("parallel"),
        name="out_proj",
    )(h, ya, yb, wa, wb)


_CONV_HALO = 32


def _conv_kernel(val_ref, gate_ref, w_ref, b_ref, lg_ref, lb_ref, o_ref, u_ref, sh_ref, *, tt, rows):
    @pl.when(pl.program_id(1) == 0)
    def _():
        u_ref[0:_CONV_HALO, :] = jnp.zeros((_CONV_HALO, u_ref.shape[1]), F32)

    u_ref[_CONV_HALO:_CONV_HALO + tt, :] = val_ref[0] * _sigmoid(gate_ref[0])
    span = sh_ref.shape[1]
    for s in range(1, _SUBLANES):
        sh_ref[s - 1] = u_ref[s:s + span, :]
    base = _CONV_HALO - (CONV_WIDTH - 1)
    c = u_ref.shape[1]
    for r0 in range(0, tt, rows):
        acc = jnp.zeros((rows // _SUBLANES, _SUBLANES, c), F32) + b_ref[...]
        for j in range(CONV_WIDTH):
            phase = (base + j) % _SUBLANES
            row = r0 + base + j - phase
            win = u_ref[row:row + rows, :] if phase == 0 else sh_ref[phase - 1, row:row + rows, :]
            acc = acc + win.reshape(rows // _SUBLANES, _SUBLANES, c) * w_ref[j]
        acc = acc.reshape(rows, c)
        mu = jnp.mean(acc, axis=-1, keepdims=True)
        cen = acc - mu
        var = jnp.mean(cen * cen, axis=-1, keepdims=True)
        y = cen * lax.rsqrt(var + CONV_LN_EPS) * lg_ref[...] + lb_ref[...]
        o_ref[0, r0:r0 + rows, :] = _silu(y)
    u_ref[0:_CONV_HALO, :] = u_ref[tt:tt + _CONV_HALO, :]


def _conv(val, gate, conv_w, conv_b, ln_g, ln_b, tt=256, rows=32):
    b, t, c = val.shape
    wrep = jnp.broadcast_to(conv_w[:, None, :], (CONV_WIDTH, _SUBLANES, c))
    vec = lambda a: a.reshape(1, c)
    tile = pl.BlockSpec((1, tt, c), lambda i, j: (i, j, 0))
    full = lambda shape: pl.BlockSpec(shape, lambda i, j: (0,) * len(shape))
    return pl.pallas_call(
        functools.partial(_conv_kernel, tt=tt, rows=rows),
        grid=(b, t // tt),
        in_specs=[tile, tile, full((CONV_WIDTH, _SUBLANES, c)), full((1, c)), full((1, c)), full((1, c))],
        out_specs=tile,
        out_shape=jax.ShapeDtypeStruct((b, t, c), F32),
        scratch_shapes=[pltpu.VMEM((tt + _CONV_HALO, c), F32),
                        pltpu.VMEM((_SUBLANES - 1, tt + _CONV_HALO - _SUBLANES, c), F32)],
        compiler_params=_cparams("parallel", "arbitrary"),
        name="conformer_conv",
    )(val, gate, wrep, vec(conv_b), vec(ln_g), vec(ln_b))


def _gla_kernel(q_ref, k_ref, v_ref, go_ref, alr_ref, wa2_ref, ba_ref, ng_ref, o_ref, s_ref, *, tt):
    c = GLA_CHUNK

    @pl.when(pl.program_id(1) == 0)
    def _():
        s_ref[...] = jnp.zeros(s_ref.shape, F32)

    nc = tt // c
    tril = _tri(c)
    tril_f = tril.astype(F32)
    ones_cv = jnp.ones((c, GLA_DV), F32)
    units = [(ci, h) for ci in range(nc) for h in range(GLA_HEADS)]
    rows_of = lambda ci: slice(ci * c, (ci + 1) * c)
    keys_of = lambda h: slice(h * GLA_DK, (h + 1) * GLA_DK)
    vals_of = lambda h: slice(h * GLA_DV, (h + 1) * GLA_DV)

    z = _dot(alr_ref[0], wa2_ref[...]) + ba_ref[...]
    log_a = _log_sigmoid(z) * (1.0 / GLA_TAU)
    q = q_ref[0] * (GLA_DK ** -0.5)
    k = k_ref[0]
    q_decs, k_decs, k_ends, decays = [], [], [], []
    for ci in range(nc):
        la = log_a[rows_of(ci)]
        bcum = _dot_hi(tril_f, la)
        b_last = bcum[c - 1:c, :]
        q_decs.append(q[rows_of(ci)] * jnp.exp(bcum))
        k_decs.append(k[rows_of(ci)] * jnp.exp(-bcum))
        k_ends.append(k[rows_of(ci)] * jnp.exp(b_last - bcum))
        decays.append(jnp.exp(_dot_tn_hi(la, ones_cv)))

    vs = [v_ref[0, rows_of(ci), vals_of(h)] for ci, h in units]
    attns = [jnp.where(tril, _dot_nt(q_decs[ci][:, keys_of(h)], k_decs[ci][:, keys_of(h)]), 0.0)
             for ci, h in units]
    kvs = [_dot_tn(k_ends[ci][:, keys_of(h)], vs[u]) for u, (ci, h) in enumerate(units)]
    intras = [_dot(attns[u], vs[u]) for u in range(len(units))]
    states = [s_ref[h] for h in range(GLA_HEADS)]
    prevs = []
    for u, (ci, h) in enumerate(units):
        prevs.append(states[h])
        states[h] = states[h] * decays[ci][keys_of(h)] + kvs[u]
    for h in range(GLA_HEADS):
        s_ref[h] = states[h]
    inters = [_dot(q_decs[ci][:, keys_of(h)], prevs[u]) for u, (ci, h) in enumerate(units)]
    for ci in range(nc):
        outs = []
        for h in range(GLA_HEADS):
            u = ci * GLA_HEADS + h
            o = intras[u] + inters[u]
            outs.append(o * lax.rsqrt(jnp.mean(o * o, axis=-1, keepdims=True) + NORM_EPS))
        o_all = jnp.concatenate(outs, axis=-1)
        o_ref[0, rows_of(ci), :] = o_all * ng_ref[...] * _silu(go_ref[0, rows_of(ci), :])


def _gla(q, k, v, g_out, a_lr, w_a2, b_a, norm_g, tt=256):
    b, t, dq = q.shape
    dv = v.shape[-1]
    tile = lambda n: pl.BlockSpec((1, tt, n), lambda i, j: (i, j, 0))
    full = lambda shape: pl.BlockSpec(shape, lambda i, j: (0, 0))
    return pl.pallas_call(
        functools.partial(_gla_kernel, tt=tt),
        grid=(b, t // tt),
        in_specs=[tile(dq), tile(dq), tile(dv), tile(dv), tile(a_lr.shape[-1]),
                  full(w_a2.shape), full((1, dq)), full((1, dv))],
        out_specs=tile(dv),
        out_shape=jax.ShapeDtypeStruct((b, t, dv), F32),
        scratch_shapes=[pltpu.VMEM((GLA_HEADS, GLA_DK, GLA_DV), F32)],
        compiler_params=_cparams("parallel", "arbitrary"),
        name="gla",
    )(q, k, v, g_out, a_lr, w_a2, b_a.reshape(1, dq), norm_g.reshape(1, dv))


_RWKV_COLS = 3 * 512 + 64 + 64 + 128


def _rwkv_kernel(p_ref, mu_ref, w2_ref, w0_ref, a2_ref, a0_ref, g2_ref, kk_ref, ka_ref, rk_ref,
                 lg_ref, lb_ref, o_ref, xs_ref, st_ref, *, tt):
    c = RWKV_CHUNK
    hd = RWKV_HD
    w = RWKV_HEADS * hd

    @pl.when(pl.program_id(1) == 0)
    def _():
        xs_ref[0:8, :] = jnp.zeros((8, xs_ref.shape[1]), F32)
        st_ref[...] = jnp.zeros(st_ref.shape, F32)

    x = p_ref[0]
    xs_ref[8:8 + tt, :] = x
    prev = xs_ref[7:7 + tt, :]
    xs_ref[7:8, :] = x[tt - 1:tt, :]
    x = x + (prev - x) * mu_ref[...]

    r = x[:, 0:w]
    k = x[:, w:2 * w]
    v = x[:, 2 * w:3 * w]
    xw = x[:, 3 * w:3 * w + 64]
    xa = x[:, 3 * w + 64:3 * w + 128]
    xg = x[:, 3 * w + 128:3 * w + 256]
    log_w = -RWKV_DECAY_SCALE * _sigmoid(w0_ref[...] + _dot(jnp.tanh(xw), w2_ref[...]))
    a = _sigmoid(a0_ref[...] + _dot(xa, a2_ref[...]))
    g = _dot(_sigmoid(xg), g2_ref[...])
    kk = k * kk_ref[...]
    k_mod = k * (1.0 + (a - 1.0) * ka_ref[...])
    rkr = r * k_mod * rk_ref[...]

    tril = _tri(c)
    tril_s = _tri(c, strict=True)
    tril_f = tril.astype(F32)
    eye = (lax.broadcasted_iota(jnp.int32, (c, c), 0)
           == lax.broadcasted_iota(jnp.int32, (c, c), 1)).astype(F32)
    zeros_cc = jnp.zeros((c, hd), F32)

    nc = tt // c
    units = [(ci, h) for ci in range(nc) for h in range(RWKV_HEADS)]
    rows_of = lambda ci: slice(ci * c, (ci + 1) * c)
    lanes_of = lambda h: slice(h * hd, (h + 1) * hd)

    exps = []
    for ci in range(nc):
        lw_c = log_w[rows_of(ci)]
        lcum = _dot_hi(tril_f, lw_c)
        l_last = lcum[c - 1:c, :]
        exps.append((jnp.exp(lcum), jnp.exp(-lcum), jnp.exp(l_last - lcum), jnp.exp(lcum - lw_c),
                     jnp.exp(l_last)))

    aqs, rqs, vs, lhss, rhss, bkes = [], [], [], [], [], []
    for ci, h in units:
        rows, hs = rows_of(ci), lanes_of(h)
        e_pos, e_neg, e_end, e_prev, _ = exps[ci]
        kk_h = kk[rows, hs]
        nrm = jnp.sqrt(jnp.sum(kk_h * kk_h, axis=-1, keepdims=True))
        kk_h = kk_h / jnp.maximum(nrm, RWKV_L2_EPS)
        k_h = k_mod[rows, hs]
        kka = kk_h * a[rows, hs]
        aq = -kk_h * e_prev[:, hs]
        rq = r[rows, hs] * e_pos[:, hs]
        aqs.append(aq)
        rqs.append(rq)
        vs.append(v[rows, hs])
        lhss.append(jnp.concatenate([aq, rq], axis=0))
        rhss.append(jnp.concatenate([kka * e_neg[:, hs], k_h * e_neg[:, hs]], axis=0))
        bkes.append(jnp.concatenate([kka * e_end[:, hs], k_h * e_end[:, hs]], axis=0))

    xss = [_dot_nt(lhs, rhs) for lhs, rhs in zip(lhss, rhss)]
    m1s = [jnp.where(tril_s, xs[:c, :c], 0.0) for xs in xss]
    m2s = [jnp.where(tril_s, xs[:c, c:], 0.0) for xs in xss]
    n12s = [jnp.concatenate([jnp.where(tril, xs[c:, :c], 0.0), jnp.where(tril, xs[c:, c:], 0.0)], axis=1)
            for xs in xss]
    tinvs = [eye + m1 for m1 in m1s]
    mps = m1s
    for _ in range(5):
        mps = [_dot(mp, mp) for mp in mps]
        tinvs = [tinv + _dot(tinv, mp) for tinv, mp in zip(tinvs, mps)]
    mvs = [_dot(m2, v_h) for m2, v_h in zip(m2s, vs)]
    pqs = [_dot(tinv, jnp.concatenate([mv, aq], axis=1)) for tinv, mv, aq in zip(tinvs, mvs, aqs)]
    pqvs = [jnp.concatenate([pq, jnp.concatenate([v_h, zeros_cc], axis=1)], axis=0)
            for pq, v_h in zip(pqs, vs)]
    yys = [_dot(n12, pqv) for n12, pqv in zip(n12s, pqvs)]
    ghs = [_dot_tn(bke, pqv) for bke, pqv in zip(bkes, pqvs)]

    states = [st_ref[h] for h in range(RWKV_HEADS)]
    ys = []
    for u, (ci, h) in enumerate(units):
        yq = rqs[u] + yys[u][:, hd:]
        gmat = eye * exps[ci][4][:, lanes_of(h)] + ghs[u][:, hd:]
        res = _dot(jnp.concatenate([yq, gmat], axis=0), states[h])
        ys.append(res[:c] + yys[u][:, :hd])
        states[h] = res[c:] + ghs[u][:, :hd]
    for h in range(RWKV_HEADS):
        st_ref[h] = states[h]

    out_rows = []
    for ci in range(nc):
        yns, bonuses = [], []
        for h in range(RWKV_HEADS):
            u = ci * RWKV_HEADS + h
            y = ys[u]
            cen = y - jnp.mean(y, axis=-1, keepdims=True)
            var = jnp.mean(cen * cen, axis=-1, keepdims=True)
            yns.append(cen * lax.rsqrt(var + RWKV_GN_EPS))
            bonuses.append(jnp.sum(rkr[rows_of(ci), lanes_of(h)], axis=-1, keepdims=True) * vs[u])
        yn = jnp.concatenate(yns, axis=1)
        bonus = jnp.concatenate(bonuses, axis=1)
        out_rows.append((yn * lg_ref[...] + lb_ref[...] + bonus) * g[rows_of(ci)])
    o_ref[0] = jnp.concatenate(out_rows, axis=0)


def _rwkv(p, mu, w_w2, w0, w_a2, a0, w_g2, k_k, k_a, r_k, lnx_g, lnx_b, tt=128):
    b, t, cols = p.shape
    w = RWKV_HEADS * RWKV_HD
    vec = lambda a_: a_.reshape(1, -1)
    full = lambda shape: pl.BlockSpec(shape, lambda i, j: (0, 0))
    return pl.pallas_call(
        functools.partial(_rwkv_kernel, tt=tt),
        grid=(b, t // tt),
        in_specs=[pl.BlockSpec((1, tt, cols), lambda i, j: (i, j, 0)), full((1, cols)),
                  full(w_w2.shape), full((1, w)), full(w_a2.shape), full((1, w)), full(w_g2.shape),
                  full((1, w)), full((1, w)), full((1, w)), full((1, w)), full((1, w))],
        out_specs=pl.BlockSpec((1, tt, w), lambda i, j: (i, j, 0)),
        out_shape=jax.ShapeDtypeStruct((b, t, w), F32),
        scratch_shapes=[pltpu.VMEM((8 + tt, cols), F32),
                        pltpu.VMEM((RWKV_HEADS, RWKV_HD, RWKV_HD), F32)],
        compiler_params=_cparams("parallel", "arbitrary"),
        name="rwkv7",
    )(p, vec(mu), w_w2, vec(w0), w_a2, vec(a0), w_g2, vec(k_k), vec(k_a), vec(r_k), vec(lnx_g), vec(lnx_b))


def _fox_cum_kernel(f_ref, bf_ref, col_ref, carry_ref, *, tt):
    @pl.when(pl.program_id(1) == 0)
    def _():
        carry_ref[...] = jnp.zeros(carry_ref.shape, F32)

    log_f = _log_sigmoid(f_ref[0] + bf_ref[...])
    cum = _dot_hi(_tri(tt).astype(F32), log_f) + carry_ref[0:1, :]
    col_ref[0] = cum
    carry_ref[...] = jnp.broadcast_to(cum[tt - 1:tt, :], carry_ref.shape)


def _fox_cum(f_logit, b_f, tt=256):
    b, t, n = f_logit.shape
    return pl.pallas_call(
        functools.partial(_fox_cum_kernel, tt=tt),
        grid=(b, t // tt),
        in_specs=[pl.BlockSpec((1, tt, n), lambda i, j: (i, j, 0)), pl.BlockSpec((1, n), lambda i, j: (0, 0))],
        out_specs=pl.BlockSpec((1, tt, n), lambda i, j: (i, j, 0)),
        out_shape=jax.ShapeDtypeStruct((b, t, n), F32),
        scratch_shapes=[pltpu.VMEM((8, n), F32)],
        compiler_params=_cparams("parallel", "arbitrary"),
        name="fox_cumsum",
    )(f_logit, b_f)


def _split3(x):
    hi = x.astype(BF16).astype(F32)
    mid = (x - hi).astype(BF16).astype(F32)
    lo = (x - hi - mid).astype(BF16).astype(F32)
    return hi, mid, lo


def _fox_kernel(q_ref, k_ref, v_ref, c_ref, o_ref, kaug_ref, vaug_ref, *, tq, groups):
    qi = pl.program_id(2)
    d = FOX_HD
    per = LANE // d
    nh = groups * per
    head0 = pl.program_id(1) * nh

    def augment(x_h, c, first):
        lane = lax.broadcasted_iota(jnp.int32, x_h.shape, 1)
        hi, mid, lo = _split3(c)
        parts = (hi, mid, lo, 1.0, 1.0, 1.0) if first else (1.0, 1.0, 1.0, -hi, -mid, -lo)
        out = jnp.where(lane < d, x_h, 0.0)
        for n, part in enumerate(parts):
            out = jnp.where(lane == d + n, part, out)
        return out.astype(BF16)

    def head_col(c_all, head):
        lane = lax.broadcasted_iota(jnp.int32, c_all.shape, 1)
        return jnp.sum(jnp.where(lane == head, c_all, 0.0), axis=-1, keepdims=True)

    def head_lanes(x, h):
        x_g = x[:, (h // per) * LANE:(h // per + 1) * LANE]
        return x_g if h % per == 0 else pltpu.roll(x_g, LANE - (h % per) * d, axis=1)

    @pl.when(qi == 0)
    def _():
        lane_t = lax.broadcasted_iota(jnp.int32, (k_ref.shape[1], LANE), 1)
        for h in range(nh):
            kaug_ref[h] = augment(head_lanes(k_ref[0], h), head_col(c_ref[0], head0 + h), first=False)
            v_h = jnp.where(lane_t < d, head_lanes(v_ref[0], h), jnp.where(lane_t == d, 1.0, 0.0))
            vaug_ref[h] = v_h.astype(BF16)

    row0 = pl.multiple_of(qi * tq, tq)
    q = q_ref[0] * (d ** -0.5)
    c_q = c_ref[0, pl.ds(row0, tq), :]
    qs = [augment(head_lanes(q, h), head_col(c_q, head0 + h), first=True) for h in range(nh)]
    causal = _tri(tq)

    def step(j, carry, diagonal):
        start = pl.multiple_of(j * tq, tq)
        ss = [lax.dot_general(qs[h], kaug_ref[h, pl.ds(start, tq), :], (((1,), (1,)), ((), ())),
                              preferred_element_type=F32) for h in range(nh)]
        ps, scales, ms = [], [], []
        for h in range(nh):
            m, _ = carry[h]
            s = jnp.where(causal, ss[h], -jnp.inf) if diagonal else ss[h]
            m_new = jnp.maximum(m, jnp.max(s, axis=-1, keepdims=True))
            scales.append(jnp.exp(m - m_new))
            ps.append(jnp.exp(s - m_new).astype(BF16))
            ms.append(m_new)
        pvs = [jnp.dot(ps[h], vaug_ref[h, pl.ds(start, tq), :], preferred_element_type=F32) for h in range(nh)]
        return tuple((ms[h], carry[h][1] * scales[h] + pvs[h]) for h in range(nh))

    init = tuple((jnp.full((tq, 1), -jnp.inf, F32), jnp.zeros((tq, LANE), F32)) for _ in range(nh))
    carry = lax.fori_loop(0, qi, lambda j, cr: step(j, cr, False), init)
    carry = step(qi, carry, True)
    lane = lax.broadcasted_iota(jnp.int32, (tq, LANE), 1)
    for grp in range(groups):
        out = None
        for hh in range(per):
            acc = carry[grp * per + hh][1]
            o_h = acc / jnp.sum(jnp.where(lane == d, acc, 0.0), axis=-1, keepdims=True)
            o_h = o_h if hh == 0 else pltpu.roll(o_h, hh * d, axis=1)
            out = o_h if out is None else jnp.where(lane >= hh * d, o_h, out)
        o_ref[0, :, grp * LANE:(grp + 1) * LANE] = out


def _fox(q, k, v, c_col, tq=256, groups=4):
    b, t, w = q.shape
    gw = groups * LANE
    return pl.pallas_call(
        functools.partial(_fox_kernel, tq=tq, groups=groups),
        grid=(b, w // gw, t // tq),
        in_specs=[pl.BlockSpec((1, tq, gw), lambda i, p, j: (i, j, p)),
                  pl.BlockSpec((1, t, gw), lambda i, p, j: (i, 0, p)),
                  pl.BlockSpec((1, t, gw), lambda i, p, j: (i, 0, p)),
                  pl.BlockSpec((1, t, LANE), lambda i, p, j: (i, 0, 0))],
        out_specs=pl.BlockSpec((1, tq, gw), lambda i, p, j: (i, j, p)),
        out_shape=jax.ShapeDtypeStruct((b, t, w), F32),
        scratch_shapes=[pltpu.VMEM((groups * LANE // FOX_HD, t, LANE), BF16),
                        pltpu.VMEM((groups * LANE // FOX_HD, t, LANE), BF16)],
        compiler_params=_cparams("parallel", "arbitrary", "arbitrary"),
        name="fox_attention",
    )(q, k, v, c_col)


def _ffn_kernel(h_ref, g_ref, w1_ref, w3_ref, w2_ref, o_ref, xn_ref, acc_ref):
    f = pl.program_id(1)

    @pl.when(f == 0)
    def _():
        xn_ref[...] = _rmsnorm(h_ref[...], g_ref[...]).astype(BF16)
        acc_ref[...] = jnp.zeros(acc_ref.shape, F32)

    xn = xn_ref[...]
    mid = _silu(jnp.dot(xn, w1_ref[...], preferred_element_type=F32)) * jnp.dot(
        xn, w3_ref[...], preferred_element_type=F32)
    acc_ref[...] += _dot(mid, w2_ref[...])

    @pl.when(f == pl.num_programs(1) - 1)
    def _():
        o_ref[...] = h_ref[...] + acc_ref[...]


def _ffn(h, g, w1, w3, w2, tm=1024, tf=256):
    m, d = h.shape
    nf = w1.shape[1]
    return pl.pallas_call(
        _ffn_kernel,
        grid=(m // tm, nf // tf),
        in_specs=[pl.BlockSpec((tm, d), lambda i, f: (i, 0)),
                  pl.BlockSpec((1, d), lambda i, f: (0, 0)),
                  pl.BlockSpec((d, tf), lambda i, f: (0, f)),
                  pl.BlockSpec((d, tf), lambda i, f: (0, f)),
                  pl.BlockSpec((tf, d), lambda i, f: (f, 0))],
        out_specs=pl.BlockSpec((tm, d), lambda i, f: (i, 0)),
        out_shape=jax.ShapeDtypeStruct((m, d), F32),
        scratch_shapes=[pltpu.VMEM((tm, d), BF16), pltpu.VMEM((tm, d), F32)],
        compiler_params=_cparams("parallel", "arbitrary"),
        name="ffn_dense",
    )(h, g.reshape(1, d), w1, w3, w2)


def _moe_route_kernel(h_ref, g_ref, r_ref, xn_ref, meta_ref, cnt_ref, carry_ref):
    @pl.when(pl.program_id(0) == 0)
    def _():
        carry_ref[...] = jnp.zeros(carry_ref.shape, F32)

    xn = _rmsnorm(h_ref[...], g_ref[...])
    xn_ref[...] = xn
    tm = xn.shape[0]
    lane = lax.broadcasted_iota(jnp.int32, (tm, LANE), 1)
    logits = jnp.where(lane < N_EXPERTS, _dot_hi(xn, r_ref[...]), -jnp.inf)
    m1 = jnp.max(logits, axis=-1, keepdims=True)
    i1 = jnp.min(jnp.where(logits == m1, lane, LANE), axis=-1, keepdims=True)
    rest = jnp.where(lane == i1, -jnp.inf, logits)
    m2 = jnp.max(rest, axis=-1, keepdims=True)
    i2 = jnp.min(jnp.where(rest == m2, lane, LANE), axis=-1, keepdims=True)
    e2 = jnp.exp(m2 - m1)
    g1 = 1.0 / (1.0 + e2)
    g2 = e2 / (1.0 + e2)
    onehot = jnp.where(lane == i1, 1.0, jnp.where(lane == i2, 1.0, 0.0))
    before = _dot(_tri(tm, strict=True).astype(F32), onehot) + carry_ref[0:1, :]
    r1 = jnp.sum(jnp.where(lane == i1, before, 0.0), axis=-1, keepdims=True)
    r2 = jnp.sum(jnp.where(lane == i2, before, 0.0), axis=-1, keepdims=True)
    meta = jnp.where(lane == 0, i1.astype(F32), jnp.where(lane == 1, i2.astype(F32), 0.0))
    meta = jnp.where(lane == 2, g1, jnp.where(lane == 3, g2, meta))
    meta_ref[...] = jnp.where(lane == 4, r1, jnp.where(lane == 5, r2, meta))
    total = carry_ref[0:1, :] + jnp.sum(onehot, axis=0, keepdims=True)
    carry_ref[...] = jnp.broadcast_to(total, carry_ref.shape)
    cnt_ref[...] = jnp.broadcast_to(total, cnt_ref.shape)


def _moe_route(h, g, router, tm=512):
    m, d = h.shape
    return pl.pallas_call(
        _moe_route_kernel,
        grid=(m // tm,),
        in_specs=[pl.BlockSpec((tm, d), lambda i: (i, 0)), pl.BlockSpec((1, d), lambda i: (0, 0)),
                  pl.BlockSpec((d, LANE), lambda i: (0, 0))],
        out_specs=[pl.BlockSpec((tm, d), lambda i: (i, 0)), pl.BlockSpec((tm, LANE), lambda i: (i, 0)),
                   pl.BlockSpec((8, LANE), lambda i: (0, 0))],
        out_shape=[jax.ShapeDtypeStruct((m, d), F32), jax.ShapeDtypeStruct((m, LANE), F32),
                   jax.ShapeDtypeStruct((8, LANE), F32)],
        scratch_shapes=[pltpu.VMEM((8, LANE), F32)],
        compiler_params=_cparams("arbitrary"),
        name="moe_route",
    )(h, g.reshape(1, d), router)


def _row_copy(src_ref, src_row, dst_ref, dst_row, sem):
    return pltpu.make_async_copy(src_ref.at[pl.ds(src_row, 1)], dst_ref.at[pl.ds(dst_row, 1)], sem)


_ROW_UNROLL = 8


def _moe_dispatch_kernel(pos_ref, xn_ref, xs_in_ref, xs_ref, sem, *, tm):
    del xs_in_ref

    def start(blk, carry):
        for u in range(_ROW_UNROLL):
            r = blk * _ROW_UNROLL + u
            for s in range(2):
                _row_copy(xn_ref, r, xs_ref, pos_ref[0, 0, 2 * r + s], sem).start(priority=s)
        return carry

    def wait(blk, carry):
        for _ in range(2 * _ROW_UNROLL):
            _row_copy(xn_ref, 0, xs_ref, 0, sem).wait()
        return carry

    lax.fori_loop(0, tm // _ROW_UNROLL, start, 0)
    lax.fori_loop(0, tm // _ROW_UNROLL, wait, 0)


def _moe_dispatch(xn, pos, n_rows, tm=256):
    m, d = xn.shape
    return pl.pallas_call(
        functools.partial(_moe_dispatch_kernel, tm=tm),
        grid=(m // tm,),
        in_specs=[pl.BlockSpec((1, 1, 2 * tm), lambda i: (i, 0, 0), memory_space=pltpu.SMEM),
                  pl.BlockSpec((tm, d), lambda i: (i, 0)), pl.BlockSpec(memory_space=pl.ANY)],
        out_specs=pl.BlockSpec(memory_space=pl.ANY),
        out_shape=jax.ShapeDtypeStruct((n_rows, d), F32),
        scratch_shapes=[pltpu.SemaphoreType.DMA(())],
        input_output_aliases={2: 0},
        compiler_params=_cparams("arbitrary"),
        name="moe_dispatch",
    )(pos.reshape(m // tm, 1, 2 * tm), xn, jnp.zeros((n_rows, d), F32))


def _moe_expert_kernel(te_ref, nu_ref, x_ref, w1_ref, w3_ref, w2_ref, y_ref, xbf_ref, acc_ref):
    del te_ref
    f = pl.program_id(1)

    @pl.when(f == 0)
    def _():
        xbf_ref[...] = x_ref[...].astype(BF16)
        acc_ref[...] = jnp.zeros(acc_ref.shape, F32)

    @pl.when(pl.program_id(0) < nu_ref[0])
    def _():
        xb = xbf_ref[...]
        mid = _silu(jnp.dot(xb, w1_ref[0], preferred_element_type=F32)) * jnp.dot(
            xb, w3_ref[0], preferred_element_type=F32)
        acc_ref[...] += _dot(mid, w2_ref[0])

    @pl.when(f == pl.num_programs(1) - 1)
    def _():
        y_ref[...] = acc_ref[...]


def _moe_experts(xs, tile_expert, n_used, w1, w3, w2, tmx, tf=512):
    n_rows, d = xs.shape
    nf = w1.shape[2] // tf
    fidx = lambda i, f, nu: jnp.where(i < nu[0], f, nf - 1)
    grid_spec = pltpu.PrefetchScalarGridSpec(
        num_scalar_prefetch=2,
        grid=(n_rows // tmx, nf),
        in_specs=[pl.BlockSpec((tmx, d), lambda i, f, te, nu: (i, 0)),
                  pl.BlockSpec((1, d, tf), lambda i, f, te, nu: (te[i], 0, fidx(i, f, nu))),
                  pl.BlockSpec((1, d, tf), lambda i, f, te, nu: (te[i], 0, fidx(i, f, nu))),
                  pl.BlockSpec((1, tf, d), lambda i, f, te, nu: (te[i], fidx(i, f, nu), 0))],
        out_specs=pl.BlockSpec((tmx, d), lambda i, f, te, nu: (i, 0)),
        scratch_shapes=[pltpu.VMEM((tmx, d), BF16), pltpu.VMEM((tmx, d), F32)],
    )
    return pl.pallas_call(
        _moe_expert_kernel,
        grid_spec=grid_spec,
        out_shape=jax.ShapeDtypeStruct((n_rows, d), F32),
        compiler_params=_cparams("arbitrary", "arbitrary"),
        name="moe_experts",
    )(tile_expert, n_used, xs, w1, w3, w2)


def _moe_combine_kernel(pos_ref, posn_ref, h_ref, meta_ref, fg_ref, ys_ref, o_ref, ybuf_ref, sem, *, tm):
    i = pl.program_id(0)
    slot = lax.rem(i, 2)

    def gather(p_ref, dst_slot):
        def start(blk, carry):
            for u in range(_ROW_UNROLL):
                r = blk * _ROW_UNROLL + u
                for s in range(2):
                    _row_copy(ys_ref, p_ref[0, 0, 2 * r + s], ybuf_ref.at[dst_slot], s * tm + r,
                              sem.at[dst_slot]).start(priority=s)
            return carry
        lax.fori_loop(0, tm // _ROW_UNROLL, start, 0)

    @pl.when(i == 0)
    def _():
        gather(pos_ref, 0)

    @pl.when(i + 1 < pl.num_programs(0))
    def _():
        gather(posn_ref, 1 - slot)

    def wait(blk, carry):
        for _ in range(2 * _ROW_UNROLL):
            _row_copy(ys_ref, 0, ybuf_ref.at[slot], 0, sem.at[slot]).wait()
        return carry

    lax.fori_loop(0, tm // _ROW_UNROLL, wait, 0)
    lane = lax.broadcasted_iota(jnp.int32, (tm, LANE), 1)
    meta = meta_ref[...]
    g1 = jnp.sum(jnp.where(lane == 2, meta, 0.0), axis=-1, keepdims=True)
    g2 = jnp.sum(jnp.where(lane == 3, meta, 0.0), axis=-1, keepdims=True)
    y = ybuf_ref[slot]
    o_ref[...] = _rmsnorm(h_ref[...] + g1 * y[:tm] + g2 * y[tm:], fg_ref[...])


def _moe_combine(h, meta, pos, ys, final_g, tm=256):
    m, d = h.shape
    nt = m // tm
    pos3 = pos.reshape(nt, 1, 2 * tm)
    smem = lambda imap: pl.BlockSpec((1, 1, 2 * tm), imap, memory_space=pltpu.SMEM)
    return pl.pallas_call(
        functools.partial(_moe_combine_kernel, tm=tm),
        grid=(nt,),
        in_specs=[smem(lambda i: (i, 0, 0)), smem(lambda i: (jnp.minimum(i + 1, nt - 1), 0, 0)),
                  pl.BlockSpec((tm, d), lambda i: (i, 0)), pl.BlockSpec((tm, LANE), lambda i: (i, 0)),
                  pl.BlockSpec((1, d), lambda i: (0, 0)), pl.BlockSpec(memory_space=pl.ANY)],
        out_specs=pl.BlockSpec((tm, d), lambda i: (i, 0)),
        out_shape=jax.ShapeDtypeStruct((m, d), F32),
        scratch_shapes=[pltpu.VMEM((2, 2 * tm, d), F32), pltpu.SemaphoreType.DMA((2,))],
        compiler_params=_cparams("arbitrary"),
        name="moe_combine",
    )(pos3, pos3, h, meta, final_g.reshape(1, d), ys)


def _moe_routed(h, g, router, w1, w3, w2, final_g, tmx=512):
    m, d = h.shape
    xn, meta, cnt = _moe_route(h, g, router)
    experts = meta[:, 0:2].astype(jnp.int32)
    rank = meta[:, 4:6].astype(jnp.int32)
    counts = cnt[0, :N_EXPERTS].astype(jnp.int32)
    padded = (counts + tmx - 1) // tmx * tmx
    ends = jnp.cumsum(padded)
    pos = (ends - padded)[experts] + rank
    n_tiles = (2 * m) // tmx + N_EXPERTS
    n_used = ends[-1] // tmx
    tile_start = jnp.minimum(jnp.arange(n_tiles, dtype=jnp.int32), n_used - 1) * tmx
    tile_expert = jnp.sum(tile_start[:, None] >= ends[None, :], axis=1).astype(jnp.int32)
    xs = _moe_dispatch(xn, pos, n_tiles * tmx)
    ys = _moe_experts(xs, tile_expert, n_used.reshape(1).astype(jnp.int32), w1, w3, w2, tmx)
    return _moe_combine(h, meta, pos, ys, final_g)


def _pad_cols(w, n):
    return jnp.pad(w, ((0, 0), (0, n - w.shape[1])))


def kernel(x, norm_mix_g, norm_ffn_g, w_in0, conv_w, conv_b, conv_ln_g, conv_ln_b, gla_w_a2, gla_b_a, gla_norm_g, w_out0, ffn_w1, ffn_w3, ffn_w2, w_in1, rwkv_mu, rwkv_w2, rwkv_w0, rwkv_a2, rwkv_a0, rwkv_g2, rwkv_k_k, rwkv_k_a, rwkv_r_k, rwkv_lnx_g, rwkv_lnx_b, fox_b_f, w_out1, moe_router, moe_w1, moe_w3, moe_w2, final_norm_g):
    b, t, d = x.shape
    m = b * t
    half = d // 2
    bf = lambda a: a.astype(BF16)
    h = x.reshape(m, d)

    gk = GLA_HEADS * GLA_DK
    o = [0, half, 2 * half, 2 * half + gk, 2 * half + 2 * gk, 3 * half + 2 * gk, 4 * half + 2 * gk]
    ws0 = [bf(w_in0[:, o[i]:o[i + 1]]) for i in range(6)] + [bf(_pad_cols(w_in0[:, o[6]:], LANE))]
    c_val, c_gate, q, k, v, g_out, a_lr = [
        a.reshape(b, t, -1) for a in _norm_proj(h, norm_mix_g[0], ws0)]
    y_conv = _conv(c_val, c_gate, conv_w, conv_b, conv_ln_g, conv_ln_b)
    w_a2 = jnp.pad(gla_w_a2, ((0, LANE - gla_w_a2.shape[0]), (0, 0)))
    y_gla = _gla(q, k, v, g_out, a_lr, w_a2, gla_b_a, gla_norm_g)
    h = _out_proj(h, y_conv.reshape(m, half), y_gla.reshape(m, half), bf(w_out0[:half]), bf(w_out0[half:]))
    h = _ffn(h, norm_ffn_g[0], bf(ffn_w1), bf(ffn_w3), bf(ffn_w2))

    rc = _RWKV_COLS
    ws1 = [bf(w_in1[:, :rc]), bf(w_in1[:, rc:rc + half]), bf(w_in1[:, rc + half:rc + 2 * half]),
           bf(w_in1[:, rc + 2 * half:rc + 3 * half]), bf(_pad_cols(w_in1[:, rc + 3 * half:], LANE))]
    p_rwkv, fq, fk, fv, f_logit = [a.reshape(b, t, -1) for a in _norm_proj(h, norm_mix_g[1], ws1)]
    y_rwkv = _rwkv(p_rwkv, rwkv_mu, rwkv_w2, rwkv_w0, rwkv_a2, rwkv_a0, rwkv_g2, rwkv_k_k, rwkv_k_a,
                   rwkv_r_k, rwkv_lnx_g, rwkv_lnx_b)
    b_f = jnp.pad(fox_b_f, (0, LANE - FOX_HEADS)).reshape(1, LANE)
    y_fox = _fox(fq, fk, fv, _fox_cum(f_logit, b_f))
    h = _out_proj(h, y_rwkv.reshape(m, half), y_fox.reshape(m, half), bf(w_out1[:half]), bf(w_out1[half:]))
    out = _moe_routed(h, norm_ffn_g[1], _pad_cols(moe_router, LANE), bf(moe_w1), bf(moe_w3), bf(moe_w2),
                      final_norm_g)
    return out.reshape(b, t, d)
```

```python
import functools

import jax
import jax.numpy as jnp
from jax import lax
from jax.experimental import pallas as pl
from jax.experimental.pallas import tpu as pltpu

F32 = jnp.float32
BF16 = jnp.bfloat16
HI = lax.Precision.HIGHEST

NORM_EPS = 1e-6
CONV_WIDTH = 31
CONV_LN_EPS = 1e-5
GLA_HEADS = 4
GLA_DK = 64
GLA_DV = 128
GLA_TAU = 16.0
GLA_CHUNK = 64
RWKV_HEADS = 8
RWKV_HD = 64
RWKV_CHUNK = 64
RWKV_DECAY_SCALE = 0.606531
RWKV_GN_EPS = 64e-5
RWKV_L2_EPS = 1e-12
FOX_HEADS = 8
FOX_HD = 64
N_EXPERTS = 8
LANE = 128
_SUBLANES = 8
VMEM_LIMIT = 56 * 1024 * 1024


def _cparams(*sem):
    return pltpu.CompilerParams(dimension_semantics=sem, vmem_limit_bytes=VMEM_LIMIT)


def _dot(a, b):
    return jnp.dot(a.astype(BF16), b.astype(BF16), preferred_element_type=F32)


def _dot_nt(a, b):
    return lax.dot_general(a.astype(BF16), b.astype(BF16), (((1,), (1,)), ((), ())),
                           preferred_element_type=F32)


def _dot_tn(a, b):
    return lax.dot_general(a.astype(BF16), b.astype(BF16), (((0,), (0,)), ((), ())),
                           preferred_element_type=F32)


def _dot_hi(a, b):
    return jnp.dot(a, b, precision=HI, preferred_element_type=F32)


def _dot_tn_hi(a, b):
    return lax.dot_general(a, b, (((0,), (0,)), ((), ())), precision=HI, preferred_element_type=F32)


def _sigmoid(x):
    return 1.0 / (1.0 + jnp.exp(-x))


def _silu(x):
    return x * _sigmoid(x)


def _log_sigmoid(x):
    return jnp.minimum(x, 0.0) - jnp.log(1.0 + jnp.exp(-jnp.abs(x)))


def _rmsnorm(x, g):
    return x * lax.rsqrt(jnp.mean(x * x, axis=-1, keepdims=True) + NORM_EPS) * g


def _tri(n, strict=False):
    r = lax.broadcasted_iota(jnp.int32, (n, n), 0)
    c = lax.broadcasted_iota(jnp.int32, (n, n), 1)
    return (r > c) if strict else (r >= c)


def _norm_proj_kernel(h_ref, g_ref, *refs):
    n = len(refs) // 2
    xn = _rmsnorm(h_ref[...], g_ref[...]).astype(BF16)
    for w_ref, o_ref in zip(refs[:n], refs[n:]):
        o_ref[...] = jnp.dot(xn, w_ref[...], preferred_element_type=F32)


def _norm_proj(h, g, ws, tm=512):
    m, d = h.shape
    return pl.pallas_call(
        _norm_proj_kernel,
        grid=(m // tm,),
        in_specs=[pl.BlockSpec((tm, d), lambda i: (i, 0)), pl.BlockSpec((1, d), lambda i: (0, 0))]
        + [pl.BlockSpec(w.shape, lambda i: (0, 0)) for w in ws],
        out_specs=[pl.BlockSpec((tm, w.shape[1]), lambda i: (i, 0)) for w in ws],
        out_shape=[jax.ShapeDtypeStruct((m, w.shape[1]), F32) for w in ws],
        compiler_params=_cparams("parallel"),
        name="norm_proj",
    )(h, g.reshape(1, d), *ws)


_CONV_HALO = 32


def _conv_kernel(val_ref, gate_ref, w_ref, b_ref, lg_ref, lb_ref, o_ref, u_ref, sh_ref, *, tt, rows):
    @pl.when(pl.program_id(1) == 0)
    def _():
        u_ref[0:_CONV_HALO, :] = jnp.zeros((_CONV_HALO, u_ref.shape[1]), F32)

    u_ref[_CONV_HALO:_CONV_HALO + tt, :] = val_ref[0] * _sigmoid(gate_ref[0])
    span = sh_ref.shape[1]
    for s in range(1, _SUBLANES):
        sh_ref[s - 1] = u_ref[s:s + span, :]
    base = _CONV_HALO - (CONV_WIDTH - 1)
    c = u_ref.shape[1]
    for r0 in range(0, tt, rows):
        acc = jnp.zeros((rows // _SUBLANES, _SUBLANES, c), F32) + b_ref[...]
        for j in range(CONV_WIDTH):
            phase = (base + j) % _SUBLANES
            row = r0 + base + j - phase
            win = u_ref[row:row + rows, :] if phase == 0 else sh_ref[phase - 1, row:row + rows, :]
            acc = acc + win.reshape(rows // _SUBLANES, _SUBLANES, c) * w_ref[j]
        acc = acc.reshape(rows, c)
        mu = jnp.mean(acc, axis=-1, keepdims=True)
        cen = acc - mu
        var = jnp.mean(cen * cen, axis=-1, keepdims=True)
        y = cen * lax.rsqrt(var + CONV_LN_EPS) * lg_ref[...] + lb_ref[...]
        o_ref[0, r0:r0 + rows, :] = _silu(y)
    u_ref[0:_CONV_HALO, :] = u_ref[tt:tt + _CONV_HALO, :]


def _conv(val, gate, conv_w, conv_b, ln_g, ln_b, tt=256, rows=32):
    b, t, c = val.shape
    wrep = jnp.broadcast_to(conv_w[:, None, :], (CONV_WIDTH, _SUBLANES, c))
    vec = lambda a: a.reshape(1, c)
    tile = pl.BlockSpec((1, tt, c), lambda i, j: (i, j, 0))
    full = lambda shape: pl.BlockSpec(shape, lambda i, j: (0,) * len(shape))
    return pl.pallas_call(
        functools.partial(_conv_kernel, tt=tt, rows=rows),
        grid=(b, t // tt),
        in_specs=[tile, tile, full((CONV_WIDTH, _SUBLANES, c)), full((1, c)), full((1, c)), full((1, c))],
        out_specs=tile,
        out_shape=jax.ShapeDtypeStruct((b, t, c), F32),
        scratch_shapes=[pltpu.VMEM((tt + _CONV_HALO, c), F32),
                        pltpu.VMEM((_SUBLANES - 1, tt + _CONV_HALO - _SUBLANES, c), F32)],
        compiler_params=_cparams("parallel", "arbitrary"),
        name="conformer_conv",
    )(val, gate, wrep, vec(conv_b), vec(ln_g), vec(ln_b))


def _gla_kernel(q_ref, k_ref, v_ref, go_ref, alr_ref, wa2_ref, ba_ref, ng_ref, o_ref, s_ref, *, tt):
    c = GLA_CHUNK

    @pl.when(pl.program_id(1) == 0)
    def _():
        s_ref[...] = jnp.zeros(s_ref.shape, F32)

    nc = tt // c
    tril = _tri(c)
    tril_f = tril.astype(F32)
    ones_cv = jnp.ones((c, GLA_DV), F32)
    units = [(ci, h) for ci in range(nc) for h in range(GLA_HEADS)]
    rows_of = lambda ci: slice(ci * c, (ci + 1) * c)
    keys_of = lambda h: slice(h * GLA_DK, (h + 1) * GLA_DK)
    vals_of = lambda h: slice(h * GLA_DV, (h + 1) * GLA_DV)

    z = _dot(alr_ref[0], wa2_ref[...]) + ba_ref[...]
    log_a = _log_sigmoid(z) * (1.0 / GLA_TAU)
    q = q_ref[0] * (GLA_DK ** -0.5)
    k = k_ref[0]
    q_decs, k_decs, k_ends, decays = [], [], [], []
    for ci in range(nc):
        la = log_a[rows_of(ci)]
        bcum = _dot_hi(tril_f, la)
        b_last = bcum[c - 1:c, :]
        q_decs.append(q[rows_of(ci)] * jnp.exp(bcum))
        k_decs.append(k[rows_of(ci)] * jnp.exp(-bcum))
        k_ends.append(k[rows_of(ci)] * jnp.exp(b_last - bcum))
        decays.append(jnp.exp(_dot_tn_hi(la, ones_cv)))

    vs = [v_ref[0, rows_of(ci), vals_of(h)] for ci, h in units]
    attns = [jnp.where(tril, _dot_nt(q_decs[ci][:, keys_of(h)], k_decs[ci][:, keys_of(h)]), 0.0)
             for ci, h in units]
    kvs = [_dot_tn(k_ends[ci][:, keys_of(h)], vs[u]) for u, (ci, h) in enumerate(units)]
    intras = [_dot(attns[u], vs[u]) for u in range(len(units))]
    states = [s_ref[h] for h in range(GLA_HEADS)]
    prevs = []
    for u, (ci, h) in enumerate(units):
        prevs.append(states[h])
        states[h] = states[h] * decays[ci][keys_of(h)] + kvs[u]
    for h in range(GLA_HEADS):
        s_ref[h] = states[h]
    inters = [_dot(q_decs[ci][:, keys_of(h)], prevs[u]) for u, (ci, h) in enumerate(units)]
    for ci in range(nc):
        outs = []
        for h in range(GLA_HEADS):
            u = ci * GLA_HEADS + h
            o = intras[u] + inters[u]
            outs.append(o * lax.rsqrt(jnp.mean(o * o, axis=-1, keepdims=True) + NORM_EPS))
        o_all = jnp.concatenate(outs, axis=-1)
        o_ref[0, rows_of(ci), :] = o_all * ng_ref[...] * _silu(go_ref[0, rows_of(ci), :])


def _gla(q, k, v, g_out, a_lr, w_a2, b_a, norm_g, tt=256):
    b, t, dq = q.shape
    dv = v.shape[-1]
    tile = lambda n: pl.BlockSpec((1, tt, n), lambda i, j: (i, j, 0))
    full = lambda shape: pl.BlockSpec(shape, lambda i, j: (0, 0))
    return pl.pallas_call(
        functools.partial(_gla_kernel, tt=tt),
        grid=(b, t // tt),
        in_specs=[tile(dq), tile(dq), tile(dv), tile(dv), tile(a_lr.shape[-1]),
                  full(w_a2.shape), full((1, dq)), full((1, dv))],
        out_specs=tile(dv),
        out_shape=jax.ShapeDtypeStruct((b, t, dv), F32),
        scratch_shapes=[pltpu.VMEM((GLA_HEADS, GLA_DK, GLA_DV), F32)],
        compiler_params=_cparams("parallel", "arbitrary"),
        name="gla",
    )(q, k, v, g_out, a_lr, w_a2, b_a.reshape(1, dq), norm_g.reshape(1, dv))


_RWKV_COLS = 3 * 512 + 64 + 64 + 128


def _rwkv_kernel(p_ref, mu_ref, w2_ref, w0_ref, a2_ref, a0_ref, g2_ref, kk_ref, ka_ref, rk_ref,
                 lg_ref, lb_ref, o_ref, xs_ref, st_ref, *, tt):
    c = RWKV_CHUNK
    hd = RWKV_HD
    w = RWKV_HEADS * hd

    @pl.when(pl.program_id(1) == 0)
    def _():
        xs_ref[0:8, :] = jnp.zeros((8, xs_ref.shape[1]), F32)
        st_ref[...] = jnp.zeros(st_ref.shape, F32)

    x = p_ref[0]
    xs_ref[8:8 + tt, :] = x
    prev = xs_ref[7:7 + tt, :]
    xs_ref[7:8, :] = x[tt - 1:tt, :]
    x = x + (prev - x) * mu_ref[...]

    r = x[:, 0:w]
    k = x[:, w:2 * w]
    v = x[:, 2 * w:3 * w]
    xw = x[:, 3 * w:3 * w + 64]
    xa = x[:, 3 * w + 64:3 * w + 128]
    xg = x[:, 3 * w + 128:3 * w + 256]
    log_w = -RWKV_DECAY_SCALE * _sigmoid(w0_ref[...] + _dot(jnp.tanh(xw), w2_ref[...]))
    a = _sigmoid(a0_ref[...] + _dot(xa, a2_ref[...]))
    g = _dot(_sigmoid(xg), g2_ref[...])
    kk = k * kk_ref[...]
    k_mod = k * (1.0 + (a - 1.0) * ka_ref[...])
    rkr = r * k_mod * rk_ref[...]

    tril = _tri(c)
    tril_s = _tri(c, strict=True)
    tril_f = tril.astype(F32)
    eye = (lax.broadcasted_iota(jnp.int32, (c, c), 0)
           == lax.broadcasted_iota(jnp.int32, (c, c), 1)).astype(F32)
    zeros_cc = jnp.zeros((c, hd), F32)

    nc = tt // c
    units = [(ci, h) for ci in range(nc) for h in range(RWKV_HEADS)]
    rows_of = lambda ci: slice(ci * c, (ci + 1) * c)
    lanes_of = lambda h: slice(h * hd, (h + 1) * hd)

    exps = []
    for ci in range(nc):
        lw_c = log_w[rows_of(ci)]
        lcum = _dot_hi(tril_f, lw_c)
        l_last = lcum[c - 1:c, :]
        exps.append((jnp.exp(lcum), jnp.exp(-lcum), jnp.exp(l_last - lcum), jnp.exp(lcum - lw_c),
                     jnp.exp(l_last)))

    aqs, rqs, vs, lhss, rhss, bkes = [], [], [], [], [], []
    for ci, h in units:
        rows, hs = rows_of(ci), lanes_of(h)
        e_pos, e_neg, e_end, e_prev, _ = exps[ci]
        kk_h = kk[rows, hs]
        nrm = jnp.sqrt(jnp.sum(kk_h * kk_h, axis=-1, keepdims=True))
        kk_h = kk_h / jnp.maximum(nrm, RWKV_L2_EPS)
        k_h = k_mod[rows, hs]
        kka = kk_h * a[rows, hs]
        aq = -kk_h * e_prev[:, hs]
        rq = r[rows, hs] * e_pos[:, hs]
        aqs.append(aq)
        rqs.append(rq)
        vs.append(v[rows, hs])
        lhss.append(jnp.concatenate([aq, rq], axis=0))
        rhss.append(jnp.concatenate([kka * e_neg[:, hs], k_h * e_neg[:, hs]], axis=0))
        bkes.append(jnp.concatenate([kka * e_end[:, hs], k_h * e_end[:, hs]], axis=0))

    xss = [_dot_nt(lhs, rhs) for lhs, rhs in zip(lhss, rhss)]
    m1s = [jnp.where(tril_s, xs[:c, :c], 0.0) for xs in xss]
    m2s = [jnp.where(tril_s, xs[:c, c:], 0.0) for xs in xss]
    n12s = [jnp.concatenate([jnp.where(tril, xs[c:, :c], 0.0), jnp.where(tril, xs[c:, c:], 0.0)], axis=1)
            for xs in xss]
    tinvs = [eye + m1 for m1 in m1s]
    mps = m1s
    for _ in range(5):
        mps = [_dot(mp, mp) for mp in mps]
        tinvs = [tinv + _dot(tinv, mp) for tinv, mp in zip(tinvs, mps)]
    mvs = [_dot(m2, v_h) for m2, v_h in zip(m2s, vs)]
    pqs = [_dot(tinv, jnp.concatenate([mv, aq], axis=1)) for tinv, mv, aq in zip(tinvs, mvs, aqs)]
    pqvs = [jnp.concatenate([pq, jnp.concatenate([v_h, zeros_cc], axis=1)], axis=0)
            for pq, v_h in zip(pqs, vs)]
    yys = [_dot(n12, pqv) for n12, pqv in zip(n12s, pqvs)]
    ghs = [_dot_tn(bke, pqv) for bke, pqv in zip(bkes, pqvs)]

    states = [st_ref[h] for h in range(RWKV_HEADS)]
    ys = []
    for u, (ci, h) in enumerate(units):
        yq = rqs[u] + yys[u][:, hd:]
        gmat = eye * exps[ci][4][:, lanes_of(h)] + ghs[u][:, hd:]
        res = _dot(jnp.concatenate([yq, gmat], axis=0), states[h])
        ys.append(res[:c] + yys[u][:, :hd])
        states[h] = res[c:] + ghs[u][:, :hd]
    for h in range(RWKV_HEADS):
        st_ref[h] = states[h]

    out_rows = []
    for ci in range(nc):
        yns, bonuses = [], []
        for h in range(RWKV_HEADS):
            u = ci * RWKV_HEADS + h
            y = ys[u]
            cen = y - jnp.mean(y, axis=-1, keepdims=True)
            var = jnp.mean(cen * cen, axis=-1, keepdims=True)
            yns.append(cen * lax.rsqrt(var + RWKV_GN_EPS))
            bonuses.append(jnp.sum(rkr[rows_of(ci), lanes_of(h)], axis=-1, keepdims=True) * vs[u])
        yn = jnp.concatenate(yns, axis=1)
        bonus = jnp.concatenate(bonuses, axis=1)
        out_rows.append((yn * lg_ref[...] + lb_ref[...] + bonus) * g[rows_of(ci)])
    o_ref[0] = jnp.concatenate(out_rows, axis=0)


def _rwkv(p, mu, w_w2, w0, w_a2, a0, w_g2, k_k, k_a, r_k, lnx_g, lnx_b, tt=128):
    b, t, cols = p.shape
    w = RWKV_HEADS * RWKV_HD
    vec = lambda a_: a_.reshape(1, -1)
    full = lambda shape: pl.BlockSpec(shape, lambda i, j: (0, 0))
    return pl.pallas_call(
        functools.partial(_rwkv_kernel, tt=tt),
        grid=(b, t // tt),
        in_specs=[pl.BlockSpec((1, tt, cols), lambda i, j: (i, j, 0)), full((1, cols)),
                  full(w_w2.shape), full((1, w)), full(w_a2.shape), full((1, w)), full(w_g2.shape),
                  full((1, w)), full((1, w)), full((1, w)), full((1, w)), full((1, w))],
        out_specs=pl.BlockSpec((1, tt, w), lambda i, j: (i, j, 0)),
        out_shape=jax.ShapeDtypeStruct((b, t, w), F32),
        scratch_shapes=[pltpu.VMEM((8 + tt, cols), F32),
                        pltpu.VMEM((RWKV_HEADS, RWKV_HD, RWKV_HD), F32)],
        compiler_params=_cparams("parallel", "arbitrary"),
        name="rwkv7",
    )(p, vec(mu), w_w2, vec(w0), w_a2, vec(a0), w_g2, vec(k_k), vec(k_a), vec(r_k), vec(lnx_g), vec(lnx_b))


def _fox_cum_kernel(f_ref, bf_ref, col_ref, carry_ref, *, tt):
    @pl.when(pl.program_id(1) == 0)
    def _():
        carry_ref[...] = jnp.zeros(carry_ref.shape, F32)

    log_f = _log_sigmoid(f_ref[0] + bf_ref[...])
    cum = _dot_hi(_tri(tt).astype(F32), log_f) + carry_ref[0:1, :]
    col_ref[0] = cum
    carry_ref[...] = jnp.broadcast_to(cum[tt - 1:tt, :], carry_ref.shape)


def _fox_cum(f_logit, b_f, tt=256):
    b, t, n = f_logit.shape
    return pl.pallas_call(
        functools.partial(_fox_cum_kernel, tt=tt),
        grid=(b, t // tt),
        in_specs=[pl.BlockSpec((1, tt, n), lambda i, j: (i, j, 0)), pl.BlockSpec((1, n), lambda i, j: (0, 0))],
        out_specs=pl.BlockSpec((1, tt, n), lambda i, j: (i, j, 0)),
        out_shape=jax.ShapeDtypeStruct((b, t, n), F32),
        scratch_shapes=[pltpu.VMEM((8, n), F32)],
        compiler_params=_cparams("parallel", "arbitrary"),
        name="fox_cumsum",
    )(f_logit, b_f)


def _split3(x):
    hi = x.astype(BF16).astype(F32)
    mid = (x - hi).astype(BF16).astype(F32)
    lo = (x - hi - mid).astype(BF16).astype(F32)
    return hi, mid, lo


def _fox_kernel(q_ref, k_ref, v_ref, c_ref, o_ref, kaug_ref, vaug_ref, *, tq, groups):
    qi = pl.program_id(2)
    d = FOX_HD
    per = LANE // d
    nh = groups * per
    head0 = pl.program_id(1) * nh

    def augment(x_h, c, first):
        lane = lax.broadcasted_iota(jnp.int32, x_h.shape, 1)
        hi, mid, lo = _split3(c)
        parts = (hi, mid, lo, 1.0, 1.0, 1.0) if first else (1.0, 1.0, 1.0, -hi, -mid, -lo)
        out = jnp.where(lane < d, x_h, 0.0)
        for n, part in enumerate(parts):
            out = jnp.where(lane == d + n, part, out)
        return out.astype(BF16)

    def head_col(c_all, head):
        lane = lax.broadcasted_iota(jnp.int32, c_all.shape, 1)
        return jnp.sum(jnp.where(lane == head, c_all, 0.0), axis=-1, keepdims=True)

    def head_lanes(x, h):
        x_g = x[:, (h // per) * LANE:(h // per + 1) * LANE]
        return x_g if h % per == 0 else pltpu.roll(x_g, LANE - (h % per) * d, axis=1)

    @pl.when(qi == 0)
    def _():
        lane_t = lax.broadcasted_iota(jnp.int32, (k_ref.shape[1], LANE), 1)
        for h in range(nh):
            kaug_ref[h] = augment(head_lanes(k_ref[0], h), head_col(c_ref[0], head0 + h), first=False)
            v_h = jnp.where(lane_t < d, head_lanes(v_ref[0], h), jnp.where(lane_t == d, 1.0, 0.0))
            vaug_ref[h] = v_h.astype(BF16)

    row0 = pl.multiple_of(qi * tq, tq)
    q = q_ref[0] * (d ** -0.5)
    c_q = c_ref[0, pl.ds(row0, tq), :]
    qs = [augment(head_lanes(q, h), head_col(c_q, head0 + h), first=True) for h in range(nh)]
    causal = _tri(tq)
    lag = 2

    def step(j, carry, diagonal):
        start = pl.multiple_of(j * tq, tq)
        ss, new = {}, []
        for h in range(nh + lag):
            if h < nh:
                ss[h] = lax.dot_general(qs[h], kaug_ref[h, pl.ds(start, tq), :], (((1,), (1,)), ((), ())),
                                        preferred_element_type=F32)
            g = h - lag
            if g >= 0:
                m, acc = carry[g]
                s = ss.pop(g)
                s = jnp.where(causal, s, -jnp.inf) if diagonal else s
                m_new = jnp.maximum(m, jnp.max(s, axis=-1, keepdims=True))
                p = jnp.exp(s - m_new).astype(BF16)
                pv = jnp.dot(p, vaug_ref[g, pl.ds(start, tq), :], preferred_element_type=F32)
                new.append((m_new, acc * jnp.exp(m - m_new) + pv))
        return tuple(new)

    init = tuple((jnp.full((tq, 1), -jnp.inf, F32), jnp.zeros((tq, LANE), F32)) for _ in range(nh))
    carry = lax.fori_loop(0, qi, lambda j, cr: step(j, cr, False), init)
    carry = step(qi, carry, True)
    lane = lax.broadcasted_iota(jnp.int32, (tq, LANE), 1)
    for grp in range(groups):
        out = None
        for hh in range(per):
            acc = carry[grp * per + hh][1]
            o_h = acc / jnp.sum(jnp.where(lane == d, acc, 0.0), axis=-1, keepdims=True)
            o_h = o_h if hh == 0 else pltpu.roll(o_h, hh * d, axis=1)
            out = o_h if out is None else jnp.where(lane >= hh * d, o_h, out)
        o_ref[0, :, grp * LANE:(grp + 1) * LANE] = out


def _fox(q, k, v, c_col, tq=256, groups=4):
    b, t, w = q.shape
    gw = groups * LANE
    return pl.pallas_call(
        functools.partial(_fox_kernel, tq=tq, groups=groups),
        grid=(b, w // gw, t // tq),
        in_specs=[pl.BlockSpec((1, tq, gw), lambda i, p, j: (i, j, p)),
                  pl.BlockSpec((1, t, gw), lambda i, p, j: (i, 0, p)),
                  pl.BlockSpec((1, t, gw), lambda i, p, j: (i, 0, p)),
                  pl.BlockSpec((1, t, LANE), lambda i, p, j: (i, 0, 0))],
        out_specs=pl.BlockSpec((1, tq, gw), lambda i, p, j: (i, j, p)),
        out_shape=jax.ShapeDtypeStruct((b, t, w), F32),
        scratch_shapes=[pltpu.VMEM((groups * LANE // FOX_HD, t, LANE), BF16),
                        pltpu.VMEM((groups * LANE // FOX_HD, t, LANE), BF16)],
        compiler_params=_cparams("parallel", "arbitrary", "arbitrary"),
        name="fox_attention",
    )(q, k, v, c_col)


def _mix_residual(h_ref, ya_ref, yb_ref, wa_ref, wb_ref):
    return h_ref[...] + _dot(ya_ref[...], wa_ref[...]) + _dot(yb_ref[...], wb_ref[...])


def _ffn_kernel(h_ref, ya_ref, yb_ref, wa_ref, wb_ref, g_ref, w1_ref, w3_ref, w2_ref, o_ref, xn_ref, acc_ref):
    f = pl.program_id(1)

    @pl.when(f == 0)
    def _():
        hn = _mix_residual(h_ref, ya_ref, yb_ref, wa_ref, wb_ref)
        xn_ref[...] = _rmsnorm(hn, g_ref[...]).astype(BF16)
        acc_ref[...] = hn

    xn = xn_ref[...]
    mid = _silu(jnp.dot(xn, w1_ref[...], preferred_element_type=F32)) * jnp.dot(
        xn, w3_ref[...], preferred_element_type=F32)
    acc_ref[...] += _dot(mid, w2_ref[...])

    @pl.when(f == pl.num_programs(1) - 1)
    def _():
        o_ref[...] = acc_ref[...]


def _ffn(h, ya, yb, wa, wb, g, w1, w3, w2, tm=512, tf=1408):
    m, d = h.shape
    k = ya.shape[1]
    nf = w1.shape[1]
    rows = lambda n: pl.BlockSpec((tm, n), lambda i, f: (i, 0))
    full = lambda shape: pl.BlockSpec(shape, lambda i, f: (0, 0))
    return pl.pallas_call(
        _ffn_kernel,
        grid=(m // tm, nf // tf),
        in_specs=[rows(d), rows(k), rows(k), full((k, d)), full((k, d)), full((1, d)),
                  pl.BlockSpec((d, tf), lambda i, f: (0, f)),
                  pl.BlockSpec((d, tf), lambda i, f: (0, f)),
                  pl.BlockSpec((tf, d), lambda i, f: (f, 0))],
        out_specs=rows(d),
        out_shape=jax.ShapeDtypeStruct((m, d), F32),
        scratch_shapes=[pltpu.VMEM((tm, d), BF16), pltpu.VMEM((tm, d), F32)],
        compiler_params=_cparams("parallel", "arbitrary"),
        name="ffn_dense",
    )(h, ya, yb, wa, wb, g.reshape(1, d), w1, w3, w2)


def _moe_route_kernel(h_ref, ya_ref, yb_ref, wa_ref, wb_ref, g_ref, r_ref, hn_ref, xn_ref, meta_ref, cnt_ref,
                      carry_ref):
    @pl.when(pl.program_id(0) == 0)
    def _():
        carry_ref[...] = jnp.zeros(carry_ref.shape, F32)

    hn = _mix_residual(h_ref, ya_ref, yb_ref, wa_ref, wb_ref)
    hn_ref[...] = hn
    xn = _rmsnorm(hn, g_ref[...])
    xn_ref[...] = xn
    tm = xn.shape[0]
    lane = lax.broadcasted_iota(jnp.int32, (tm, LANE), 1)
    logits = jnp.where(lane < N_EXPERTS, _dot_hi(xn, r_ref[...]), -jnp.inf)
    m1 = jnp.max(logits, axis=-1, keepdims=True)
    i1 = jnp.min(jnp.where(logits == m1, lane, LANE), axis=-1, keepdims=True)
    rest = jnp.where(lane == i1, -jnp.inf, logits)
    m2 = jnp.max(rest, axis=-1, keepdims=True)
    i2 = jnp.min(jnp.where(rest == m2, lane, LANE), axis=-1, keepdims=True)
    e2 = jnp.exp(m2 - m1)
    g1 = 1.0 / (1.0 + e2)
    g2 = e2 / (1.0 + e2)
    onehot = jnp.where(lane == i1, 1.0, jnp.where(lane == i2, 1.0, 0.0))
    before = _dot(_tri(tm, strict=True).astype(F32), onehot) + carry_ref[0:1, :]
    r1 = jnp.sum(jnp.where(lane == i1, before, 0.0), axis=-1, keepdims=True)
    r2 = jnp.sum(jnp.where(lane == i2, before, 0.0), axis=-1, keepdims=True)
    meta = jnp.where(lane == 0, i1.astype(F32), jnp.where(lane == 1, i2.astype(F32), 0.0))
    meta = jnp.where(lane == 2, g1, jnp.where(lane == 3, g2, meta))
    meta_ref[...] = jnp.where(lane == 4, r1, jnp.where(lane == 5, r2, meta))
    total = carry_ref[0:1, :] + jnp.sum(onehot, axis=0, keepdims=True)
    carry_ref[...] = jnp.broadcast_to(total, carry_ref.shape)
    cnt_ref[...] = jnp.broadcast_to(total, cnt_ref.shape)


def _moe_route(h, ya, yb, wa, wb, g, router, tm=512):
    m, d = h.shape
    k = ya.shape[1]
    rows = lambda n: pl.BlockSpec((tm, n), lambda i: (i, 0))
    full = lambda shape: pl.BlockSpec(shape, lambda i: (0, 0))
    return pl.pallas_call(
        _moe_route_kernel,
        grid=(m // tm,),
        in_specs=[rows(d), rows(k), rows(k), full((k, d)), full((k, d)), full((1, d)), full((d, LANE))],
        out_specs=[rows(d), rows(d), rows(LANE), full((8, LANE))],
        out_shape=[jax.ShapeDtypeStruct((m, d), F32), jax.ShapeDtypeStruct((m, d), F32),
                   jax.ShapeDtypeStruct((m, LANE), F32), jax.ShapeDtypeStruct((8, LANE), F32)],
        scratch_shapes=[pltpu.VMEM((8, LANE), F32)],
        compiler_params=_cparams("arbitrary"),
        name="moe_route",
    )(h, ya, yb, wa, wb, g.reshape(1, d), router)


def _row_copy(src_ref, src_group, src_sub, dst_ref, dst_group, dst_sub, sem):
    return pltpu.make_async_copy(src_ref.at[src_group, pl.ds(src_sub, 1)],
                                 dst_ref.at[dst_group, pl.ds(dst_sub, 1)], sem)


def _split_row(p):
    return lax.shift_right_logical(p, 3), lax.bitwise_and(p, _SUBLANES - 1)


def _moe_dispatch_kernel(pos_ref, xn_ref, xs_in_ref, xs_ref, sem, *, tm):
    del xs_in_ref

    def start(grp, carry):
        for u in range(_SUBLANES):
            for s in range(2):
                dst_group, dst_sub = _split_row(pos_ref[0, 0, 2 * (grp * _SUBLANES + u) + s])
                _row_copy(xn_ref, grp, u, xs_ref, dst_group, dst_sub, sem).start(priority=s)
        return carry

    def wait(grp, carry):
        for _ in range(2 * _SUBLANES):
            _row_copy(xn_ref, 0, 0, xs_ref, 0, 0, sem).wait()
        return carry

    lax.fori_loop(0, tm // _SUBLANES, start, 0)
    lax.fori_loop(0, tm // _SUBLANES, wait, 0)


def _moe_dispatch(xn, pos, n_rows, tm=256):
    m, d = xn.shape
    xs = pl.pallas_call(
        functools.partial(_moe_dispatch_kernel, tm=tm),
        grid=(m // tm,),
        in_specs=[pl.BlockSpec((1, 1, 2 * tm), lambda i: (i, 0, 0), memory_space=pltpu.SMEM),
                  pl.BlockSpec((tm // _SUBLANES, _SUBLANES, d), lambda i: (i, 0, 0)),
                  pl.BlockSpec(memory_space=pl.ANY)],
        out_specs=pl.BlockSpec(memory_space=pl.ANY),
        out_shape=jax.ShapeDtypeStruct((n_rows // _SUBLANES, _SUBLANES, d), F32),
        scratch_shapes=[pltpu.SemaphoreType.DMA(())],
        input_output_aliases={2: 0},
        compiler_params=_cparams("arbitrary"),
        name="moe_dispatch",
    )(pos.reshape(m // tm, 1, 2 * tm), xn.reshape(m // _SUBLANES, _SUBLANES, d),
      jnp.zeros((n_rows // _SUBLANES, _SUBLANES, d), F32))
    return xs.reshape(n_rows, d)


def _moe_expert_kernel(te_ref, nu_ref, x_ref, w1_ref, w3_ref, w2_ref, y_ref, xbf_ref, acc_ref):
    del te_ref
    f = pl.program_id(1)

    @pl.when(f == 0)
    def _():
        xbf_ref[...] = x_ref[...].astype(BF16)
        acc_ref[...] = jnp.zeros(acc_ref.shape, F32)

    @pl.when(pl.program_id(0) < nu_ref[0])
    def _():
        xb = xbf_ref[...]
        mid = _silu(jnp.dot(xb, w1_ref[0], preferred_element_type=F32)) * jnp.dot(
            xb, w3_ref[0], preferred_element_type=F32)
        acc_ref[...] += _dot(mid, w2_ref[0])

    @pl.when(f == pl.num_programs(1) - 1)
    def _():
        y_ref[...] = acc_ref[...]


def _moe_experts(xs, tile_expert, n_used, w1, w3, w2, tmx, tf=1792):
    n_rows, d = xs.shape
    nf = w1.shape[2] // tf
    fidx = lambda i, f, nu: jnp.where(i < nu[0], f, nf - 1)
    grid_spec = pltpu.PrefetchScalarGridSpec(
        num_scalar_prefetch=2,
        grid=(n_rows // tmx, nf),
        in_specs=[pl.BlockSpec((tmx, d), lambda i, f, te, nu: (i, 0)),
                  pl.BlockSpec((1, d, tf), lambda i, f, te, nu: (te[i], 0, fidx(i, f, nu))),
                  pl.BlockSpec((1, d, tf), lambda i, f, te, nu: (te[i], 0, fidx(i, f, nu))),
                  pl.BlockSpec((1, tf, d), lambda i, f, te, nu: (te[i], fidx(i, f, nu), 0))],
        out_specs=pl.BlockSpec((tmx, d), lambda i, f, te, nu: (i, 0)),
        scratch_shapes=[pltpu.VMEM((tmx, d), BF16), pltpu.VMEM((tmx, d), F32)],
    )
    return pl.pallas_call(
        _moe_expert_kernel,
        grid_spec=grid_spec,
        out_shape=jax.ShapeDtypeStruct((n_rows, d), F32),
        compiler_params=_cparams("arbitrary", "arbitrary"),
        name="moe_experts",
    )(tile_expert, n_used, xs, w1, w3, w2)


def _moe_combine_kernel(pos_ref, posn_ref, h_ref, meta_ref, fg_ref, ys_ref, o_ref, ybuf_ref, sem, *, tm):
    i = pl.program_id(0)
    slot = lax.rem(i, 2)

    groups = tm // _SUBLANES

    def gather(p_ref, dst_slot):
        def start(grp, carry):
            for u in range(_SUBLANES):
                for s in range(2):
                    src_group, src_sub = _split_row(p_ref[0, 0, 2 * (grp * _SUBLANES + u) + s])
                    _row_copy(ys_ref, src_group, src_sub, ybuf_ref.at[dst_slot], s * groups + grp, u,
                              sem.at[dst_slot]).start(priority=s)
            return carry
        lax.fori_loop(0, groups, start, 0)

    @pl.when(i == 0)
    def _():
        gather(pos_ref, 0)

    @pl.when(i + 1 < pl.num_programs(0))
    def _():
        gather(posn_ref, 1 - slot)

    def wait(grp, carry):
        for _ in range(2 * _SUBLANES):
            _row_copy(ys_ref, 0, 0, ybuf_ref.at[slot], 0, 0, sem.at[slot]).wait()
        return carry

    lax.fori_loop(0, groups, wait, 0)
    lane = lax.broadcasted_iota(jnp.int32, (tm, LANE), 1)
    meta = meta_ref[...]
    g1 = jnp.sum(jnp.where(lane == 2, meta, 0.0), axis=-1, keepdims=True)
    g2 = jnp.sum(jnp.where(lane == 3, meta, 0.0), axis=-1, keepdims=True)
    y = ybuf_ref[slot].reshape(2 * tm, -1)
    o_ref[...] = _rmsnorm(h_ref[...] + g1 * y[:tm] + g2 * y[tm:], fg_ref[...])


def _moe_combine(h, meta, pos, ys, final_g, tm=256):
    m, d = h.shape
    nt = m // tm
    pos3 = pos.reshape(nt, 1, 2 * tm)
    smem = lambda imap: pl.BlockSpec((1, 1, 2 * tm), imap, memory_space=pltpu.SMEM)
    return pl.pallas_call(
        functools.partial(_moe_combine_kernel, tm=tm),
        grid=(nt,),
        in_specs=[smem(lambda i: (i, 0, 0)), smem(lambda i: (jnp.minimum(i + 1, nt - 1), 0, 0)),
                  pl.BlockSpec((tm, d), lambda i: (i, 0)), pl.BlockSpec((tm, LANE), lambda i: (i, 0)),
                  pl.BlockSpec((1, d), lambda i: (0, 0)), pl.BlockSpec(memory_space=pl.ANY)],
        out_specs=pl.BlockSpec((tm, d), lambda i: (i, 0)),
        out_shape=jax.ShapeDtypeStruct((m, d), F32),
        scratch_shapes=[pltpu.VMEM((2, 2 * tm // _SUBLANES, _SUBLANES, d), F32), pltpu.SemaphoreType.DMA((2,))],
        compiler_params=_cparams("arbitrary"),
        name="moe_combine",
    )(pos3, pos3, h, meta, final_g.reshape(1, d), ys.reshape(-1, _SUBLANES, d))


def _moe_routed(h, ya, yb, wa, wb, g, router, w1, w3, w2, final_g, tmx=512):
    m, d = h.shape
    h, xn, meta, cnt = _moe_route(h, ya, yb, wa, wb, g, router)
    experts = meta[:, 0:2].astype(jnp.int32)
    rank = meta[:, 4:6].astype(jnp.int32)
    counts = cnt[0, :N_EXPERTS].astype(jnp.int32)
    padded = (counts + tmx - 1) // tmx * tmx
    ends = jnp.cumsum(padded)
    pos = (ends - padded)[experts] + rank
    n_tiles = (2 * m) // tmx + N_EXPERTS
    n_used = ends[-1] // tmx
    tile_start = jnp.minimum(jnp.arange(n_tiles, dtype=jnp.int32), n_used - 1) * tmx
    tile_expert = jnp.sum(tile_start[:, None] >= ends[None, :], axis=1).astype(jnp.int32)
    xs = _moe_dispatch(xn, pos, n_tiles * tmx)
    ys = _moe_experts(xs, tile_expert, n_used.reshape(1).astype(jnp.int32), w1, w3, w2, tmx)
    return _moe_combine(h, meta, pos, ys, final_g)


def _pad_cols(w, n):
    return jnp.pad(w, ((0, 0), (0, n - w.shape[1])))


def kernel(x, norm_mix_g, norm_ffn_g, w_in0, conv_w, conv_b, conv_ln_g, conv_ln_b, gla_w_a2, gla_b_a, gla_norm_g, w_out0, ffn_w1, ffn_w3, ffn_w2, w_in1, rwkv_mu, rwkv_w2, rwkv_w0, rwkv_a2, rwkv_a0, rwkv_g2, rwkv_k_k, rwkv_k_a, rwkv_r_k, rwkv_lnx_g, rwkv_lnx_b, fox_b_f, w_out1, moe_router, moe_w1, moe_w3, moe_w2, final_norm_g):
    b, t, d = x.shape
    m = b * t
    half = d // 2
    bf = lambda a: a.astype(BF16)
    h = x.reshape(m, d)

    gk = GLA_HEADS * GLA_DK
    o = [0, half, 2 * half, 2 * half + gk, 2 * half + 2 * gk, 3 * half + 2 * gk, 4 * half + 2 * gk]
    ws0 = [bf(w_in0[:, o[i]:o[i + 1]]) for i in range(6)] + [bf(_pad_cols(w_in0[:, o[6]:], LANE))]
    c_val, c_gate, q, k, v, g_out, a_lr = [
        a.reshape(b, t, -1) for a in _norm_proj(h, norm_mix_g[0], ws0)]
    y_conv = _conv(c_val, c_gate, conv_w, conv_b, conv_ln_g, conv_ln_b)
    w_a2 = jnp.pad(gla_w_a2, ((0, LANE - gla_w_a2.shape[0]), (0, 0)))
    y_gla = _gla(q, k, v, g_out, a_lr, w_a2, gla_b_a, gla_norm_g)
    h = _ffn(h, y_conv.reshape(m, half), y_gla.reshape(m, half), bf(w_out0[:half]), bf(w_out0[half:]),
             norm_ffn_g[0], bf(ffn_w1), bf(ffn_w3), bf(ffn_w2))

    rc = _RWKV_COLS
    ws1 = [bf(w_in1[:, :rc]), bf(w_in1[:, rc:rc + half]), bf(w_in1[:, rc + half:rc + 2 * half]),
           bf(w_in1[:, rc + 2 * half:rc + 3 * half]), bf(_pad_cols(w_in1[:, rc + 3 * half:], LANE))]
    p_rwkv, fq, fk, fv, f_logit = [a.reshape(b, t, -1) for a in _norm_proj(h, norm_mix_g[1], ws1)]
    y_rwkv = _rwkv(p_rwkv, rwkv_mu, rwkv_w2, rwkv_w0, rwkv_a2, rwkv_a0, rwkv_g2, rwkv_k_k, rwkv_k_a,
                   rwkv_r_k, rwkv_lnx_g, rwkv_lnx_b)
    b_f = jnp.pad(fox_b_f, (0, LANE - FOX_HEADS)).reshape(1, LANE)
    y_fox = _fox(fq, fk, fv, _fox_cum(f_logit, b_f))
    out = _moe_routed(h, y_rwkv.reshape(m, half), y_fox.reshape(m, half), bf(w_out1[:half]), bf(w_out1[half:]),
                      norm_ffn_g[1], _pad_cols(moe_router, LANE), bf(moe_w1), bf(moe_w3), bf(moe_w2),
                      final_norm_g)
    return out.reshape(b, t, d)
```

```python
import functools

import jax
import jax.numpy as jnp
from jax import lax
from jax.experimental import pallas as pl
from jax.experimental.pallas import tpu as pltpu

F32 = jnp.float32
BF16 = jnp.bfloat16
HI = lax.Precision.HIGHEST

NORM_EPS = 1e-6
CONV_WIDTH = 31
CONV_LN_EPS = 1e-5
GLA_HEADS = 4
GLA_DK = 64
GLA_DV = 128
GLA_TAU = 16.0
GLA_CHUNK = 64
RWKV_HEADS = 8
RWKV_HD = 64
RWKV_CHUNK = 64
RWKV_DECAY_SCALE = 0.606531
RWKV_GN_EPS = 64e-5
RWKV_L2_EPS = 1e-12
FOX_HEADS = 8
FOX_HD = 64
N_EXPERTS = 8
LANE = 128
_SUBLANES = 8
VMEM_LIMIT = 56 * 1024 * 1024


def _cparams(*sem):
    return pltpu.CompilerParams(dimension_semantics=sem, vmem_limit_bytes=VMEM_LIMIT)


def _dot(a, b):
    return jnp.dot(a.astype(BF16), b.astype(BF16), preferred_element_type=F32)


def _dot_nt(a, b):
    return lax.dot_general(a.astype(BF16), b.astype(BF16), (((1,), (1,)), ((), ())),
                           preferred_element_type=F32)


def _dot_tn(a, b):
    return lax.dot_general(a.astype(BF16), b.astype(BF16), (((0,), (0,)), ((), ())),
                           preferred_element_type=F32)


def _dot_hi(a, b):
    return jnp.dot(a, b, precision=HI, preferred_element_type=F32)


def _dot_tn_hi(a, b):
    return lax.dot_general(a, b, (((0,), (0,)), ((), ())), precision=HI, preferred_element_type=F32)


def _sigmoid(x):
    return 1.0 / (1.0 + jnp.exp(-x))


def _silu(x):
    return x * _sigmoid(x)


def _log_sigmoid(x):
    return jnp.minimum(x, 0.0) - jnp.log(1.0 + jnp.exp(-jnp.abs(x)))


def _rmsnorm(x, g):
    return x * lax.rsqrt(jnp.mean(x * x, axis=-1, keepdims=True) + NORM_EPS) * g


def _tri(n, strict=False):
    r = lax.broadcasted_iota(jnp.int32, (n, n), 0)
    c = lax.broadcasted_iota(jnp.int32, (n, n), 1)
    return (r > c) if strict else (r >= c)


def _norm_proj_kernel(h_ref, g_ref, *refs):
    n = len(refs) // 2
    xn = _rmsnorm(h_ref[...], g_ref[...]).astype(BF16)
    for w_ref, o_ref in zip(refs[:n], refs[n:]):
        o_ref[...] = jnp.dot(xn, w_ref[...], preferred_element_type=F32)


def _norm_proj(h, g, ws, tm=512):
    m, d = h.shape
    return pl.pallas_call(
        _norm_proj_kernel,
        grid=(m // tm,),
        in_specs=[pl.BlockSpec((tm, d), lambda i: (i, 0)), pl.BlockSpec((1, d), lambda i: (0, 0))]
        + [pl.BlockSpec(w.shape, lambda i: (0, 0)) for w in ws],
        out_specs=[pl.BlockSpec((tm, w.shape[1]), lambda i: (i, 0)) for w in ws],
        out_shape=[jax.ShapeDtypeStruct((m, w.shape[1]), F32) for w in ws],
        compiler_params=_cparams("parallel"),
        name="norm_proj",
    )(h, g.reshape(1, d), *ws)


_CONV_HALO = 32


def _conv_kernel(val_ref, gate_ref, w_ref, b_ref, lg_ref, lb_ref, o_ref, u_ref, sh_ref, *, tt, rows):
    @pl.when(pl.program_id(1) == 0)
    def _():
        u_ref[0:_CONV_HALO, :] = jnp.zeros((_CONV_HALO, u_ref.shape[1]), F32)

    u_ref[_CONV_HALO:_CONV_HALO + tt, :] = val_ref[0] * _sigmoid(gate_ref[0])
    span = sh_ref.shape[1]
    for s in range(1, _SUBLANES):
        sh_ref[s - 1] = u_ref[s:s + span, :]
    base = _CONV_HALO - (CONV_WIDTH - 1)
    c = u_ref.shape[1]
    for r0 in range(0, tt, rows):
        acc = jnp.zeros((rows // _SUBLANES, _SUBLANES, c), F32) + b_ref[...]
        for j in range(CONV_WIDTH):
            phase = (base + j) % _SUBLANES
            row = r0 + base + j - phase
            win = u_ref[row:row + rows, :] if phase == 0 else sh_ref[phase - 1, row:row + rows, :]
            acc = acc + win.reshape(rows // _SUBLANES, _SUBLANES, c) * w_ref[j]
        acc = acc.reshape(rows, c)
        mu = jnp.mean(acc, axis=-1, keepdims=True)
        cen = acc - mu
        var = jnp.mean(cen * cen, axis=-1, keepdims=True)
        y = cen * lax.rsqrt(var + CONV_LN_EPS) * lg_ref[...] + lb_ref[...]
        o_ref[0, r0:r0 + rows, :] = _silu(y)
    u_ref[0:_CONV_HALO, :] = u_ref[tt:tt + _CONV_HALO, :]


def _conv(val, gate, conv_w, conv_b, ln_g, ln_b, tt=256, rows=32):
    b, t, c = val.shape
    wrep = jnp.broadcast_to(conv_w[:, None, :], (CONV_WIDTH, _SUBLANES, c))
    vec = lambda a: a.reshape(1, c)
    tile = pl.BlockSpec((1, tt, c), lambda i, j: (i, j, 0))
    full = lambda shape: pl.BlockSpec(shape, lambda i, j: (0,) * len(shape))
    return pl.pallas_call(
        functools.partial(_conv_kernel, tt=tt, rows=rows),
        grid=(b, t // tt),
        in_specs=[tile, tile, full((CONV_WIDTH, _SUBLANES, c)), full((1, c)), full((1, c)), full((1, c))],
        out_specs=tile,
        out_shape=jax.ShapeDtypeStruct((b, t, c), F32),
        scratch_shapes=[pltpu.VMEM((tt + _CONV_HALO, c), F32),
                        pltpu.VMEM((_SUBLANES - 1, tt + _CONV_HALO - _SUBLANES, c), F32)],
        compiler_params=_cparams("parallel", "arbitrary"),
        name="conformer_conv",
    )(val, gate, wrep, vec(conv_b), vec(ln_g), vec(ln_b))


def _gla_kernel(q_ref, k_ref, v_ref, go_ref, alr_ref, wa2_ref, ba_ref, ng_ref, o_ref, s_ref, *, tt):
    c = GLA_CHUNK

    @pl.when(pl.program_id(1) == 0)
    def _():
        s_ref[...] = jnp.zeros(s_ref.shape, F32)

    nc = tt // c
    tril = _tri(c)
    tril_f = tril.astype(F32)
    ones_cv = jnp.ones((c, GLA_DV), F32)
    units = [(ci, h) for ci in range(nc) for h in range(GLA_HEADS)]
    rows_of = lambda ci: slice(ci * c, (ci + 1) * c)
    keys_of = lambda h: slice(h * GLA_DK, (h + 1) * GLA_DK)
    vals_of = lambda h: slice(h * GLA_DV, (h + 1) * GLA_DV)

    z = _dot(alr_ref[0], wa2_ref[...]) + ba_ref[...]
    log_a = _log_sigmoid(z) * (1.0 / GLA_TAU)
    q = q_ref[0] * (GLA_DK ** -0.5)
    k = k_ref[0]
    q_decs, k_decs, k_ends, decays = [], [], [], []
    for ci in range(nc):
        la = log_a[rows_of(ci)]
        bcum = _dot_hi(tril_f, la)
        b_last = bcum[c - 1:c, :]
        q_decs.append(q[rows_of(ci)] * jnp.exp(bcum))
        k_decs.append(k[rows_of(ci)] * jnp.exp(-bcum))
        k_ends.append(k[rows_of(ci)] * jnp.exp(b_last - bcum))
        decays.append(jnp.exp(_dot_tn_hi(la, ones_cv)))

    vs = [v_ref[0, rows_of(ci), vals_of(h)] for ci, h in units]
    attns = [jnp.where(tril, _dot_nt(q_decs[ci][:, keys_of(h)], k_decs[ci][:, keys_of(h)]), 0.0)
             for ci, h in units]
    kvs = [_dot_tn(k_ends[ci][:, keys_of(h)], vs[u]) for u, (ci, h) in enumerate(units)]
    intras = [_dot(attns[u], vs[u]) for u in range(len(units))]
    states = [s_ref[h] for h in range(GLA_HEADS)]
    prevs = []
    for u, (ci, h) in enumerate(units):
        prevs.append(states[h])
        states[h] = states[h] * decays[ci][keys_of(h)] + kvs[u]
    for h in range(GLA_HEADS):
        s_ref[h] = states[h]
    inters = [_dot(q_decs[ci][:, keys_of(h)], prevs[u]) for u, (ci, h) in enumerate(units)]
    for ci in range(nc):
        outs = []
        for h in range(GLA_HEADS):
            u = ci * GLA_HEADS + h
            o = intras[u] + inters[u]
            outs.append(o * lax.rsqrt(jnp.mean(o * o, axis=-1, keepdims=True) + NORM_EPS))
        o_all = jnp.concatenate(outs, axis=-1)
        o_ref[0, rows_of(ci), :] = o_all * ng_ref[...] * _silu(go_ref[0, rows_of(ci), :])


def _gla(q, k, v, g_out, a_lr, w_a2, b_a, norm_g, tt=256):
    b, t, dq = q.shape
    dv = v.shape[-1]
    tile = lambda n: pl.BlockSpec((1, tt, n), lambda i, j: (i, j, 0))
    full = lambda shape: pl.BlockSpec(shape, lambda i, j: (0, 0))
    return pl.pallas_call(
        functools.partial(_gla_kernel, tt=tt),
        grid=(b, t // tt),
        in_specs=[tile(dq), tile(dq), tile(dv), tile(dv), tile(a_lr.shape[-1]),
                  full(w_a2.shape), full((1, dq)), full((1, dv))],
        out_specs=tile(dv),
        out_shape=jax.ShapeDtypeStruct((b, t, dv), F32),
        scratch_shapes=[pltpu.VMEM((GLA_HEADS, GLA_DK, GLA_DV), F32)],
        compiler_params=_cparams("parallel", "arbitrary"),
        name="gla",
    )(q, k, v, g_out, a_lr, w_a2, b_a.reshape(1, dq), norm_g.reshape(1, dv))


_RWKV_COLS = 3 * 512 + 64 + 64 + 128


def _rwkv_kernel(p_ref, mu_ref, w2_ref, w0_ref, a2_ref, a0_ref, g2_ref, kk_ref, ka_ref, rk_ref,
                 lg_ref, lb_ref, o_ref, xs_ref, st_ref, *, tt):
    c = RWKV_CHUNK
    hd = RWKV_HD
    w = RWKV_HEADS * hd

    @pl.when(pl.program_id(1) == 0)
    def _():
        xs_ref[0:8, :] = jnp.zeros((8, xs_ref.shape[1]), F32)
        st_ref[...] = jnp.zeros(st_ref.shape, F32)

    x = p_ref[0]
    xs_ref[8:8 + tt, :] = x
    prev = xs_ref[7:7 + tt, :]
    xs_ref[7:8, :] = x[tt - 1:tt, :]
    x = x + (prev - x) * mu_ref[...]

    r = x[:, 0:w]
    k = x[:, w:2 * w]
    v = x[:, 2 * w:3 * w]
    xw = x[:, 3 * w:3 * w + 64]
    xa = x[:, 3 * w + 64:3 * w + 128]
    xg = x[:, 3 * w + 128:3 * w + 256]
    log_w = -RWKV_DECAY_SCALE * _sigmoid(w0_ref[...] + _dot(jnp.tanh(xw), w2_ref[...]))
    a = _sigmoid(a0_ref[...] + _dot(xa, a2_ref[...]))
    g = _dot(_sigmoid(xg), g2_ref[...])
    kk = k * kk_ref[...]
    k_mod = k * (1.0 + (a - 1.0) * ka_ref[...])
    rkr = r * k_mod * rk_ref[...]

    tril = _tri(c)
    tril_s = _tri(c, strict=True)
    tril_f = tril.astype(F32)
    eye = (lax.broadcasted_iota(jnp.int32, (c, c), 0)
           == lax.broadcasted_iota(jnp.int32, (c, c), 1)).astype(F32)
    zeros_cc = jnp.zeros((c, hd), F32)

    nc = tt // c
    units = [(ci, h) for ci in range(nc) for h in range(RWKV_HEADS)]
    rows_of = lambda ci: slice(ci * c, (ci + 1) * c)
    lanes_of = lambda h: slice(h * hd, (h + 1) * hd)

    exps = []
    for ci in range(nc):
        lw_c = log_w[rows_of(ci)]
        lcum = _dot_hi(tril_f, lw_c)
        l_last = lcum[c - 1:c, :]
        exps.append((jnp.exp(lcum), jnp.exp(-lcum), jnp.exp(l_last - lcum), jnp.exp(lcum - lw_c),
                     jnp.exp(l_last)))

    aqs, rqs, vs, lhss, rhss, bkes = [], [], [], [], [], []
    for ci, h in units:
        rows, hs = rows_of(ci), lanes_of(h)
        e_pos, e_neg, e_end, e_prev, _ = exps[ci]
        kk_h = kk[rows, hs]
        nrm = jnp.sqrt(jnp.sum(kk_h * kk_h, axis=-1, keepdims=True))
        kk_h = kk_h / jnp.maximum(nrm, RWKV_L2_EPS)
        k_h = k_mod[rows, hs]
        kka = kk_h * a[rows, hs]
        aq = -kk_h * e_prev[:, hs]
        rq = r[rows, hs] * e_pos[:, hs]
        aqs.append(aq)
        rqs.append(rq)
        vs.append(v[rows, hs])
        lhss.append(jnp.concatenate([aq, rq], axis=0))
        rhss.append(jnp.concatenate([kka * e_neg[:, hs], k_h * e_neg[:, hs]], axis=0))
        bkes.append(jnp.concatenate([kka * e_end[:, hs], k_h * e_end[:, hs]], axis=0))

    xss = [_dot_nt(lhs, rhs) for lhs, rhs in zip(lhss, rhss)]
    m1s = [jnp.where(tril_s, xs[:c, :c], 0.0) for xs in xss]
    m2s = [jnp.where(tril_s, xs[:c, c:], 0.0) for xs in xss]
    n12s = [jnp.concatenate([jnp.where(tril, xs[c:, :c], 0.0), jnp.where(tril, xs[c:, c:], 0.0)], axis=1)
            for xs in xss]
    tinvs = [eye + m1 for m1 in m1s]
    mps = m1s
    for _ in range(5):
        mps = [_dot(mp, mp) for mp in mps]
        tinvs = [tinv + _dot(tinv, mp) for tinv, mp in zip(tinvs, mps)]
    mvs = [_dot(m2, v_h) for m2, v_h in zip(m2s, vs)]
    pqs = [_dot(tinv, jnp.concatenate([mv, aq], axis=1)) for tinv, mv, aq in zip(tinvs, mvs, aqs)]
    pqvs = [jnp.concatenate([pq, jnp.concatenate([v_h, zeros_cc], axis=1)], axis=0)
            for pq, v_h in zip(pqs, vs)]
    yys = [_dot(n12, pqv) for n12, pqv in zip(n12s, pqvs)]
    ghs = [_dot_tn(bke, pqv) for bke, pqv in zip(bkes, pqvs)]

    states = [st_ref[h] for h in range(RWKV_HEADS)]
    ys = []
    for u, (ci, h) in enumerate(units):
        yq = rqs[u] + yys[u][:, hd:]
        gmat = eye * exps[ci][4][:, lanes_of(h)] + ghs[u][:, hd:]
        res = _dot(jnp.concatenate([yq, gmat], axis=0), states[h])
        ys.append(res[:c] + yys[u][:, :hd])
        states[h] = res[c:] + ghs[u][:, :hd]
    for h in range(RWKV_HEADS):
        st_ref[h] = states[h]

    out_rows = []
    for ci in range(nc):
        yns, bonuses = [], []
        for h in range(RWKV_HEADS):
            u = ci * RWKV_HEADS + h
            y = ys[u]
            cen = y - jnp.mean(y, axis=-1, keepdims=True)
            var = jnp.mean(cen * cen, axis=-1, keepdims=True)
            yns.append(cen * lax.rsqrt(var + RWKV_GN_EPS))
            bonuses.append(jnp.sum(rkr[rows_of(ci), lanes_of(h)], axis=-1, keepdims=True) * vs[u])
        yn = jnp.concatenate(yns, axis=1)
        bonus = jnp.concatenate(bonuses, axis=1)
        out_rows.append((yn * lg_ref[...] + lb_ref[...] + bonus) * g[rows_of(ci)])
    o_ref[0] = jnp.concatenate(out_rows, axis=0)


def _rwkv(p, mu, w_w2, w0, w_a2, a0, w_g2, k_k, k_a, r_k, lnx_g, lnx_b, tt=128):
    b, t, cols = p.shape
    w = RWKV_HEADS * RWKV_HD
    vec = lambda a_: a_.reshape(1, -1)
    full = lambda shape: pl.BlockSpec(shape, lambda i, j: (0, 0))
    return pl.pallas_call(
        functools.partial(_rwkv_kernel, tt=tt),
        grid=(b, t // tt),
        in_specs=[pl.BlockSpec((1, tt, cols), lambda i, j: (i, j, 0)), full((1, cols)),
                  full(w_w2.shape), full((1, w)), full(w_a2.shape), full((1, w)), full(w_g2.shape),
                  full((1, w)), full((1, w)), full((1, w)), full((1, w)), full((1, w))],
        out_specs=pl.BlockSpec((1, tt, w), lambda i, j: (i, j, 0)),
        out_shape=jax.ShapeDtypeStruct((b, t, w), F32),
        scratch_shapes=[pltpu.VMEM((8 + tt, cols), F32),
                        pltpu.VMEM((RWKV_HEADS, RWKV_HD, RWKV_HD), F32)],
        compiler_params=_cparams("parallel", "arbitrary"),
        name="rwkv7",
    )(p, vec(mu), w_w2, vec(w0), w_a2, vec(a0), w_g2, vec(k_k), vec(k_a), vec(r_k), vec(lnx_g), vec(lnx_b))


def _fox_cum_kernel(f_ref, bf_ref, col_ref, carry_ref, *, tt):
    @pl.when(pl.program_id(1) == 0)
    def _():
        carry_ref[...] = jnp.zeros(carry_ref.shape, F32)

    log_f = _log_sigmoid(f_ref[0] + bf_ref[...])
    cum = _dot_hi(_tri(tt).astype(F32), log_f) + carry_ref[0:1, :]
    col_ref[0] = cum
    carry_ref[...] = jnp.broadcast_to(cum[tt - 1:tt, :], carry_ref.shape)


def _fox_cum(f_logit, b_f, tt=256):
    b, t, n = f_logit.shape
    return pl.pallas_call(
        functools.partial(_fox_cum_kernel, tt=tt),
        grid=(b, t // tt),
        in_specs=[pl.BlockSpec((1, tt, n), lambda i, j: (i, j, 0)), pl.BlockSpec((1, n), lambda i, j: (0, 0))],
        out_specs=pl.BlockSpec((1, tt, n), lambda i, j: (i, j, 0)),
        out_shape=jax.ShapeDtypeStruct((b, t, n), F32),
        scratch_shapes=[pltpu.VMEM((8, n), F32)],
        compiler_params=_cparams("parallel", "arbitrary"),
        name="fox_cumsum",
    )(f_logit, b_f)


def _split3(x):
    hi = x.astype(BF16).astype(F32)
    mid = (x - hi).astype(BF16).astype(F32)
    lo = (x - hi - mid).astype(BF16).astype(F32)
    return hi, mid, lo


def _fox_kernel(q_ref, k_ref, v_ref, c_ref, o_ref, kaug_ref, vaug_ref, *, tq, groups):
    qi = pl.program_id(2)
    d = FOX_HD
    per = LANE // d
    nh = groups * per
    head0 = pl.program_id(1) * nh

    def augment(x_h, c, first):
        lane = lax.broadcasted_iota(jnp.int32, x_h.shape, 1)
        hi, mid, lo = _split3(c)
        parts = (hi, mid, lo, 1.0, 1.0, 1.0) if first else (1.0, 1.0, 1.0, -hi, -mid, -lo)
        out = jnp.where(lane < d, x_h, 0.0)
        for n, part in enumerate(parts):
            out = jnp.where(lane == d + n, part, out)
        return out.astype(BF16)

    def head_col(c_all, head):
        lane = lax.broadcasted_iota(jnp.int32, c_all.shape, 1)
        return jnp.sum(jnp.where(lane == head, c_all, 0.0), axis=-1, keepdims=True)

    def head_lanes(x, h):
        x_g = x[:, (h // per) * LANE:(h // per + 1) * LANE]
        return x_g if h % per == 0 else pltpu.roll(x_g, LANE - (h % per) * d, axis=1)

    @pl.when(qi == 0)
    def _():
        lane_t = lax.broadcasted_iota(jnp.int32, (k_ref.shape[1], LANE), 1)
        for h in range(nh):
            kaug_ref[h] = augment(head_lanes(k_ref[0], h), head_col(c_ref[0], head0 + h), first=False)
            v_h = jnp.where(lane_t < d, head_lanes(v_ref[0], h), jnp.where(lane_t == d, 1.0, 0.0))
            vaug_ref[h] = v_h.astype(BF16)

    row0 = pl.multiple_of(qi * tq, tq)
    q = q_ref[0] * (d ** -0.5)
    c_q = c_ref[0, pl.ds(row0, tq), :]
    qs = [augment(head_lanes(q, h), head_col(c_q, head0 + h), first=True) for h in range(nh)]
    causal = _tri(tq)
    lag = 2

    def step(j, carry, diagonal):
        start = pl.multiple_of(j * tq, tq)
        ss, new = {}, []
        for h in range(nh + lag):
            if h < nh:
                ss[h] = lax.dot_general(qs[h], kaug_ref[h, pl.ds(start, tq), :], (((1,), (1,)), ((), ())),
                                        preferred_element_type=F32)
            g = h - lag
            if g >= 0:
                m, acc = carry[g]
                s = ss.pop(g)
                s = jnp.where(causal, s, -jnp.inf) if diagonal else s
                m_new = jnp.maximum(m, jnp.max(s, axis=-1, keepdims=True))
                p = jnp.exp(s - m_new).astype(BF16)
                pv = jnp.dot(p, vaug_ref[g, pl.ds(start, tq), :], preferred_element_type=F32)
                new.append((m_new, acc * jnp.exp(m - m_new) + pv))
        return tuple(new)

    init = tuple((jnp.full((tq, 1), -jnp.inf, F32), jnp.zeros((tq, LANE), F32)) for _ in range(nh))
    carry = lax.fori_loop(0, qi, lambda j, cr: step(j, cr, False), init)
    carry = step(qi, carry, True)
    lane = lax.broadcasted_iota(jnp.int32, (tq, LANE), 1)
    for grp in range(groups):
        out = None
        for hh in range(per):
            acc = carry[grp * per + hh][1]
            o_h = acc / jnp.sum(jnp.where(lane == d, acc, 0.0), axis=-1, keepdims=True)
            o_h = o_h if hh == 0 else pltpu.roll(o_h, hh * d, axis=1)
            out = o_h if out is None else jnp.where(lane >= hh * d, o_h, out)
        o_ref[0, :, grp * LANE:(grp + 1) * LANE] = out


def _fox(q, k, v, c_col, tq=256, groups=4):
    b, t, w = q.shape
    gw = groups * LANE
    return pl.pallas_call(
        functools.partial(_fox_kernel, tq=tq, groups=groups),
        grid=(b, w // gw, t // tq),
        in_specs=[pl.BlockSpec((1, tq, gw), lambda i, p, j: (i, j, p)),
                  pl.BlockSpec((1, t, gw), lambda i, p, j: (i, 0, p)),
                  pl.BlockSpec((1, t, gw), lambda i, p, j: (i, 0, p)),
                  pl.BlockSpec((1, t, LANE), lambda i, p, j: (i, 0, 0))],
        out_specs=pl.BlockSpec((1, tq, gw), lambda i, p, j: (i, j, p)),
        out_shape=jax.ShapeDtypeStruct((b, t, w), F32),
        scratch_shapes=[pltpu.VMEM((groups * LANE // FOX_HD, t, LANE), BF16),
                        pltpu.VMEM((groups * LANE // FOX_HD, t, LANE), BF16)],
        compiler_params=_cparams("parallel", "arbitrary", "arbitrary"),
        name="fox_attention",
    )(q, k, v, c_col)


def _serpentine(i, f, nf):
    return jnp.where(i % 2 == 0, f, nf - 1 - f)


def _mix_residual(h_ref, ya_ref, yb_ref, wa_ref, wb_ref):
    return h_ref[...] + _dot(ya_ref[...], wa_ref[...]) + _dot(yb_ref[...], wb_ref[...])


def _ffn_kernel(h_ref, ya_ref, yb_ref, wa_ref, wb_ref, g_ref, w1_ref, w3_ref, w2_ref, o_ref, xn_ref, acc_ref):
    f = pl.program_id(1)

    @pl.when(f == 0)
    def _():
        hn = _mix_residual(h_ref, ya_ref, yb_ref, wa_ref, wb_ref)
        xn_ref[...] = _rmsnorm(hn, g_ref[...]).astype(BF16)
        acc_ref[...] = hn

    xn = xn_ref[...]
    mid = _silu(jnp.dot(xn, w1_ref[...], preferred_element_type=F32)) * jnp.dot(
        xn, w3_ref[...], preferred_element_type=F32)
    acc_ref[...] += _dot(mid, w2_ref[...])

    @pl.when(f == pl.num_programs(1) - 1)
    def _():
        o_ref[...] = acc_ref[...]


def _ffn(h, ya, yb, wa, wb, g, w1, w3, w2, tm=512, tf=1408):
    m, d = h.shape
    k = ya.shape[1]
    nf = w1.shape[1] // tf
    rows = lambda n: pl.BlockSpec((tm, n), lambda i, f: (i, 0))
    full = lambda shape: pl.BlockSpec(shape, lambda i, f: (0, 0))
    return pl.pallas_call(
        _ffn_kernel,
        grid=(m // tm, nf),
        in_specs=[rows(d), rows(k), rows(k), full((k, d)), full((k, d)), full((1, d)),
                  pl.BlockSpec((d, tf), lambda i, f: (0, _serpentine(i, f, nf))),
                  pl.BlockSpec((d, tf), lambda i, f: (0, _serpentine(i, f, nf))),
                  pl.BlockSpec((tf, d), lambda i, f: (_serpentine(i, f, nf), 0))],
        out_specs=rows(d),
        out_shape=jax.ShapeDtypeStruct((m, d), F32),
        scratch_shapes=[pltpu.VMEM((tm, d), BF16), pltpu.VMEM((tm, d), F32)],
        compiler_params=_cparams("parallel", "arbitrary"),
        name="ffn_dense",
    )(h, ya, yb, wa, wb, g.reshape(1, d), w1, w3, w2)


def _split2(x):
    hi = x.astype(BF16)
    return hi, (x - hi.astype(F32)).astype(BF16)


def _moe_route_kernel(h_ref, ya_ref, yb_ref, wa_ref, wb_ref, g_ref, r_ref, tri_ref, hn_ref, xn_ref, meta_ref,
                      cnt_ref, carry_ref):
    @pl.when(pl.program_id(0) == 0)
    def _():
        carry_ref[...] = jnp.zeros(carry_ref.shape, F32)

    hn = _mix_residual(h_ref, ya_ref, yb_ref, wa_ref, wb_ref)
    hn_ref[...] = hn
    xn = _rmsnorm(hn, g_ref[...])
    xn_ref[...] = xn
    tm = xn.shape[0]
    lane = lax.broadcasted_iota(jnp.int32, (tm, LANE), 1)
    x_hi, x_lo = _split2(xn)
    r_hi, r_lo = _split2(r_ref[...])
    dot = functools.partial(jnp.dot, preferred_element_type=F32)
    logits = jnp.where(lane < N_EXPERTS, dot(x_hi, r_hi) + (dot(x_hi, r_lo) + dot(x_lo, r_hi)), -jnp.inf)
    m1 = jnp.max(logits, axis=-1, keepdims=True)
    i1 = jnp.min(jnp.where(logits == m1, lane, LANE), axis=-1, keepdims=True)
    rest = jnp.where(lane == i1, -jnp.inf, logits)
    m2 = jnp.max(rest, axis=-1, keepdims=True)
    i2 = jnp.min(jnp.where(rest == m2, lane, LANE), axis=-1, keepdims=True)
    e2 = jnp.exp(m2 - m1)
    g1 = 1.0 / (1.0 + e2)
    g2 = e2 / (1.0 + e2)
    onehot = jnp.where(lane == i1, 1.0, jnp.where(lane == i2, 1.0, 0.0))
    before = dot(tri_ref[...], onehot.astype(BF16)) + carry_ref[0:1, :]
    r1 = jnp.sum(jnp.where(lane == i1, before, 0.0), axis=-1, keepdims=True)
    r2 = jnp.sum(jnp.where(lane == i2, before, 0.0), axis=-1, keepdims=True)
    meta = jnp.where(lane == 0, i1.astype(F32), jnp.where(lane == 1, i2.astype(F32), 0.0))
    meta = jnp.where(lane == 2, g1, jnp.where(lane == 3, g2, meta))
    meta_ref[...] = jnp.where(lane == 4, r1, jnp.where(lane == 5, r2, meta))
    total = carry_ref[0:1, :] + jnp.sum(onehot, axis=0, keepdims=True)
    carry_ref[...] = jnp.broadcast_to(total, carry_ref.shape)
    cnt_ref[...] = jnp.broadcast_to(total, cnt_ref.shape)


def _moe_route(h, ya, yb, wa, wb, g, router, tm=512):
    m, d = h.shape
    k = ya.shape[1]
    rows = lambda n: pl.BlockSpec((tm, n), lambda i: (i, 0))
    full = lambda shape: pl.BlockSpec(shape, lambda i: (0, 0))
    return pl.pallas_call(
        _moe_route_kernel,
        grid=(m // tm,),
        in_specs=[rows(d), rows(k), rows(k), full((k, d)), full((k, d)), full((1, d)), full((d, LANE)),
                  full((tm, tm))],
        out_specs=[rows(d), rows(d), rows(LANE), full((8, LANE))],
        out_shape=[jax.ShapeDtypeStruct((m, d), F32), jax.ShapeDtypeStruct((m, d), F32),
                   jax.ShapeDtypeStruct((m, LANE), F32), jax.ShapeDtypeStruct((8, LANE), F32)],
        scratch_shapes=[pltpu.VMEM((8, LANE), F32)],
        compiler_params=_cparams("arbitrary"),
        name="moe_route",
    )(h, ya, yb, wa, wb, g.reshape(1, d), router, jnp.tril(jnp.ones((tm, tm), BF16), -1))


def _row_copy(src_ref, src_group, src_sub, dst_ref, dst_group, dst_sub, sem):
    return pltpu.make_async_copy(src_ref.at[src_group, pl.ds(src_sub, 1)],
                                 dst_ref.at[dst_group, pl.ds(dst_sub, 1)], sem)


def _split_row(p):
    return lax.shift_right_logical(p, 3), lax.bitwise_and(p, _SUBLANES - 1)


def _moe_dispatch_kernel(pos_ref, xn_ref, xs_in_ref, xs_ref, sem, *, tm):
    del xs_in_ref

    def start(grp, carry):
        for u in range(_SUBLANES):
            for s in range(2):
                dst_group, dst_sub = _split_row(pos_ref[0, 0, 2 * (grp * _SUBLANES + u) + s])
                _row_copy(xn_ref, grp, u, xs_ref, dst_group, dst_sub, sem).start(priority=s)
        return carry

    def wait(grp, carry):
        for _ in range(2 * _SUBLANES):
            _row_copy(xn_ref, 0, 0, xs_ref, 0, 0, sem).wait()
        return carry

    lax.fori_loop(0, tm // _SUBLANES, start, 0)
    lax.fori_loop(0, tm // _SUBLANES, wait, 0)


def _moe_dispatch(xn, pos, n_rows, tm=256):
    m, d = xn.shape
    xs = pl.pallas_call(
        functools.partial(_moe_dispatch_kernel, tm=tm),
        grid=(m // tm,),
        in_specs=[pl.BlockSpec((1, 1, 2 * tm), lambda i: (i, 0, 0), memory_space=pltpu.SMEM),
                  pl.BlockSpec((tm // _SUBLANES, _SUBLANES, d), lambda i: (i, 0, 0)),
                  pl.BlockSpec(memory_space=pl.ANY)],
        out_specs=pl.BlockSpec(memory_space=pl.ANY),
        out_shape=jax.ShapeDtypeStruct((n_rows // _SUBLANES, _SUBLANES, d), F32),
        scratch_shapes=[pltpu.SemaphoreType.DMA(())],
        input_output_aliases={2: 0},
        compiler_params=_cparams("arbitrary"),
        name="moe_dispatch",
    )(pos.reshape(m // tm, 1, 2 * tm), xn.reshape(m // _SUBLANES, _SUBLANES, d),
      jnp.zeros((n_rows // _SUBLANES, _SUBLANES, d), F32))
    return xs.reshape(n_rows, d)


def _moe_expert_kernel(te_ref, nu_ref, x_ref, w1_ref, w3_ref, w2_ref, y_ref, xbf_ref, acc_ref):
    del te_ref
    f = pl.program_id(1)

    @pl.when(f == 0)
    def _():
        xbf_ref[...] = x_ref[...].astype(BF16)
        acc_ref[...] = jnp.zeros(acc_ref.shape, F32)

    @pl.when(pl.program_id(0) < nu_ref[0])
    def _():
        xb = xbf_ref[...]
        mid = _silu(jnp.dot(xb, w1_ref[0], preferred_element_type=F32)) * jnp.dot(
            xb, w3_ref[0], preferred_element_type=F32)
        acc_ref[...] += _dot(mid, w2_ref[0])

    @pl.when(f == pl.num_programs(1) - 1)
    def _():
        y_ref[...] = acc_ref[...]


def _moe_experts(xs, tile_expert, n_used, w1, w3, w2, tmx, tf=1792):
    n_rows, d = xs.shape
    nf = w1.shape[2] // tf

    def fidx(i, f, nu):
        last = i >= nu[0]
        return _serpentine(jnp.where(last, nu[0] - 1, i), jnp.where(last, nf - 1, f), nf)

    grid_spec = pltpu.PrefetchScalarGridSpec(
        num_scalar_prefetch=2,
        grid=(n_rows // tmx, nf),
        in_specs=[pl.BlockSpec((tmx, d), lambda i, f, te, nu: (i, 0)),
                  pl.BlockSpec((1, d, tf), lambda i, f, te, nu: (te[i], 0, fidx(i, f, nu))),
                  pl.BlockSpec((1, d, tf), lambda i, f, te, nu: (te[i], 0, fidx(i, f, nu))),
                  pl.BlockSpec((1, tf, d), lambda i, f, te, nu: (te[i], fidx(i, f, nu), 0))],
        out_specs=pl.BlockSpec((tmx, d), lambda i, f, te, nu: (i, 0)),
        scratch_shapes=[pltpu.VMEM((tmx, d), BF16), pltpu.VMEM((tmx, d), F32)],
    )
    return pl.pallas_call(
        _moe_expert_kernel,
        grid_spec=grid_spec,
        out_shape=jax.ShapeDtypeStruct((n_rows, d), F32),
        compiler_params=_cparams("arbitrary", "arbitrary"),
        name="moe_experts",
    )(tile_expert, n_used, xs, w1, w3, w2)


def _moe_combine_kernel(pos_ref, posn_ref, h_ref, meta_ref, fg_ref, ys_ref, o_ref, ybuf_ref, sem, *, tm):
    i = pl.program_id(0)
    slot = lax.rem(i, 2)

    groups = tm // _SUBLANES

    def gather(p_ref, dst_slot):
        def start(grp, carry):
            for u in range(_SUBLANES):
                for s in range(2):
                    src_group, src_sub = _split_row(p_ref[0, 0, 2 * (grp * _SUBLANES + u) + s])
                    _row_copy(ys_ref, src_group, src_sub, ybuf_ref.at[dst_slot], s * groups + grp, u,
                              sem.at[dst_slot]).start(priority=s)
            return carry
        lax.fori_loop(0, groups, start, 0)

    @pl.when(i == 0)
    def _():
        gather(pos_ref, 0)

    @pl.when(i + 1 < pl.num_programs(0))
    def _():
        gather(posn_ref, 1 - slot)

    def wait(grp, carry):
        for _ in range(2 * _SUBLANES):
            _row_copy(ys_ref, 0, 0, ybuf_ref.at[slot], 0, 0, sem.at[slot]).wait()
        return carry

    lax.fori_loop(0, groups, wait, 0)
    lane = lax.broadcasted_iota(jnp.int32, (tm, LANE), 1)
    meta = meta_ref[...]
    g1 = jnp.sum(jnp.where(lane == 2, meta, 0.0), axis=-1, keepdims=True)
    g2 = jnp.sum(jnp.where(lane == 3, meta, 0.0), axis=-1, keepdims=True)
    y = ybuf_ref[slot].reshape(2 * tm, -1)
    o_ref[...] = _rmsnorm(h_ref[...] + g1 * y[:tm] + g2 * y[tm:], fg_ref[...])


def _moe_combine(h, meta, pos, ys, final_g, tm=256):
    m, d = h.shape
    nt = m // tm
    pos3 = pos.reshape(nt, 1, 2 * tm)
    smem = lambda imap: pl.BlockSpec((1, 1, 2 * tm), imap, memory_space=pltpu.SMEM)
    return pl.pallas_call(
        functools.partial(_moe_combine_kernel, tm=tm),
        grid=(nt,),
        in_specs=[smem(lambda i: (i, 0, 0)), smem(lambda i: (jnp.minimum(i + 1, nt - 1), 0, 0)),
                  pl.BlockSpec((tm, d), lambda i: (i, 0)), pl.BlockSpec((tm, LANE), lambda i: (i, 0)),
                  pl.BlockSpec((1, d), lambda i: (0, 0)), pl.BlockSpec(memory_space=pl.ANY)],
        out_specs=pl.BlockSpec((tm, d), lambda i: (i, 0)),
        out_shape=jax.ShapeDtypeStruct((m, d), F32),
        scratch_shapes=[pltpu.VMEM((2, 2 * tm // _SUBLANES, _SUBLANES, d), F32), pltpu.SemaphoreType.DMA((2,))],
        compiler_params=_cparams("arbitrary"),
        name="moe_combine",
    )(pos3, pos3, h, meta, final_g.reshape(1, d), ys.reshape(-1, _SUBLANES, d))


def _moe_routed(h, ya, yb, wa, wb, g, router, w1, w3, w2, final_g, tmx=512):
    m, d = h.shape
    h, xn, meta, cnt = _moe_route(h, ya, yb, wa, wb, g, router)
    experts = meta[:, 0:2].astype(jnp.int32)
    rank = meta[:, 4:6].astype(jnp.int32)
    counts = cnt[0, :N_EXPERTS].astype(jnp.int32)
    padded = (counts + tmx - 1) // tmx * tmx
    ends = jnp.cumsum(padded)
    pos = (ends - padded)[experts] + rank
    n_tiles = (2 * m) // tmx + N_EXPERTS
    n_used = ends[-1] // tmx
    tile_start = jnp.minimum(jnp.arange(n_tiles, dtype=jnp.int32), n_used - 1) * tmx
    tile_expert = jnp.sum(tile_start[:, None] >= ends[None, :], axis=1).astype(jnp.int32)
    xs = _moe_dispatch(xn, pos, n_tiles * tmx)
    ys = _moe_experts(xs, tile_expert, n_used.reshape(1).astype(jnp.int32), w1, w3, w2, tmx)
    return _moe_combine(h, meta, pos, ys, final_g)


def _pad_cols(w, n):
    return jnp.pad(w, ((0, 0), (0, n - w.shape[1])))


def kernel(x, norm_mix_g, norm_ffn_g, w_in0, conv_w, conv_b, conv_ln_g, conv_ln_b, gla_w_a2, gla_b_a, gla_norm_g, w_out0, ffn_w1, ffn_w3, ffn_w2, w_in1, rwkv_mu, rwkv_w2, rwkv_w0, rwkv_a2, rwkv_a0, rwkv_g2, rwkv_k_k, rwkv_k_a, rwkv_r_k, rwkv_lnx_g, rwkv_lnx_b, fox_b_f, w_out1, moe_router, moe_w1, moe_w3, moe_w2, final_norm_g):
    b, t, d = x.shape
    m = b * t
    half = d // 2
    bf = lambda a: a.astype(BF16)
    h = x.reshape(m, d)

    gk = GLA_HEADS * GLA_DK
    o = [0, half, 2 * half, 2 * half + gk, 2 * half + 2 * gk, 3 * half + 2 * gk, 4 * half + 2 * gk]
    ws0 = [bf(w_in0[:, o[i]:o[i + 1]]) for i in range(6)] + [bf(_pad_cols(w_in0[:, o[6]:], LANE))]
    c_val, c_gate, q, k, v, g_out, a_lr = [
        a.reshape(b, t, -1) for a in _norm_proj(h, norm_mix_g[0], ws0)]
    y_conv = _conv(c_val, c_gate, conv_w, conv_b, conv_ln_g, conv_ln_b)
    w_a2 = jnp.pad(gla_w_a2, ((0, LANE - gla_w_a2.shape[0]), (0, 0)))
    y_gla = _gla(q, k, v, g_out, a_lr, w_a2, gla_b_a, gla_norm_g)
    h = _ffn(h, y_conv.reshape(m, half), y_gla.reshape(m, half), bf(w_out0[:half]), bf(w_out0[half:]),
             norm_ffn_g[0], bf(ffn_w1), bf(ffn_w3), bf(ffn_w2))

    rc = _RWKV_COLS
    ws1 = [bf(w_in1[:, :rc]), bf(w_in1[:, rc:rc + half]), bf(w_in1[:, rc + half:rc + 2 * half]),
           bf(w_in1[:, rc + 2 * half:rc + 3 * half]), bf(_pad_cols(w_in1[:, rc + 3 * half:], LANE))]
    p_rwkv, fq, fk, fv, f_logit = [a.reshape(b, t, -1) for a in _norm_proj(h, norm_mix_g[1], ws1)]
    y_rwkv = _rwkv(p_rwkv, rwkv_mu, rwkv_w2, rwkv_w0, rwkv_a2, rwkv_a0, rwkv_g2, rwkv_k_k, rwkv_k_a,
                   rwkv_r_k, rwkv_lnx_g, rwkv_lnx_b)
    b_f = jnp.pad(fox_b_f, (0, LANE - FOX_HEADS)).reshape(1, LANE)
    y_fox = _fox(fq, fk, fv, _fox_cum(f_logit, b_f))
    out = _moe_routed(h, y_rwkv.reshape(m, half), y_fox.reshape(m, half), bf(w_out1[:half]), bf(w_out1[half:]),
                      norm_ffn_g[1], _pad_cols(moe_router, LANE), bf(moe_w1), bf(moe_w3), bf(moe_w2),
                      final_norm_g)
    return out.reshape(b, t, d)
```

```python
import functools

import jax
import jax.numpy as jnp
from jax import lax
from jax.experimental import pallas as pl
from jax.experimental.pallas import tpu as pltpu

F32 = jnp.float32
BF16 = jnp.bfloat16
HI = lax.Precision.HIGHEST

NORM_EPS = 1e-6
CONV_WIDTH = 31
CONV_LN_EPS = 1e-5
GLA_HEADS = 4
GLA_DK = 64
GLA_DV = 128
GLA_TAU = 16.0
GLA_CHUNK = 64
RWKV_HEADS = 8
RWKV_HD = 64
RWKV_CHUNK = 64
RWKV_DECAY_SCALE = 0.606531
RWKV_GN_EPS = 64e-5
RWKV_L2_EPS = 1e-12
FOX_HEADS = 8
FOX_HD = 64
N_EXPERTS = 8
LANE = 128
_SUBLANES = 8
VMEM_LIMIT = 56 * 1024 * 1024


def _cparams(*sem):
    return pltpu.CompilerParams(dimension_semantics=sem, vmem_limit_bytes=VMEM_LIMIT)


def _dot(a, b):
    return jnp.dot(a.astype(BF16), b.astype(BF16), preferred_element_type=F32)


def _dot_nt(a, b):
    return lax.dot_general(a.astype(BF16), b.astype(BF16), (((1,), (1,)), ((), ())),
                           preferred_element_type=F32)


def _dot_tn(a, b):
    return lax.dot_general(a.astype(BF16), b.astype(BF16), (((0,), (0,)), ((), ())),
                           preferred_element_type=F32)


def _dot_hi(a, b):
    return jnp.dot(a, b, precision=HI, preferred_element_type=F32)


def _dot_tn_hi(a, b):
    return lax.dot_general(a, b, (((0,), (0,)), ((), ())), precision=HI, preferred_element_type=F32)


def _sigmoid(x):
    return 1.0 / (1.0 + jnp.exp(-x))


def _silu(x):
    return x * _sigmoid(x)


def _log_sigmoid(x):
    return jnp.minimum(x, 0.0) - jnp.log(1.0 + jnp.exp(-jnp.abs(x)))


def _rmsnorm(x, g):
    return x * lax.rsqrt(jnp.mean(x * x, axis=-1, keepdims=True) + NORM_EPS) * g


def _tri(n, strict=False):
    r = lax.broadcasted_iota(jnp.int32, (n, n), 0)
    c = lax.broadcasted_iota(jnp.int32, (n, n), 1)
    return (r > c) if strict else (r >= c)


def _norm_proj_kernel(h_ref, g_ref, *refs):
    n = len(refs) // 2
    xn = _rmsnorm(h_ref[...], g_ref[...]).astype(BF16)
    for w_ref, o_ref in zip(refs[:n], refs[n:]):
        o_ref[...] = jnp.dot(xn, w_ref[...], preferred_element_type=F32)


def _norm_proj(h, g, ws, tm=512):
    m, d = h.shape
    return pl.pallas_call(
        _norm_proj_kernel,
        grid=(m // tm,),
        in_specs=[pl.BlockSpec((tm, d), lambda i: (i, 0)), pl.BlockSpec((1, d), lambda i: (0, 0))]
        + [pl.BlockSpec(w.shape, lambda i: (0, 0)) for w in ws],
        out_specs=[pl.BlockSpec((tm, w.shape[1]), lambda i: (i, 0)) for w in ws],
        out_shape=[jax.ShapeDtypeStruct((m, w.shape[1]), F32) for w in ws],
        compiler_params=_cparams("parallel"),
        name="norm_proj",
    )(h, g.reshape(1, d), *ws)


_CONV_HALO = 32
_CONV_ROWS = 32
_GLA_SPAN = 256


def _conv_body(val, gate, w_ref, b_ref, lg_ref, lb_ref, o_ref, u_ref, sh_ref):
    tt, c = val.shape
    rows = _CONV_ROWS
    u_ref[_CONV_HALO:_CONV_HALO + tt, :] = val * _sigmoid(gate)
    span = sh_ref.shape[1]
    for s in range(1, _SUBLANES):
        sh_ref[s - 1] = u_ref[s:s + span, :]
    base = _CONV_HALO - (CONV_WIDTH - 1)
    for r0 in range(0, tt, rows):
        acc = jnp.zeros((rows // _SUBLANES, _SUBLANES, c), F32) + b_ref[...]
        for j in range(CONV_WIDTH):
            phase = (base + j) % _SUBLANES
            row = r0 + base + j - phase
            win = u_ref[row:row + rows, :] if phase == 0 else sh_ref[phase - 1, row:row + rows, :]
            acc = acc + win.reshape(rows // _SUBLANES, _SUBLANES, c) * w_ref[j]
        acc = acc.reshape(rows, c)
        mu = jnp.mean(acc, axis=-1, keepdims=True)
        cen = acc - mu
        var = jnp.mean(cen * cen, axis=-1, keepdims=True)
        y = cen * lax.rsqrt(var + CONV_LN_EPS) * lg_ref[...] + lb_ref[...]
        o_ref[0, r0:r0 + rows, :] = _silu(y)
    u_ref[0:_CONV_HALO, :] = u_ref[tt:tt + _CONV_HALO, :]


def _gla_body(q, k, v, go, alr, wa2_ref, ba_ref, ng_ref, o_ref, s_ref, row0):
    c = GLA_CHUNK
    nc = q.shape[0] // c
    tril = _tri(c)
    tril_f = tril.astype(F32)
    ones_cv = jnp.ones((c, GLA_DV), F32)
    units = [(ci, h) for ci in range(nc) for h in range(GLA_HEADS)]
    rows_of = lambda ci: slice(ci * c, (ci + 1) * c)
    keys_of = lambda h: slice(h * GLA_DK, (h + 1) * GLA_DK)
    vals_of = lambda h: slice(h * GLA_DV, (h + 1) * GLA_DV)

    z = _dot(alr, wa2_ref[...]) + ba_ref[...]
    log_a = _log_sigmoid(z) * (1.0 / GLA_TAU)
    q = q * (GLA_DK ** -0.5)
    q_decs, k_decs, k_ends, decays = [], [], [], []
    for ci in range(nc):
        la = log_a[rows_of(ci)]
        bcum = _dot_hi(tril_f, la)
        b_last = bcum[c - 1:c, :]
        q_decs.append(q[rows_of(ci)] * jnp.exp(bcum))
        k_decs.append(k[rows_of(ci)] * jnp.exp(-bcum))
        k_ends.append(k[rows_of(ci)] * jnp.exp(b_last - bcum))
        decays.append(jnp.exp(_dot_tn_hi(la, ones_cv)))

    vs = [v[rows_of(ci), vals_of(h)] for ci, h in units]
    attns = [jnp.where(tril, _dot_nt(q_decs[ci][:, keys_of(h)], k_decs[ci][:, keys_of(h)]), 0.0)
             for ci, h in units]
    kvs = [_dot_tn(k_ends[ci][:, keys_of(h)], vs[u]) for u, (ci, h) in enumerate(units)]
    intras = [_dot(attns[u], vs[u]) for u in range(len(units))]
    states = [s_ref[h] for h in range(GLA_HEADS)]
    prevs = []
    for u, (ci, h) in enumerate(units):
        prevs.append(states[h])
        states[h] = states[h] * decays[ci][keys_of(h)] + kvs[u]
    for h in range(GLA_HEADS):
        s_ref[h] = states[h]
    inters = [_dot(q_decs[ci][:, keys_of(h)], prevs[u]) for u, (ci, h) in enumerate(units)]
    for ci in range(nc):
        outs = []
        for h in range(GLA_HEADS):
            u = ci * GLA_HEADS + h
            o = intras[u] + inters[u]
            outs.append(o * lax.rsqrt(jnp.mean(o * o, axis=-1, keepdims=True) + NORM_EPS))
        o_all = jnp.concatenate(outs, axis=-1)
        o_ref[0, row0 + ci * c:row0 + (ci + 1) * c, :] = o_all * ng_ref[...] * _silu(go[rows_of(ci)])


def _layer0_kernel(h_ref, g_ref, wcv_ref, wcg_ref, wq_ref, wk_ref, wv_ref, wgo_ref, walr_ref,
                   cw_ref, cb_ref, clg_ref, clb_ref, wa2_ref, ba_ref, ng_ref,
                   yc_ref, yg_ref, u_ref, sh_ref, s_ref):
    @pl.when(pl.program_id(1) == 0)
    def _():
        u_ref[0:_CONV_HALO, :] = jnp.zeros((_CONV_HALO, u_ref.shape[1]), F32)
        s_ref[...] = jnp.zeros(s_ref.shape, F32)

    xn = _rmsnorm(h_ref[0], g_ref[...]).astype(BF16)
    proj = lambda w_ref, rows=slice(None): jnp.dot(xn[rows], w_ref[...], preferred_element_type=F32)
    _conv_body(proj(wcv_ref), proj(wcg_ref), cw_ref, cb_ref, clg_ref, clb_ref, yc_ref, u_ref, sh_ref)
    for row0 in range(0, xn.shape[0], _GLA_SPAN):
        rows = slice(row0, row0 + _GLA_SPAN)
        _gla_body(proj(wq_ref, rows), proj(wk_ref, rows), proj(wv_ref, rows), proj(wgo_ref, rows),
                  proj(walr_ref, rows), wa2_ref, ba_ref, ng_ref, yg_ref, s_ref, row0)


def _layer0_mix(x, g, w_in, conv_w, conv_b, ln_g, ln_b, w_a2, b_a, norm_g, tt=512):
    b, t, d = x.shape
    c = d // 2
    gk = GLA_HEADS * GLA_DK
    o = [0, c, 2 * c, 2 * c + gk, 2 * c + 2 * gk, 3 * c + 2 * gk, 4 * c + 2 * gk]
    ws = [w_in[:, o[i]:o[i + 1]].astype(BF16) for i in range(6)] + [_pad_cols(w_in[:, o[6]:], LANE).astype(BF16)]
    wrep = jnp.broadcast_to(conv_w[:, None, :], (CONV_WIDTH, _SUBLANES, c))
    w_a2p = jnp.pad(w_a2, ((0, LANE - w_a2.shape[0]), (0, 0)))
    vec = lambda a: a.reshape(1, -1)
    full = lambda a: pl.BlockSpec(a.shape, lambda i, j: (0,) * a.ndim)
    tile = lambda n: pl.BlockSpec((1, tt, n), lambda i, j: (i, j, 0))
    consts = ws + [wrep, vec(conv_b), vec(ln_g), vec(ln_b), w_a2p, vec(b_a), vec(norm_g)]
    return pl.pallas_call(
        _layer0_kernel,
        grid=(b, t // tt),
        in_specs=[tile(d), full(vec(g))] + [full(a) for a in consts],
        out_specs=[tile(c), tile(c)],
        out_shape=[jax.ShapeDtypeStruct((b, t, c), F32), jax.ShapeDtypeStruct((b, t, c), F32)],
        scratch_shapes=[pltpu.VMEM((tt + _CONV_HALO, c), F32),
                        pltpu.VMEM((_SUBLANES - 1, tt + _CONV_HALO - _SUBLANES, c), F32),
                        pltpu.VMEM((GLA_HEADS, GLA_DK, GLA_DV), F32)],
        compiler_params=_cparams("parallel", "arbitrary"),
        name="layer0_mix",
    )(x, vec(g), *consts)


_RWKV_COLS = 3 * 512 + 64 + 64 + 128


def _rwkv_kernel(p_ref, mu_ref, w2_ref, w0_ref, a2_ref, a0_ref, g2_ref, kk_ref, ka_ref, rk_ref,
                 lg_ref, lb_ref, o_ref, xs_ref, st_ref, *, tt):
    c = RWKV_CHUNK
    hd = RWKV_HD
    w = RWKV_HEADS * hd

    @pl.when(pl.program_id(1) == 0)
    def _():
        xs_ref[0:8, :] = jnp.zeros((8, xs_ref.shape[1]), F32)
        st_ref[...] = jnp.zeros(st_ref.shape, F32)

    x = p_ref[0]
    xs_ref[8:8 + tt, :] = x
    prev = xs_ref[7:7 + tt, :]
    xs_ref[7:8, :] = x[tt - 1:tt, :]
    x = x + (prev - x) * mu_ref[...]

    r = x[:, 0:w]
    k = x[:, w:2 * w]
    v = x[:, 2 * w:3 * w]
    xw = x[:, 3 * w:3 * w + 64]
    xa = x[:, 3 * w + 64:3 * w + 128]
    xg = x[:, 3 * w + 128:3 * w + 256]
    log_w = -RWKV_DECAY_SCALE * _sigmoid(w0_ref[...] + _dot(jnp.tanh(xw), w2_ref[...]))
    a = _sigmoid(a0_ref[...] + _dot(xa, a2_ref[...]))
    g = _dot(_sigmoid(xg), g2_ref[...])
    kk = k * kk_ref[...]
    k_mod = k * (1.0 + (a - 1.0) * ka_ref[...])
    rkr = r * k_mod * rk_ref[...]

    tril = _tri(c)
    tril_s = _tri(c, strict=True)
    tril_f = tril.astype(F32)
    eye = (lax.broadcasted_iota(jnp.int32, (c, c), 0)
           == lax.broadcasted_iota(jnp.int32, (c, c), 1)).astype(F32)
    zeros_cc = jnp.zeros((c, hd), F32)

    nc = tt // c
    units = [(ci, h) for ci in range(nc) for h in range(RWKV_HEADS)]
    rows_of = lambda ci: slice(ci * c, (ci + 1) * c)
    lanes_of = lambda h: slice(h * hd, (h + 1) * hd)

    exps = []
    for ci in range(nc):
        lw_c = log_w[rows_of(ci)]
        lcum = _dot_hi(tril_f, lw_c)
        l_last = lcum[c - 1:c, :]
        exps.append((jnp.exp(lcum), jnp.exp(-lcum), jnp.exp(l_last - lcum), jnp.exp(lcum - lw_c),
                     jnp.exp(l_last)))

    aqs, rqs, vs, lhss, rhss, bkes = [], [], [], [], [], []
    for ci, h in units:
        rows, hs = rows_of(ci), lanes_of(h)
        e_pos, e_neg, e_end, e_prev, _ = exps[ci]
        kk_h = kk[rows, hs]
        nrm = jnp.sqrt(jnp.sum(kk_h * kk_h, axis=-1, keepdims=True))
        kk_h = kk_h / jnp.maximum(nrm, RWKV_L2_EPS)
        k_h = k_mod[rows, hs]
        kka = kk_h * a[rows, hs]
        aq = -kk_h * e_prev[:, hs]
        rq = r[rows, hs] * e_pos[:, hs]
        aqs.append(aq)
        rqs.append(rq)
        vs.append(v[rows, hs])
        lhss.append(jnp.concatenate([aq, rq], axis=0))
        rhss.append(jnp.concatenate([kka * e_neg[:, hs], k_h * e_neg[:, hs]], axis=0))
        bkes.append(jnp.concatenate([kka * e_end[:, hs], k_h * e_end[:, hs]], axis=0))

    xss = [_dot_nt(lhs, rhs) for lhs, rhs in zip(lhss, rhss)]
    m1s = [jnp.where(tril_s, xs[:c, :c], 0.0) for xs in xss]
    m2s = [jnp.where(tril_s, xs[:c, c:], 0.0) for xs in xss]
    n12s = [jnp.concatenate([jnp.where(tril, xs[c:, :c], 0.0), jnp.where(tril, xs[c:, c:], 0.0)], axis=1)
            for xs in xss]
    tinvs = [eye + m1 for m1 in m1s]
    mps = m1s
    for _ in range(5):
        mps = [_dot(mp, mp) for mp in mps]
        tinvs = [tinv + _dot(tinv, mp) for tinv, mp in zip(tinvs, mps)]
    mvs = [_dot(m2, v_h) for m2, v_h in zip(m2s, vs)]
    pqs = [_dot(tinv, jnp.concatenate([mv, aq], axis=1)) for tinv, mv, aq in zip(tinvs, mvs, aqs)]
    pqvs = [jnp.concatenate([pq, jnp.concatenate([v_h, zeros_cc], axis=1)], axis=0)
            for pq, v_h in zip(pqs, vs)]
    yys = [_dot(n12, pqv) for n12, pqv in zip(n12s, pqvs)]
    ghs = [_dot_tn(bke, pqv) for bke, pqv in zip(bkes, pqvs)]

    states = [st_ref[h] for h in range(RWKV_HEADS)]
    ys = []
    for u, (ci, h) in enumerate(units):
        yq = rqs[u] + yys[u][:, hd:]
        gmat = eye * exps[ci][4][:, lanes_of(h)] + ghs[u][:, hd:]
        res = _dot(jnp.concatenate([yq, gmat], axis=0), states[h])
        ys.append(res[:c] + yys[u][:, :hd])
        states[h] = res[c:] + ghs[u][:, :hd]
    for h in range(RWKV_HEADS):
        st_ref[h] = states[h]

    out_rows = []
    for ci in range(nc):
        yns, bonuses = [], []
        for h in range(RWKV_HEADS):
            u = ci * RWKV_HEADS + h
            y = ys[u]
            cen = y - jnp.mean(y, axis=-1, keepdims=True)
            var = jnp.mean(cen * cen, axis=-1, keepdims=True)
            yns.append(cen * lax.rsqrt(var + RWKV_GN_EPS))
            bonuses.append(jnp.sum(rkr[rows_of(ci), lanes_of(h)], axis=-1, keepdims=True) * vs[u])
        yn = jnp.concatenate(yns, axis=1)
        bonus = jnp.concatenate(bonuses, axis=1)
        out_rows.append((yn * lg_ref[...] + lb_ref[...] + bonus) * g[rows_of(ci)])
    o_ref[0] = jnp.concatenate(out_rows, axis=0)


def _rwkv(p, mu, w_w2, w0, w_a2, a0, w_g2, k_k, k_a, r_k, lnx_g, lnx_b, tt=128):
    b, t, cols = p.shape
    w = RWKV_HEADS * RWKV_HD
    vec = lambda a_: a_.reshape(1, -1)
    full = lambda shape: pl.BlockSpec(shape, lambda i, j: (0, 0))
    return pl.pallas_call(
        functools.partial(_rwkv_kernel, tt=tt),
        grid=(b, t // tt),
        in_specs=[pl.BlockSpec((1, tt, cols), lambda i, j: (i, j, 0)), full((1, cols)),
                  full(w_w2.shape), full((1, w)), full(w_a2.shape), full((1, w)), full(w_g2.shape),
                  full((1, w)), full((1, w)), full((1, w)), full((1, w)), full((1, w))],
        out_specs=pl.BlockSpec((1, tt, w), lambda i, j: (i, j, 0)),
        out_shape=jax.ShapeDtypeStruct((b, t, w), F32),
        scratch_shapes=[pltpu.VMEM((8 + tt, cols), F32),
                        pltpu.VMEM((RWKV_HEADS, RWKV_HD, RWKV_HD), F32)],
        compiler_params=_cparams("parallel", "arbitrary"),
        name="rwkv7",
    )(p, vec(mu), w_w2, vec(w0), w_a2, vec(a0), w_g2, vec(k_k), vec(k_a), vec(r_k), vec(lnx_g), vec(lnx_b))


def _fox_cum_kernel(f_ref, bf_ref, col_ref, carry_ref, *, tt):
    @pl.when(pl.program_id(1) == 0)
    def _():
        carry_ref[...] = jnp.zeros(carry_ref.shape, F32)

    log_f = _log_sigmoid(f_ref[0] + bf_ref[...])
    cum = _dot_hi(_tri(tt).astype(F32), log_f) + carry_ref[0:1, :]
    col_ref[0] = cum
    carry_ref[...] = jnp.broadcast_to(cum[tt - 1:tt, :], carry_ref.shape)


def _fox_cum(f_logit, b_f, tt=256):
    b, t, n = f_logit.shape
    return pl.pallas_call(
        functools.partial(_fox_cum_kernel, tt=tt),
        grid=(b, t // tt),
        in_specs=[pl.BlockSpec((1, tt, n), lambda i, j: (i, j, 0)), pl.BlockSpec((1, n), lambda i, j: (0, 0))],
        out_specs=pl.BlockSpec((1, tt, n), lambda i, j: (i, j, 0)),
        out_shape=jax.ShapeDtypeStruct((b, t, n), F32),
        scratch_shapes=[pltpu.VMEM((8, n), F32)],
        compiler_params=_cparams("parallel", "arbitrary"),
        name="fox_cumsum",
    )(f_logit, b_f)


def _split3(x):
    hi = x.astype(BF16).astype(F32)
    mid = (x - hi).astype(BF16).astype(F32)
    lo = (x - hi - mid).astype(BF16).astype(F32)
    return hi, mid, lo


def _fox_kernel(q_ref, k_ref, v_ref, c_ref, o_ref, kaug_ref, vaug_ref, *, tq, groups):
    qi = pl.program_id(2)
    d = FOX_HD
    per = LANE // d
    nh = groups * per
    head0 = pl.program_id(1) * nh

    def augment(x_h, c, first):
        lane = lax.broadcasted_iota(jnp.int32, x_h.shape, 1)
        hi, mid, lo = _split3(c)
        parts = (hi, mid, lo, 1.0, 1.0, 1.0) if first else (1.0, 1.0, 1.0, -hi, -mid, -lo)
        out = jnp.where(lane < d, x_h, 0.0)
        for n, part in enumerate(parts):
            out = jnp.where(lane == d + n, part, out)
        return out.astype(BF16)

    def head_col(c_all, head):
        lane = lax.broadcasted_iota(jnp.int32, c_all.shape, 1)
        return jnp.sum(jnp.where(lane == head, c_all, 0.0), axis=-1, keepdims=True)

    def head_lanes(x, h):
        x_g = x[:, (h // per) * LANE:(h // per + 1) * LANE]
        return x_g if h % per == 0 else pltpu.roll(x_g, LANE - (h % per) * d, axis=1)

    @pl.when(qi == 0)
    def _():
        lane_t = lax.broadcasted_iota(jnp.int32, (k_ref.shape[1], LANE), 1)
        for h in range(nh):
            kaug_ref[h] = augment(head_lanes(k_ref[0], h), head_col(c_ref[0], head0 + h), first=False)
            v_h = jnp.where(lane_t < d, head_lanes(v_ref[0], h), jnp.where(lane_t == d, 1.0, 0.0))
            vaug_ref[h] = v_h.astype(BF16)

    row0 = pl.multiple_of(qi * tq, tq)
    q = q_ref[0] * (d ** -0.5)
    c_q = c_ref[0, pl.ds(row0, tq), :]
    qs = [augment(head_lanes(q, h), head_col(c_q, head0 + h), first=True) for h in range(nh)]
    causal = _tri(tq)
    lag = 2

    def step(j, carry, diagonal):
        start = pl.multiple_of(j * tq, tq)
        ss, new = {}, []
        for h in range(nh + lag):
            if h < nh:
                ss[h] = lax.dot_general(qs[h], kaug_ref[h, pl.ds(start, tq), :], (((1,), (1,)), ((), ())),
                                        preferred_element_type=F32)
            g = h - lag
            if g >= 0:
                m, acc = carry[g]
                s = ss.pop(g)
                s = jnp.where(causal, s, -jnp.inf) if diagonal else s
                m_new = jnp.maximum(m, jnp.max(s, axis=-1, keepdims=True))
                p = jnp.exp(s - m_new).astype(BF16)
                pv = jnp.dot(p, vaug_ref[g, pl.ds(start, tq), :], preferred_element_type=F32)
                new.append((m_new, acc * jnp.exp(m - m_new) + pv))
        return tuple(new)

    init = tuple((jnp.full((tq, 1), -jnp.inf, F32), jnp.zeros((tq, LANE), F32)) for _ in range(nh))
    carry = lax.fori_loop(0, qi, lambda j, cr: step(j, cr, False), init)
    carry = step(qi, carry, True)
    lane = lax.broadcasted_iota(jnp.int32, (tq, LANE), 1)
    for grp in range(groups):
        out = None
        for hh in range(per):
            acc = carry[grp * per + hh][1]
            o_h = acc / jnp.sum(jnp.where(lane == d, acc, 0.0), axis=-1, keepdims=True)
            o_h = o_h if hh == 0 else pltpu.roll(o_h, hh * d, axis=1)
            out = o_h if out is None else jnp.where(lane >= hh * d, o_h, out)
        o_ref[0, :, grp * LANE:(grp + 1) * LANE] = out


def _fox(q, k, v, c_col, tq=256, groups=4):
    b, t, w = q.shape
    gw = groups * LANE
    return pl.pallas_call(
        functools.partial(_fox_kernel, tq=tq, groups=groups),
        grid=(b, w // gw, t // tq),
        in_specs=[pl.BlockSpec((1, tq, gw), lambda i, p, j: (i, j, p)),
                  pl.BlockSpec((1, t, gw), lambda i, p, j: (i, 0, p)),
                  pl.BlockSpec((1, t, gw), lambda i, p, j: (i, 0, p)),
                  pl.BlockSpec((1, t, LANE), lambda i, p, j: (i, 0, 0))],
        out_specs=pl.BlockSpec((1, tq, gw), lambda i, p, j: (i, j, p)),
        out_shape=jax.ShapeDtypeStruct((b, t, w), F32),
        scratch_shapes=[pltpu.VMEM((groups * LANE // FOX_HD, t, LANE), BF16),
                        pltpu.VMEM((groups * LANE // FOX_HD, t, LANE), BF16)],
        compiler_params=_cparams("parallel", "arbitrary", "arbitrary"),
        name="fox_attention",
    )(q, k, v, c_col)


def _serpentine(i, f, nf):
    return jnp.where(i % 2 == 0, f, nf - 1 - f)


def _mix_residual(h_ref, ya_ref, yb_ref, wa_ref, wb_ref):
    return h_ref[...] + _dot(ya_ref[...], wa_ref[...]) + _dot(yb_ref[...], wb_ref[...])


def _ffn_kernel(h_ref, ya_ref, yb_ref, wa_ref, wb_ref, g_ref, w1_ref, w3_ref, w2_ref, o_ref, xn_ref, acc_ref):
    f = pl.program_id(1)

    @pl.when(f == 0)
    def _():
        hn = _mix_residual(h_ref, ya_ref, yb_ref, wa_ref, wb_ref)
        xn_ref[...] = _rmsnorm(hn, g_ref[...]).astype(BF16)
        acc_ref[...] = hn

    xn = xn_ref[...]
    mid = _silu(jnp.dot(xn, w1_ref[...], preferred_element_type=F32)) * jnp.dot(
        xn, w3_ref[...], preferred_element_type=F32)
    acc_ref[...] += _dot(mid, w2_ref[...])

    @pl.when(f == pl.num_programs(1) - 1)
    def _():
        o_ref[...] = acc_ref[...]


def _ffn(h, ya, yb, wa, wb, g, w1, w3, w2, tm=512, tf=1408):
    m, d = h.shape
    k = ya.shape[1]
    nf = w1.shape[1] // tf
    rows = lambda n: pl.BlockSpec((tm, n), lambda i, f: (i, 0))
    full = lambda shape: pl.BlockSpec(shape, lambda i, f: (0, 0))
    return pl.pallas_call(
        _ffn_kernel,
        grid=(m // tm, nf),
        in_specs=[rows(d), rows(k), rows(k), full((k, d)), full((k, d)), full((1, d)),
                  pl.BlockSpec((d, tf), lambda i, f: (0, _serpentine(i, f, nf))),
                  pl.BlockSpec((d, tf), lambda i, f: (0, _serpentine(i, f, nf))),
                  pl.BlockSpec((tf, d), lambda i, f: (_serpentine(i, f, nf), 0))],
        out_specs=rows(d),
        out_shape=jax.ShapeDtypeStruct((m, d), F32),
        scratch_shapes=[pltpu.VMEM((tm, d), BF16), pltpu.VMEM((tm, d), F32)],
        compiler_params=_cparams("parallel", "arbitrary"),
        name="ffn_dense",
    )(h, ya, yb, wa, wb, g.reshape(1, d), w1, w3, w2)


def _split2(x):
    hi = x.astype(BF16)
    return hi, (x - hi.astype(F32)).astype(BF16)


def _moe_route_kernel(h_ref, ya_ref, yb_ref, wa_ref, wb_ref, g_ref, r_ref, tri_ref, hn_ref, xn_ref, meta_ref,
                      cnt_ref, carry_ref):
    @pl.when(pl.program_id(0) == 0)
    def _():
        carry_ref[...] = jnp.zeros(carry_ref.shape, F32)

    hn = _mix_residual(h_ref, ya_ref, yb_ref, wa_ref, wb_ref)
    hn_ref[...] = hn
    xn = _rmsnorm(hn, g_ref[...])
    xn_ref[...] = xn
    tm = xn.shape[0]
    lane = lax.broadcasted_iota(jnp.int32, (tm, LANE), 1)
    x_hi, x_lo = _split2(xn)
    r_hi, r_lo = _split2(r_ref[...])
    dot = functools.partial(jnp.dot, preferred_element_type=F32)
    logits = jnp.where(lane < N_EXPERTS, dot(x_hi, r_hi) + (dot(x_hi, r_lo) + dot(x_lo, r_hi)), -jnp.inf)
    m1 = jnp.max(logits, axis=-1, keepdims=True)
    i1 = jnp.min(jnp.where(logits == m1, lane, LANE), axis=-1, keepdims=True)
    rest = jnp.where(lane == i1, -jnp.inf, logits)
    m2 = jnp.max(rest, axis=-1, keepdims=True)
    i2 = jnp.min(jnp.where(rest == m2, lane, LANE), axis=-1, keepdims=True)
    e2 = jnp.exp(m2 - m1)
    g1 = 1.0 / (1.0 + e2)
    g2 = e2 / (1.0 + e2)
    onehot = jnp.where(lane == i1, 1.0, jnp.where(lane == i2, 1.0, 0.0))
    before = dot(tri_ref[...], onehot.astype(BF16)) + carry_ref[0:1, :]
    r1 = jnp.sum(jnp.where(lane == i1, before, 0.0), axis=-1, keepdims=True)
    r2 = jnp.sum(jnp.where(lane == i2, before, 0.0), axis=-1, keepdims=True)
    meta = jnp.where(lane == 0, i1.astype(F32), jnp.where(lane == 1, i2.astype(F32), 0.0))
    meta = jnp.where(lane == 2, g1, jnp.where(lane == 3, g2, meta))
    meta_ref[...] = jnp.where(lane == 4, r1, jnp.where(lane == 5, r2, meta))
    total = carry_ref[0:1, :] + jnp.sum(onehot, axis=0, keepdims=True)
    carry_ref[...] = jnp.broadcast_to(total, carry_ref.shape)
    cnt_ref[...] = jnp.broadcast_to(total, cnt_ref.shape)


def _moe_route(h, ya, yb, wa, wb, g, router, tm=512):
    m, d = h.shape
    k = ya.shape[1]
    rows = lambda n: pl.BlockSpec((tm, n), lambda i: (i, 0))
    full = lambda shape: pl.BlockSpec(shape, lambda i: (0, 0))
    return pl.pallas_call(
        _moe_route_kernel,
        grid=(m // tm,),
        in_specs=[rows(d), rows(k), rows(k), full((k, d)), full((k, d)), full((1, d)), full((d, LANE)),
                  full((tm, tm))],
        out_specs=[rows(d), rows(d), rows(LANE), full((8, LANE))],
        out_shape=[jax.ShapeDtypeStruct((m, d), F32), jax.ShapeDtypeStruct((m, d), F32),
                   jax.ShapeDtypeStruct((m, LANE), F32), jax.ShapeDtypeStruct((8, LANE), F32)],
        scratch_shapes=[pltpu.VMEM((8, LANE), F32)],
        compiler_params=_cparams("arbitrary"),
        name="moe_route",
    )(h, ya, yb, wa, wb, g.reshape(1, d), router, jnp.tril(jnp.ones((tm, tm), BF16), -1))


def _row_copy(src_ref, src_group, src_sub, dst_ref, dst_group, dst_sub, sem):
    return pltpu.make_async_copy(src_ref.at[src_group, pl.ds(src_sub, 1)],
                                 dst_ref.at[dst_group, pl.ds(dst_sub, 1)], sem)


def _split_row(p):
    return lax.shift_right_logical(p, 3), lax.bitwise_and(p, _SUBLANES - 1)


def _moe_dispatch_kernel(pos_ref, xn_ref, xs_in_ref, xs_ref, sem, *, tm):
    del xs_in_ref

    def start(grp, carry):
        for u in range(_SUBLANES):
            for s in range(2):
                dst_group, dst_sub = _split_row(pos_ref[0, 0, 2 * (grp * _SUBLANES + u) + s])
                _row_copy(xn_ref, grp, u, xs_ref, dst_group, dst_sub, sem).start(priority=s)
        return carry

    def wait(grp, carry):
        for _ in range(2 * _SUBLANES):
            _row_copy(xn_ref, 0, 0, xs_ref, 0, 0, sem).wait()
        return carry

    lax.fori_loop(0, tm // _SUBLANES, start, 0)
    lax.fori_loop(0, tm // _SUBLANES, wait, 0)


def _moe_dispatch(xn, pos, n_rows, tm=256):
    m, d = xn.shape
    xs = pl.pallas_call(
        functools.partial(_moe_dispatch_kernel, tm=tm),
        grid=(m // tm,),
        in_specs=[pl.BlockSpec((1, 1, 2 * tm), lambda i: (i, 0, 0), memory_space=pltpu.SMEM),
                  pl.BlockSpec((tm // _SUBLANES, _SUBLANES, d), lambda i: (i, 0, 0)),
                  pl.BlockSpec(memory_space=pl.ANY)],
        out_specs=pl.BlockSpec(memory_space=pl.ANY),
        out_shape=jax.ShapeDtypeStruct((n_rows // _SUBLANES, _SUBLANES, d), F32),
        scratch_shapes=[pltpu.SemaphoreType.DMA(())],
        input_output_aliases={2: 0},
        compiler_params=_cparams("arbitrary"),
        name="moe_dispatch",
    )(pos.reshape(m // tm, 1, 2 * tm), xn.reshape(m // _SUBLANES, _SUBLANES, d),
      jnp.zeros((n_rows // _SUBLANES, _SUBLANES, d), F32))
    return xs.reshape(n_rows, d)


def _moe_expert_kernel(te_ref, nu_ref, x_ref, w1_ref, w3_ref, w2_ref, y_ref, xbf_ref, acc_ref):
    del te_ref
    f = pl.program_id(1)

    @pl.when(f == 0)
    def _():
        xbf_ref[...] = x_ref[...].astype(BF16)
        acc_ref[...] = jnp.zeros(acc_ref.shape, F32)

    @pl.when(pl.program_id(0) < nu_ref[0])
    def _():
        xb = xbf_ref[...]
        mid = _silu(jnp.dot(xb, w1_ref[0], preferred_element_type=F32)) * jnp.dot(
            xb, w3_ref[0], preferred_element_type=F32)
        acc_ref[...] += _dot(mid, w2_ref[0])

    @pl.when(f == pl.num_programs(1) - 1)
    def _():
        y_ref[...] = acc_ref[...]


def _moe_experts(xs, tile_expert, n_used, w1, w3, w2, tmx, tf=1792):
    n_rows, d = xs.shape
    nf = w1.shape[2] // tf

    def fidx(i, f, nu):
        last = i >= nu[0]
        return _serpentine(jnp.where(last, nu[0] - 1, i), jnp.where(last, nf - 1, f), nf)

    grid_spec = pltpu.PrefetchScalarGridSpec(
        num_scalar_prefetch=2,
        grid=(n_rows // tmx, nf),
        in_specs=[pl.BlockSpec((tmx, d), lambda i, f, te, nu: (i, 0)),
                  pl.BlockSpec((1, d, tf), lambda i, f, te, nu: (te[i], 0, fidx(i, f, nu))),
                  pl.BlockSpec((1, d, tf), lambda i, f, te, nu: (te[i], 0, fidx(i, f, nu))),
                  pl.BlockSpec((1, tf, d), lambda i, f, te, nu: (te[i], fidx(i, f, nu), 0))],
        out_specs=pl.BlockSpec((tmx, d), lambda i, f, te, nu: (i, 0)),
        scratch_shapes=[pltpu.VMEM((tmx, d), BF16), pltpu.VMEM((tmx, d), F32)],
    )
    return pl.pallas_call(
        _moe_expert_kernel,
        grid_spec=grid_spec,
        out_shape=jax.ShapeDtypeStruct((n_rows, d), F32),
        compiler_params=_cparams("arbitrary", "arbitrary"),
        name="moe_experts",
    )(tile_expert, n_used, xs, w1, w3, w2)


def _moe_combine_kernel(pos_ref, posn_ref, h_ref, meta_ref, fg_ref, ys_ref, o_ref, ybuf_ref, sem, *, tm):
    i = pl.program_id(0)
    slot = lax.rem(i, 2)

    groups = tm // _SUBLANES

    def gather(p_ref, dst_slot):
        def start(grp, carry):
            for u in range(_SUBLANES):
                for s in range(2):
                    src_group, src_sub = _split_row(p_ref[0, 0, 2 * (grp * _SUBLANES + u) + s])
                    _row_copy(ys_ref, src_group, src_sub, ybuf_ref.at[dst_slot], s * groups + grp, u,
                              sem.at[dst_slot]).start(priority=s)
            return carry
        lax.fori_loop(0, groups, start, 0)

    @pl.when(i == 0)
    def _():
        gather(pos_ref, 0)

    @pl.when(i + 1 < pl.num_programs(0))
    def _():
        gather(posn_ref, 1 - slot)

    def wait(grp, carry):
        for _ in range(2 * _SUBLANES):
            _row_copy(ys_ref, 0, 0, ybuf_ref.at[slot], 0, 0, sem.at[slot]).wait()
        return carry

    lax.fori_loop(0, groups, wait, 0)
    lane = lax.broadcasted_iota(jnp.int32, (tm, LANE), 1)
    meta = meta_ref[...]
    g1 = jnp.sum(jnp.where(lane == 2, meta, 0.0), axis=-1, keepdims=True)
    g2 = jnp.sum(jnp.where(lane == 3, meta, 0.0), axis=-1, keepdims=True)
    y = ybuf_ref[slot].reshape(2 * tm, -1)
    o_ref[...] = _rmsnorm(h_ref[...] + g1 * y[:tm] + g2 * y[tm:], fg_ref[...])


def _moe_combine(h, meta, pos, ys, final_g, tm=256):
    m, d = h.shape
    nt = m // tm
    pos3 = pos.reshape(nt, 1, 2 * tm)
    smem = lambda imap: pl.BlockSpec((1, 1, 2 * tm), imap, memory_space=pltpu.SMEM)
    return pl.pallas_call(
        functools.partial(_moe_combine_kernel, tm=tm),
        grid=(nt,),
        in_specs=[smem(lambda i: (i, 0, 0)), smem(lambda i: (jnp.minimum(i + 1, nt - 1), 0, 0)),
                  pl.BlockSpec((tm, d), lambda i: (i, 0)), pl.BlockSpec((tm, LANE), lambda i: (i, 0)),
                  pl.BlockSpec((1, d), lambda i: (0, 0)), pl.BlockSpec(memory_space=pl.ANY)],
        out_specs=pl.BlockSpec((tm, d), lambda i: (i, 0)),
        out_shape=jax.ShapeDtypeStruct((m, d), F32),
        scratch_shapes=[pltpu.VMEM((2, 2 * tm // _SUBLANES, _SUBLANES, d), F32), pltpu.SemaphoreType.DMA((2,))],
        compiler_params=_cparams("arbitrary"),
        name="moe_combine",
    )(pos3, pos3, h, meta, final_g.reshape(1, d), ys.reshape(-1, _SUBLANES, d))


def _moe_routed(h, ya, yb, wa, wb, g, router, w1, w3, w2, final_g, tmx=512):
    m, d = h.shape
    h, xn, meta, cnt = _moe_route(h, ya, yb, wa, wb, g, router)
    experts = meta[:, 0:2].astype(jnp.int32)
    rank = meta[:, 4:6].astype(jnp.int32)
    counts = cnt[0, :N_EXPERTS].astype(jnp.int32)
    padded = (counts + tmx - 1) // tmx * tmx
    ends = jnp.cumsum(padded)
    pos = (ends - padded)[experts] + rank
    n_tiles = (2 * m) // tmx + N_EXPERTS
    n_used = ends[-1] // tmx
    tile_start = jnp.minimum(jnp.arange(n_tiles, dtype=jnp.int32), n_used - 1) * tmx
    tile_expert = jnp.sum(tile_start[:, None] >= ends[None, :], axis=1).astype(jnp.int32)
    xs = _moe_dispatch(xn, pos, n_tiles * tmx)
    ys = _moe_experts(xs, tile_expert, n_used.reshape(1).astype(jnp.int32), w1, w3, w2, tmx)
    return _moe_combine(h, meta, pos, ys, final_g)


def _pad_cols(w, n):
    return jnp.pad(w, ((0, 0), (0, n - w.shape[1])))


def kernel(x, norm_mix_g, norm_ffn_g, w_in0, conv_w, conv_b, conv_ln_g, conv_ln_b, gla_w_a2, gla_b_a, gla_norm_g, w_out0, ffn_w1, ffn_w3, ffn_w2, w_in1, rwkv_mu, rwkv_w2, rwkv_w0, rwkv_a2, rwkv_a0, rwkv_g2, rwkv_k_k, rwkv_k_a, rwkv_r_k, rwkv_lnx_g, rwkv_lnx_b, fox_b_f, w_out1, moe_router, moe_w1, moe_w3, moe_w2, final_norm_g):
    b, t, d = x.shape
    m = b * t
    half = d // 2
    bf = lambda a: a.astype(BF16)
    h = x.reshape(m, d)

    y_conv, y_gla = _layer0_mix(x, norm_mix_g[0], w_in0, conv_w, conv_b, conv_ln_g, conv_ln_b,
                                gla_w_a2, gla_b_a, gla_norm_g)
    h = _ffn(h, y_conv.reshape(m, half), y_gla.reshape(m, half), bf(w_out0[:half]), bf(w_out0[half:]),
             norm_ffn_g[0], bf(ffn_w1), bf(ffn_w3), bf(ffn_w2))

    rc = _RWKV_COLS
    ws1 = [bf(w_in1[:, :rc]), bf(w_in1[:, rc:rc + half]), bf(w_in1[:, rc + half:rc + 2 * half]),
           bf(w_in1[:, rc + 2 * half:rc + 3 * half]), bf(_pad_cols(w_in1[:, rc + 3 * half:], LANE))]
    p_rwkv, fq, fk, fv, f_logit = [a.reshape(b, t, -1) for a in _norm_proj(h, norm_mix_g[1], ws1)]
    y_rwkv = _rwkv(p_rwkv, rwkv_mu, rwkv_w2, rwkv_w0, rwkv_a2, rwkv_a0, rwkv_g2, rwkv_k_k, rwkv_k_a,
                   rwkv_r_k, rwkv_lnx_g, rwkv_lnx_b)
    b_f = jnp.pad(fox_b_f, (0, LANE - FOX_HEADS)).reshape(1, LANE)
    y_fox = _fox(fq, fk, fv, _fox_cum(f_logit, b_f))
    out = _moe_routed(h, y_rwkv.reshape(m, half), y_fox.reshape(m, half), bf(w_out1[:half]), bf(w_out1[half:]),
                      norm_ffn_g[1], _pad_cols(moe_router, LANE), bf(moe_w1), bf(moe_w3), bf(moe_w2),
                      final_norm_g)
    return out.reshape(b, t, d)
```

```python
import functools

import jax
import jax.numpy as jnp
from jax import lax
from jax.experimental import pallas as pl
from jax.experimental.pallas import tpu as pltpu

F32 = jnp.float32
BF16 = jnp.bfloat16
HI = lax.Precision.HIGHEST

NORM_EPS = 1e-6
CONV_WIDTH = 31
CONV_LN_EPS = 1e-5
GLA_HEADS = 4
GLA_DK = 64
GLA_DV = 128
GLA_TAU = 16.0
GLA_CHUNK = 64
RWKV_HEADS = 8
RWKV_HD = 64
RWKV_CHUNK = 64
RWKV_DECAY_SCALE = 0.606531
RWKV_GN_EPS = 64e-5
RWKV_L2_EPS = 1e-12
FOX_HEADS = 8
FOX_HD = 64
N_EXPERTS = 8
LANE = 128
_SUBLANES = 8
VMEM_LIMIT = 56 * 1024 * 1024


def _cparams(*sem):
    return pltpu.CompilerParams(dimension_semantics=sem, vmem_limit_bytes=VMEM_LIMIT)


def _dot(a, b):
    return jnp.dot(a.astype(BF16), b.astype(BF16), preferred_element_type=F32)


def _dot_nt(a, b):
    return lax.dot_general(a.astype(BF16), b.astype(BF16), (((1,), (1,)), ((), ())),
                           preferred_element_type=F32)


def _dot_tn(a, b):
    return lax.dot_general(a.astype(BF16), b.astype(BF16), (((0,), (0,)), ((), ())),
                           preferred_element_type=F32)


def _dot_hi(a, b):
    return jnp.dot(a, b, precision=HI, preferred_element_type=F32)


def _dot_tn_hi(a, b):
    return lax.dot_general(a, b, (((0,), (0,)), ((), ())), precision=HI, preferred_element_type=F32)


def _sigmoid(x):
    return 1.0 / (1.0 + jnp.exp(-x))


def _silu(x):
    return x * _sigmoid(x)


def _log_sigmoid(x):
    return jnp.minimum(x, 0.0) - jnp.log(1.0 + jnp.exp(-jnp.abs(x)))


def _rmsnorm(x, g):
    return x * lax.rsqrt(jnp.mean(x * x, axis=-1, keepdims=True) + NORM_EPS) * g


def _tri(n, strict=False):
    r = lax.broadcasted_iota(jnp.int32, (n, n), 0)
    c = lax.broadcasted_iota(jnp.int32, (n, n), 1)
    return (r > c) if strict else (r >= c)


def _norm_proj_kernel(h_ref, g_ref, *refs):
    n = len(refs) // 2
    xn = _rmsnorm(h_ref[...], g_ref[...]).astype(BF16)
    for w_ref, o_ref in zip(refs[:n], refs[n:]):
        o_ref[...] = jnp.dot(xn, w_ref[...], preferred_element_type=F32)


def _norm_proj(h, g, ws, tm=512):
    m, d = h.shape
    return pl.pallas_call(
        _norm_proj_kernel,
        grid=(m // tm,),
        in_specs=[pl.BlockSpec((tm, d), lambda i: (i, 0)), pl.BlockSpec((1, d), lambda i: (0, 0))]
        + [pl.BlockSpec(w.shape, lambda i: (0, 0)) for w in ws],
        out_specs=[pl.BlockSpec((tm, w.shape[1]), lambda i: (i, 0)) for w in ws],
        out_shape=[jax.ShapeDtypeStruct((m, w.shape[1]), F32) for w in ws],
        compiler_params=_cparams("parallel"),
        name="norm_proj",
    )(h, g.reshape(1, d), *ws)


_CONV_HALO = 32
_CONV_ROWS = 32
_GLA_SPAN = 256


def _conv_body(val, gate, w_ref, b_ref, lg_ref, lb_ref, o_ref, u_ref, sh_ref):
    tt, c = val.shape
    rows = _CONV_ROWS
    u_ref[_CONV_HALO:_CONV_HALO + tt, :] = val * _sigmoid(gate)
    span = sh_ref.shape[1]
    for s in range(1, _SUBLANES):
        sh_ref[s - 1] = u_ref[s:s + span, :]
    base = _CONV_HALO - (CONV_WIDTH - 1)
    for r0 in range(0, tt, rows):
        acc = jnp.zeros((rows // _SUBLANES, _SUBLANES, c), F32) + b_ref[...]
        for j in range(CONV_WIDTH):
            phase = (base + j) % _SUBLANES
            row = r0 + base + j - phase
            win = u_ref[row:row + rows, :] if phase == 0 else sh_ref[phase - 1, row:row + rows, :]
            acc = acc + win.reshape(rows // _SUBLANES, _SUBLANES, c) * w_ref[j]
        acc = acc.reshape(rows, c)
        mu = jnp.mean(acc, axis=-1, keepdims=True)
        cen = acc - mu
        var = jnp.mean(cen * cen, axis=-1, keepdims=True)
        y = cen * lax.rsqrt(var + CONV_LN_EPS) * lg_ref[...] + lb_ref[...]
        o_ref[0, r0:r0 + rows, :] = _silu(y)
    u_ref[0:_CONV_HALO, :] = u_ref[tt:tt + _CONV_HALO, :]


def _gla_body(q, k, v, go, alr, wa2_ref, ba_ref, ng_ref, o_ref, s_ref, row0):
    c = GLA_CHUNK
    nc = q.shape[0] // c
    tril = _tri(c)
    tril_f = tril.astype(F32)
    ones_cv = jnp.ones((c, GLA_DV), F32)
    units = [(ci, h) for ci in range(nc) for h in range(GLA_HEADS)]
    rows_of = lambda ci: slice(ci * c, (ci + 1) * c)
    keys_of = lambda h: slice(h * GLA_DK, (h + 1) * GLA_DK)
    vals_of = lambda h: slice(h * GLA_DV, (h + 1) * GLA_DV)

    z = _dot(alr, wa2_ref[...]) + ba_ref[...]
    log_a = _log_sigmoid(z) * (1.0 / GLA_TAU)
    q = q * (GLA_DK ** -0.5)
    q_decs, k_decs, k_ends, decays = [], [], [], []
    for ci in range(nc):
        la = log_a[rows_of(ci)]
        bcum = _dot_hi(tril_f, la)
        b_last = bcum[c - 1:c, :]
        q_decs.append(q[rows_of(ci)] * jnp.exp(bcum))
        k_decs.append(k[rows_of(ci)] * jnp.exp(-bcum))
        k_ends.append(k[rows_of(ci)] * jnp.exp(b_last - bcum))
        decays.append(jnp.exp(_dot_tn_hi(la, ones_cv)))

    vs = [v[rows_of(ci), vals_of(h)] for ci, h in units]
    attns = [jnp.where(tril, _dot_nt(q_decs[ci][:, keys_of(h)], k_decs[ci][:, keys_of(h)]), 0.0)
             for ci, h in units]
    kvs = [_dot_tn(k_ends[ci][:, keys_of(h)], vs[u]) for u, (ci, h) in enumerate(units)]
    intras = [_dot(attns[u], vs[u]) for u in range(len(units))]
    states = [s_ref[h] for h in range(GLA_HEADS)]
    prevs = []
    for u, (ci, h) in enumerate(units):
        prevs.append(states[h])
        states[h] = states[h] * decays[ci][keys_of(h)] + kvs[u]
    for h in range(GLA_HEADS):
        s_ref[h] = states[h]
    inters = [_dot(q_decs[ci][:, keys_of(h)], prevs[u]) for u, (ci, h) in enumerate(units)]
    for ci in range(nc):
        outs = []
        for h in range(GLA_HEADS):
            u = ci * GLA_HEADS + h
            o = intras[u] + inters[u]
            outs.append(o * lax.rsqrt(jnp.mean(o * o, axis=-1, keepdims=True) + NORM_EPS))
        o_all = jnp.concatenate(outs, axis=-1)
        o_ref[0, row0 + ci * c:row0 + (ci + 1) * c, :] = o_all * ng_ref[...] * _silu(go[rows_of(ci)])


def _layer0_kernel(h_ref, g_ref, wcv_ref, wcg_ref, wq_ref, wk_ref, wv_ref, wgo_ref, walr_ref,
                   cw_ref, cb_ref, clg_ref, clb_ref, wa2_ref, ba_ref, ng_ref,
                   yc_ref, yg_ref, u_ref, sh_ref, s_ref):
    @pl.when(pl.program_id(1) == 0)
    def _():
        u_ref[0:_CONV_HALO, :] = jnp.zeros((_CONV_HALO, u_ref.shape[1]), F32)
        s_ref[...] = jnp.zeros(s_ref.shape, F32)

    xn = _rmsnorm(h_ref[0], g_ref[...]).astype(BF16)
    proj = lambda w_ref, rows=slice(None): jnp.dot(xn[rows], w_ref[...], preferred_element_type=F32)
    _conv_body(proj(wcv_ref), proj(wcg_ref), cw_ref, cb_ref, clg_ref, clb_ref, yc_ref, u_ref, sh_ref)
    for row0 in range(0, xn.shape[0], _GLA_SPAN):
        rows = slice(row0, row0 + _GLA_SPAN)
        _gla_body(proj(wq_ref, rows), proj(wk_ref, rows), proj(wv_ref, rows), proj(wgo_ref, rows),
                  proj(walr_ref, rows), wa2_ref, ba_ref, ng_ref, yg_ref, s_ref, row0)


def _layer0_mix(x, g, w_in, conv_w, conv_b, ln_g, ln_b, w_a2, b_a, norm_g, tt=512):
    b, t, d = x.shape
    c = d // 2
    gk = GLA_HEADS * GLA_DK
    o = [0, c, 2 * c, 2 * c + gk, 2 * c + 2 * gk, 3 * c + 2 * gk, 4 * c + 2 * gk]
    ws = [w_in[:, o[i]:o[i + 1]].astype(BF16) for i in range(6)] + [_pad_cols(w_in[:, o[6]:], LANE).astype(BF16)]
    wrep = jnp.broadcast_to(conv_w[:, None, :], (CONV_WIDTH, _SUBLANES, c))
    w_a2p = jnp.pad(w_a2, ((0, LANE - w_a2.shape[0]), (0, 0)))
    vec = lambda a: a.reshape(1, -1)
    full = lambda a: pl.BlockSpec(a.shape, lambda i, j: (0,) * a.ndim)
    tile = lambda n: pl.BlockSpec((1, tt, n), lambda i, j: (i, j, 0))
    consts = ws + [wrep, vec(conv_b), vec(ln_g), vec(ln_b), w_a2p, vec(b_a), vec(norm_g)]
    return pl.pallas_call(
        _layer0_kernel,
        grid=(b, t // tt),
        in_specs=[tile(d), full(vec(g))] + [full(a) for a in consts],
        out_specs=[tile(c), tile(c)],
        out_shape=[jax.ShapeDtypeStruct((b, t, c), F32), jax.ShapeDtypeStruct((b, t, c), F32)],
        scratch_shapes=[pltpu.VMEM((tt + _CONV_HALO, c), F32),
                        pltpu.VMEM((_SUBLANES - 1, tt + _CONV_HALO - _SUBLANES, c), F32),
                        pltpu.VMEM((GLA_HEADS, GLA_DK, GLA_DV), F32)],
        compiler_params=_cparams("parallel", "arbitrary"),
        name="layer0_mix",
    )(x, vec(g), *consts)


_RWKV_COLS = 3 * 512 + 64 + 64 + 128


def _rwkv_kernel(p_ref, mu_ref, w2_ref, w0_ref, a2_ref, a0_ref, g2_ref, kk_ref, ka_ref, rk_ref,
                 lg_ref, lb_ref, o_ref, xs_ref, st_ref, *, tt):
    c = RWKV_CHUNK
    hd = RWKV_HD
    w = RWKV_HEADS * hd

    @pl.when(pl.program_id(1) == 0)
    def _():
        xs_ref[0:8, :] = jnp.zeros((8, xs_ref.shape[1]), F32)
        st_ref[...] = jnp.zeros(st_ref.shape, F32)

    x = p_ref[0]
    xs_ref[8:8 + tt, :] = x
    prev = xs_ref[7:7 + tt, :]
    xs_ref[7:8, :] = x[tt - 1:tt, :]
    x = x + (prev - x) * mu_ref[...]

    r = x[:, 0:w]
    k = x[:, w:2 * w]
    v = x[:, 2 * w:3 * w]
    xw = x[:, 3 * w:3 * w + 64]
    xa = x[:, 3 * w + 64:3 * w + 128]
    xg = x[:, 3 * w + 128:3 * w + 256]
    log_w = -RWKV_DECAY_SCALE * _sigmoid(w0_ref[...] + _dot(jnp.tanh(xw), w2_ref[...]))
    a = _sigmoid(a0_ref[...] + _dot(xa, a2_ref[...]))
    g = _dot(_sigmoid(xg), g2_ref[...])
    kk = k * kk_ref[...]
    k_mod = k * (1.0 + (a - 1.0) * ka_ref[...])
    rkr = r * k_mod * rk_ref[...]

    tril = _tri(c)
    tril_s = _tri(c, strict=True)
    tril_f = tril.astype(F32)
    eye = (lax.broadcasted_iota(jnp.int32, (c, c), 0)
           == lax.broadcasted_iota(jnp.int32, (c, c), 1)).astype(F32)
    zeros_cc = jnp.zeros((c, hd), F32)

    nc = tt // c
    units = [(ci, h) for ci in range(nc) for h in range(RWKV_HEADS)]
    rows_of = lambda ci: slice(ci * c, (ci + 1) * c)
    lanes_of = lambda h: slice(h * hd, (h + 1) * hd)

    exps = []
    for ci in range(nc):
        lw_c = log_w[rows_of(ci)]
        lcum = _dot_hi(tril_f, lw_c)
        l_last = lcum[c - 1:c, :]
        exps.append((jnp.exp(lcum), jnp.exp(-lcum), jnp.exp(l_last - lcum), jnp.exp(lcum - lw_c),
                     jnp.exp(l_last)))

    aqs, rqs, vs, lhss, rhss, bkes = [], [], [], [], [], []
    for ci, h in units:
        rows, hs = rows_of(ci), lanes_of(h)
        e_pos, e_neg, e_end, e_prev, _ = exps[ci]
        kk_h = kk[rows, hs]
        nrm = jnp.sqrt(jnp.sum(kk_h * kk_h, axis=-1, keepdims=True))
        kk_h = kk_h / jnp.maximum(nrm, RWKV_L2_EPS)
        k_h = k_mod[rows, hs]
        kka = kk_h * a[rows, hs]
        aq = -kk_h * e_prev[:, hs]
        rq = r[rows, hs] * e_pos[:, hs]
        aqs.append(aq)
        rqs.append(rq)
        vs.append(v[rows, hs])
        lhss.append(jnp.concatenate([aq, rq], axis=0))
        rhss.append(jnp.concatenate([kka * e_neg[:, hs], k_h * e_neg[:, hs]], axis=0))
        bkes.append(jnp.concatenate([kka * e_end[:, hs], k_h * e_end[:, hs]], axis=0))

    xss = [_dot_nt(lhs, rhs) for lhs, rhs in zip(lhss, rhss)]
    m1s = [jnp.where(tril_s, xs[:c, :c], 0.0) for xs in xss]
    m2s = [jnp.where(tril_s, xs[:c, c:], 0.0) for xs in xss]
    n12s = [jnp.concatenate([jnp.where(tril, xs[c:, :c], 0.0), jnp.where(tril, xs[c:, c:], 0.0)], axis=1)
            for xs in xss]
    tinvs = [eye + m1 for m1 in m1s]
    mps = m1s
    for _ in range(5):
        mps = [_dot(mp, mp) for mp in mps]
        tinvs = [tinv + _dot(tinv, mp) for tinv, mp in zip(tinvs, mps)]
    mvs = [_dot(m2, v_h) for m2, v_h in zip(m2s, vs)]
    pqs = [_dot(tinv, jnp.concatenate([mv, aq], axis=1)) for tinv, mv, aq in zip(tinvs, mvs, aqs)]
    pqvs = [jnp.concatenate([pq, jnp.concatenate([v_h, zeros_cc], axis=1)], axis=0)
            for pq, v_h in zip(pqs, vs)]
    yys = [_dot(n12, pqv) for n12, pqv in zip(n12s, pqvs)]
    ghs = [_dot_tn(bke, pqv) for bke, pqv in zip(bkes, pqvs)]

    states = [st_ref[h] for h in range(RWKV_HEADS)]
    ys = []
    for u, (ci, h) in enumerate(units):
        yq = rqs[u] + yys[u][:, hd:]
        gmat = eye * exps[ci][4][:, lanes_of(h)] + ghs[u][:, hd:]
        res = _dot(jnp.concatenate([yq, gmat], axis=0), states[h])
        ys.append(res[:c] + yys[u][:, :hd])
        states[h] = res[c:] + ghs[u][:, :hd]
    for h in range(RWKV_HEADS):
        st_ref[h] = states[h]

    out_rows = []
    for ci in range(nc):
        yns, bonuses = [], []
        for h in range(RWKV_HEADS):
            u = ci * RWKV_HEADS + h
            y = ys[u]
            cen = y - jnp.mean(y, axis=-1, keepdims=True)
            var = jnp.mean(cen * cen, axis=-1, keepdims=True)
            yns.append(cen * lax.rsqrt(var + RWKV_GN_EPS))
            bonuses.append(jnp.sum(rkr[rows_of(ci), lanes_of(h)], axis=-1, keepdims=True) * vs[u])
        yn = jnp.concatenate(yns, axis=1)
        bonus = jnp.concatenate(bonuses, axis=1)
        out_rows.append((yn * lg_ref[...] + lb_ref[...] + bonus) * g[rows_of(ci)])
    o_ref[0] = jnp.concatenate(out_rows, axis=0)


def _rwkv(p, mu, w_w2, w0, w_a2, a0, w_g2, k_k, k_a, r_k, lnx_g, lnx_b, tt=128):
    b, t, cols = p.shape
    w = RWKV_HEADS * RWKV_HD
    vec = lambda a_: a_.reshape(1, -1)
    full = lambda shape: pl.BlockSpec(shape, lambda i, j: (0, 0))
    return pl.pallas_call(
        functools.partial(_rwkv_kernel, tt=tt),
        grid=(b, t // tt),
        in_specs=[pl.BlockSpec((1, tt, cols), lambda i, j: (i, j, 0)), full((1, cols)),
                  full(w_w2.shape), full((1, w)), full(w_a2.shape), full((1, w)), full(w_g2.shape),
                  full((1, w)), full((1, w)), full((1, w)), full((1, w)), full((1, w))],
        out_specs=pl.BlockSpec((1, tt, w), lambda i, j: (i, j, 0)),
        out_shape=jax.ShapeDtypeStruct((b, t, w), F32),
        scratch_shapes=[pltpu.VMEM((8 + tt, cols), F32),
                        pltpu.VMEM((RWKV_HEADS, RWKV_HD, RWKV_HD), F32)],
        compiler_params=_cparams("parallel", "arbitrary"),
        name="rwkv7",
    )(p, vec(mu), w_w2, vec(w0), w_a2, vec(a0), w_g2, vec(k_k), vec(k_a), vec(r_k), vec(lnx_g), vec(lnx_b))


def _fox_cum_kernel(f_ref, bf_ref, col_ref, carry_ref, *, tt):
    @pl.when(pl.program_id(1) == 0)
    def _():
        carry_ref[...] = jnp.zeros(carry_ref.shape, F32)

    log_f = _log_sigmoid(f_ref[0] + bf_ref[...])
    cum = _dot_hi(_tri(tt).astype(F32), log_f) + carry_ref[0:1, :]
    col_ref[0] = cum
    carry_ref[...] = jnp.broadcast_to(cum[tt - 1:tt, :], carry_ref.shape)


def _fox_cum(f_logit, b_f, tt=256):
    b, t, n = f_logit.shape
    return pl.pallas_call(
        functools.partial(_fox_cum_kernel, tt=tt),
        grid=(b, t // tt),
        in_specs=[pl.BlockSpec((1, tt, n), lambda i, j: (i, j, 0)), pl.BlockSpec((1, n), lambda i, j: (0, 0))],
        out_specs=pl.BlockSpec((1, tt, n), lambda i, j: (i, j, 0)),
        out_shape=jax.ShapeDtypeStruct((b, t, n), F32),
        scratch_shapes=[pltpu.VMEM((8, n), F32)],
        compiler_params=_cparams("parallel", "arbitrary"),
        name="fox_cumsum",
    )(f_logit, b_f)


def _split3(x):
    hi = x.astype(BF16).astype(F32)
    mid = (x - hi).astype(BF16).astype(F32)
    lo = (x - hi - mid).astype(BF16).astype(F32)
    return hi, mid, lo


def _fox_kernel(q_ref, k_ref, v_ref, c_ref, o_ref, kaug_ref, vaug_ref, *, tq, groups):
    qi = pl.program_id(2)
    d = FOX_HD
    per = LANE // d
    nh = groups * per
    head0 = pl.program_id(1) * nh

    def augment(x_h, c, first):
        lane = lax.broadcasted_iota(jnp.int32, x_h.shape, 1)
        hi, mid, lo = _split3(c)
        parts = (hi, mid, lo, 1.0, 1.0, 1.0) if first else (1.0, 1.0, 1.0, -hi, -mid, -lo)
        out = jnp.where(lane < d, x_h, 0.0)
        for n, part in enumerate(parts):
            out = jnp.where(lane == d + n, part, out)
        return out.astype(BF16)

    def head_col(c_all, head):
        lane = lax.broadcasted_iota(jnp.int32, c_all.shape, 1)
        return jnp.sum(jnp.where(lane == head, c_all, 0.0), axis=-1, keepdims=True)

    def head_lanes(x, h):
        x_g = x[:, (h // per) * LANE:(h // per + 1) * LANE]
        return x_g if h % per == 0 else pltpu.roll(x_g, LANE - (h % per) * d, axis=1)

    @pl.when(qi == 0)
    def _():
        lane_t = lax.broadcasted_iota(jnp.int32, (k_ref.shape[1], LANE), 1)
        for h in range(nh):
            kaug_ref[h] = augment(head_lanes(k_ref[0], h), head_col(c_ref[0], head0 + h), first=False)
            v_h = jnp.where(lane_t < d, head_lanes(v_ref[0], h), jnp.where(lane_t == d, 1.0, 0.0))
            vaug_ref[h] = v_h.astype(BF16)

    row0 = pl.multiple_of(qi * tq, tq)
    q = q_ref[0] * (d ** -0.5)
    c_q = c_ref[0, pl.ds(row0, tq), :]
    qs = [augment(head_lanes(q, h), head_col(c_q, head0 + h), first=True) for h in range(nh)]
    causal = _tri(tq)
    lag = 2

    def step(j, carry, diagonal):
        start = pl.multiple_of(j * tq, tq)
        ss, new = {}, []
        for h in range(nh + lag):
            if h < nh:
                ss[h] = lax.dot_general(qs[h], kaug_ref[h, pl.ds(start, tq), :], (((1,), (1,)), ((), ())),
                                        preferred_element_type=F32)
            g = h - lag
            if g >= 0:
                m, acc = carry[g]
                s = ss.pop(g)
                s = jnp.where(causal, s, -jnp.inf) if diagonal else s
                m_new = jnp.maximum(m, jnp.max(s, axis=-1, keepdims=True))
                p = jnp.exp(s - m_new).astype(BF16)
                pv = jnp.dot(p, vaug_ref[g, pl.ds(start, tq), :], preferred_element_type=F32)
                new.append((m_new, acc * jnp.exp(m - m_new) + pv))
        return tuple(new)

    init = tuple((jnp.full((tq, 1), -jnp.inf, F32), jnp.zeros((tq, LANE), F32)) for _ in range(nh))
    carry = lax.fori_loop(0, qi, lambda j, cr: step(j, cr, False), init)
    carry = step(qi, carry, True)
    lane = lax.broadcasted_iota(jnp.int32, (tq, LANE), 1)
    for grp in range(groups):
        out = None
        for hh in range(per):
            acc = carry[grp * per + hh][1]
            o_h = acc / jnp.sum(jnp.where(lane == d, acc, 0.0), axis=-1, keepdims=True)
            o_h = o_h if hh == 0 else pltpu.roll(o_h, hh * d, axis=1)
            out = o_h if out is None else jnp.where(lane >= hh * d, o_h, out)
        o_ref[0, :, grp * LANE:(grp + 1) * LANE] = out


def _fox(q, k, v, c_col, tq=256, groups=4):
    b, t, w = q.shape
    gw = groups * LANE
    return pl.pallas_call(
        functools.partial(_fox_kernel, tq=tq, groups=groups),
        grid=(b, w // gw, t // tq),
        in_specs=[pl.BlockSpec((1, tq, gw), lambda i, p, j: (i, j, p)),
                  pl.BlockSpec((1, t, gw), lambda i, p, j: (i, 0, p)),
                  pl.BlockSpec((1, t, gw), lambda i, p, j: (i, 0, p)),
                  pl.BlockSpec((1, t, LANE), lambda i, p, j: (i, 0, 0))],
        out_specs=pl.BlockSpec((1, tq, gw), lambda i, p, j: (i, j, p)),
        out_shape=jax.ShapeDtypeStruct((b, t, w), F32),
        scratch_shapes=[pltpu.VMEM((groups * LANE // FOX_HD, t, LANE), BF16),
                        pltpu.VMEM((groups * LANE // FOX_HD, t, LANE), BF16)],
        compiler_params=_cparams("parallel", "arbitrary", "arbitrary"),
        name="fox_attention",
    )(q, k, v, c_col)


def _serpentine(i, f, nf):
    return jnp.where(i % 2 == 0, f, nf - 1 - f)


def _mix_residual(h_ref, ya_ref, yb_ref, wa_ref, wb_ref):
    return h_ref[...] + _dot(ya_ref[...], wa_ref[...]) + _dot(yb_ref[...], wb_ref[...])


def _ffn_kernel(h_ref, ya_ref, yb_ref, wa_ref, wb_ref, g_ref, w1_ref, w3_ref, w2_ref, *refs, n_cast, n_side):
    cast_in, (o_ref, *cast_out, zero_ref), (xn_ref, acc_ref) = (
        refs[:n_cast], refs[n_cast:2 * n_cast + 2], refs[2 * n_cast + 2:])
    f = pl.program_id(1)

    @pl.when(f == 0)
    def _():
        hn = _mix_residual(h_ref, ya_ref, yb_ref, wa_ref, wb_ref)
        xn_ref[...] = _rmsnorm(hn, g_ref[...]).astype(BF16)
        acc_ref[...] = hn

    xn = xn_ref[...]
    mid = _silu(jnp.dot(xn, w1_ref[...], preferred_element_type=F32)) * jnp.dot(
        xn, w3_ref[...], preferred_element_type=F32)
    acc_ref[...] += _dot(mid, w2_ref[...])

    @pl.when(pl.program_id(0) * pl.num_programs(1) + f < n_side)
    def _():
        for src, dst in zip(cast_in, cast_out):
            dst[...] = src[...].astype(BF16)
        zero_ref[...] = jnp.zeros(zero_ref.shape, F32)

    @pl.when(f == pl.num_programs(1) - 1)
    def _():
        o_ref[...] = acc_ref[...]


def _ffn(h, ya, yb, wa, wb, g, w1, w3, w2, casts, zero_rows, tm=1024, tf=256):
    m, d = h.shape
    k = ya.shape[1]
    nf = w1.shape[1] // tf
    n_side = 1 << (((m // tm) * nf).bit_length() - 1)
    rows = lambda n: pl.BlockSpec((tm, n), lambda i, f: (i, 0))
    full = lambda shape: pl.BlockSpec(shape, lambda i, f: (0, 0))
    piece = lambda a_rows, cols: pl.BlockSpec(
        (a_rows // n_side, cols), lambda i, f: (jnp.minimum(i * nf + f, n_side - 1), 0))
    cast_specs = [piece(*a.shape) for a in casts]
    outs = pl.pallas_call(
        functools.partial(_ffn_kernel, n_cast=len(casts), n_side=n_side),
        grid=(m // tm, nf),
        in_specs=[rows(d), rows(k), rows(k), full((k, d)), full((k, d)), full((1, d)),
                  pl.BlockSpec((d, tf), lambda i, f: (0, _serpentine(i, f, nf))),
                  pl.BlockSpec((d, tf), lambda i, f: (0, _serpentine(i, f, nf))),
                  pl.BlockSpec((tf, d), lambda i, f: (_serpentine(i, f, nf), 0))] + cast_specs,
        out_specs=[rows(d)] + cast_specs + [piece(zero_rows, d)],
        out_shape=[jax.ShapeDtypeStruct((m, d), F32)] + [jax.ShapeDtypeStruct(a.shape, BF16) for a in casts]
        + [jax.ShapeDtypeStruct((zero_rows, d), F32)],
        scratch_shapes=[pltpu.VMEM((tm, d), BF16), pltpu.VMEM((tm, d), F32)],
        compiler_params=_cparams("parallel", "arbitrary"),
        name="ffn_dense",
    )(h, ya, yb, wa, wb, g.reshape(1, d), w1, w3, w2, *casts)
    return outs[0], outs[1:-1], outs[-1]


def _split2(x):
    hi = x.astype(BF16)
    return hi, (x - hi.astype(F32)).astype(BF16)


def _moe_route_kernel(h_ref, ya_ref, yb_ref, wa_ref, wb_ref, g_ref, r_ref, tri_ref, hn_ref, xn_ref, meta_ref,
                      cnt_ref, carry_ref):
    @pl.when(pl.program_id(0) == 0)
    def _():
        carry_ref[...] = jnp.zeros(carry_ref.shape, F32)

    hn = _mix_residual(h_ref, ya_ref, yb_ref, wa_ref, wb_ref)
    hn_ref[...] = hn
    xn = _rmsnorm(hn, g_ref[...])
    xn_ref[...] = xn
    tm = xn.shape[0]
    lane = lax.broadcasted_iota(jnp.int32, (tm, LANE), 1)
    x_hi, x_lo = _split2(xn)
    r_hi, r_lo = _split2(r_ref[...])
    dot = functools.partial(jnp.dot, preferred_element_type=F32)
    logits = jnp.where(lane < N_EXPERTS, dot(x_hi, r_hi) + (dot(x_hi, r_lo) + dot(x_lo, r_hi)), -jnp.inf)
    m1 = jnp.max(logits, axis=-1, keepdims=True)
    i1 = jnp.min(jnp.where(logits == m1, lane, LANE), axis=-1, keepdims=True)
    rest = jnp.where(lane == i1, -jnp.inf, logits)
    m2 = jnp.max(rest, axis=-1, keepdims=True)
    i2 = jnp.min(jnp.where(rest == m2, lane, LANE), axis=-1, keepdims=True)
    e2 = jnp.exp(m2 - m1)
    g1 = 1.0 / (1.0 + e2)
    g2 = e2 / (1.0 + e2)
    onehot = jnp.where(lane == i1, 1.0, jnp.where(lane == i2, 1.0, 0.0))
    before = dot(tri_ref[...], onehot.astype(BF16)) + carry_ref[0:1, :]
    r1 = jnp.sum(jnp.where(lane == i1, before, 0.0), axis=-1, keepdims=True)
    r2 = jnp.sum(jnp.where(lane == i2, before, 0.0), axis=-1, keepdims=True)
    meta = jnp.where(lane == 0, i1.astype(F32), jnp.where(lane == 1, i2.astype(F32), 0.0))
    meta = jnp.where(lane == 2, g1, jnp.where(lane == 3, g2, meta))
    meta_ref[...] = jnp.where(lane == 4, r1, jnp.where(lane == 5, r2, meta))
    total = carry_ref[0:1, :] + jnp.sum(onehot, axis=0, keepdims=True)
    carry_ref[...] = jnp.broadcast_to(total, carry_ref.shape)
    cnt_ref[...] = jnp.broadcast_to(total, cnt_ref.shape)


def _moe_route(h, ya, yb, wa, wb, g, router, tm=512):
    m, d = h.shape
    k = ya.shape[1]
    rows = lambda n: pl.BlockSpec((tm, n), lambda i: (i, 0))
    full = lambda shape: pl.BlockSpec(shape, lambda i: (0, 0))
    return pl.pallas_call(
        _moe_route_kernel,
        grid=(m // tm,),
        in_specs=[rows(d), rows(k), rows(k), full((k, d)), full((k, d)), full((1, d)), full((d, LANE)),
                  full((tm, tm))],
        out_specs=[rows(d), rows(d), rows(LANE), full((8, LANE))],
        out_shape=[jax.ShapeDtypeStruct((m, d), F32), jax.ShapeDtypeStruct((m, d), F32),
                   jax.ShapeDtypeStruct((m, LANE), F32), jax.ShapeDtypeStruct((8, LANE), F32)],
        scratch_shapes=[pltpu.VMEM((8, LANE), F32)],
        compiler_params=_cparams("arbitrary"),
        name="moe_route",
    )(h, ya, yb, wa, wb, g.reshape(1, d), router, jnp.tril(jnp.ones((tm, tm), BF16), -1))


def _row_copy(src_ref, src_group, src_sub, dst_ref, dst_group, dst_sub, sem):
    return pltpu.make_async_copy(src_ref.at[src_group, pl.ds(src_sub, 1)],
                                 dst_ref.at[dst_group, pl.ds(dst_sub, 1)], sem)


def _split_row(p):
    return lax.shift_right_logical(p, 3), lax.bitwise_and(p, _SUBLANES - 1)


def _moe_dispatch_kernel(pos_ref, xn_ref, xs_in_ref, xs_ref, sem, *, tm):
    del xs_in_ref

    def start(grp, carry):
        for u in range(_SUBLANES):
            for s in range(2):
                dst_group, dst_sub = _split_row(pos_ref[0, 0, 2 * (grp * _SUBLANES + u) + s])
                _row_copy(xn_ref, grp, u, xs_ref, dst_group, dst_sub, sem).start(priority=s)
        return carry

    def wait(grp, carry):
        for _ in range(2 * _SUBLANES):
            _row_copy(xn_ref, 0, 0, xs_ref, 0, 0, sem).wait()
        return carry

    lax.fori_loop(0, tm // _SUBLANES, start, 0)
    lax.fori_loop(0, tm // _SUBLANES, wait, 0)


def _moe_dispatch(xn, pos, zeros, tm=256):
    m, d = xn.shape
    n_rows = zeros.shape[0]
    xs = pl.pallas_call(
        functools.partial(_moe_dispatch_kernel, tm=tm),
        grid=(m // tm,),
        in_specs=[pl.BlockSpec((1, 1, 2 * tm), lambda i: (i, 0, 0), memory_space=pltpu.SMEM),
                  pl.BlockSpec((tm // _SUBLANES, _SUBLANES, d), lambda i: (i, 0, 0)),
                  pl.BlockSpec(memory_space=pl.ANY)],
        out_specs=pl.BlockSpec(memory_space=pl.ANY),
        out_shape=jax.ShapeDtypeStruct((n_rows // _SUBLANES, _SUBLANES, d), F32),
        scratch_shapes=[pltpu.SemaphoreType.DMA(())],
        input_output_aliases={2: 0},
        compiler_params=_cparams("arbitrary"),
        name="moe_dispatch",
    )(pos.reshape(m // tm, 1, 2 * tm), xn.reshape(m // _SUBLANES, _SUBLANES, d),
      zeros.reshape(n_rows // _SUBLANES, _SUBLANES, d))
    return xs.reshape(n_rows, d)


def _moe_expert_kernel(te_ref, nu_ref, x_ref, w1_ref, w3_ref, w2_ref, y_ref, xbf_ref, acc_ref):
    del te_ref
    f = pl.program_id(1)

    @pl.when(f == 0)
    def _():
        xbf_ref[...] = x_ref[...].astype(BF16)
        acc_ref[...] = jnp.zeros(acc_ref.shape, F32)

    @pl.when(pl.program_id(0) < nu_ref[0])
    def _():
        xb = xbf_ref[...]
        mid = _silu(jnp.dot(xb, w1_ref[0], preferred_element_type=F32)) * jnp.dot(
            xb, w3_ref[0], preferred_element_type=F32)
        acc_ref[...] += _dot(mid, w2_ref[0])

    @pl.when(f == pl.num_programs(1) - 1)
    def _():
        y_ref[...] = acc_ref[...]


def _moe_experts(xs, tile_expert, n_used, w1, w3, w2, tmx, tf=1792):
    n_rows, d = xs.shape
    nf = w1.shape[2] // tf

    def fidx(i, f, nu):
        last = i >= nu[0]
        return _serpentine(jnp.where(last, nu[0] - 1, i), jnp.where(last, nf - 1, f), nf)

    grid_spec = pltpu.PrefetchScalarGridSpec(
        num_scalar_prefetch=2,
        grid=(n_rows // tmx, nf),
        in_specs=[pl.BlockSpec((tmx, d), lambda i, f, te, nu: (i, 0)),
                  pl.BlockSpec((1, d, tf), lambda i, f, te, nu: (te[i], 0, fidx(i, f, nu))),
                  pl.BlockSpec((1, d, tf), lambda i, f, te, nu: (te[i], 0, fidx(i, f, nu))),
                  pl.BlockSpec((1, tf, d), lambda i, f, te, nu: (te[i], fidx(i, f, nu), 0))],
        out_specs=pl.BlockSpec((tmx, d), lambda i, f, te, nu: (i, 0)),
        scratch_shapes=[pltpu.VMEM((tmx, d), BF16), pltpu.VMEM((tmx, d), F32)],
    )
    return pl.pallas_call(
        _moe_expert_kernel,
        grid_spec=grid_spec,
        out_shape=jax.ShapeDtypeStruct((n_rows, d), F32),
        compiler_params=_cparams("arbitrary", "arbitrary"),
        name="moe_experts",
    )(tile_expert, n_used, xs, w1, w3, w2)


def _moe_combine_kernel(pos_ref, posn_ref, h_ref, meta_ref, fg_ref, ys_ref, o_ref, ybuf_ref, sem, *, tm):
    i = pl.program_id(0)
    slot = lax.rem(i, 2)

    groups = tm // _SUBLANES

    def gather(p_ref, dst_slot):
        def start(grp, carry):
            for u in range(_SUBLANES):
                for s in range(2):
                    src_group, src_sub = _split_row(p_ref[0, 0, 2 * (grp * _SUBLANES + u) + s])
                    _row_copy(ys_ref, src_group, src_sub, ybuf_ref.at[dst_slot], s * groups + grp, u,
                              sem.at[dst_slot]).start(priority=s)
            return carry
        lax.fori_loop(0, groups, start, 0)

    @pl.when(i == 0)
    def _():
        gather(pos_ref, 0)

    @pl.when(i + 1 < pl.num_programs(0))
    def _():
        gather(posn_ref, 1 - slot)

    def wait(grp, carry):
        for _ in range(2 * _SUBLANES):
            _row_copy(ys_ref, 0, 0, ybuf_ref.at[slot], 0, 0, sem.at[slot]).wait()
        return carry

    lax.fori_loop(0, groups, wait, 0)
    lane = lax.broadcasted_iota(jnp.int32, (tm, LANE), 1)
    meta = meta_ref[...]
    g1 = jnp.sum(jnp.where(lane == 2, meta, 0.0), axis=-1, keepdims=True)
    g2 = jnp.sum(jnp.where(lane == 3, meta, 0.0), axis=-1, keepdims=True)
    y = ybuf_ref[slot].reshape(2 * tm, -1)
    o_ref[...] = _rmsnorm(h_ref[...] + g1 * y[:tm] + g2 * y[tm:], fg_ref[...])


def _moe_combine(h, meta, pos, ys, final_g, tm=256):
    m, d = h.shape
    nt = m // tm
    pos3 = pos.reshape(nt, 1, 2 * tm)
    smem = lambda imap: pl.BlockSpec((1, 1, 2 * tm), imap, memory_space=pltpu.SMEM)
    return pl.pallas_call(
        functools.partial(_moe_combine_kernel, tm=tm),
        grid=(nt,),
        in_specs=[smem(lambda i: (i, 0, 0)), smem(lambda i: (jnp.minimum(i + 1, nt - 1), 0, 0)),
                  pl.BlockSpec((tm, d), lambda i: (i, 0)), pl.BlockSpec((tm, LANE), lambda i: (i, 0)),
                  pl.BlockSpec((1, d), lambda i: (0, 0)), pl.BlockSpec(memory_space=pl.ANY)],
        out_specs=pl.BlockSpec((tm, d), lambda i: (i, 0)),
        out_shape=jax.ShapeDtypeStruct((m, d), F32),
        scratch_shapes=[pltpu.VMEM((2, 2 * tm // _SUBLANES, _SUBLANES, d), F32), pltpu.SemaphoreType.DMA((2,))],
        compiler_params=_cparams("arbitrary"),
        name="moe_combine",
    )(pos3, pos3, h, meta, final_g.reshape(1, d), ys.reshape(-1, _SUBLANES, d))


_MOE_TILE = 512


def _moe_sorted_rows(m):
    return ((2 * m) // _MOE_TILE + N_EXPERTS) * _MOE_TILE


def _moe_routed(h, ya, yb, wa, wb, g, router, w1, w3, w2, final_g, zeros):
    m, d = h.shape
    tmx = _MOE_TILE
    h, xn, meta, cnt = _moe_route(h, ya, yb, wa, wb, g, router)
    experts = meta[:, 0:2].astype(jnp.int32)
    rank = meta[:, 4:6].astype(jnp.int32)
    counts = cnt[0, :N_EXPERTS].astype(jnp.int32)
    padded = (counts + tmx - 1) // tmx * tmx
    ends = jnp.cumsum(padded)
    pos = (ends - padded)[experts] + rank
    n_tiles = zeros.shape[0] // tmx
    n_used = ends[-1] // tmx
    tile_start = jnp.minimum(jnp.arange(n_tiles, dtype=jnp.int32), n_used - 1) * tmx
    tile_expert = jnp.sum(tile_start[:, None] >= ends[None, :], axis=1).astype(jnp.int32)
    xs = _moe_dispatch(xn, pos, zeros)
    ys = _moe_experts(xs, tile_expert, n_used.reshape(1).astype(jnp.int32), w1, w3, w2, tmx)
    return _moe_combine(h, meta, pos, ys, final_g)


def _pad_cols(w, n):
    return jnp.pad(w, ((0, 0), (0, n - w.shape[1])))


def kernel(x, norm_mix_g, norm_ffn_g, w_in0, conv_w, conv_b, conv_ln_g, conv_ln_b, gla_w_a2, gla_b_a, gla_norm_g, w_out0, ffn_w1, ffn_w3, ffn_w2, w_in1, rwkv_mu, rwkv_w2, rwkv_w0, rwkv_a2, rwkv_a0, rwkv_g2, rwkv_k_k, rwkv_k_a, rwkv_r_k, rwkv_lnx_g, rwkv_lnx_b, fox_b_f, w_out1, moe_router, moe_w1, moe_w3, moe_w2, final_norm_g):
    b, t, d = x.shape
    m = b * t
    half = d // 2
    bf = lambda a: a.astype(BF16)
    h = x.reshape(m, d)

    y_conv, y_gla = _layer0_mix(x, norm_mix_g[0], w_in0, conv_w, conv_b, conv_ln_g, conv_ln_b,
                                gla_w_a2, gla_b_a, gla_norm_g)
    flat = lambda w: w.reshape(-1, w.shape[-1])
    h, (e_w1, e_w3, e_w2), moe_zeros = _ffn(
        h, y_conv.reshape(m, half), y_gla.reshape(m, half), bf(w_out0[:half]), bf(w_out0[half:]),
        norm_ffn_g[0], bf(ffn_w1), bf(ffn_w3), bf(ffn_w2),
        casts=[flat(moe_w1), flat(moe_w3), flat(moe_w2)], zero_rows=_moe_sorted_rows(m))

    rc = _RWKV_COLS
    ws1 = [bf(w_in1[:, :rc]), bf(w_in1[:, rc:rc + half]), bf(w_in1[:, rc + half:rc + 2 * half]),
           bf(w_in1[:, rc + 2 * half:rc + 3 * half]), bf(_pad_cols(w_in1[:, rc + 3 * half:], LANE))]
    p_rwkv, fq, fk, fv, f_logit = [a.reshape(b, t, -1) for a in _norm_proj(h, norm_mix_g[1], ws1)]
    y_rwkv = _rwkv(p_rwkv, rwkv_mu, rwkv_w2, rwkv_w0, rwkv_a2, rwkv_a0, rwkv_g2, rwkv_k_k, rwkv_k_a,
                   rwkv_r_k, rwkv_lnx_g, rwkv_lnx_b)
    b_f = jnp.pad(fox_b_f, (0, LANE - FOX_HEADS)).reshape(1, LANE)
    y_fox = _fox(fq, fk, fv, _fox_cum(f_logit, b_f))
    out = _moe_routed(h, y_rwkv.reshape(m, half), y_fox.reshape(m, half), bf(w_out1[:half]), bf(w_out1[half:]),
                      norm_ffn_g[1], _pad_cols(moe_router, LANE), e_w1.reshape(moe_w1.shape),
                      e_w3.reshape(moe_w3.shape), e_w2.reshape(moe_w2.shape), final_norm_g, moe_zeros)
    return out.reshape(b, t, d)
```

```python
import functools

import jax
import jax.numpy as jnp
from jax import lax
from jax.experimental import pallas as pl
from jax.experimental.pallas import tpu as pltpu

F32 = jnp.float32
BF16 = jnp.bfloat16
HI = lax.Precision.HIGHEST

NORM_EPS = 1e-6
CONV_WIDTH = 31
CONV_LN_EPS = 1e-5
GLA_HEADS = 4
GLA_DK = 64
GLA_DV = 128
GLA_TAU = 16.0
GLA_CHUNK = 64
RWKV_HEADS = 8
RWKV_HD = 64
RWKV_CHUNK = 64
RWKV_DECAY_SCALE = 0.606531
RWKV_GN_EPS = 64e-5
RWKV_L2_EPS = 1e-12
FOX_HEADS = 8
FOX_HD = 64
N_EXPERTS = 8
LANE = 128
_SUBLANES = 8
VMEM_LIMIT = 56 * 1024 * 1024


def _cparams(*sem):
    return pltpu.CompilerParams(dimension_semantics=sem, vmem_limit_bytes=VMEM_LIMIT)


def _dot(a, b):
    return jnp.dot(a.astype(BF16), b.astype(BF16), preferred_element_type=F32)


def _dot_nt(a, b):
    return lax.dot_general(a.astype(BF16), b.astype(BF16), (((1,), (1,)), ((), ())),
                           preferred_element_type=F32)


def _dot_tn(a, b):
    return lax.dot_general(a.astype(BF16), b.astype(BF16), (((0,), (0,)), ((), ())),
                           preferred_element_type=F32)


def _dot_hi(a, b):
    return jnp.dot(a, b, precision=HI, preferred_element_type=F32)


def _dot_tn_hi(a, b):
    return lax.dot_general(a, b, (((0,), (0,)), ((), ())), precision=HI, preferred_element_type=F32)


def _sigmoid(x):
    return 1.0 / (1.0 + jnp.exp(-x))


def _silu(x):
    return x * _sigmoid(x)


def _log_sigmoid(x):
    return jnp.minimum(x, 0.0) - jnp.log(1.0 + jnp.exp(-jnp.abs(x)))


def _rmsnorm(x, g):
    return x * lax.rsqrt(jnp.mean(x * x, axis=-1, keepdims=True) + NORM_EPS) * g


def _tri(n, strict=False):
    r = lax.broadcasted_iota(jnp.int32, (n, n), 0)
    c = lax.broadcasted_iota(jnp.int32, (n, n), 1)
    return (r > c) if strict else (r >= c)


def _norm_proj_kernel(h_ref, g_ref, *refs):
    n = len(refs) // 2
    xn = _rmsnorm(h_ref[...], g_ref[...]).astype(BF16)
    for w_ref, o_ref in zip(refs[:n], refs[n:]):
        o_ref[...] = jnp.dot(xn, w_ref[...], preferred_element_type=F32).astype(o_ref.dtype)


def _norm_proj(h, g, ws, dtypes, tm=512):
    m, d = h.shape
    return pl.pallas_call(
        _norm_proj_kernel,
        grid=(m // tm,),
        in_specs=[pl.BlockSpec((tm, d), lambda i: (i, 0)), pl.BlockSpec((1, d), lambda i: (0, 0))]
        + [pl.BlockSpec(w.shape, lambda i: (0, 0)) for w in ws],
        out_specs=[pl.BlockSpec((tm, w.shape[1]), lambda i: (i, 0)) for w in ws],
        out_shape=[jax.ShapeDtypeStruct((m, w.shape[1]), dt) for w, dt in zip(ws, dtypes)],
        compiler_params=_cparams("parallel"),
        name="norm_proj",
    )(h, g.reshape(1, d), *ws)


_CONV_HALO = 32
_CONV_ROWS = 32
_GLA_SPAN = 256


def _conv_body(val, gate, w_ref, b_ref, lg_ref, lb_ref, o_ref, u_ref, sh_ref):
    tt, c = val.shape
    rows = _CONV_ROWS
    u_ref[_CONV_HALO:_CONV_HALO + tt, :] = val * _sigmoid(gate)
    span = sh_ref.shape[1]
    for s in range(1, _SUBLANES):
        sh_ref[s - 1] = u_ref[s:s + span, :]
    base = _CONV_HALO - (CONV_WIDTH - 1)
    for r0 in range(0, tt, rows):
        acc = jnp.zeros((rows // _SUBLANES, _SUBLANES, c), F32) + b_ref[...]
        for j in range(CONV_WIDTH):
            phase = (base + j) % _SUBLANES
            row = r0 + base + j - phase
            win = u_ref[row:row + rows, :] if phase == 0 else sh_ref[phase - 1, row:row + rows, :]
            acc = acc + win.reshape(rows // _SUBLANES, _SUBLANES, c) * w_ref[j]
        acc = acc.reshape(rows, c)
        mu = jnp.mean(acc, axis=-1, keepdims=True)
        cen = acc - mu
        var = jnp.mean(cen * cen, axis=-1, keepdims=True)
        y = cen * lax.rsqrt(var + CONV_LN_EPS) * lg_ref[...] + lb_ref[...]
        o_ref[0, r0:r0 + rows, :] = _silu(y).astype(o_ref.dtype)
    u_ref[0:_CONV_HALO, :] = u_ref[tt:tt + _CONV_HALO, :]


def _gla_body(q, k, v, go, alr, wa2_ref, ba_ref, ng_ref, o_ref, s_ref, row0):
    c = GLA_CHUNK
    nc = q.shape[0] // c
    tril = _tri(c)
    tril_f = tril.astype(F32)
    ones_cv = jnp.ones((c, GLA_DV), F32)
    units = [(ci, h) for ci in range(nc) for h in range(GLA_HEADS)]
    rows_of = lambda ci: slice(ci * c, (ci + 1) * c)
    keys_of = lambda h: slice(h * GLA_DK, (h + 1) * GLA_DK)
    vals_of = lambda h: slice(h * GLA_DV, (h + 1) * GLA_DV)

    z = _dot(alr, wa2_ref[...]) + ba_ref[...]
    log_a = _log_sigmoid(z) * (1.0 / GLA_TAU)
    q = q * (GLA_DK ** -0.5)
    q_decs, k_decs, k_ends, decays = [], [], [], []
    for ci in range(nc):
        la = log_a[rows_of(ci)]
        bcum = _dot_hi(tril_f, la)
        b_last = bcum[c - 1:c, :]
        q_decs.append(q[rows_of(ci)] * jnp.exp(bcum))
        k_decs.append(k[rows_of(ci)] * jnp.exp(-bcum))
        k_ends.append(k[rows_of(ci)] * jnp.exp(b_last - bcum))
        decays.append(jnp.exp(_dot_tn_hi(la, ones_cv)))

    vs = [v[rows_of(ci), vals_of(h)] for ci, h in units]
    attns = [jnp.where(tril, _dot_nt(q_decs[ci][:, keys_of(h)], k_decs[ci][:, keys_of(h)]), 0.0)
             for ci, h in units]
    kvs = [_dot_tn(k_ends[ci][:, keys_of(h)], vs[u]) for u, (ci, h) in enumerate(units)]
    intras = [_dot(attns[u], vs[u]) for u in range(len(units))]
    states = [s_ref[h] for h in range(GLA_HEADS)]
    prevs = []
    for u, (ci, h) in enumerate(units):
        prevs.append(states[h])
        states[h] = states[h] * decays[ci][keys_of(h)] + kvs[u]
    for h in range(GLA_HEADS):
        s_ref[h] = states[h]
    inters = [_dot(q_decs[ci][:, keys_of(h)], prevs[u]) for u, (ci, h) in enumerate(units)]
    for ci in range(nc):
        outs = []
        for h in range(GLA_HEADS):
            u = ci * GLA_HEADS + h
            o = intras[u] + inters[u]
            outs.append(o * lax.rsqrt(jnp.mean(o * o, axis=-1, keepdims=True) + NORM_EPS))
        o_all = jnp.concatenate(outs, axis=-1)
        o_ref[0, row0 + ci * c:row0 + (ci + 1) * c, :] = (
            o_all * ng_ref[...] * _silu(go[rows_of(ci)])).astype(o_ref.dtype)


def _layer0_kernel(h_ref, g_ref, wcv_ref, wcg_ref, wq_ref, wk_ref, wv_ref, wgo_ref, walr_ref,
                   cw_ref, cb_ref, clg_ref, clb_ref, wa2_ref, ba_ref, ng_ref,
                   yc_ref, yg_ref, u_ref, sh_ref, s_ref):
    @pl.when(pl.program_id(1) == 0)
    def _():
        u_ref[0:_CONV_HALO, :] = jnp.zeros((_CONV_HALO, u_ref.shape[1]), F32)
        s_ref[...] = jnp.zeros(s_ref.shape, F32)

    xn = _rmsnorm(h_ref[0], g_ref[...]).astype(BF16)
    proj = lambda w_ref, rows=slice(None): jnp.dot(xn[rows], w_ref[...], preferred_element_type=F32)
    _conv_body(proj(wcv_ref), proj(wcg_ref), cw_ref, cb_ref, clg_ref, clb_ref, yc_ref, u_ref, sh_ref)
    for row0 in range(0, xn.shape[0], _GLA_SPAN):
        rows = slice(row0, row0 + _GLA_SPAN)
        _gla_body(proj(wq_ref, rows), proj(wk_ref, rows), proj(wv_ref, rows), proj(wgo_ref, rows),
                  proj(walr_ref, rows), wa2_ref, ba_ref, ng_ref, yg_ref, s_ref, row0)


def _layer0_mix(x, g, w_in, conv_w, conv_b, ln_g, ln_b, w_a2, b_a, norm_g, tt=512):
    b, t, d = x.shape
    c = d // 2
    gk = GLA_HEADS * GLA_DK
    o = [0, c, 2 * c, 2 * c + gk, 2 * c + 2 * gk, 3 * c + 2 * gk, 4 * c + 2 * gk]
    ws = [w_in[:, o[i]:o[i + 1]].astype(BF16) for i in range(6)] + [_pad_cols(w_in[:, o[6]:], LANE).astype(BF16)]
    wrep = jnp.broadcast_to(conv_w[:, None, :], (CONV_WIDTH, _SUBLANES, c))
    w_a2p = jnp.pad(w_a2, ((0, LANE - w_a2.shape[0]), (0, 0)))
    vec = lambda a: a.reshape(1, -1)
    full = lambda a: pl.BlockSpec(a.shape, lambda i, j: (0,) * a.ndim)
    tile = lambda n: pl.BlockSpec((1, tt, n), lambda i, j: (i, j, 0))
    consts = ws + [wrep, vec(conv_b), vec(ln_g), vec(ln_b), w_a2p, vec(b_a), vec(norm_g)]
    return pl.pallas_call(
        _layer0_kernel,
        grid=(b, t // tt),
        in_specs=[tile(d), full(vec(g))] + [full(a) for a in consts],
        out_specs=[tile(c), tile(c)],
        out_shape=[jax.ShapeDtypeStruct((b, t, c), BF16), jax.ShapeDtypeStruct((b, t, c), BF16)],
        scratch_shapes=[pltpu.VMEM((tt + _CONV_HALO, c), F32),
                        pltpu.VMEM((_SUBLANES - 1, tt + _CONV_HALO - _SUBLANES, c), F32),
                        pltpu.VMEM((GLA_HEADS, GLA_DK, GLA_DV), F32)],
        compiler_params=_cparams("parallel", "arbitrary"),
        name="layer0_mix",
    )(x, vec(g), *consts)


_RWKV_COLS = 3 * 512 + 64 + 64 + 128


def _rwkv_kernel(h_ref, ng_ref, win_ref, mu_ref, w2_ref, w0_ref, a2_ref, a0_ref, g2_ref, kk_ref, ka_ref, rk_ref,
                 lg_ref, lb_ref, o_ref, xs_ref, st_ref, *, tt):
    c = RWKV_CHUNK
    hd = RWKV_HD
    w = RWKV_HEADS * hd

    @pl.when(pl.program_id(1) == 0)
    def _():
        xs_ref[0:8, :] = jnp.zeros((8, xs_ref.shape[1]), F32)
        st_ref[...] = jnp.zeros(st_ref.shape, F32)

    x = jnp.dot(_rmsnorm(h_ref[0], ng_ref[...]).astype(BF16), win_ref[...], preferred_element_type=F32)
    xs_ref[8:8 + tt, :] = x
    prev = xs_ref[7:7 + tt, :]
    xs_ref[7:8, :] = x[tt - 1:tt, :]
    x = x + (prev - x) * mu_ref[...]

    r = x[:, 0:w]
    k = x[:, w:2 * w]
    v = x[:, 2 * w:3 * w]
    xw = x[:, 3 * w:3 * w + 64]
    xa = x[:, 3 * w + 64:3 * w + 128]
    xg = x[:, 3 * w + 128:3 * w + 256]
    log_w = -RWKV_DECAY_SCALE * _sigmoid(w0_ref[...] + _dot(jnp.tanh(xw), w2_ref[...]))
    a = _sigmoid(a0_ref[...] + _dot(xa, a2_ref[...]))
    g = _dot(_sigmoid(xg), g2_ref[...])
    kk = k * kk_ref[...]
    k_mod = k * (1.0 + (a - 1.0) * ka_ref[...])
    rkr = r * k_mod * rk_ref[...]

    tril = _tri(c)
    tril_s = _tri(c, strict=True)
    tril_f = tril.astype(F32)
    eye = (lax.broadcasted_iota(jnp.int32, (c, c), 0)
           == lax.broadcasted_iota(jnp.int32, (c, c), 1)).astype(F32)
    zeros_cc = jnp.zeros((c, hd), F32)

    nc = tt // c
    units = [(ci, h) for ci in range(nc) for h in range(RWKV_HEADS)]
    rows_of = lambda ci: slice(ci * c, (ci + 1) * c)
    lanes_of = lambda h: slice(h * hd, (h + 1) * hd)

    exps = []
    for ci in range(nc):
        lw_c = log_w[rows_of(ci)]
        lcum = _dot_hi(tril_f, lw_c)
        l_last = lcum[c - 1:c, :]
        exps.append((jnp.exp(lcum), jnp.exp(-lcum), jnp.exp(l_last - lcum), jnp.exp(lcum - lw_c),
                     jnp.exp(l_last)))

    aqs, rqs, vs, lhss, rhss, bkes = [], [], [], [], [], []
    for ci, h in units:
        rows, hs = rows_of(ci), lanes_of(h)
        e_pos, e_neg, e_end, e_prev, _ = exps[ci]
        kk_h = kk[rows, hs]
        nrm = jnp.sqrt(jnp.sum(kk_h * kk_h, axis=-1, keepdims=True))
        kk_h = kk_h / jnp.maximum(nrm, RWKV_L2_EPS)
        k_h = k_mod[rows, hs]
        kka = kk_h * a[rows, hs]
        aq = -kk_h * e_prev[:, hs]
        rq = r[rows, hs] * e_pos[:, hs]
        aqs.append(aq)
        rqs.append(rq)
        vs.append(v[rows, hs])
        lhss.append(jnp.concatenate([aq, rq], axis=0))
        rhss.append(jnp.concatenate([kka * e_neg[:, hs], k_h * e_neg[:, hs]], axis=0))
        bkes.append(jnp.concatenate([kka * e_end[:, hs], k_h * e_end[:, hs]], axis=0))

    xss = [_dot_nt(lhs, rhs) for lhs, rhs in zip(lhss, rhss)]
    m1s = [jnp.where(tril_s, xs[:c, :c], 0.0) for xs in xss]
    m2s = [jnp.where(tril_s, xs[:c, c:], 0.0) for xs in xss]
    n12s = [jnp.concatenate([jnp.where(tril, xs[c:, :c], 0.0), jnp.where(tril, xs[c:, c:], 0.0)], axis=1)
            for xs in xss]
    tinvs = [eye + m1 for m1 in m1s]
    mps = m1s
    for _ in range(5):
        mps = [_dot(mp, mp) for mp in mps]
        tinvs = [tinv + _dot(tinv, mp) for tinv, mp in zip(tinvs, mps)]
    mvs = [_dot(m2, v_h) for m2, v_h in zip(m2s, vs)]
    pqs = [_dot(tinv, jnp.concatenate([mv, aq], axis=1)) for tinv, mv, aq in zip(tinvs, mvs, aqs)]
    pqvs = [jnp.concatenate([pq, jnp.concatenate([v_h, zeros_cc], axis=1)], axis=0)
            for pq, v_h in zip(pqs, vs)]
    yys = [_dot(n12, pqv) for n12, pqv in zip(n12s, pqvs)]
    ghs = [_dot_tn(bke, pqv) for bke, pqv in zip(bkes, pqvs)]

    states = [st_ref[h] for h in range(RWKV_HEADS)]
    ys = []
    for u, (ci, h) in enumerate(units):
        yq = rqs[u] + yys[u][:, hd:]
        gmat = eye * exps[ci][4][:, lanes_of(h)] + ghs[u][:, hd:]
        res = _dot(jnp.concatenate([yq, gmat], axis=0), states[h])
        ys.append(res[:c] + yys[u][:, :hd])
        states[h] = res[c:] + ghs[u][:, :hd]
    for h in range(RWKV_HEADS):
        st_ref[h] = states[h]

    out_rows = []
    for ci in range(nc):
        yns, bonuses = [], []
        for h in range(RWKV_HEADS):
            u = ci * RWKV_HEADS + h
            y = ys[u]
            cen = y - jnp.mean(y, axis=-1, keepdims=True)
            var = jnp.mean(cen * cen, axis=-1, keepdims=True)
            yns.append(cen * lax.rsqrt(var + RWKV_GN_EPS))
            bonuses.append(jnp.sum(rkr[rows_of(ci), lanes_of(h)], axis=-1, keepdims=True) * vs[u])
        yn = jnp.concatenate(yns, axis=1)
        bonus = jnp.concatenate(bonuses, axis=1)
        out_rows.append((yn * lg_ref[...] + lb_ref[...] + bonus) * g[rows_of(ci)])
    o_ref[0] = jnp.concatenate(out_rows, axis=0).astype(o_ref.dtype)


def _rwkv(h, norm_g, w_in, mu, w_w2, w0, w_a2, a0, w_g2, k_k, k_a, r_k, lnx_g, lnx_b, tt=128):
    b, t, d = h.shape
    cols = w_in.shape[1]
    w = RWKV_HEADS * RWKV_HD
    vec = lambda a_: a_.reshape(1, -1)
    full = lambda shape: pl.BlockSpec(shape, lambda i, j: (0, 0))
    return pl.pallas_call(
        functools.partial(_rwkv_kernel, tt=tt),
        grid=(b, t // tt),
        in_specs=[pl.BlockSpec((1, tt, d), lambda i, j: (i, j, 0)), full((1, d)), full((d, cols)), full((1, cols)),
                  full(w_w2.shape), full((1, w)), full(w_a2.shape), full((1, w)), full(w_g2.shape),
                  full((1, w)), full((1, w)), full((1, w)), full((1, w)), full((1, w))],
        out_specs=pl.BlockSpec((1, tt, w), lambda i, j: (i, j, 0)),
        out_shape=jax.ShapeDtypeStruct((b, t, w), BF16),
        scratch_shapes=[pltpu.VMEM((8 + tt, cols), F32),
                        pltpu.VMEM((RWKV_HEADS, RWKV_HD, RWKV_HD), F32)],
        compiler_params=_cparams("parallel", "arbitrary"),
        name="rwkv7",
    )(h, vec(norm_g), w_in, vec(mu), w_w2, vec(w0), w_a2, vec(a0), w_g2, vec(k_k), vec(k_a), vec(r_k),
      vec(lnx_g), vec(lnx_b))


def _fox_cum_kernel(f_ref, bf_ref, col_ref, carry_ref, *, tt):
    @pl.when(pl.program_id(1) == 0)
    def _():
        carry_ref[...] = jnp.zeros(carry_ref.shape, F32)

    log_f = _log_sigmoid(f_ref[0] + bf_ref[...])
    cum = _dot_hi(_tri(tt).astype(F32), log_f) + carry_ref[0:1, :]
    col_ref[0] = cum
    carry_ref[...] = jnp.broadcast_to(cum[tt - 1:tt, :], carry_ref.shape)


def _fox_cum(f_logit, b_f, tt=256):
    b, t, n = f_logit.shape
    return pl.pallas_call(
        functools.partial(_fox_cum_kernel, tt=tt),
        grid=(b, t // tt),
        in_specs=[pl.BlockSpec((1, tt, n), lambda i, j: (i, j, 0)), pl.BlockSpec((1, n), lambda i, j: (0, 0))],
        out_specs=pl.BlockSpec((1, tt, n), lambda i, j: (i, j, 0)),
        out_shape=jax.ShapeDtypeStruct((b, t, n), F32),
        scratch_shapes=[pltpu.VMEM((8, n), F32)],
        compiler_params=_cparams("parallel", "arbitrary"),
        name="fox_cumsum",
    )(f_logit, b_f)


def _split3(x):
    hi = x.astype(BF16).astype(F32)
    mid = (x - hi).astype(BF16).astype(F32)
    lo = (x - hi - mid).astype(BF16).astype(F32)
    return hi, mid, lo


def _fox_kernel(q_ref, k_ref, v_ref, c_ref, o_ref, kaug_ref, vaug_ref, *, tq, groups):
    qi = pl.program_id(2)
    d = FOX_HD
    per = LANE // d
    nh = groups * per
    head0 = pl.program_id(1) * nh

    def augment(x_h, c, first):
        lane = lax.broadcasted_iota(jnp.int32, x_h.shape, 1)
        hi, mid, lo = _split3(c)
        parts = (hi, mid, lo, 1.0, 1.0, 1.0) if first else (1.0, 1.0, 1.0, -hi, -mid, -lo)
        out = jnp.where(lane < d, x_h, 0.0)
        for n, part in enumerate(parts):
            out = jnp.where(lane == d + n, part, out)
        return out.astype(BF16)

    def head_col(c_all, head):
        lane = lax.broadcasted_iota(jnp.int32, c_all.shape, 1)
        return jnp.sum(jnp.where(lane == head, c_all, 0.0), axis=-1, keepdims=True)

    def head_lanes(x, h):
        x_g = x[:, (h // per) * LANE:(h // per + 1) * LANE]
        return x_g if h % per == 0 else pltpu.roll(x_g, LANE - (h % per) * d, axis=1)

    @pl.when(qi == 0)
    def _():
        lane_t = lax.broadcasted_iota(jnp.int32, (k_ref.shape[1], LANE), 1)
        for h in range(nh):
            kaug_ref[h] = augment(head_lanes(k_ref[0].astype(F32), h), head_col(c_ref[0], head0 + h), first=False)
            v_h = jnp.where(lane_t < d, head_lanes(v_ref[0].astype(F32), h), jnp.where(lane_t == d, 1.0, 0.0))
            vaug_ref[h] = v_h.astype(BF16)

    row0 = pl.multiple_of(qi * tq, tq)
    q = q_ref[0].astype(F32) * (d ** -0.5)
    c_q = c_ref[0, pl.ds(row0, tq), :]
    qs = [augment(head_lanes(q, h), head_col(c_q, head0 + h), first=True) for h in range(nh)]
    causal = _tri(tq)
    lag = 2

    def step(j, carry, diagonal):
        start = pl.multiple_of(j * tq, tq)
        ss, new = {}, []
        for h in range(nh + lag):
            if h < nh:
                ss[h] = lax.dot_general(qs[h], kaug_ref[h, pl.ds(start, tq), :], (((1,), (1,)), ((), ())),
                                        preferred_element_type=F32)
            g = h - lag
            if g >= 0:
                m, acc = carry[g]
                s = ss.pop(g)
                s = jnp.where(causal, s, -jnp.inf) if diagonal else s
                m_new = jnp.maximum(m, jnp.max(s, axis=-1, keepdims=True))
                p = jnp.exp(s - m_new).astype(BF16)
                pv = jnp.dot(p, vaug_ref[g, pl.ds(start, tq), :], preferred_element_type=F32)
                new.append((m_new, acc * jnp.exp(m - m_new) + pv))
        return tuple(new)

    init = tuple((jnp.full((tq, 1), -jnp.inf, F32), jnp.zeros((tq, LANE), F32)) for _ in range(nh))
    carry = lax.fori_loop(0, qi, lambda j, cr: step(j, cr, False), init)
    carry = step(qi, carry, True)
    lane = lax.broadcasted_iota(jnp.int32, (tq, LANE), 1)
    for grp in range(groups):
        out = None
        for hh in range(per):
            acc = carry[grp * per + hh][1]
            o_h = acc / jnp.sum(jnp.where(lane == d, acc, 0.0), axis=-1, keepdims=True)
            o_h = o_h if hh == 0 else pltpu.roll(o_h, hh * d, axis=1)
            out = o_h if out is None else jnp.where(lane >= hh * d, o_h, out)
        o_ref[0, :, grp * LANE:(grp + 1) * LANE] = out.astype(o_ref.dtype)


def _fox(q, k, v, c_col, tq=256, groups=4):
    b, t, w = q.shape
    gw = groups * LANE
    return pl.pallas_call(
        functools.partial(_fox_kernel, tq=tq, groups=groups),
        grid=(b, w // gw, t // tq),
        in_specs=[pl.BlockSpec((1, tq, gw), lambda i, p, j: (i, j, p)),
                  pl.BlockSpec((1, t, gw), lambda i, p, j: (i, 0, p)),
                  pl.BlockSpec((1, t, gw), lambda i, p, j: (i, 0, p)),
                  pl.BlockSpec((1, t, LANE), lambda i, p, j: (i, 0, 0))],
        out_specs=pl.BlockSpec((1, tq, gw), lambda i, p, j: (i, j, p)),
        out_shape=jax.ShapeDtypeStruct((b, t, w), BF16),
        scratch_shapes=[pltpu.VMEM((groups * LANE // FOX_HD, t, LANE), BF16),
                        pltpu.VMEM((groups * LANE // FOX_HD, t, LANE), BF16)],
        compiler_params=_cparams("parallel", "arbitrary", "arbitrary"),
        name="fox_attention",
    )(q, k, v, c_col)


def _serpentine(i, f, nf):
    return jnp.where(i % 2 == 0, f, nf - 1 - f)


def _mix_residual(h_ref, ya_ref, yb_ref, wa_ref, wb_ref):
    return h_ref[...] + _dot(ya_ref[...], wa_ref[...]) + _dot(yb_ref[...], wb_ref[...])


def _ffn_kernel(h_ref, ya_ref, yb_ref, wa_ref, wb_ref, g_ref, w1_ref, w3_ref, w2_ref, *refs, n_cast, n_side):
    cast_in, (o_ref, *cast_out, zero_ref), (xn_ref, acc_ref) = (
        refs[:n_cast], refs[n_cast:2 * n_cast + 2], refs[2 * n_cast + 2:])
    f = pl.program_id(1)

    @pl.when(f == 0)
    def _():
        hn = _mix_residual(h_ref, ya_ref, yb_ref, wa_ref, wb_ref)
        xn_ref[...] = _rmsnorm(hn, g_ref[...]).astype(BF16)
        acc_ref[...] = hn

    xn = xn_ref[...]
    mid = _silu(jnp.dot(xn, w1_ref[...], preferred_element_type=F32)) * jnp.dot(
        xn, w3_ref[...], preferred_element_type=F32)
    acc_ref[...] += _dot(mid, w2_ref[...])

    @pl.when(pl.program_id(0) * pl.num_programs(1) + f < n_side)
    def _():
        for src, dst in zip(cast_in, cast_out):
            dst[...] = src[...].astype(BF16)
        zero_ref[...] = jnp.zeros(zero_ref.shape, F32)

    @pl.when(f == pl.num_programs(1) - 1)
    def _():
        o_ref[...] = acc_ref[...]


def _ffn(h, ya, yb, wa, wb, g, w1, w3, w2, casts, zero_rows, tm=1024, tf=256):
    m, d = h.shape
    k = ya.shape[1]
    nf = w1.shape[1] // tf
    n_side = 1 << (((m // tm) * nf).bit_length() - 1)
    rows = lambda n: pl.BlockSpec((tm, n), lambda i, f: (i, 0))
    full = lambda shape: pl.BlockSpec(shape, lambda i, f: (0, 0))
    piece = lambda a_rows, cols: pl.BlockSpec(
        (a_rows // n_side, cols), lambda i, f: (jnp.minimum(i * nf + f, n_side - 1), 0))
    cast_specs = [piece(*a.shape) for a in casts]
    outs = pl.pallas_call(
        functools.partial(_ffn_kernel, n_cast=len(casts), n_side=n_side),
        grid=(m // tm, nf),
        in_specs=[rows(d), rows(k), rows(k), full((k, d)), full((k, d)), full((1, d)),
                  pl.BlockSpec((d, tf), lambda i, f: (0, _serpentine(i, f, nf))),
                  pl.BlockSpec((d, tf), lambda i, f: (0, _serpentine(i, f, nf))),
                  pl.BlockSpec((tf, d), lambda i, f: (_serpentine(i, f, nf), 0))] + cast_specs,
        out_specs=[rows(d)] + cast_specs + [piece(zero_rows, d)],
        out_shape=[jax.ShapeDtypeStruct((m, d), F32)] + [jax.ShapeDtypeStruct(a.shape, BF16) for a in casts]
        + [jax.ShapeDtypeStruct((zero_rows, d), F32)],
        scratch_shapes=[pltpu.VMEM((tm, d), BF16), pltpu.VMEM((tm, d), F32)],
        compiler_params=_cparams("parallel", "arbitrary"),
        name="ffn_dense",
    )(h, ya, yb, wa, wb, g.reshape(1, d), w1, w3, w2, *casts)
    return outs[0], outs[1:-1], outs[-1]


def _split2(x):
    hi = x.astype(BF16)
    return hi, (x - hi.astype(F32)).astype(BF16)


def _moe_route_kernel(h_ref, ya_ref, yb_ref, wa_ref, wb_ref, g_ref, r_ref, tri_ref, hn_ref, xn_ref, meta_ref,
                      cnt_ref, carry_ref):
    @pl.when(pl.program_id(0) == 0)
    def _():
        carry_ref[...] = jnp.zeros(carry_ref.shape, F32)

    hn = _mix_residual(h_ref, ya_ref, yb_ref, wa_ref, wb_ref)
    hn_ref[...] = hn
    xn = _rmsnorm(hn, g_ref[...])
    xn_ref[...] = xn
    tm = xn.shape[0]
    lane = lax.broadcasted_iota(jnp.int32, (tm, LANE), 1)
    x_hi, x_lo = _split2(xn)
    r_hi, r_lo = _split2(r_ref[...])
    dot = functools.partial(jnp.dot, preferred_element_type=F32)
    logits = jnp.where(lane < N_EXPERTS, dot(x_hi, r_hi) + (dot(x_hi, r_lo) + dot(x_lo, r_hi)), -jnp.inf)
    m1 = jnp.max(logits, axis=-1, keepdims=True)
    i1 = jnp.min(jnp.where(logits == m1, lane, LANE), axis=-1, keepdims=True)
    rest = jnp.where(lane == i1, -jnp.inf, logits)
    m2 = jnp.max(rest, axis=-1, keepdims=True)
    i2 = jnp.min(jnp.where(rest == m2, lane, LANE), axis=-1, keepdims=True)
    e2 = jnp.exp(m2 - m1)
    g1 = 1.0 / (1.0 + e2)
    g2 = e2 / (1.0 + e2)
    onehot = jnp.where(lane == i1, 1.0, jnp.where(lane == i2, 1.0, 0.0))
    before = dot(tri_ref[...], onehot.astype(BF16)) + carry_ref[0:1, :]
    r1 = jnp.sum(jnp.where(lane == i1, before, 0.0), axis=-1, keepdims=True)
    r2 = jnp.sum(jnp.where(lane == i2, before, 0.0), axis=-1, keepdims=True)
    meta = jnp.where(lane == 0, i1.astype(F32), jnp.where(lane == 1, i2.astype(F32), 0.0))
    meta = jnp.where(lane == 2, g1, jnp.where(lane == 3, g2, meta))
    meta_ref[...] = jnp.where(lane == 4, r1, jnp.where(lane == 5, r2, meta))
    total = carry_ref[0:1, :] + jnp.sum(onehot, axis=0, keepdims=True)
    carry_ref[...] = jnp.broadcast_to(total, carry_ref.shape)
    cnt_ref[...] = jnp.broadcast_to(total, cnt_ref.shape)


def _moe_route(h, ya, yb, wa, wb, g, router, tm=512):
    m, d = h.shape
    k = ya.shape[1]
    rows = lambda n: pl.BlockSpec((tm, n), lambda i: (i, 0))
    full = lambda shape: pl.BlockSpec(shape, lambda i: (0, 0))
    return pl.pallas_call(
        _moe_route_kernel,
        grid=(m // tm,),
        in_specs=[rows(d), rows(k), rows(k), full((k, d)), full((k, d)), full((1, d)), full((d, LANE)),
                  full((tm, tm))],
        out_specs=[rows(d), rows(d), rows(LANE), full((8, LANE))],
        out_shape=[jax.ShapeDtypeStruct((m, d), F32), jax.ShapeDtypeStruct((m, d), F32),
                   jax.ShapeDtypeStruct((m, LANE), F32), jax.ShapeDtypeStruct((8, LANE), F32)],
        scratch_shapes=[pltpu.VMEM((8, LANE), F32)],
        compiler_params=_cparams("arbitrary"),
        name="moe_route",
    )(h, ya, yb, wa, wb, g.reshape(1, d), router, jnp.tril(jnp.ones((tm, tm), BF16), -1))


def _row_copy(src_ref, src_group, src_sub, dst_ref, dst_group, dst_sub, sem):
    return pltpu.make_async_copy(src_ref.at[src_group, pl.ds(src_sub, 1)],
                                 dst_ref.at[dst_group, pl.ds(dst_sub, 1)], sem)


def _split_row(p):
    return lax.shift_right_logical(p, 3), lax.bitwise_and(p, _SUBLANES - 1)


def _moe_dispatch_kernel(pos_ref, xn_ref, xs_in_ref, xs_ref, sem, *, tm):
    del xs_in_ref

    def start(grp, carry):
        for u in range(_SUBLANES):
            for s in range(2):
                dst_group, dst_sub = _split_row(pos_ref[0, 0, 2 * (grp * _SUBLANES + u) + s])
                _row_copy(xn_ref, grp, u, xs_ref, dst_group, dst_sub, sem).start(priority=s)
        return carry

    def wait(grp, carry):
        for _ in range(2 * _SUBLANES):
            _row_copy(xn_ref, 0, 0, xs_ref, 0, 0, sem).wait()
        return carry

    lax.fori_loop(0, tm // _SUBLANES, start, 0)
    lax.fori_loop(0, tm // _SUBLANES, wait, 0)


def _moe_dispatch(xn, pos, zeros, tm=256):
    m, d = xn.shape
    n_rows = zeros.shape[0]
    xs = pl.pallas_call(
        functools.partial(_moe_dispatch_kernel, tm=tm),
        grid=(m // tm,),
        in_specs=[pl.BlockSpec((1, 1, 2 * tm), lambda i: (i, 0, 0), memory_space=pltpu.SMEM),
                  pl.BlockSpec((tm // _SUBLANES, _SUBLANES, d), lambda i: (i, 0, 0)),
                  pl.BlockSpec(memory_space=pl.ANY)],
        out_specs=pl.BlockSpec(memory_space=pl.ANY),
        out_shape=jax.ShapeDtypeStruct((n_rows // _SUBLANES, _SUBLANES, d), F32),
        scratch_shapes=[pltpu.SemaphoreType.DMA(())],
        input_output_aliases={2: 0},
        compiler_params=_cparams("arbitrary"),
        name="moe_dispatch",
    )(pos.reshape(m // tm, 1, 2 * tm), xn.reshape(m // _SUBLANES, _SUBLANES, d),
      zeros.reshape(n_rows // _SUBLANES, _SUBLANES, d))
    return xs.reshape(n_rows, d)


def _moe_expert_kernel(te_ref, nu_ref, x_ref, w1_ref, w3_ref, w2_ref, y_ref, xbf_ref, acc_ref):
    del te_ref
    f = pl.program_id(1)

    @pl.when(f == 0)
    def _():
        xbf_ref[...] = x_ref[...].astype(BF16)
        acc_ref[...] = jnp.zeros(acc_ref.shape, F32)

    @pl.when(pl.program_id(0) < nu_ref[0])
    def _():
        xb = xbf_ref[...]
        mid = _silu(jnp.dot(xb, w1_ref[0], preferred_element_type=F32)) * jnp.dot(
            xb, w3_ref[0], preferred_element_type=F32)
        acc_ref[...] += _dot(mid, w2_ref[0])

    @pl.when(f == pl.num_programs(1) - 1)
    def _():
        y_ref[...] = acc_ref[...]


def _moe_experts(xs, tile_expert, n_used, w1, w3, w2, tmx, tf=1792):
    n_rows, d = xs.shape
    nf = w1.shape[2] // tf

    def fidx(i, f, nu):
        last = i >= nu[0]
        return _serpentine(jnp.where(last, nu[0] - 1, i), jnp.where(last, nf - 1, f), nf)

    grid_spec = pltpu.PrefetchScalarGridSpec(
        num_scalar_prefetch=2,
        grid=(n_rows // tmx, nf),
        in_specs=[pl.BlockSpec((tmx, d), lambda i, f, te, nu: (i, 0)),
                  pl.BlockSpec((1, d, tf), lambda i, f, te, nu: (te[i], 0, fidx(i, f, nu))),
                  pl.BlockSpec((1, d, tf), lambda i, f, te, nu: (te[i], 0, fidx(i, f, nu))),
                  pl.BlockSpec((1, tf, d), lambda i, f, te, nu: (te[i], fidx(i, f, nu), 0))],
        out_specs=pl.BlockSpec((tmx, d), lambda i, f, te, nu: (i, 0)),
        scratch_shapes=[pltpu.VMEM((tmx, d), BF16), pltpu.VMEM((tmx, d), F32)],
    )
    return pl.pallas_call(
        _moe_expert_kernel,
        grid_spec=grid_spec,
        out_shape=jax.ShapeDtypeStruct((n_rows, d), F32),
        compiler_params=_cparams("arbitrary", "arbitrary"),
        name="moe_experts",
    )(tile_expert, n_used, xs, w1, w3, w2)


def _moe_combine_kernel(pos_ref, posn_ref, h_ref, meta_ref, fg_ref, ys_ref, o_ref, ybuf_ref, sem, *, tm):
    i = pl.program_id(0)
    slot = lax.rem(i, 2)

    groups = tm // _SUBLANES

    def gather(p_ref, dst_slot):
        def start(grp, carry):
            for u in range(_SUBLANES):
                for s in range(2):
                    src_group, src_sub = _split_row(p_ref[0, 0, 2 * (grp * _SUBLANES + u) + s])
                    _row_copy(ys_ref, src_group, src_sub, ybuf_ref.at[dst_slot], s * groups + grp, u,
                              sem.at[dst_slot]).start(priority=s)
            return carry
        lax.fori_loop(0, groups, start, 0)

    @pl.when(i == 0)
    def _():
        gather(pos_ref, 0)

    @pl.when(i + 1 < pl.num_programs(0))
    def _():
        gather(posn_ref, 1 - slot)

    def wait(grp, carry):
        for _ in range(2 * _SUBLANES):
            _row_copy(ys_ref, 0, 0, ybuf_ref.at[slot], 0, 0, sem.at[slot]).wait()
        return carry

    lax.fori_loop(0, groups, wait, 0)
    lane = lax.broadcasted_iota(jnp.int32, (tm, LANE), 1)
    meta = meta_ref[...]
    g1 = jnp.sum(jnp.where(lane == 2, meta, 0.0), axis=-1, keepdims=True)
    g2 = jnp.sum(jnp.where(lane == 3, meta, 0.0), axis=-1, keepdims=True)
    y = ybuf_ref[slot].reshape(2 * tm, -1)
    o_ref[...] = _rmsnorm(h_ref[...] + g1 * y[:tm] + g2 * y[tm:], fg_ref[...])


def _moe_combine(h, meta, pos, ys, final_g, tm=256):
    m, d = h.shape
    nt = m // tm
    pos3 = pos.reshape(nt, 1, 2 * tm)
    smem = lambda imap: pl.BlockSpec((1, 1, 2 * tm), imap, memory_space=pltpu.SMEM)
    return pl.pallas_call(
        functools.partial(_moe_combine_kernel, tm=tm),
        grid=(nt,),
        in_specs=[smem(lambda i: (i, 0, 0)), smem(lambda i: (jnp.minimum(i + 1, nt - 1), 0, 0)),
                  pl.BlockSpec((tm, d), lambda i: (i, 0)), pl.BlockSpec((tm, LANE), lambda i: (i, 0)),
                  pl.BlockSpec((1, d), lambda i: (0, 0)), pl.BlockSpec(memory_space=pl.ANY)],
        out_specs=pl.BlockSpec((tm, d), lambda i: (i, 0)),
        out_shape=jax.ShapeDtypeStruct((m, d), F32),
        scratch_shapes=[pltpu.VMEM((2, 2 * tm // _SUBLANES, _SUBLANES, d), F32), pltpu.SemaphoreType.DMA((2,))],
        compiler_params=_cparams("arbitrary"),
        name="moe_combine",
    )(pos3, pos3, h, meta, final_g.reshape(1, d), ys.reshape(-1, _SUBLANES, d))


_MOE_TILE = 512


def _moe_sorted_rows(m):
    return ((2 * m) // _MOE_TILE + N_EXPERTS) * _MOE_TILE


def _moe_routed(h, ya, yb, wa, wb, g, router, w1, w3, w2, final_g, zeros):
    m, d = h.shape
    tmx = _MOE_TILE
    h, xn, meta, cnt = _moe_route(h, ya, yb, wa, wb, g, router)
    experts = meta[:, 0:2].astype(jnp.int32)
    rank = meta[:, 4:6].astype(jnp.int32)
    counts = cnt[0, :N_EXPERTS].astype(jnp.int32)
    padded = (counts + tmx - 1) // tmx * tmx
    ends = jnp.cumsum(padded)
    pos = (ends - padded)[experts] + rank
    n_tiles = zeros.shape[0] // tmx
    n_used = ends[-1] // tmx
    tile_start = jnp.minimum(jnp.arange(n_tiles, dtype=jnp.int32), n_used - 1) * tmx
    tile_expert = jnp.sum(tile_start[:, None] >= ends[None, :], axis=1).astype(jnp.int32)
    xs = _moe_dispatch(xn, pos, zeros)
    ys = _moe_experts(xs, tile_expert, n_used.reshape(1).astype(jnp.int32), w1, w3, w2, tmx)
    return _moe_combine(h, meta, pos, ys, final_g)


def _pad_cols(w, n):
    return jnp.pad(w, ((0, 0), (0, n - w.shape[1])))


def kernel(x, norm_mix_g, norm_ffn_g, w_in0, conv_w, conv_b, conv_ln_g, conv_ln_b, gla_w_a2, gla_b_a, gla_norm_g, w_out0, ffn_w1, ffn_w3, ffn_w2, w_in1, rwkv_mu, rwkv_w2, rwkv_w0, rwkv_a2, rwkv_a0, rwkv_g2, rwkv_k_k, rwkv_k_a, rwkv_r_k, rwkv_lnx_g, rwkv_lnx_b, fox_b_f, w_out1, moe_router, moe_w1, moe_w3, moe_w2, final_norm_g):
    b, t, d = x.shape
    m = b * t
    half = d // 2
    bf = lambda a: a.astype(BF16)
    h = x.reshape(m, d)

    y_conv, y_gla = _layer0_mix(x, norm_mix_g[0], w_in0, conv_w, conv_b, conv_ln_g, conv_ln_b,
                                gla_w_a2, gla_b_a, gla_norm_g)
    flat = lambda w: w.reshape(-1, w.shape[-1])
    h, (e_w1, e_w3, e_w2), moe_zeros = _ffn(
        h, y_conv.reshape(m, half), y_gla.reshape(m, half), bf(w_out0[:half]), bf(w_out0[half:]),
        norm_ffn_g[0], bf(ffn_w1), bf(ffn_w3), bf(ffn_w2),
        casts=[flat(moe_w1), flat(moe_w3), flat(moe_w2)], zero_rows=_moe_sorted_rows(m))

    rc = _RWKV_COLS
    y_rwkv = _rwkv(h.reshape(b, t, d), norm_mix_g[1], bf(w_in1[:, :rc]), rwkv_mu, rwkv_w2, rwkv_w0, rwkv_a2,
                   rwkv_a0, rwkv_g2, rwkv_k_k, rwkv_k_a, rwkv_r_k, rwkv_lnx_g, rwkv_lnx_b)
    ws1 = [bf(w_in1[:, rc:rc + half]), bf(w_in1[:, rc + half:rc + 2 * half]),
           bf(w_in1[:, rc + 2 * half:rc + 3 * half]), bf(_pad_cols(w_in1[:, rc + 3 * half:], LANE))]
    fq, fk, fv, f_logit = [a.reshape(b, t, -1) for a in _norm_proj(h, norm_mix_g[1], ws1, [BF16] * 3 + [F32])]
    b_f = jnp.pad(fox_b_f, (0, LANE - FOX_HEADS)).reshape(1, LANE)
    y_fox = _fox(fq, fk, fv, _fox_cum(f_logit, b_f))
    out = _moe_routed(h, y_rwkv.reshape(m, half), y_fox.reshape(m, half), bf(w_out1[:half]), bf(w_out1[half:]),
                      norm_ffn_g[1], _pad_cols(moe_router, LANE), e_w1.reshape(moe_w1.shape),
                      e_w3.reshape(moe_w3.shape), e_w2.reshape(moe_w2.shape), final_norm_g, moe_zeros)
    return out.reshape(b, t, d)
```

```python
import functools

import jax
import jax.numpy as jnp
from jax import lax
from jax.experimental import pallas as pl
from jax.experimental.pallas import tpu as pltpu

F32 = jnp.float32
BF16 = jnp.bfloat16
HI = lax.Precision.HIGHEST

NORM_EPS = 1e-6
CONV_WIDTH = 31
CONV_LN_EPS = 1e-5
GLA_HEADS = 4
GLA_DK = 64
GLA_DV = 128
GLA_TAU = 16.0
GLA_CHUNK = 64
RWKV_HEADS = 8
RWKV_HD = 64
RWKV_CHUNK = 64
RWKV_DECAY_SCALE = 0.606531
RWKV_GN_EPS = 64e-5
RWKV_L2_EPS = 1e-12
FOX_HEADS = 8
FOX_HD = 64
N_EXPERTS = 8
LANE = 128
_SUBLANES = 8
VMEM_LIMIT = 56 * 1024 * 1024


def _cparams(*sem):
    return pltpu.CompilerParams(dimension_semantics=sem, vmem_limit_bytes=VMEM_LIMIT)


def _dot(a, b):
    return jnp.dot(a.astype(BF16), b.astype(BF16), preferred_element_type=F32)


def _dot_nt(a, b):
    return lax.dot_general(a.astype(BF16), b.astype(BF16), (((1,), (1,)), ((), ())),
                           preferred_element_type=F32)


def _dot_tn(a, b):
    return lax.dot_general(a.astype(BF16), b.astype(BF16), (((0,), (0,)), ((), ())),
                           preferred_element_type=F32)


def _dot_hi(a, b):
    return jnp.dot(a, b, precision=HI, preferred_element_type=F32)


def _dot_tn_hi(a, b):
    return lax.dot_general(a, b, (((0,), (0,)), ((), ())), precision=HI, preferred_element_type=F32)


def _sigmoid(x):
    return 1.0 / (1.0 + jnp.exp(-x))


def _silu(x):
    return x * _sigmoid(x)


def _log_sigmoid(x):
    return jnp.minimum(x, 0.0) - jnp.log(1.0 + jnp.exp(-jnp.abs(x)))


def _rmsnorm(x, g):
    return x * lax.rsqrt(jnp.mean(x * x, axis=-1, keepdims=True) + NORM_EPS) * g


def _tri(n, strict=False):
    r = lax.broadcasted_iota(jnp.int32, (n, n), 0)
    c = lax.broadcasted_iota(jnp.int32, (n, n), 1)
    return (r > c) if strict else (r >= c)


def _norm_proj_kernel(h_ref, g_ref, *refs):
    n = len(refs) // 2
    xn = _rmsnorm(h_ref[...], g_ref[...]).astype(BF16)
    for w_ref, o_ref in zip(refs[:n], refs[n:]):
        o_ref[...] = jnp.dot(xn, w_ref[...], preferred_element_type=F32).astype(o_ref.dtype)


def _norm_proj(h, g, ws, dtypes, tm=512):
    m, d = h.shape
    return pl.pallas_call(
        _norm_proj_kernel,
        grid=(m // tm,),
        in_specs=[pl.BlockSpec((tm, d), lambda i: (i, 0)), pl.BlockSpec((1, d), lambda i: (0, 0))]
        + [pl.BlockSpec(w.shape, lambda i: (0, 0)) for w in ws],
        out_specs=[pl.BlockSpec((tm, w.shape[1]), lambda i: (i, 0)) for w in ws],
        out_shape=[jax.ShapeDtypeStruct((m, w.shape[1]), dt) for w, dt in zip(ws, dtypes)],
        compiler_params=_cparams("parallel"),
        name="norm_proj",
    )(h, g.reshape(1, d), *ws)


_CONV_HALO = 32
_CONV_ROWS = 32
_GLA_SPAN = 256


def _conv_body(val, gate, w_ref, b_ref, lg_ref, lb_ref, o_ref, u_ref, sh_ref):
    tt, c = val.shape
    rows = _CONV_ROWS
    u_ref[_CONV_HALO:_CONV_HALO + tt, :] = val * _sigmoid(gate)
    span = sh_ref.shape[1]
    for s in range(1, _SUBLANES):
        sh_ref[s - 1] = u_ref[s:s + span, :]
    base = _CONV_HALO - (CONV_WIDTH - 1)
    for r0 in range(0, tt, rows):
        acc = jnp.zeros((rows // _SUBLANES, _SUBLANES, c), F32) + b_ref[...]
        for j in range(CONV_WIDTH):
            phase = (base + j) % _SUBLANES
            row = r0 + base + j - phase
            win = u_ref[row:row + rows, :] if phase == 0 else sh_ref[phase - 1, row:row + rows, :]
            acc = acc + win.reshape(rows // _SUBLANES, _SUBLANES, c) * w_ref[j]
        acc = acc.reshape(rows, c)
        mu = jnp.mean(acc, axis=-1, keepdims=True)
        cen = acc - mu
        var = jnp.mean(cen * cen, axis=-1, keepdims=True)
        y = cen * lax.rsqrt(var + CONV_LN_EPS) * lg_ref[...] + lb_ref[...]
        o_ref[0, r0:r0 + rows, :] = _silu(y).astype(o_ref.dtype)
    u_ref[0:_CONV_HALO, :] = u_ref[tt:tt + _CONV_HALO, :]


def _gla_body(q, k, v, go, alr, wa2_ref, ba_ref, ng_ref, o_ref, s_ref, row0):
    c = GLA_CHUNK
    nc = q.shape[0] // c
    tril = _tri(c)
    tril_f = tril.astype(F32)
    ones_cv = jnp.ones((c, GLA_DV), F32)
    units = [(ci, h) for ci in range(nc) for h in range(GLA_HEADS)]
    rows_of = lambda ci: slice(ci * c, (ci + 1) * c)
    keys_of = lambda h: slice(h * GLA_DK, (h + 1) * GLA_DK)
    vals_of = lambda h: slice(h * GLA_DV, (h + 1) * GLA_DV)

    z = _dot(alr, wa2_ref[...]) + ba_ref[...]
    log_a = _log_sigmoid(z) * (1.0 / GLA_TAU)
    q = q * (GLA_DK ** -0.5)
    q_decs, k_decs, k_ends, decays = [], [], [], []
    for ci in range(nc):
        la = log_a[rows_of(ci)]
        bcum = _dot_hi(tril_f, la)
        b_last = bcum[c - 1:c, :]
        q_decs.append(q[rows_of(ci)] * jnp.exp(bcum))
        k_decs.append(k[rows_of(ci)] * jnp.exp(-bcum))
        k_ends.append(k[rows_of(ci)] * jnp.exp(b_last - bcum))
        decays.append(jnp.exp(_dot_tn_hi(la, ones_cv)))

    vs = [v[rows_of(ci), vals_of(h)] for ci, h in units]
    attns = [jnp.where(tril, _dot_nt(q_decs[ci][:, keys_of(h)], k_decs[ci][:, keys_of(h)]), 0.0)
             for ci, h in units]
    kvs = [_dot_tn(k_ends[ci][:, keys_of(h)], vs[u]) for u, (ci, h) in enumerate(units)]
    intras = [_dot(attns[u], vs[u]) for u in range(len(units))]
    states = [s_ref[h] for h in range(GLA_HEADS)]
    prevs = []
    for u, (ci, h) in enumerate(units):
        prevs.append(states[h])
        states[h] = states[h] * decays[ci][keys_of(h)] + kvs[u]
    for h in range(GLA_HEADS):
        s_ref[h] = states[h]
    inters = [_dot(q_decs[ci][:, keys_of(h)], prevs[u]) for u, (ci, h) in enumerate(units)]
    for ci in range(nc):
        outs = []
        for h in range(GLA_HEADS):
            u = ci * GLA_HEADS + h
            o = intras[u] + inters[u]
            outs.append(o * lax.rsqrt(jnp.mean(o * o, axis=-1, keepdims=True) + NORM_EPS))
        o_all = jnp.concatenate(outs, axis=-1)
        o_ref[0, row0 + ci * c:row0 + (ci + 1) * c, :] = (
            o_all * ng_ref[...] * _silu(go[rows_of(ci)])).astype(o_ref.dtype)


def _layer0_kernel(h_ref, g_ref, wcv_ref, wcg_ref, wq_ref, wk_ref, wv_ref, wgo_ref, walr_ref,
                   cw_ref, cb_ref, clg_ref, clb_ref, wa2_ref, ba_ref, ng_ref,
                   yc_ref, yg_ref, u_ref, sh_ref, s_ref):
    @pl.when(pl.program_id(1) == 0)
    def _():
        u_ref[0:_CONV_HALO, :] = jnp.zeros((_CONV_HALO, u_ref.shape[1]), F32)
        s_ref[...] = jnp.zeros(s_ref.shape, F32)

    xn = _rmsnorm(h_ref[0], g_ref[...]).astype(BF16)
    proj = lambda w_ref, rows=slice(None): jnp.dot(xn[rows], w_ref[...], preferred_element_type=F32)
    _conv_body(proj(wcv_ref), proj(wcg_ref), cw_ref, cb_ref, clg_ref, clb_ref, yc_ref, u_ref, sh_ref)
    for row0 in range(0, xn.shape[0], _GLA_SPAN):
        rows = slice(row0, row0 + _GLA_SPAN)
        _gla_body(proj(wq_ref, rows), proj(wk_ref, rows), proj(wv_ref, rows), proj(wgo_ref, rows),
                  proj(walr_ref, rows), wa2_ref, ba_ref, ng_ref, yg_ref, s_ref, row0)


def _layer0_mix(x, g, w_in, conv_w, conv_b, ln_g, ln_b, w_a2, b_a, norm_g, tt=512):
    b, t, d = x.shape
    c = d // 2
    gk = GLA_HEADS * GLA_DK
    o = [0, c, 2 * c, 2 * c + gk, 2 * c + 2 * gk, 3 * c + 2 * gk, 4 * c + 2 * gk]
    ws = [w_in[:, o[i]:o[i + 1]].astype(BF16) for i in range(6)] + [_pad_cols(w_in[:, o[6]:], LANE).astype(BF16)]
    wrep = jnp.broadcast_to(conv_w[:, None, :], (CONV_WIDTH, _SUBLANES, c))
    w_a2p = jnp.pad(w_a2, ((0, LANE - w_a2.shape[0]), (0, 0)))
    vec = lambda a: a.reshape(1, -1)
    full = lambda a: pl.BlockSpec(a.shape, lambda i, j: (0,) * a.ndim)
    tile = lambda n: pl.BlockSpec((1, tt, n), lambda i, j: (i, j, 0))
    consts = ws + [wrep, vec(conv_b), vec(ln_g), vec(ln_b), w_a2p, vec(b_a), vec(norm_g)]
    return pl.pallas_call(
        _layer0_kernel,
        grid=(b, t // tt),
        in_specs=[tile(d), full(vec(g))] + [full(a) for a in consts],
        out_specs=[tile(c), tile(c)],
        out_shape=[jax.ShapeDtypeStruct((b, t, c), BF16), jax.ShapeDtypeStruct((b, t, c), BF16)],
        scratch_shapes=[pltpu.VMEM((tt + _CONV_HALO, c), F32),
                        pltpu.VMEM((_SUBLANES - 1, tt + _CONV_HALO - _SUBLANES, c), F32),
                        pltpu.VMEM((GLA_HEADS, GLA_DK, GLA_DV), F32)],
        compiler_params=_cparams("parallel", "arbitrary"),
        name="layer0_mix",
    )(x, vec(g), *consts)


_RWKV_COLS = 3 * 512 + 64 + 64 + 128


def _rwkv_kernel(h_ref, ng_ref, win_ref, mu_ref, w2_ref, w0_ref, a2_ref, a0_ref, g2_ref, kk_ref, ka_ref, rk_ref,
                 lg_ref, lb_ref, *refs, tt, n_cast):
    cast_in, (o_ref, *cast_out, zero_ref), (xs_ref, st_ref) = (
        refs[:n_cast], refs[n_cast:2 * n_cast + 2], refs[2 * n_cast + 2:])
    c = RWKV_CHUNK
    hd = RWKV_HD
    w = RWKV_HEADS * hd
    for src, dst in zip(cast_in, cast_out):
        dst[...] = src[...].astype(BF16)
    zero_ref[...] = jnp.zeros(zero_ref.shape, F32)

    @pl.when(pl.program_id(1) == 0)
    def _():
        xs_ref[0:8, :] = jnp.zeros((8, xs_ref.shape[1]), F32)
        st_ref[...] = jnp.zeros(st_ref.shape, F32)

    x = jnp.dot(_rmsnorm(h_ref[0], ng_ref[...]).astype(BF16), win_ref[...], preferred_element_type=F32)
    xs_ref[8:8 + tt, :] = x
    prev = xs_ref[7:7 + tt, :]
    xs_ref[7:8, :] = x[tt - 1:tt, :]
    x = x + (prev - x) * mu_ref[...]

    r = x[:, 0:w]
    k = x[:, w:2 * w]
    v = x[:, 2 * w:3 * w]
    xw = x[:, 3 * w:3 * w + 64]
    xa = x[:, 3 * w + 64:3 * w + 128]
    xg = x[:, 3 * w + 128:3 * w + 256]
    log_w = -RWKV_DECAY_SCALE * _sigmoid(w0_ref[...] + _dot(jnp.tanh(xw), w2_ref[...]))
    a = _sigmoid(a0_ref[...] + _dot(xa, a2_ref[...]))
    g = _dot(_sigmoid(xg), g2_ref[...])
    kk = k * kk_ref[...]
    k_mod = k * (1.0 + (a - 1.0) * ka_ref[...])
    rkr = r * k_mod * rk_ref[...]

    tril = _tri(c)
    tril_s = _tri(c, strict=True)
    tril_f = tril.astype(F32)
    eye = (lax.broadcasted_iota(jnp.int32, (c, c), 0)
           == lax.broadcasted_iota(jnp.int32, (c, c), 1)).astype(F32)
    zeros_cc = jnp.zeros((c, hd), F32)

    nc = tt // c
    units = [(ci, h) for ci in range(nc) for h in range(RWKV_HEADS)]
    rows_of = lambda ci: slice(ci * c, (ci + 1) * c)
    lanes_of = lambda h: slice(h * hd, (h + 1) * hd)

    exps = []
    for ci in range(nc):
        lw_c = log_w[rows_of(ci)]
        lcum = _dot_hi(tril_f, lw_c)
        l_last = lcum[c - 1:c, :]
        exps.append((jnp.exp(lcum), jnp.exp(-lcum), jnp.exp(l_last - lcum), jnp.exp(lcum - lw_c),
                     jnp.exp(l_last)))

    aqs, rqs, vs, lhss, rhss, bkes = [], [], [], [], [], []
    for ci, h in units:
        rows, hs = rows_of(ci), lanes_of(h)
        e_pos, e_neg, e_end, e_prev, _ = exps[ci]
        kk_h = kk[rows, hs]
        nrm = jnp.sqrt(jnp.sum(kk_h * kk_h, axis=-1, keepdims=True))
        kk_h = kk_h / jnp.maximum(nrm, RWKV_L2_EPS)
        k_h = k_mod[rows, hs]
        kka = kk_h * a[rows, hs]
        aq = -kk_h * e_prev[:, hs]
        rq = r[rows, hs] * e_pos[:, hs]
        aqs.append(aq)
        rqs.append(rq)
        vs.append(v[rows, hs])
        lhss.append(jnp.concatenate([aq, rq], axis=0))
        rhss.append(jnp.concatenate([kka * e_neg[:, hs], k_h * e_neg[:, hs]], axis=0))
        bkes.append(jnp.concatenate([kka * e_end[:, hs], k_h * e_end[:, hs]], axis=0))

    xss = [_dot_nt(lhs, rhs) for lhs, rhs in zip(lhss, rhss)]
    m1s = [jnp.where(tril_s, xs[:c, :c], 0.0) for xs in xss]
    m2s = [jnp.where(tril_s, xs[:c, c:], 0.0) for xs in xss]
    n12s = [jnp.concatenate([jnp.where(tril, xs[c:, :c], 0.0), jnp.where(tril, xs[c:, c:], 0.0)], axis=1)
            for xs in xss]
    tinvs = [eye + m1 for m1 in m1s]
    mps = m1s
    for _ in range(5):
        mps = [_dot(mp, mp) for mp in mps]
        tinvs = [tinv + _dot(tinv, mp) for tinv, mp in zip(tinvs, mps)]
    mvs = [_dot(m2, v_h) for m2, v_h in zip(m2s, vs)]
    pqs = [_dot(tinv, jnp.concatenate([mv, aq], axis=1)) for tinv, mv, aq in zip(tinvs, mvs, aqs)]
    pqvs = [jnp.concatenate([pq, jnp.concatenate([v_h, zeros_cc], axis=1)], axis=0)
            for pq, v_h in zip(pqs, vs)]
    yys = [_dot(n12, pqv) for n12, pqv in zip(n12s, pqvs)]
    ghs = [_dot_tn(bke, pqv) for bke, pqv in zip(bkes, pqvs)]

    states = [st_ref[h] for h in range(RWKV_HEADS)]
    ys = []
    for u, (ci, h) in enumerate(units):
        yq = rqs[u] + yys[u][:, hd:]
        gmat = eye * exps[ci][4][:, lanes_of(h)] + ghs[u][:, hd:]
        res = _dot(jnp.concatenate([yq, gmat], axis=0), states[h])
        ys.append(res[:c] + yys[u][:, :hd])
        states[h] = res[c:] + ghs[u][:, :hd]
    for h in range(RWKV_HEADS):
        st_ref[h] = states[h]

    out_rows = []
    for ci in range(nc):
        yns, bonuses = [], []
        for h in range(RWKV_HEADS):
            u = ci * RWKV_HEADS + h
            y = ys[u]
            cen = y - jnp.mean(y, axis=-1, keepdims=True)
            var = jnp.mean(cen * cen, axis=-1, keepdims=True)
            yns.append(cen * lax.rsqrt(var + RWKV_GN_EPS))
            bonuses.append(jnp.sum(rkr[rows_of(ci), lanes_of(h)], axis=-1, keepdims=True) * vs[u])
        yn = jnp.concatenate(yns, axis=1)
        bonus = jnp.concatenate(bonuses, axis=1)
        out_rows.append((yn * lg_ref[...] + lb_ref[...] + bonus) * g[rows_of(ci)])
    o_ref[0] = jnp.concatenate(out_rows, axis=0).astype(o_ref.dtype)


def _rwkv(h, norm_g, w_in, mu, w_w2, w0, w_a2, a0, w_g2, k_k, k_a, r_k, lnx_g, lnx_b, casts, zero_rows, tt=128):
    b, t, d = h.shape
    cols = w_in.shape[1]
    w = RWKV_HEADS * RWKV_HD
    nt = t // tt
    steps = b * nt
    vec = lambda a_: a_.reshape(1, -1)
    full = lambda shape: pl.BlockSpec(shape, lambda i, j: (0, 0))
    piece = lambda a_rows, a_cols: pl.BlockSpec((a_rows // steps, a_cols), lambda i, j: (i * nt + j, 0))
    cast_specs = [piece(*a.shape) for a in casts]
    outs = pl.pallas_call(
        functools.partial(_rwkv_kernel, tt=tt, n_cast=len(casts)),
        grid=(b, nt),
        in_specs=[pl.BlockSpec((1, tt, d), lambda i, j: (i, j, 0)), full((1, d)), full((d, cols)), full((1, cols)),
                  full(w_w2.shape), full((1, w)), full(w_a2.shape), full((1, w)), full(w_g2.shape),
                  full((1, w)), full((1, w)), full((1, w)), full((1, w)), full((1, w))] + cast_specs,
        out_specs=[pl.BlockSpec((1, tt, w), lambda i, j: (i, j, 0))] + cast_specs + [piece(zero_rows, d)],
        out_shape=[jax.ShapeDtypeStruct((b, t, w), BF16)] + [jax.ShapeDtypeStruct(a.shape, BF16) for a in casts]
        + [jax.ShapeDtypeStruct((zero_rows, d), F32)],
        scratch_shapes=[pltpu.VMEM((8 + tt, cols), F32),
                        pltpu.VMEM((RWKV_HEADS, RWKV_HD, RWKV_HD), F32)],
        compiler_params=_cparams("parallel", "arbitrary"),
        name="rwkv7",
    )(h, vec(norm_g), w_in, vec(mu), w_w2, vec(w0), w_a2, vec(a0), w_g2, vec(k_k), vec(k_a), vec(r_k),
      vec(lnx_g), vec(lnx_b), *casts)
    return outs[0], outs[1:-1], outs[-1]


def _fox_cum_kernel(f_ref, bf_ref, col_ref, carry_ref, *, tt):
    @pl.when(pl.program_id(1) == 0)
    def _():
        carry_ref[...] = jnp.zeros(carry_ref.shape, F32)

    log_f = _log_sigmoid(f_ref[0] + bf_ref[...])
    cum = _dot_hi(_tri(tt).astype(F32), log_f) + carry_ref[0:1, :]
    col_ref[0] = cum
    carry_ref[...] = jnp.broadcast_to(cum[tt - 1:tt, :], carry_ref.shape)


def _fox_cum(f_logit, b_f, tt=256):
    b, t, n = f_logit.shape
    return pl.pallas_call(
        functools.partial(_fox_cum_kernel, tt=tt),
        grid=(b, t // tt),
        in_specs=[pl.BlockSpec((1, tt, n), lambda i, j: (i, j, 0)), pl.BlockSpec((1, n), lambda i, j: (0, 0))],
        out_specs=pl.BlockSpec((1, tt, n), lambda i, j: (i, j, 0)),
        out_shape=jax.ShapeDtypeStruct((b, t, n), F32),
        scratch_shapes=[pltpu.VMEM((8, n), F32)],
        compiler_params=_cparams("parallel", "arbitrary"),
        name="fox_cumsum",
    )(f_logit, b_f)


def _split3(x):
    hi = x.astype(BF16).astype(F32)
    mid = (x - hi).astype(BF16).astype(F32)
    lo = (x - hi - mid).astype(BF16).astype(F32)
    return hi, mid, lo


def _fox_kernel(q_ref, k_ref, v_ref, c_ref, o_ref, kaug_ref, vaug_ref, *, tq, groups):
    qi = pl.program_id(2)
    d = FOX_HD
    per = LANE // d
    nh = groups * per
    head0 = pl.program_id(1) * nh

    def augment(x_h, c, first):
        lane = lax.broadcasted_iota(jnp.int32, x_h.shape, 1)
        hi, mid, lo = _split3(c)
        parts = (hi, mid, lo, 1.0, 1.0, 1.0) if first else (1.0, 1.0, 1.0, -hi, -mid, -lo)
        out = jnp.where(lane < d, x_h, 0.0)
        for n, part in enumerate(parts):
            out = jnp.where(lane == d + n, part, out)
        return out.astype(BF16)

    def head_col(c_all, head):
        lane = lax.broadcasted_iota(jnp.int32, c_all.shape, 1)
        return jnp.sum(jnp.where(lane == head, c_all, 0.0), axis=-1, keepdims=True)

    def head_lanes(x, h):
        x_g = x[:, (h // per) * LANE:(h // per + 1) * LANE]
        return x_g if h % per == 0 else pltpu.roll(x_g, LANE - (h % per) * d, axis=1)

    @pl.when(qi == 0)
    def _():
        lane_t = lax.broadcasted_iota(jnp.int32, (k_ref.shape[1], LANE), 1)
        for h in range(nh):
            kaug_ref[h] = augment(head_lanes(k_ref[0].astype(F32), h), head_col(c_ref[0], head0 + h), first=False)
            v_h = jnp.where(lane_t < d, head_lanes(v_ref[0].astype(F32), h), jnp.where(lane_t == d, 1.0, 0.0))
            vaug_ref[h] = v_h.astype(BF16)

    row0 = pl.multiple_of(qi * tq, tq)
    q = q_ref[0].astype(F32) * (d ** -0.5)
    c_q = c_ref[0, pl.ds(row0, tq), :]
    qs = [augment(head_lanes(q, h), head_col(c_q, head0 + h), first=True) for h in range(nh)]
    causal = _tri(tq)
    lag = 2

    def step(j, carry, diagonal):
        start = pl.multiple_of(j * tq, tq)
        ss, new = {}, []
        for h in range(nh + lag):
            if h < nh:
                ss[h] = lax.dot_general(qs[h], kaug_ref[h, pl.ds(start, tq), :], (((1,), (1,)), ((), ())),
                                        preferred_element_type=F32)
            g = h - lag
            if g >= 0:
                m, acc = carry[g]
                s = ss.pop(g)
                s = jnp.where(causal, s, -jnp.inf) if diagonal else s
                m_new = jnp.maximum(m, jnp.max(s, axis=-1, keepdims=True))
                p = jnp.exp(s - m_new).astype(BF16)
                pv = jnp.dot(p, vaug_ref[g, pl.ds(start, tq), :], preferred_element_type=F32)
                new.append((m_new, acc * jnp.exp(m - m_new) + pv))
        return tuple(new)

    init = tuple((jnp.full((tq, 1), -jnp.inf, F32), jnp.zeros((tq, LANE), F32)) for _ in range(nh))
    carry = lax.fori_loop(0, qi, lambda j, cr: step(j, cr, False), init)
    carry = step(qi, carry, True)
    lane = lax.broadcasted_iota(jnp.int32, (tq, LANE), 1)
    for grp in range(groups):
        out = None
        for hh in range(per):
            acc = carry[grp * per + hh][1]
            o_h = acc / jnp.sum(jnp.where(lane == d, acc, 0.0), axis=-1, keepdims=True)
            o_h = o_h if hh == 0 else pltpu.roll(o_h, hh * d, axis=1)
            out = o_h if out is None else jnp.where(lane >= hh * d, o_h, out)
        o_ref[0, :, grp * LANE:(grp + 1) * LANE] = out.astype(o_ref.dtype)


def _fox(q, k, v, c_col, tq=256, groups=4):
    b, t, w = q.shape
    gw = groups * LANE
    return pl.pallas_call(
        functools.partial(_fox_kernel, tq=tq, groups=groups),
        grid=(b, w // gw, t // tq),
        in_specs=[pl.BlockSpec((1, tq, gw), lambda i, p, j: (i, j, p)),
                  pl.BlockSpec((1, t, gw), lambda i, p, j: (i, 0, p)),
                  pl.BlockSpec((1, t, gw), lambda i, p, j: (i, 0, p)),
                  pl.BlockSpec((1, t, LANE), lambda i, p, j: (i, 0, 0))],
        out_specs=pl.BlockSpec((1, tq, gw), lambda i, p, j: (i, j, p)),
        out_shape=jax.ShapeDtypeStruct((b, t, w), BF16),
        scratch_shapes=[pltpu.VMEM((groups * LANE // FOX_HD, t, LANE), BF16),
                        pltpu.VMEM((groups * LANE // FOX_HD, t, LANE), BF16)],
        compiler_params=_cparams("parallel", "arbitrary", "arbitrary"),
        name="fox_attention",
    )(q, k, v, c_col)


def _serpentine(i, f, nf):
    return jnp.where(i % 2 == 0, f, nf - 1 - f)


def _mix_residual(h_ref, ya_ref, yb_ref, wa_ref, wb_ref):
    return h_ref[...] + _dot(ya_ref[...], wa_ref[...]) + _dot(yb_ref[...], wb_ref[...])


def _ffn_kernel(h_ref, ya_ref, yb_ref, wa_ref, wb_ref, g_ref, w1_ref, w3_ref, w2_ref, o_ref, xn_ref, acc_ref):
    f = pl.program_id(1)

    @pl.when(f == 0)
    def _():
        hn = _mix_residual(h_ref, ya_ref, yb_ref, wa_ref, wb_ref)
        xn_ref[...] = _rmsnorm(hn, g_ref[...]).astype(BF16)
        acc_ref[...] = hn

    xn = xn_ref[...]
    mid = _silu(jnp.dot(xn, w1_ref[...], preferred_element_type=F32)) * jnp.dot(
        xn, w3_ref[...], preferred_element_type=F32)
    acc_ref[...] += _dot(mid, w2_ref[...])

    @pl.when(f == pl.num_programs(1) - 1)
    def _():
        o_ref[...] = acc_ref[...]


def _ffn(h, ya, yb, wa, wb, g, w1, w3, w2, tm=1024, tf=256):
    m, d = h.shape
    k = ya.shape[1]
    nf = w1.shape[1] // tf
    rows = lambda n: pl.BlockSpec((tm, n), lambda i, f: (i, 0))
    full = lambda shape: pl.BlockSpec(shape, lambda i, f: (0, 0))
    return pl.pallas_call(
        _ffn_kernel,
        grid=(m // tm, nf),
        in_specs=[rows(d), rows(k), rows(k), full((k, d)), full((k, d)), full((1, d)),
                  pl.BlockSpec((d, tf), lambda i, f: (0, _serpentine(i, f, nf))),
                  pl.BlockSpec((d, tf), lambda i, f: (0, _serpentine(i, f, nf))),
                  pl.BlockSpec((tf, d), lambda i, f: (_serpentine(i, f, nf), 0))],
        out_specs=rows(d),
        out_shape=jax.ShapeDtypeStruct((m, d), F32),
        scratch_shapes=[pltpu.VMEM((tm, d), BF16), pltpu.VMEM((tm, d), F32)],
        compiler_params=_cparams("parallel", "arbitrary"),
        name="ffn_dense",
    )(h, ya, yb, wa, wb, g.reshape(1, d), w1, w3, w2)


def _split2(x):
    hi = x.astype(BF16)
    return hi, (x - hi.astype(F32)).astype(BF16)


def _moe_route_kernel(h_ref, ya_ref, yb_ref, wa_ref, wb_ref, g_ref, r_ref, tri_ref, hn_ref, xn_ref, meta_ref,
                      cnt_ref, carry_ref):
    @pl.when(pl.program_id(0) == 0)
    def _():
        carry_ref[...] = jnp.zeros(carry_ref.shape, F32)

    hn = _mix_residual(h_ref, ya_ref, yb_ref, wa_ref, wb_ref)
    hn_ref[...] = hn
    xn = _rmsnorm(hn, g_ref[...])
    xn_ref[...] = xn
    tm = xn.shape[0]
    lane = lax.broadcasted_iota(jnp.int32, (tm, LANE), 1)
    x_hi, x_lo = _split2(xn)
    r_hi, r_lo = _split2(r_ref[...])
    dot = functools.partial(jnp.dot, preferred_element_type=F32)
    logits = jnp.where(lane < N_EXPERTS, dot(x_hi, r_hi) + (dot(x_hi, r_lo) + dot(x_lo, r_hi)), -jnp.inf)
    m1 = jnp.max(logits, axis=-1, keepdims=True)
    i1 = jnp.min(jnp.where(logits == m1, lane, LANE), axis=-1, keepdims=True)
    rest = jnp.where(lane == i1, -jnp.inf, logits)
    m2 = jnp.max(rest, axis=-1, keepdims=True)
    i2 = jnp.min(jnp.where(rest == m2, lane, LANE), axis=-1, keepdims=True)
    e2 = jnp.exp(m2 - m1)
    g1 = 1.0 / (1.0 + e2)
    g2 = e2 / (1.0 + e2)
    onehot = jnp.where(lane == i1, 1.0, jnp.where(lane == i2, 1.0, 0.0))
    before = dot(tri_ref[...], onehot.astype(BF16)) + carry_ref[0:1, :]
    r1 = jnp.sum(jnp.where(lane == i1, before, 0.0), axis=-1, keepdims=True)
    r2 = jnp.sum(jnp.where(lane == i2, before, 0.0), axis=-1, keepdims=True)
    meta = jnp.where(lane == 0, i1.astype(F32), jnp.where(lane == 1, i2.astype(F32), 0.0))
    meta = jnp.where(lane == 2, g1, jnp.where(lane == 3, g2, meta))
    meta_ref[...] = jnp.where(lane == 4, r1, jnp.where(lane == 5, r2, meta))
    total = carry_ref[0:1, :] + jnp.sum(onehot, axis=0, keepdims=True)
    carry_ref[...] = jnp.broadcast_to(total, carry_ref.shape)
    cnt_ref[...] = jnp.broadcast_to(total, cnt_ref.shape)


def _moe_route(h, ya, yb, wa, wb, g, router, tm=512):
    m, d = h.shape
    k = ya.shape[1]
    rows = lambda n: pl.BlockSpec((tm, n), lambda i: (i, 0))
    full = lambda shape: pl.BlockSpec(shape, lambda i: (0, 0))
    return pl.pallas_call(
        _moe_route_kernel,
        grid=(m // tm,),
        in_specs=[rows(d), rows(k), rows(k), full((k, d)), full((k, d)), full((1, d)), full((d, LANE)),
                  full((tm, tm))],
        out_specs=[rows(d), rows(d), rows(LANE), full((8, LANE))],
        out_shape=[jax.ShapeDtypeStruct((m, d), F32), jax.ShapeDtypeStruct((m, d), F32),
                   jax.ShapeDtypeStruct((m, LANE), F32), jax.ShapeDtypeStruct((8, LANE), F32)],
        scratch_shapes=[pltpu.VMEM((8, LANE), F32)],
        compiler_params=_cparams("arbitrary"),
        name="moe_route",
    )(h, ya, yb, wa, wb, g.reshape(1, d), router, jnp.tril(jnp.ones((tm, tm), BF16), -1))


def _row_copy(src_ref, src_group, src_sub, dst_ref, dst_group, dst_sub, sem):
    return pltpu.make_async_copy(src_ref.at[src_group, pl.ds(src_sub, 1)],
                                 dst_ref.at[dst_group, pl.ds(dst_sub, 1)], sem)


def _split_row(p):
    return lax.shift_right_logical(p, 3), lax.bitwise_and(p, _SUBLANES - 1)


def _moe_dispatch_kernel(pos_ref, xn_ref, xs_in_ref, xs_ref, sem, *, tm):
    del xs_in_ref

    def start(grp, carry):
        for u in range(_SUBLANES):
            for s in range(2):
                dst_group, dst_sub = _split_row(pos_ref[0, 0, 2 * (grp * _SUBLANES + u) + s])
                _row_copy(xn_ref, grp, u, xs_ref, dst_group, dst_sub, sem).start(priority=s)
        return carry

    def wait(grp, carry):
        for _ in range(2 * _SUBLANES):
            _row_copy(xn_ref, 0, 0, xs_ref, 0, 0, sem).wait()
        return carry

    lax.fori_loop(0, tm // _SUBLANES, start, 0)
    lax.fori_loop(0, tm // _SUBLANES, wait, 0)


def _moe_dispatch(xn, pos, zeros, tm=256):
    m, d = xn.shape
    n_rows = zeros.shape[0]
    xs = pl.pallas_call(
        functools.partial(_moe_dispatch_kernel, tm=tm),
        grid=(m // tm,),
        in_specs=[pl.BlockSpec((1, 1, 2 * tm), lambda i: (i, 0, 0), memory_space=pltpu.SMEM),
                  pl.BlockSpec((tm // _SUBLANES, _SUBLANES, d), lambda i: (i, 0, 0)),
                  pl.BlockSpec(memory_space=pl.ANY)],
        out_specs=pl.BlockSpec(memory_space=pl.ANY),
        out_shape=jax.ShapeDtypeStruct((n_rows // _SUBLANES, _SUBLANES, d), F32),
        scratch_shapes=[pltpu.SemaphoreType.DMA(())],
        input_output_aliases={2: 0},
        compiler_params=_cparams("arbitrary"),
        name="moe_dispatch",
    )(pos.reshape(m // tm, 1, 2 * tm), xn.reshape(m // _SUBLANES, _SUBLANES, d),
      zeros.reshape(n_rows // _SUBLANES, _SUBLANES, d))
    return xs.reshape(n_rows, d)


def _moe_expert_kernel(te_ref, nu_ref, x_ref, w1_ref, w3_ref, w2_ref, y_ref, xbf_ref, acc_ref):
    del te_ref
    f = pl.program_id(1)

    @pl.when(f == 0)
    def _():
        xbf_ref[...] = x_ref[...].astype(BF16)
        acc_ref[...] = jnp.zeros(acc_ref.shape, F32)

    @pl.when(pl.program_id(0) < nu_ref[0])
    def _():
        xb = xbf_ref[...]
        mid = _silu(jnp.dot(xb, w1_ref[0], preferred_element_type=F32)) * jnp.dot(
            xb, w3_ref[0], preferred_element_type=F32)
        acc_ref[...] += _dot(mid, w2_ref[0])

    @pl.when(f == pl.num_programs(1) - 1)
    def _():
        y_ref[...] = acc_ref[...]


def _moe_experts(xs, tile_expert, n_used, w1, w3, w2, tmx, tf=1792):
    n_rows, d = xs.shape
    nf = w1.shape[2] // tf

    def fidx(i, f, nu):
        last = i >= nu[0]
        return _serpentine(jnp.where(last, nu[0] - 1, i), jnp.where(last, nf - 1, f), nf)

    grid_spec = pltpu.PrefetchScalarGridSpec(
        num_scalar_prefetch=2,
        grid=(n_rows // tmx, nf),
        in_specs=[pl.BlockSpec((tmx, d), lambda i, f, te, nu: (i, 0)),
                  pl.BlockSpec((1, d, tf), lambda i, f, te, nu: (te[i], 0, fidx(i, f, nu))),
                  pl.BlockSpec((1, d, tf), lambda i, f, te, nu: (te[i], 0, fidx(i, f, nu))),
                  pl.BlockSpec((1, tf, d), lambda i, f, te, nu: (te[i], fidx(i, f, nu), 0))],
        out_specs=pl.BlockSpec((tmx, d), lambda i, f, te, nu: (i, 0)),
        scratch_shapes=[pltpu.VMEM((tmx, d), BF16), pltpu.VMEM((tmx, d), F32)],
    )
    return pl.pallas_call(
        _moe_expert_kernel,
        grid_spec=grid_spec,
        out_shape=jax.ShapeDtypeStruct((n_rows, d), F32),
        compiler_params=_cparams("arbitrary", "arbitrary"),
        name="moe_experts",
    )(tile_expert, n_used, xs, w1, w3, w2)


def _moe_combine_kernel(pos_ref, posn_ref, h_ref, meta_ref, fg_ref, ys_ref, o_ref, ybuf_ref, sem, *, tm):
    i = pl.program_id(0)
    slot = lax.rem(i, 2)

    groups = tm // _SUBLANES

    def gather(p_ref, dst_slot):
        def start(grp, carry):
            for u in range(_SUBLANES):
                for s in range(2):
                    src_group, src_sub = _split_row(p_ref[0, 0, 2 * (grp * _SUBLANES + u) + s])
                    _row_copy(ys_ref, src_group, src_sub, ybuf_ref.at[dst_slot], s * groups + grp, u,
                              sem.at[dst_slot]).start(priority=s)
            return carry
        lax.fori_loop(0, groups, start, 0)

    @pl.when(i == 0)
    def _():
        gather(pos_ref, 0)

    @pl.when(i + 1 < pl.num_programs(0))
    def _():
        gather(posn_ref, 1 - slot)

    def wait(grp, carry):
        for _ in range(2 * _SUBLANES):
            _row_copy(ys_ref, 0, 0, ybuf_ref.at[slot], 0, 0, sem.at[slot]).wait()
        return carry

    lax.fori_loop(0, groups, wait, 0)
    lane = lax.broadcasted_iota(jnp.int32, (tm, LANE), 1)
    meta = meta_ref[...]
    g1 = jnp.sum(jnp.where(lane == 2, meta, 0.0), axis=-1, keepdims=True)
    g2 = jnp.sum(jnp.where(lane == 3, meta, 0.0), axis=-1, keepdims=True)
    y = ybuf_ref[slot].reshape(2 * tm, -1)
    o_ref[...] = _rmsnorm(h_ref[...] + g1 * y[:tm] + g2 * y[tm:], fg_ref[...])


def _moe_combine(h, meta, pos, ys, final_g, tm=256):
    m, d = h.shape
    nt = m // tm
    pos3 = pos.reshape(nt, 1, 2 * tm)
    smem = lambda imap: pl.BlockSpec((1, 1, 2 * tm), imap, memory_space=pltpu.SMEM)
    return pl.pallas_call(
        functools.partial(_moe_combine_kernel, tm=tm),
        grid=(nt,),
        in_specs=[smem(lambda i: (i, 0, 0)), smem(lambda i: (jnp.minimum(i + 1, nt - 1), 0, 0)),
                  pl.BlockSpec((tm, d), lambda i: (i, 0)), pl.BlockSpec((tm, LANE), lambda i: (i, 0)),
                  pl.BlockSpec((1, d), lambda i: (0, 0)), pl.BlockSpec(memory_space=pl.ANY)],
        out_specs=pl.BlockSpec((tm, d), lambda i: (i, 0)),
        out_shape=jax.ShapeDtypeStruct((m, d), F32),
        scratch_shapes=[pltpu.VMEM((2, 2 * tm // _SUBLANES, _SUBLANES, d), F32), pltpu.SemaphoreType.DMA((2,))],
        compiler_params=_cparams("arbitrary"),
        name="moe_combine",
    )(pos3, pos3, h, meta, final_g.reshape(1, d), ys.reshape(-1, _SUBLANES, d))


_MOE_TILE = 512


def _moe_sorted_rows(m):
    return ((2 * m) // _MOE_TILE + N_EXPERTS) * _MOE_TILE


def _moe_routed(h, ya, yb, wa, wb, g, router, w1, w3, w2, final_g, zeros):
    m, d = h.shape
    tmx = _MOE_TILE
    h, xn, meta, cnt = _moe_route(h, ya, yb, wa, wb, g, router)
    experts = meta[:, 0:2].astype(jnp.int32)
    rank = meta[:, 4:6].astype(jnp.int32)
    counts = cnt[0, :N_EXPERTS].astype(jnp.int32)
    padded = (counts + tmx - 1) // tmx * tmx
    ends = jnp.cumsum(padded)
    pos = (ends - padded)[experts] + rank
    n_tiles = zeros.shape[0] // tmx
    n_used = ends[-1] // tmx
    tile_start = jnp.minimum(jnp.arange(n_tiles, dtype=jnp.int32), n_used - 1) * tmx
    tile_expert = jnp.sum(tile_start[:, None] >= ends[None, :], axis=1).astype(jnp.int32)
    xs = _moe_dispatch(xn, pos, zeros)
    ys = _moe_experts(xs, tile_expert, n_used.reshape(1).astype(jnp.int32), w1, w3, w2, tmx)
    return _moe_combine(h, meta, pos, ys, final_g)


def _pad_cols(w, n):
    return jnp.pad(w, ((0, 0), (0, n - w.shape[1])))


def kernel(x, norm_mix_g, norm_ffn_g, w_in0, conv_w, conv_b, conv_ln_g, conv_ln_b, gla_w_a2, gla_b_a, gla_norm_g, w_out0, ffn_w1, ffn_w3, ffn_w2, w_in1, rwkv_mu, rwkv_w2, rwkv_w0, rwkv_a2, rwkv_a0, rwkv_g2, rwkv_k_k, rwkv_k_a, rwkv_r_k, rwkv_lnx_g, rwkv_lnx_b, fox_b_f, w_out1, moe_router, moe_w1, moe_w3, moe_w2, final_norm_g):
    b, t, d = x.shape
    m = b * t
    half = d // 2
    bf = lambda a: a.astype(BF16)
    h = x.reshape(m, d)

    y_conv, y_gla = _layer0_mix(x, norm_mix_g[0], w_in0, conv_w, conv_b, conv_ln_g, conv_ln_b,
                                gla_w_a2, gla_b_a, gla_norm_g)
    h = _ffn(h, y_conv.reshape(m, half), y_gla.reshape(m, half), bf(w_out0[:half]), bf(w_out0[half:]),
             norm_ffn_g[0], bf(ffn_w1), bf(ffn_w3), bf(ffn_w2))

    rc = _RWKV_COLS
    flat = lambda w: w.reshape(-1, w.shape[-1])
    y_rwkv, (e_w1, e_w3, e_w2), moe_zeros = _rwkv(
        h.reshape(b, t, d), norm_mix_g[1], bf(w_in1[:, :rc]), rwkv_mu, rwkv_w2, rwkv_w0, rwkv_a2, rwkv_a0,
        rwkv_g2, rwkv_k_k, rwkv_k_a, rwkv_r_k, rwkv_lnx_g, rwkv_lnx_b,
        casts=[flat(moe_w1), flat(moe_w3), flat(moe_w2)], zero_rows=_moe_sorted_rows(m))
    ws1 = [bf(w_in1[:, rc:rc + half]), bf(w_in1[:, rc + half:rc + 2 * half]),
           bf(w_in1[:, rc + 2 * half:rc + 3 * half]), bf(_pad_cols(w_in1[:, rc + 3 * half:], LANE))]
    fq, fk, fv, f_logit = [a.reshape(b, t, -1) for a in _norm_proj(h, norm_mix_g[1], ws1, [BF16] * 3 + [F32])]
    b_f = jnp.pad(fox_b_f, (0, LANE - FOX_HEADS)).reshape(1, LANE)
    y_fox = _fox(fq, fk, fv, _fox_cum(f_logit, b_f))
    out = _moe_routed(h, y_rwkv.reshape(m, half), y_fox.reshape(m, half), bf(w_out1[:half]), bf(w_out1[half:]),
                      norm_ffn_g[1], _pad_cols(moe_router, LANE), e_w1.reshape(moe_w1.shape),
                      e_w3.reshape(moe_w3.shape), e_w2.reshape(moe_w2.shape), final_norm_g, moe_zeros)
    return out.reshape(b, t, d)
```

```python
import functools

import jax
import jax.numpy as jnp
from jax import lax
from jax.experimental import pallas as pl
from jax.experimental.pallas import tpu as pltpu

F32 = jnp.float32
BF16 = jnp.bfloat16
HI = lax.Precision.HIGHEST

NORM_EPS = 1e-6
CONV_WIDTH = 31
CONV_LN_EPS = 1e-5
GLA_HEADS = 4
GLA_DK = 64
GLA_DV = 128
GLA_TAU = 16.0
GLA_CHUNK = 64
RWKV_HEADS = 8
RWKV_HD = 64
RWKV_CHUNK = 64
RWKV_DECAY_SCALE = 0.606531
RWKV_GN_EPS = 64e-5
RWKV_L2_EPS = 1e-12
FOX_HEADS = 8
FOX_HD = 64
N_EXPERTS = 8
LANE = 128
_SUBLANES = 8
VMEM_LIMIT = 56 * 1024 * 1024


def _cparams(*sem):
    return pltpu.CompilerParams(dimension_semantics=sem, vmem_limit_bytes=VMEM_LIMIT)


def _dot(a, b):
    return jnp.dot(a.astype(BF16), b.astype(BF16), preferred_element_type=F32)


def _dot_nt(a, b):
    return lax.dot_general(a.astype(BF16), b.astype(BF16), (((1,), (1,)), ((), ())),
                           preferred_element_type=F32)


def _dot_tn(a, b):
    return lax.dot_general(a.astype(BF16), b.astype(BF16), (((0,), (0,)), ((), ())),
                           preferred_element_type=F32)


def _dot_hi(a, b):
    return jnp.dot(a, b, precision=HI, preferred_element_type=F32)


def _dot_tn_hi(a, b):
    return lax.dot_general(a, b, (((0,), (0,)), ((), ())), precision=HI, preferred_element_type=F32)


def _sigmoid(x):
    return 1.0 / (1.0 + jnp.exp(-x))


def _silu(x):
    return x * _sigmoid(x)


def _log_sigmoid(x):
    return jnp.minimum(x, 0.0) - jnp.log(1.0 + jnp.exp(-jnp.abs(x)))


def _rmsnorm(x, g):
    return x * lax.rsqrt(jnp.mean(x * x, axis=-1, keepdims=True) + NORM_EPS) * g


def _tri(n, strict=False):
    r = lax.broadcasted_iota(jnp.int32, (n, n), 0)
    c = lax.broadcasted_iota(jnp.int32, (n, n), 1)
    return (r > c) if strict else (r >= c)


_CUM_ROWS = 256


def _fox_proj_kernel(h_ref, g_ref, wq_ref, wk_ref, wv_ref, wf_ref, bf_ref, q_ref, k_ref, v_ref, c_ref, carry_ref):
    @pl.when(pl.program_id(1) == 0)
    def _():
        carry_ref[...] = jnp.zeros(carry_ref.shape, F32)

    xn = _rmsnorm(h_ref[0], g_ref[...]).astype(BF16)
    for w_ref, o_ref in ((wq_ref, q_ref), (wk_ref, k_ref), (wv_ref, v_ref)):
        o_ref[0] = jnp.dot(xn, w_ref[...], preferred_element_type=F32).astype(o_ref.dtype)
    log_f = _log_sigmoid(jnp.dot(xn, wf_ref[...], preferred_element_type=F32) + bf_ref[...])
    tri = _tri(_CUM_ROWS).astype(F32)
    last = carry_ref[0:1, :]
    for r0 in range(0, xn.shape[0], _CUM_ROWS):
        hi, mid, lo = _split3(log_f[r0:r0 + _CUM_ROWS])
        cum = (_dot(tri, lo) + _dot(tri, mid)) + _dot(tri, hi) + last
        c_ref[0, r0:r0 + _CUM_ROWS, :] = cum
        last = cum[_CUM_ROWS - 1:_CUM_ROWS, :]
    carry_ref[...] = jnp.broadcast_to(last, carry_ref.shape)


def _fox_proj(h, g, wq, wk, wv, wf, b_f, tt=512):
    b, t, d = h.shape
    w = wq.shape[1]
    full = lambda a: pl.BlockSpec(a.shape, lambda i, j: (0, 0))
    tile = lambda n: pl.BlockSpec((1, tt, n), lambda i, j: (i, j, 0))
    g2 = g.reshape(1, d)
    return pl.pallas_call(
        _fox_proj_kernel,
        grid=(b, t // tt),
        in_specs=[tile(d), full(g2), full(wq), full(wk), full(wv), full(wf), full(b_f)],
        out_specs=[tile(w), tile(w), tile(w), tile(LANE)],
        out_shape=[jax.ShapeDtypeStruct((b, t, w), BF16)] * 3 + [jax.ShapeDtypeStruct((b, t, LANE), F32)],
        scratch_shapes=[pltpu.VMEM((8, LANE), F32)],
        compiler_params=_cparams("parallel", "arbitrary"),
        name="fox_proj",
    )(h, g2, wq, wk, wv, wf, b_f)


_CONV_HALO = 32
_CONV_ROWS = 32
_GLA_SPAN = 256


def _conv_body(val, gate, w_ref, b_ref, lg_ref, lb_ref, o_ref, u_ref, sh_ref):
    tt, c = val.shape
    rows = _CONV_ROWS
    u_ref[_CONV_HALO:_CONV_HALO + tt, :] = val * _sigmoid(gate)
    span = sh_ref.shape[1]
    for s in range(1, _SUBLANES):
        sh_ref[s - 1] = u_ref[s:s + span, :]
    base = _CONV_HALO - (CONV_WIDTH - 1)
    for r0 in range(0, tt, rows):
        acc = jnp.zeros((rows // _SUBLANES, _SUBLANES, c), F32) + b_ref[...]
        for j in range(CONV_WIDTH):
            phase = (base + j) % _SUBLANES
            row = r0 + base + j - phase
            win = u_ref[row:row + rows, :] if phase == 0 else sh_ref[phase - 1, row:row + rows, :]
            acc = acc + win.reshape(rows // _SUBLANES, _SUBLANES, c) * w_ref[j]
        acc = acc.reshape(rows, c)
        mu = jnp.mean(acc, axis=-1, keepdims=True)
        cen = acc - mu
        var = jnp.mean(cen * cen, axis=-1, keepdims=True)
        y = cen * lax.rsqrt(var + CONV_LN_EPS) * lg_ref[...] + lb_ref[...]
        o_ref[0, r0:r0 + rows, :] = _silu(y).astype(o_ref.dtype)
    u_ref[0:_CONV_HALO, :] = u_ref[tt:tt + _CONV_HALO, :]


def _gla_body(q, k, v, go, alr, wa2_ref, ba_ref, ng_ref, o_ref, s_ref, row0):
    c = GLA_CHUNK
    nc = q.shape[0] // c
    tril = _tri(c)
    tril_f = tril.astype(F32)
    ones_cv = jnp.ones((c, GLA_DV), F32)
    units = [(ci, h) for ci in range(nc) for h in range(GLA_HEADS)]
    rows_of = lambda ci: slice(ci * c, (ci + 1) * c)
    keys_of = lambda h: slice(h * GLA_DK, (h + 1) * GLA_DK)
    vals_of = lambda h: slice(h * GLA_DV, (h + 1) * GLA_DV)

    z = _dot(alr, wa2_ref[...]) + ba_ref[...]
    log_a = _log_sigmoid(z) * (1.0 / GLA_TAU)
    q = q * (GLA_DK ** -0.5)
    q_decs, k_decs, k_ends, decays = [], [], [], []
    for ci in range(nc):
        la = log_a[rows_of(ci)]
        bcum = _dot_hi(tril_f, la)
        b_last = bcum[c - 1:c, :]
        q_decs.append(q[rows_of(ci)] * jnp.exp(bcum))
        k_decs.append(k[rows_of(ci)] * jnp.exp(-bcum))
        k_ends.append(k[rows_of(ci)] * jnp.exp(b_last - bcum))
        decays.append(jnp.exp(_dot_tn_hi(la, ones_cv)))

    vs = [v[rows_of(ci), vals_of(h)] for ci, h in units]
    attns = [jnp.where(tril, _dot_nt(q_decs[ci][:, keys_of(h)], k_decs[ci][:, keys_of(h)]), 0.0)
             for ci, h in units]
    kvs = [_dot_tn(k_ends[ci][:, keys_of(h)], vs[u]) for u, (ci, h) in enumerate(units)]
    intras = [_dot(attns[u], vs[u]) for u in range(len(units))]
    states = [s_ref[h] for h in range(GLA_HEADS)]
    prevs = []
    for u, (ci, h) in enumerate(units):
        prevs.append(states[h])
        states[h] = states[h] * decays[ci][keys_of(h)] + kvs[u]
    for h in range(GLA_HEADS):
        s_ref[h] = states[h]
    inters = [_dot(q_decs[ci][:, keys_of(h)], prevs[u]) for u, (ci, h) in enumerate(units)]
    for ci in range(nc):
        outs = []
        for h in range(GLA_HEADS):
            u = ci * GLA_HEADS + h
            o = intras[u] + inters[u]
            outs.append(o * lax.rsqrt(jnp.mean(o * o, axis=-1, keepdims=True) + NORM_EPS))
        o_all = jnp.concatenate(outs, axis=-1)
        o_ref[0, row0 + ci * c:row0 + (ci + 1) * c, :] = (
            o_all * ng_ref[...] * _silu(go[rows_of(ci)])).astype(o_ref.dtype)


def _layer0_kernel(h_ref, g_ref, wcv_ref, wcg_ref, wq_ref, wk_ref, wv_ref, wgo_ref, walr_ref,
                   cw_ref, cb_ref, clg_ref, clb_ref, wa2_ref, ba_ref, ng_ref,
                   yc_ref, yg_ref, u_ref, sh_ref, s_ref):
    @pl.when(pl.program_id(1) == 0)
    def _():
        u_ref[0:_CONV_HALO, :] = jnp.zeros((_CONV_HALO, u_ref.shape[1]), F32)
        s_ref[...] = jnp.zeros(s_ref.shape, F32)

    xn = _rmsnorm(h_ref[0], g_ref[...]).astype(BF16)
    proj = lambda w_ref, rows=slice(None): jnp.dot(xn[rows], w_ref[...], preferred_element_type=F32)
    _conv_body(proj(wcv_ref), proj(wcg_ref), cw_ref, cb_ref, clg_ref, clb_ref, yc_ref, u_ref, sh_ref)
    for row0 in range(0, xn.shape[0], _GLA_SPAN):
        rows = slice(row0, row0 + _GLA_SPAN)
        _gla_body(proj(wq_ref, rows), proj(wk_ref, rows), proj(wv_ref, rows), proj(wgo_ref, rows),
                  proj(walr_ref, rows), wa2_ref, ba_ref, ng_ref, yg_ref, s_ref, row0)


def _layer0_mix(x, g, w_in, conv_w, conv_b, ln_g, ln_b, w_a2, b_a, norm_g, tt=512):
    b, t, d = x.shape
    c = d // 2
    gk = GLA_HEADS * GLA_DK
    o = [0, c, 2 * c, 2 * c + gk, 2 * c + 2 * gk, 3 * c + 2 * gk, 4 * c + 2 * gk]
    ws = [w_in[:, o[i]:o[i + 1]].astype(BF16) for i in range(6)] + [_pad_cols(w_in[:, o[6]:], LANE).astype(BF16)]
    wrep = jnp.broadcast_to(conv_w[:, None, :], (CONV_WIDTH, _SUBLANES, c))
    w_a2p = jnp.pad(w_a2, ((0, LANE - w_a2.shape[0]), (0, 0)))
    vec = lambda a: a.reshape(1, -1)
    full = lambda a: pl.BlockSpec(a.shape, lambda i, j: (0,) * a.ndim)
    tile = lambda n: pl.BlockSpec((1, tt, n), lambda i, j: (i, j, 0))
    consts = ws + [wrep, vec(conv_b), vec(ln_g), vec(ln_b), w_a2p, vec(b_a), vec(norm_g)]
    return pl.pallas_call(
        _layer0_kernel,
        grid=(b, t // tt),
        in_specs=[tile(d), full(vec(g))] + [full(a) for a in consts],
        out_specs=[tile(c), tile(c)],
        out_shape=[jax.ShapeDtypeStruct((b, t, c), BF16), jax.ShapeDtypeStruct((b, t, c), BF16)],
        scratch_shapes=[pltpu.VMEM((tt + _CONV_HALO, c), F32),
                        pltpu.VMEM((_SUBLANES - 1, tt + _CONV_HALO - _SUBLANES, c), F32),
                        pltpu.VMEM((GLA_HEADS, GLA_DK, GLA_DV), F32)],
        compiler_params=_cparams("parallel", "arbitrary"),
        name="layer0_mix",
    )(x, vec(g), *consts)


_RWKV_COLS = 3 * 512 + 64 + 64 + 128


def _rwkv_kernel(h_ref, ng_ref, win_ref, mu_ref, w2_ref, w0_ref, a2_ref, a0_ref, g2_ref, kk_ref, ka_ref, rk_ref,
                 lg_ref, lb_ref, *refs, tt, n_cast):
    cast_in, (o_ref, *cast_out, zero_ref), (xs_ref, st_ref) = (
        refs[:n_cast], refs[n_cast:2 * n_cast + 2], refs[2 * n_cast + 2:])
    c = RWKV_CHUNK
    hd = RWKV_HD
    w = RWKV_HEADS * hd
    for src, dst in zip(cast_in, cast_out):
        dst[...] = src[...].astype(BF16)
    zero_ref[...] = jnp.zeros(zero_ref.shape, F32)

    @pl.when(pl.program_id(1) == 0)
    def _():
        xs_ref[0:8, :] = jnp.zeros((8, xs_ref.shape[1]), F32)
        st_ref[...] = jnp.zeros(st_ref.shape, F32)

    x = jnp.dot(_rmsnorm(h_ref[0], ng_ref[...]).astype(BF16), win_ref[...], preferred_element_type=F32)
    xs_ref[8:8 + tt, :] = x
    prev = xs_ref[7:7 + tt, :]
    xs_ref[7:8, :] = x[tt - 1:tt, :]
    x = x + (prev - x) * mu_ref[...]

    r = x[:, 0:w]
    k = x[:, w:2 * w]
    v = x[:, 2 * w:3 * w]
    xw = x[:, 3 * w:3 * w + 64]
    xa = x[:, 3 * w + 64:3 * w + 128]
    xg = x[:, 3 * w + 128:3 * w + 256]
    log_w = -RWKV_DECAY_SCALE * _sigmoid(w0_ref[...] + _dot(jnp.tanh(xw), w2_ref[...]))
    a = _sigmoid(a0_ref[...] + _dot(xa, a2_ref[...]))
    g = _dot(_sigmoid(xg), g2_ref[...])
    kk = k * kk_ref[...]
    k_mod = k * (1.0 + (a - 1.0) * ka_ref[...])
    rkr = r * k_mod * rk_ref[...]

    tril = _tri(c)
    tril_s = _tri(c, strict=True)
    tril_f = tril.astype(F32)
    eye = (lax.broadcasted_iota(jnp.int32, (c, c), 0)
           == lax.broadcasted_iota(jnp.int32, (c, c), 1)).astype(F32)
    zeros_cc = jnp.zeros((c, hd), F32)

    nc = tt // c
    units = [(ci, h) for ci in range(nc) for h in range(RWKV_HEADS)]
    rows_of = lambda ci: slice(ci * c, (ci + 1) * c)
    lanes_of = lambda h: slice(h * hd, (h + 1) * hd)

    exps = []
    for ci in range(nc):
        lw_c = log_w[rows_of(ci)]
        lcum = _dot_hi(tril_f, lw_c)
        l_last = lcum[c - 1:c, :]
        exps.append((jnp.exp(lcum), jnp.exp(-lcum), jnp.exp(l_last - lcum), jnp.exp(lcum - lw_c),
                     jnp.exp(l_last)))

    aqs, rqs, vs, lhss, rhss, bkes = [], [], [], [], [], []
    for ci, h in units:
        rows, hs = rows_of(ci), lanes_of(h)
        e_pos, e_neg, e_end, e_prev, _ = exps[ci]
        kk_h = kk[rows, hs]
        nrm = jnp.sqrt(jnp.sum(kk_h * kk_h, axis=-1, keepdims=True))
        kk_h = kk_h / jnp.maximum(nrm, RWKV_L2_EPS)
        k_h = k_mod[rows, hs]
        kka = kk_h * a[rows, hs]
        aq = -kk_h * e_prev[:, hs]
        rq = r[rows, hs] * e_pos[:, hs]
        aqs.append(aq)
        rqs.append(rq)
        vs.append(v[rows, hs])
        lhss.append(jnp.concatenate([aq, rq], axis=0))
        rhss.append(jnp.concatenate([kka * e_neg[:, hs], k_h * e_neg[:, hs]], axis=0))
        bkes.append(jnp.concatenate([kka * e_end[:, hs], k_h * e_end[:, hs]], axis=0))

    xss = [_dot_nt(lhs, rhs) for lhs, rhs in zip(lhss, rhss)]
    m1s = [jnp.where(tril_s, xs[:c, :c], 0.0) for xs in xss]
    m2s = [jnp.where(tril_s, xs[:c, c:], 0.0) for xs in xss]
    n12s = [jnp.concatenate([jnp.where(tril, xs[c:, :c], 0.0), jnp.where(tril, xs[c:, c:], 0.0)], axis=1)
            for xs in xss]
    tinvs = [eye + m1 for m1 in m1s]
    mps = m1s
    for _ in range(5):
        mps = [_dot(mp, mp) for mp in mps]
        tinvs = [tinv + _dot(tinv, mp) for tinv, mp in zip(tinvs, mps)]
    mvs = [_dot(m2, v_h) for m2, v_h in zip(m2s, vs)]
    pqs = [_dot(tinv, jnp.concatenate([mv, aq], axis=1)) for tinv, mv, aq in zip(tinvs, mvs, aqs)]
    pqvs = [jnp.concatenate([pq, jnp.concatenate([v_h, zeros_cc], axis=1)], axis=0)
            for pq, v_h in zip(pqs, vs)]
    yys = [_dot(n12, pqv) for n12, pqv in zip(n12s, pqvs)]
    ghs = [_dot_tn(bke, pqv) for bke, pqv in zip(bkes, pqvs)]

    states = [st_ref[h] for h in range(RWKV_HEADS)]
    ys = []
    for u, (ci, h) in enumerate(units):
        yq = rqs[u] + yys[u][:, hd:]
        gmat = eye * exps[ci][4][:, lanes_of(h)] + ghs[u][:, hd:]
        res = _dot(jnp.concatenate([yq, gmat], axis=0), states[h])
        ys.append(res[:c] + yys[u][:, :hd])
        states[h] = res[c:] + ghs[u][:, :hd]
    for h in range(RWKV_HEADS):
        st_ref[h] = states[h]

    out_rows = []
    for ci in range(nc):
        yns, bonuses = [], []
        for h in range(RWKV_HEADS):
            u = ci * RWKV_HEADS + h
            y = ys[u]
            cen = y - jnp.mean(y, axis=-1, keepdims=True)
            var = jnp.mean(cen * cen, axis=-1, keepdims=True)
            yns.append(cen * lax.rsqrt(var + RWKV_GN_EPS))
            bonuses.append(jnp.sum(rkr[rows_of(ci), lanes_of(h)], axis=-1, keepdims=True) * vs[u])
        yn = jnp.concatenate(yns, axis=1)
        bonus = jnp.concatenate(bonuses, axis=1)
        out_rows.append((yn * lg_ref[...] + lb_ref[...] + bonus) * g[rows_of(ci)])
    o_ref[0] = jnp.concatenate(out_rows, axis=0).astype(o_ref.dtype)


def _rwkv(h, norm_g, w_in, mu, w_w2, w0, w_a2, a0, w_g2, k_k, k_a, r_k, lnx_g, lnx_b, casts, zero_rows, tt=128):
    b, t, d = h.shape
    cols = w_in.shape[1]
    w = RWKV_HEADS * RWKV_HD
    nt = t // tt
    steps = b * nt
    vec = lambda a_: a_.reshape(1, -1)
    full = lambda shape: pl.BlockSpec(shape, lambda i, j: (0, 0))
    piece = lambda a_rows, a_cols: pl.BlockSpec((a_rows // steps, a_cols), lambda i, j: (i * nt + j, 0))
    cast_specs = [piece(*a.shape) for a in casts]
    outs = pl.pallas_call(
        functools.partial(_rwkv_kernel, tt=tt, n_cast=len(casts)),
        grid=(b, nt),
        in_specs=[pl.BlockSpec((1, tt, d), lambda i, j: (i, j, 0)), full((1, d)), full((d, cols)), full((1, cols)),
                  full(w_w2.shape), full((1, w)), full(w_a2.shape), full((1, w)), full(w_g2.shape),
                  full((1, w)), full((1, w)), full((1, w)), full((1, w)), full((1, w))] + cast_specs,
        out_specs=[pl.BlockSpec((1, tt, w), lambda i, j: (i, j, 0))] + cast_specs + [piece(zero_rows, d)],
        out_shape=[jax.ShapeDtypeStruct((b, t, w), BF16)] + [jax.ShapeDtypeStruct(a.shape, BF16) for a in casts]
        + [jax.ShapeDtypeStruct((zero_rows, d), F32)],
        scratch_shapes=[pltpu.VMEM((8 + tt, cols), F32),
                        pltpu.VMEM((RWKV_HEADS, RWKV_HD, RWKV_HD), F32)],
        compiler_params=_cparams("parallel", "arbitrary"),
        name="rwkv7",
    )(h, vec(norm_g), w_in, vec(mu), w_w2, vec(w0), w_a2, vec(a0), w_g2, vec(k_k), vec(k_a), vec(r_k),
      vec(lnx_g), vec(lnx_b), *casts)
    return outs[0], outs[1:-1], outs[-1]


def _split3(x):
    hi = x.astype(BF16).astype(F32)
    mid = (x - hi).astype(BF16).astype(F32)
    lo = (x - hi - mid).astype(BF16).astype(F32)
    return hi, mid, lo


def _fox_kernel(q_ref, k_ref, v_ref, c_ref, o_ref, kaug_ref, vaug_ref, *, tq, groups):
    qi = pl.program_id(2)
    d = FOX_HD
    per = LANE // d
    nh = groups * per
    head0 = pl.program_id(1) * nh

    def pack_parts(c_all):
        lane = lax.broadcasted_iota(jnp.int32, c_all.shape, 1)
        hi, mid, lo = _split3(c_all)
        packed = jnp.where(lane < FOX_HEADS, hi, jnp.where(
            lane < 2 * FOX_HEADS, pltpu.roll(mid, FOX_HEADS, axis=1), jnp.where(
                lane < 3 * FOX_HEADS, pltpu.roll(lo, 2 * FOX_HEADS, axis=1), 0.0)))
        return packed.astype(BF16)

    def augment(x_h, parts, head, first):
        r = lax.broadcasted_iota(jnp.int32, (LANE, LANE), 0)
        col = lax.broadcasted_iota(jnp.int32, (LANE, LANE), 1)
        base = d if first else d + 3
        hit = (r - head == (col - base) * FOX_HEADS) & (col >= base) & (col < base + 3)
        sel = jnp.where(hit, 1.0 if first else -1.0, 0.0).astype(BF16)
        lane = lax.broadcasted_iota(jnp.int32, x_h.shape, 1)
        ones_lo = d + 3 if first else d
        fixed = jnp.where(lane < d, x_h, jnp.where((lane >= ones_lo) & (lane < ones_lo + 3), 1.0, 0.0))
        return (fixed + jnp.dot(parts, sel, preferred_element_type=F32)).astype(BF16)

    def head_lanes(x, h):
        x_g = x[:, (h // per) * LANE:(h // per + 1) * LANE]
        return x_g if h % per == 0 else pltpu.roll(x_g, LANE - (h % per) * d, axis=1)

    @pl.when(qi == 0)
    def _():
        lane_t = lax.broadcasted_iota(jnp.int32, (k_ref.shape[1], LANE), 1)
        parts = pack_parts(c_ref[0])
        for h in range(nh):
            kaug_ref[h] = augment(head_lanes(k_ref[0].astype(F32), h), parts, head0 + h, first=False)
            v_h = jnp.where(lane_t < d, head_lanes(v_ref[0].astype(F32), h), jnp.where(lane_t == d, 1.0, 0.0))
            vaug_ref[h] = v_h.astype(BF16)

    row0 = pl.multiple_of(qi * tq, tq)
    q = q_ref[0].astype(F32) * (d ** -0.5)
    parts_q = pack_parts(c_ref[0, pl.ds(row0, tq), :])
    qs = [augment(head_lanes(q, h), parts_q, head0 + h, first=True) for h in range(nh)]
    causal = _tri(tq)
    lag = 2

    def step(j, carry, diagonal):
        start = pl.multiple_of(j * tq, tq)
        ss, new = {}, []
        for h in range(nh + lag):
            if h < nh:
                ss[h] = lax.dot_general(qs[h], kaug_ref[h, pl.ds(start, tq), :], (((1,), (1,)), ((), ())),
                                        preferred_element_type=F32)
            g = h - lag
            if g >= 0:
                m, acc = carry[g]
                s = ss.pop(g)
                s = jnp.where(causal, s, -jnp.inf) if diagonal else s
                m_new = jnp.maximum(m, jnp.max(s, axis=-1, keepdims=True))
                p = jnp.exp(s - m_new).astype(BF16)
                pv = jnp.dot(p, vaug_ref[g, pl.ds(start, tq), :], preferred_element_type=F32)
                new.append((m_new, acc * jnp.exp(m - m_new) + pv))
        return tuple(new)

    init = tuple((jnp.full((tq, 1), -jnp.inf, F32), jnp.zeros((tq, LANE), F32)) for _ in range(nh))
    carry = lax.fori_loop(0, qi, lambda j, cr: step(j, cr, False), init)
    carry = step(qi, carry, True)
    lane = lax.broadcasted_iota(jnp.int32, (tq, LANE), 1)
    for grp in range(groups):
        out = None
        for hh in range(per):
            acc = carry[grp * per + hh][1]
            o_h = acc / jnp.sum(jnp.where(lane == d, acc, 0.0), axis=-1, keepdims=True)
            o_h = o_h if hh == 0 else pltpu.roll(o_h, hh * d, axis=1)
            out = o_h if out is None else jnp.where(lane >= hh * d, o_h, out)
        o_ref[0, :, grp * LANE:(grp + 1) * LANE] = out.astype(o_ref.dtype)


def _fox(q, k, v, c_col, tq=256, groups=4):
    b, t, w = q.shape
    gw = groups * LANE
    return pl.pallas_call(
        functools.partial(_fox_kernel, tq=tq, groups=groups),
        grid=(b, w // gw, t // tq),
        in_specs=[pl.BlockSpec((1, tq, gw), lambda i, p, j: (i, j, p)),
                  pl.BlockSpec((1, t, gw), lambda i, p, j: (i, 0, p)),
                  pl.BlockSpec((1, t, gw), lambda i, p, j: (i, 0, p)),
                  pl.BlockSpec((1, t, LANE), lambda i, p, j: (i, 0, 0))],
        out_specs=pl.BlockSpec((1, tq, gw), lambda i, p, j: (i, j, p)),
        out_shape=jax.ShapeDtypeStruct((b, t, w), BF16),
        scratch_shapes=[pltpu.VMEM((groups * LANE // FOX_HD, t, LANE), BF16),
                        pltpu.VMEM((groups * LANE // FOX_HD, t, LANE), BF16)],
        compiler_params=_cparams("parallel", "arbitrary", "arbitrary"),
        name="fox_attention",
    )(q, k, v, c_col)


def _serpentine(i, f, nf):
    return jnp.where(i % 2 == 0, f, nf - 1 - f)


def _mix_residual(h_ref, ya_ref, yb_ref, wa_ref, wb_ref):
    return h_ref[...] + _dot(ya_ref[...], wa_ref[...]) + _dot(yb_ref[...], wb_ref[...])


def _ffn_kernel(h_ref, ya_ref, yb_ref, wa_ref, wb_ref, g_ref, w1_ref, w3_ref, w2_ref, o_ref, xn_ref, acc_ref):
    f = pl.program_id(1)

    @pl.when(f == 0)
    def _():
        hn = _mix_residual(h_ref, ya_ref, yb_ref, wa_ref, wb_ref)
        xn_ref[...] = _rmsnorm(hn, g_ref[...]).astype(BF16)
        acc_ref[...] = hn

    xn = xn_ref[...]
    mid = _silu(jnp.dot(xn, w1_ref[...], preferred_element_type=F32)) * jnp.dot(
        xn, w3_ref[...], preferred_element_type=F32)
    acc_ref[...] += _dot(mid, w2_ref[...])

    @pl.when(f == pl.num_programs(1) - 1)
    def _():
        o_ref[...] = acc_ref[...]


def _ffn(h, ya, yb, wa, wb, g, w1, w3, w2, tm=512, tf=1408):
    m, d = h.shape
    k = ya.shape[1]
    nf = w1.shape[1] // tf
    rows = lambda n: pl.BlockSpec((tm, n), lambda i, f: (i, 0))
    full = lambda shape: pl.BlockSpec(shape, lambda i, f: (0, 0))
    return pl.pallas_call(
        _ffn_kernel,
        grid=(m // tm, nf),
        in_specs=[rows(d), rows(k), rows(k), full((k, d)), full((k, d)), full((1, d)),
                  pl.BlockSpec((d, tf), lambda i, f: (0, _serpentine(i, f, nf))),
                  pl.BlockSpec((d, tf), lambda i, f: (0, _serpentine(i, f, nf))),
                  pl.BlockSpec((tf, d), lambda i, f: (_serpentine(i, f, nf), 0))],
        out_specs=rows(d),
        out_shape=jax.ShapeDtypeStruct((m, d), F32),
        scratch_shapes=[pltpu.VMEM((tm, d), BF16), pltpu.VMEM((tm, d), F32)],
        compiler_params=_cparams("parallel", "arbitrary"),
        name="ffn_dense",
    )(h, ya, yb, wa, wb, g.reshape(1, d), w1, w3, w2)


def _split2(x):
    hi = x.astype(BF16)
    return hi, (x - hi.astype(F32)).astype(BF16)


def _moe_route_kernel(h_ref, ya_ref, yb_ref, wa_ref, wb_ref, g_ref, r_ref, tri_ref, hn_ref, xn_ref, meta_ref,
                      cnt_ref, carry_ref):
    @pl.when(pl.program_id(0) == 0)
    def _():
        carry_ref[...] = jnp.zeros(carry_ref.shape, F32)

    hn = _mix_residual(h_ref, ya_ref, yb_ref, wa_ref, wb_ref)
    hn_ref[...] = hn
    xn = _rmsnorm(hn, g_ref[...])
    xn_ref[...] = xn
    tm = xn.shape[0]
    lane = lax.broadcasted_iota(jnp.int32, (tm, LANE), 1)
    x_hi, x_lo = _split2(xn)
    r_hi, r_lo = _split2(r_ref[...])
    dot = functools.partial(jnp.dot, preferred_element_type=F32)
    logits = jnp.where(lane < N_EXPERTS, dot(x_hi, r_hi) + (dot(x_hi, r_lo) + dot(x_lo, r_hi)), -jnp.inf)
    m1 = jnp.max(logits, axis=-1, keepdims=True)
    i1 = jnp.min(jnp.where(logits == m1, lane, LANE), axis=-1, keepdims=True)
    rest = jnp.where(lane == i1, -jnp.inf, logits)
    m2 = jnp.max(rest, axis=-1, keepdims=True)
    i2 = jnp.min(jnp.where(rest == m2, lane, LANE), axis=-1, keepdims=True)
    e2 = jnp.exp(m2 - m1)
    g1 = 1.0 / (1.0 + e2)
    g2 = e2 / (1.0 + e2)
    onehot = jnp.where(lane == i1, 1.0, jnp.where(lane == i2, 1.0, 0.0))
    before = dot(tri_ref[...], onehot.astype(BF16)) + carry_ref[0:1, :]
    r1 = jnp.sum(jnp.where(lane == i1, before, 0.0), axis=-1, keepdims=True)
    r2 = jnp.sum(jnp.where(lane == i2, before, 0.0), axis=-1, keepdims=True)
    meta = jnp.where(lane == 0, i1.astype(F32), jnp.where(lane == 1, i2.astype(F32), 0.0))
    meta = jnp.where(lane == 2, g1, jnp.where(lane == 3, g2, meta))
    meta_ref[...] = jnp.where(lane == 4, r1, jnp.where(lane == 5, r2, meta))
    total = carry_ref[0:1, :] + jnp.sum(onehot, axis=0, keepdims=True)
    carry_ref[...] = jnp.broadcast_to(total, carry_ref.shape)
    cnt_ref[...] = jnp.broadcast_to(total, cnt_ref.shape)


def _moe_route(h, ya, yb, wa, wb, g, router, tm=512):
    m, d = h.shape
    k = ya.shape[1]
    rows = lambda n: pl.BlockSpec((tm, n), lambda i: (i, 0))
    full = lambda shape: pl.BlockSpec(shape, lambda i: (0, 0))
    return pl.pallas_call(
        _moe_route_kernel,
        grid=(m // tm,),
        in_specs=[rows(d), rows(k), rows(k), full((k, d)), full((k, d)), full((1, d)), full((d, LANE)),
                  full((tm, tm))],
        out_specs=[rows(d), rows(d), rows(LANE), full((8, LANE))],
        out_shape=[jax.ShapeDtypeStruct((m, d), F32), jax.ShapeDtypeStruct((m, d), F32),
                   jax.ShapeDtypeStruct((m, LANE), F32), jax.ShapeDtypeStruct((8, LANE), F32)],
        scratch_shapes=[pltpu.VMEM((8, LANE), F32)],
        compiler_params=_cparams("arbitrary"),
        name="moe_route",
    )(h, ya, yb, wa, wb, g.reshape(1, d), router, jnp.tril(jnp.ones((tm, tm), BF16), -1))


def _row_copy(src_ref, src_group, src_sub, dst_ref, dst_group, dst_sub, sem):
    return pltpu.make_async_copy(src_ref.at[src_group, pl.ds(src_sub, 1)],
                                 dst_ref.at[dst_group, pl.ds(dst_sub, 1)], sem)


def _split_row(p):
    return lax.shift_right_logical(p, 3), lax.bitwise_and(p, _SUBLANES - 1)


def _moe_dispatch_kernel(pos_ref, xn_ref, xs_in_ref, xs_ref, sem, *, tm):
    del xs_in_ref

    def start(grp, carry):
        for u in range(_SUBLANES):
            for s in range(2):
                dst_group, dst_sub = _split_row(pos_ref[0, 0, 2 * (grp * _SUBLANES + u) + s])
                _row_copy(xn_ref, grp, u, xs_ref, dst_group, dst_sub, sem).start(priority=s)
        return carry

    def wait(grp, carry):
        for _ in range(2 * _SUBLANES):
            _row_copy(xn_ref, 0, 0, xs_ref, 0, 0, sem).wait()
        return carry

    lax.fori_loop(0, tm // _SUBLANES, start, 0)
    lax.fori_loop(0, tm // _SUBLANES, wait, 0)


def _moe_dispatch(xn, pos, zeros, tm=256):
    m, d = xn.shape
    n_rows = zeros.shape[0]
    xs = pl.pallas_call(
        functools.partial(_moe_dispatch_kernel, tm=tm),
        grid=(m // tm,),
        in_specs=[pl.BlockSpec((1, 1, 2 * tm), lambda i: (i, 0, 0), memory_space=pltpu.SMEM),
                  pl.BlockSpec((tm // _SUBLANES, _SUBLANES, d), lambda i: (i, 0, 0)),
                  pl.BlockSpec(memory_space=pl.ANY)],
        out_specs=pl.BlockSpec(memory_space=pl.ANY),
        out_shape=jax.ShapeDtypeStruct((n_rows // _SUBLANES, _SUBLANES, d), F32),
        scratch_shapes=[pltpu.SemaphoreType.DMA(())],
        input_output_aliases={2: 0},
        compiler_params=_cparams("arbitrary"),
        name="moe_dispatch",
    )(pos.reshape(m // tm, 1, 2 * tm), xn.reshape(m // _SUBLANES, _SUBLANES, d),
      zeros.reshape(n_rows // _SUBLANES, _SUBLANES, d))
    return xs.reshape(n_rows, d)


def _moe_expert_kernel(te_ref, nu_ref, x_ref, w1_ref, w3_ref, w2_ref, y_ref, xbf_ref, acc_ref):
    del te_ref
    f = pl.program_id(1)

    @pl.when(f == 0)
    def _():
        xbf_ref[...] = x_ref[...].astype(BF16)
        acc_ref[...] = jnp.zeros(acc_ref.shape, F32)

    @pl.when(pl.program_id(0) < nu_ref[0])
    def _():
        xb = xbf_ref[...]
        mid = _silu(jnp.dot(xb, w1_ref[0], preferred_element_type=F32)) * jnp.dot(
            xb, w3_ref[0], preferred_element_type=F32)
        acc_ref[...] += _dot(mid, w2_ref[0])

    @pl.when(f == pl.num_programs(1) - 1)
    def _():
        y_ref[...] = acc_ref[...]


def _moe_experts(xs, tile_expert, n_used, w1, w3, w2, tmx, tf=1792):
    n_rows, d = xs.shape
    nf = w1.shape[2] // tf

    def fidx(i, f, nu):
        last = i >= nu[0]
        return _serpentine(jnp.where(last, nu[0] - 1, i), jnp.where(last, nf - 1, f), nf)

    grid_spec = pltpu.PrefetchScalarGridSpec(
        num_scalar_prefetch=2,
        grid=(n_rows // tmx, nf),
        in_specs=[pl.BlockSpec((tmx, d), lambda i, f, te, nu: (i, 0)),
                  pl.BlockSpec((1, d, tf), lambda i, f, te, nu: (te[i], 0, fidx(i, f, nu))),
                  pl.BlockSpec((1, d, tf), lambda i, f, te, nu: (te[i], 0, fidx(i, f, nu))),
                  pl.BlockSpec((1, tf, d), lambda i, f, te, nu: (te[i], fidx(i, f, nu), 0))],
        out_specs=pl.BlockSpec((tmx, d), lambda i, f, te, nu: (i, 0)),
        scratch_shapes=[pltpu.VMEM((tmx, d), BF16), pltpu.VMEM((tmx, d), F32)],
    )
    return pl.pallas_call(
        _moe_expert_kernel,
        grid_spec=grid_spec,
        out_shape=jax.ShapeDtypeStruct((n_rows, d), F32),
        compiler_params=_cparams("arbitrary", "arbitrary"),
        name="moe_experts",
    )(tile_expert, n_used, xs, w1, w3, w2)


def _moe_combine_kernel(pos_ref, posn_ref, h_ref, meta_ref, fg_ref, ys_ref, o_ref, ybuf_ref, sem, *, tm):
    i = pl.program_id(0)
    slot = lax.rem(i, 2)

    groups = tm // _SUBLANES

    def gather(p_ref, dst_slot):
        def start(grp, carry):
            for u in range(_SUBLANES):
                for s in range(2):
                    src_group, src_sub = _split_row(p_ref[0, 0, 2 * (grp * _SUBLANES + u) + s])
                    _row_copy(ys_ref, src_group, src_sub, ybuf_ref.at[dst_slot], s * groups + grp, u,
                              sem.at[dst_slot]).start(priority=s)
            return carry
        lax.fori_loop(0, groups, start, 0)

    @pl.when(i == 0)
    def _():
        gather(pos_ref, 0)

    @pl.when(i + 1 < pl.num_programs(0))
    def _():
        gather(posn_ref, 1 - slot)

    def wait(grp, carry):
        for _ in range(2 * _SUBLANES):
            _row_copy(ys_ref, 0, 0, ybuf_ref.at[slot], 0, 0, sem.at[slot]).wait()
        return carry

    lax.fori_loop(0, groups, wait, 0)
    lane = lax.broadcasted_iota(jnp.int32, (tm, LANE), 1)
    meta = meta_ref[...]
    g1 = jnp.sum(jnp.where(lane == 2, meta, 0.0), axis=-1, keepdims=True)
    g2 = jnp.sum(jnp.where(lane == 3, meta, 0.0), axis=-1, keepdims=True)
    y = ybuf_ref[slot].reshape(2 * tm, -1)
    o_ref[...] = _rmsnorm(h_ref[...] + g1 * y[:tm] + g2 * y[tm:], fg_ref[...])


def _moe_combine(h, meta, pos, ys, final_g, tm=256):
    m, d = h.shape
    nt = m // tm
    pos3 = pos.reshape(nt, 1, 2 * tm)
    smem = lambda imap: pl.BlockSpec((1, 1, 2 * tm), imap, memory_space=pltpu.SMEM)
    return pl.pallas_call(
        functools.partial(_moe_combine_kernel, tm=tm),
        grid=(nt,),
        in_specs=[smem(lambda i: (i, 0, 0)), smem(lambda i: (jnp.minimum(i + 1, nt - 1), 0, 0)),
                  pl.BlockSpec((tm, d), lambda i: (i, 0)), pl.BlockSpec((tm, LANE), lambda i: (i, 0)),
                  pl.BlockSpec((1, d), lambda i: (0, 0)), pl.BlockSpec(memory_space=pl.ANY)],
        out_specs=pl.BlockSpec((tm, d), lambda i: (i, 0)),
        out_shape=jax.ShapeDtypeStruct((m, d), F32),
        scratch_shapes=[pltpu.VMEM((2, 2 * tm // _SUBLANES, _SUBLANES, d), F32), pltpu.SemaphoreType.DMA((2,))],
        compiler_params=_cparams("arbitrary"),
        name="moe_combine",
    )(pos3, pos3, h, meta, final_g.reshape(1, d), ys.reshape(-1, _SUBLANES, d))


_MOE_TILE = 512


def _moe_sorted_rows(m):
    return ((2 * m) // _MOE_TILE + N_EXPERTS) * _MOE_TILE


def _moe_routed(h, ya, yb, wa, wb, g, router, w1, w3, w2, final_g, zeros):
    m, d = h.shape
    tmx = _MOE_TILE
    h, xn, meta, cnt = _moe_route(h, ya, yb, wa, wb, g, router)
    experts = meta[:, 0:2].astype(jnp.int32)
    rank = meta[:, 4:6].astype(jnp.int32)
    counts = cnt[0, :N_EXPERTS].astype(jnp.int32)
    padded = (counts + tmx - 1) // tmx * tmx
    ends = jnp.cumsum(padded)
    pos = (ends - padded)[experts] + rank
    n_tiles = zeros.shape[0] // tmx
    n_used = ends[-1] // tmx
    tile_start = jnp.minimum(jnp.arange(n_tiles, dtype=jnp.int32), n_used - 1) * tmx
    tile_expert = jnp.sum(tile_start[:, None] >= ends[None, :], axis=1).astype(jnp.int32)
    xs = _moe_dispatch(xn, pos, zeros)
    ys = _moe_experts(xs, tile_expert, n_used.reshape(1).astype(jnp.int32), w1, w3, w2, tmx)
    return _moe_combine(h, meta, pos, ys, final_g)


def _pad_cols(w, n):
    return jnp.pad(w, ((0, 0), (0, n - w.shape[1])))


def kernel(x, norm_mix_g, norm_ffn_g, w_in0, conv_w, conv_b, conv_ln_g, conv_ln_b, gla_w_a2, gla_b_a, gla_norm_g, w_out0, ffn_w1, ffn_w3, ffn_w2, w_in1, rwkv_mu, rwkv_w2, rwkv_w0, rwkv_a2, rwkv_a0, rwkv_g2, rwkv_k_k, rwkv_k_a, rwkv_r_k, rwkv_lnx_g, rwkv_lnx_b, fox_b_f, w_out1, moe_router, moe_w1, moe_w3, moe_w2, final_norm_g):
    b, t, d = x.shape
    m = b * t
    half = d // 2
    bf = lambda a: a.astype(BF16)
    h = x.reshape(m, d)

    y_conv, y_gla = _layer0_mix(x, norm_mix_g[0], w_in0, conv_w, conv_b, conv_ln_g, conv_ln_b,
                                gla_w_a2, gla_b_a, gla_norm_g)
    h = _ffn(h, y_conv.reshape(m, half), y_gla.reshape(m, half), bf(w_out0[:half]), bf(w_out0[half:]),
             norm_ffn_g[0], bf(ffn_w1), bf(ffn_w3), bf(ffn_w2))

    rc = _RWKV_COLS
    flat = lambda w: w.reshape(-1, w.shape[-1])
    y_rwkv, (e_w1, e_w3, e_w2), moe_zeros = _rwkv(
        h.reshape(b, t, d), norm_mix_g[1], bf(w_in1[:, :rc]), rwkv_mu, rwkv_w2, rwkv_w0, rwkv_a2, rwkv_a0,
        rwkv_g2, rwkv_k_k, rwkv_k_a, rwkv_r_k, rwkv_lnx_g, rwkv_lnx_b,
        casts=[flat(moe_w1), flat(moe_w3), flat(moe_w2)], zero_rows=_moe_sorted_rows(m))
    b_f = jnp.pad(fox_b_f, (0, LANE - FOX_HEADS)).reshape(1, LANE)
    fq, fk, fv, c_col = _fox_proj(
        h.reshape(b, t, d), norm_mix_g[1], bf(w_in1[:, rc:rc + half]), bf(w_in1[:, rc + half:rc + 2 * half]),
        bf(w_in1[:, rc + 2 * half:rc + 3 * half]), bf(_pad_cols(w_in1[:, rc + 3 * half:], LANE)), b_f)
    y_fox = _fox(fq, fk, fv, c_col)
    out = _moe_routed(h, y_rwkv.reshape(m, half), y_fox.reshape(m, half), bf(w_out1[:half]), bf(w_out1[half:]),
                      norm_ffn_g[1], _pad_cols(moe_router, LANE), e_w1.reshape(moe_w1.shape),
                      e_w3.reshape(moe_w3.shape), e_w2.reshape(moe_w2.shape), final_norm_g, moe_zeros)
    return out.reshape(b, t, d)
```

```python
import functools

import jax
import jax.numpy as jnp
from jax import lax
from jax.experimental import pallas as pl
from jax.experimental.pallas import tpu as pltpu

F32 = jnp.float32
BF16 = jnp.bfloat16

NORM_EPS = 1e-6
CONV_WIDTH = 31
CONV_LN_EPS = 1e-5
GLA_HEADS = 4
GLA_DK = 64
GLA_DV = 128
GLA_TAU = 16.0
GLA_CHUNK = 64
RWKV_HEADS = 8
RWKV_HD = 64
RWKV_CHUNK = 64
RWKV_DECAY_SCALE = 0.606531
RWKV_GN_EPS = 64e-5
RWKV_L2_EPS = 1e-12
FOX_HEADS = 8
FOX_HD = 64
N_EXPERTS = 8
LANE = 128
_SUBLANES = 8
VMEM_LIMIT = 56 * 1024 * 1024


def _cparams(*sem):
    return pltpu.CompilerParams(dimension_semantics=sem, vmem_limit_bytes=VMEM_LIMIT)


def _dot(a, b):
    return jnp.dot(a.astype(BF16), b.astype(BF16), preferred_element_type=F32)


def _dot_nt(a, b):
    return lax.dot_general(a.astype(BF16), b.astype(BF16), (((1,), (1,)), ((), ())),
                           preferred_element_type=F32)


def _dot_tn(a, b):
    return lax.dot_general(a.astype(BF16), b.astype(BF16), (((0,), (0,)), ((), ())),
                           preferred_element_type=F32)


def _split3(x):
    hi = x.astype(BF16).astype(F32)
    mid = (x - hi).astype(BF16).astype(F32)
    lo = (x - hi - mid).astype(BF16).astype(F32)
    return hi, mid, lo


def _dot_01(sel, x):
    hi, mid, lo = _split3(x)
    return (_dot(sel, lo) + _dot(sel, mid)) + _dot(sel, hi)


def _dot_tn_01(x, sel):
    hi, mid, lo = _split3(x)
    return (_dot_tn(lo, sel) + _dot_tn(mid, sel)) + _dot_tn(hi, sel)


def _sigmoid(x):
    return 1.0 / (1.0 + jnp.exp(-x))


def _silu(x):
    return x * _sigmoid(x)


def _log_sigmoid(x):
    return jnp.minimum(x, 0.0) - jnp.log(1.0 + jnp.exp(-jnp.abs(x)))


def _rmsnorm(x, g):
    return x * lax.rsqrt(jnp.mean(x * x, axis=-1, keepdims=True) + NORM_EPS) * g


def _tri(n, strict=False):
    r = lax.broadcasted_iota(jnp.int32, (n, n), 0)
    c = lax.broadcasted_iota(jnp.int32, (n, n), 1)
    return (r > c) if strict else (r >= c)


_CUM_ROWS = 256


def _fox_proj_kernel(h_ref, g_ref, wq_ref, wk_ref, wv_ref, wf_ref, bf_ref, q_ref, k_ref, v_ref, c_ref, carry_ref):
    @pl.when(pl.program_id(1) == 0)
    def _():
        carry_ref[...] = jnp.zeros(carry_ref.shape, F32)

    xn = _rmsnorm(h_ref[0], g_ref[...]).astype(BF16)
    for w_ref, o_ref in ((wq_ref, q_ref), (wk_ref, k_ref), (wv_ref, v_ref)):
        o_ref[0] = jnp.dot(xn, w_ref[...], preferred_element_type=F32).astype(o_ref.dtype)
    log_f = _log_sigmoid(jnp.dot(xn, wf_ref[...], preferred_element_type=F32) + bf_ref[...])
    tri = _tri(_CUM_ROWS).astype(F32)
    last = carry_ref[0:1, :]
    for r0 in range(0, xn.shape[0], _CUM_ROWS):
        cum = _dot_01(tri, log_f[r0:r0 + _CUM_ROWS]) + last
        c_ref[0, r0:r0 + _CUM_ROWS, :] = cum
        last = cum[_CUM_ROWS - 1:_CUM_ROWS, :]
    carry_ref[...] = jnp.broadcast_to(last, carry_ref.shape)


def _fox_proj(h, g, wq, wk, wv, wf, b_f, tt=512):
    b, t, d = h.shape
    w = wq.shape[1]
    full = lambda a: pl.BlockSpec(a.shape, lambda i, j: (0, 0))
    tile = lambda n: pl.BlockSpec((1, tt, n), lambda i, j: (i, j, 0))
    g2 = g.reshape(1, d)
    return pl.pallas_call(
        _fox_proj_kernel,
        grid=(b, t // tt),
        in_specs=[tile(d), full(g2), full(wq), full(wk), full(wv), full(wf), full(b_f)],
        out_specs=[tile(w), tile(w), tile(w), tile(LANE)],
        out_shape=[jax.ShapeDtypeStruct((b, t, w), BF16)] * 3 + [jax.ShapeDtypeStruct((b, t, LANE), F32)],
        scratch_shapes=[pltpu.VMEM((8, LANE), F32)],
        compiler_params=_cparams("parallel", "arbitrary"),
        name="fox_proj",
    )(h, g2, wq, wk, wv, wf, b_f)


_CONV_HALO = 32
_CONV_ROWS = 32
_GLA_SPAN = 256


def _conv_body(val, gate, w_ref, b_ref, lg_ref, lb_ref, o_ref, u_ref, sh_ref):
    tt, c = val.shape
    rows = _CONV_ROWS
    u_ref[_CONV_HALO:_CONV_HALO + tt, :] = val * _sigmoid(gate)
    span = sh_ref.shape[1]
    for s in range(1, _SUBLANES):
        sh_ref[s - 1] = u_ref[s:s + span, :]
    base = _CONV_HALO - (CONV_WIDTH - 1)
    for r0 in range(0, tt, rows):
        acc = jnp.zeros((rows // _SUBLANES, _SUBLANES, c), F32) + b_ref[...]
        for j in range(CONV_WIDTH):
            phase = (base + j) % _SUBLANES
            row = r0 + base + j - phase
            win = u_ref[row:row + rows, :] if phase == 0 else sh_ref[phase - 1, row:row + rows, :]
            acc = acc + win.reshape(rows // _SUBLANES, _SUBLANES, c) * w_ref[j]
        acc = acc.reshape(rows, c)
        mu = jnp.mean(acc, axis=-1, keepdims=True)
        cen = acc - mu
        var = jnp.mean(cen * cen, axis=-1, keepdims=True)
        y = cen * lax.rsqrt(var + CONV_LN_EPS) * lg_ref[...] + lb_ref[...]
        o_ref[0, r0:r0 + rows, :] = _silu(y).astype(o_ref.dtype)
    u_ref[0:_CONV_HALO, :] = u_ref[tt:tt + _CONV_HALO, :]


def _gla_body(q, k, v, go, alr, wa2_ref, ba_ref, ng_ref, o_ref, s_ref, row0):
    c = GLA_CHUNK
    nc = q.shape[0] // c
    tril = _tri(c)
    tril_f = tril.astype(F32)
    ones_cv = jnp.ones((c, GLA_DV), F32)
    units = [(ci, h) for ci in range(nc) for h in range(GLA_HEADS)]
    rows_of = lambda ci: slice(ci * c, (ci + 1) * c)
    keys_of = lambda h: slice(h * GLA_DK, (h + 1) * GLA_DK)
    vals_of = lambda h: slice(h * GLA_DV, (h + 1) * GLA_DV)

    z = _dot(alr, wa2_ref[...]) + ba_ref[...]
    log_a = _log_sigmoid(z) * (1.0 / GLA_TAU)
    q = q * (GLA_DK ** -0.5)
    q_decs, k_decs, k_ends, decays = [], [], [], []
    for ci in range(nc):
        la = log_a[rows_of(ci)]
        bcum = _dot_01(tril_f, la)
        b_last = bcum[c - 1:c, :]
        q_decs.append(q[rows_of(ci)] * jnp.exp(bcum))
        k_decs.append(k[rows_of(ci)] * jnp.exp(-bcum))
        k_ends.append(k[rows_of(ci)] * jnp.exp(b_last - bcum))
        decays.append(jnp.exp(_dot_tn_01(la, ones_cv)))

    vs = [v[rows_of(ci), vals_of(h)] for ci, h in units]
    attns = [jnp.where(tril, _dot_nt(q_decs[ci][:, keys_of(h)], k_decs[ci][:, keys_of(h)]), 0.0)
             for ci, h in units]
    kvs = [_dot_tn(k_ends[ci][:, keys_of(h)], vs[u]) for u, (ci, h) in enumerate(units)]
    intras = [_dot(attns[u], vs[u]) for u in range(len(units))]
    states = [s_ref[h] for h in range(GLA_HEADS)]
    prevs = []
    for u, (ci, h) in enumerate(units):
        prevs.append(states[h])
        states[h] = states[h] * decays[ci][keys_of(h)] + kvs[u]
    for h in range(GLA_HEADS):
        s_ref[h] = states[h]
    inters = [_dot(q_decs[ci][:, keys_of(h)], prevs[u]) for u, (ci, h) in enumerate(units)]
    for ci in range(nc):
        outs = []
        for h in range(GLA_HEADS):
            u = ci * GLA_HEADS + h
            o = intras[u] + inters[u]
            outs.append(o * lax.rsqrt(jnp.mean(o * o, axis=-1, keepdims=True) + NORM_EPS))
        o_all = jnp.concatenate(outs, axis=-1)
        o_ref[0, row0 + ci * c:row0 + (ci + 1) * c, :] = (
            o_all * ng_ref[...] * _silu(go[rows_of(ci)])).astype(o_ref.dtype)


def _layer0_kernel(h_ref, g_ref, wcv_ref, wcg_ref, wq_ref, wk_ref, wv_ref, wgo_ref, walr_ref,
                   cw_ref, cb_ref, clg_ref, clb_ref, wa2_ref, ba_ref, ng_ref,
                   yc_ref, yg_ref, u_ref, sh_ref, s_ref):
    @pl.when(pl.program_id(1) == 0)
    def _():
        u_ref[0:_CONV_HALO, :] = jnp.zeros((_CONV_HALO, u_ref.shape[1]), F32)
        s_ref[...] = jnp.zeros(s_ref.shape, F32)

    xn = _rmsnorm(h_ref[0], g_ref[...]).astype(BF16)
    proj = lambda w_ref, rows=slice(None): jnp.dot(xn[rows], w_ref[...], preferred_element_type=F32)
    _conv_body(proj(wcv_ref), proj(wcg_ref), cw_ref, cb_ref, clg_ref, clb_ref, yc_ref, u_ref, sh_ref)
    for row0 in range(0, xn.shape[0], _GLA_SPAN):
        rows = slice(row0, row0 + _GLA_SPAN)
        _gla_body(proj(wq_ref, rows), proj(wk_ref, rows), proj(wv_ref, rows), proj(wgo_ref, rows),
                  proj(walr_ref, rows), wa2_ref, ba_ref, ng_ref, yg_ref, s_ref, row0)


def _layer0_mix(x, g, w_in, conv_w, conv_b, ln_g, ln_b, w_a2, b_a, norm_g, tt=512):
    b, t, d = x.shape
    c = d // 2
    gk = GLA_HEADS * GLA_DK
    o = [0, c, 2 * c, 2 * c + gk, 2 * c + 2 * gk, 3 * c + 2 * gk, 4 * c + 2 * gk]
    ws = [w_in[:, o[i]:o[i + 1]].astype(BF16) for i in range(6)] + [_pad_cols(w_in[:, o[6]:], LANE).astype(BF16)]
    wrep = jnp.broadcast_to(conv_w[:, None, :], (CONV_WIDTH, _SUBLANES, c))
    w_a2p = jnp.pad(w_a2, ((0, LANE - w_a2.shape[0]), (0, 0)))
    vec = lambda a: a.reshape(1, -1)
    full = lambda a: pl.BlockSpec(a.shape, lambda i, j: (0,) * a.ndim)
    tile = lambda n: pl.BlockSpec((1, tt, n), lambda i, j: (i, j, 0))
    consts = ws + [wrep, vec(conv_b), vec(ln_g), vec(ln_b), w_a2p, vec(b_a), vec(norm_g)]
    return pl.pallas_call(
        _layer0_kernel,
        grid=(b, t // tt),
        in_specs=[tile(d), full(vec(g))] + [full(a) for a in consts],
        out_specs=[tile(c), tile(c)],
        out_shape=[jax.ShapeDtypeStruct((b, t, c), BF16), jax.ShapeDtypeStruct((b, t, c), BF16)],
        scratch_shapes=[pltpu.VMEM((tt + _CONV_HALO, c), F32),
                        pltpu.VMEM((_SUBLANES - 1, tt + _CONV_HALO - _SUBLANES, c), F32),
                        pltpu.VMEM((GLA_HEADS, GLA_DK, GLA_DV), F32)],
        compiler_params=_cparams("parallel", "arbitrary"),
        name="layer0_mix",
    )(x, vec(g), *consts)


_RWKV_COLS = 3 * 512 + 64 + 64 + 128


def _rwkv_kernel(h_ref, ng_ref, win_ref, mu_ref, w2_ref, w0_ref, a2_ref, a0_ref, g2_ref, kk_ref, ka_ref, rk_ref,
                 lg_ref, lb_ref, *refs, tt, n_cast):
    cast_in, (o_ref, *cast_out, zero_ref), (xs_ref, st_ref) = (
        refs[:n_cast], refs[n_cast:2 * n_cast + 2], refs[2 * n_cast + 2:])
    c = RWKV_CHUNK
    hd = RWKV_HD
    w = RWKV_HEADS * hd
    for src, dst in zip(cast_in, cast_out):
        dst[...] = src[...].astype(BF16)
    zero_ref[...] = jnp.zeros(zero_ref.shape, F32)

    @pl.when(pl.program_id(1) == 0)
    def _():
        xs_ref[0:8, :] = jnp.zeros((8, xs_ref.shape[1]), F32)
        st_ref[...] = jnp.zeros(st_ref.shape, F32)

    x = jnp.dot(_rmsnorm(h_ref[0], ng_ref[...]).astype(BF16), win_ref[...], preferred_element_type=F32)
    xs_ref[8:8 + tt, :] = x
    prev = xs_ref[7:7 + tt, :]
    xs_ref[7:8, :] = x[tt - 1:tt, :]
    x = x + (prev - x) * mu_ref[...]

    r = x[:, 0:w]
    k = x[:, w:2 * w]
    v = x[:, 2 * w:3 * w]
    xw = x[:, 3 * w:3 * w + 64]
    xa = x[:, 3 * w + 64:3 * w + 128]
    xg = x[:, 3 * w + 128:3 * w + 256]
    log_w = -RWKV_DECAY_SCALE * _sigmoid(w0_ref[...] + _dot(jnp.tanh(xw), w2_ref[...]))
    a = _sigmoid(a0_ref[...] + _dot(xa, a2_ref[...]))
    g = _dot(_sigmoid(xg), g2_ref[...])
    kk = k * kk_ref[...]
    k_mod = k * (1.0 + (a - 1.0) * ka_ref[...])
    rkr = r * k_mod * rk_ref[...]

    tril = _tri(c)
    tril_s = _tri(c, strict=True)
    tril_f = tril.astype(F32)
    eye = (lax.broadcasted_iota(jnp.int32, (c, c), 0)
           == lax.broadcasted_iota(jnp.int32, (c, c), 1)).astype(F32)
    zeros_cc = jnp.zeros((c, hd), F32)

    nc = tt // c
    units = [(ci, h) for ci in range(nc) for h in range(RWKV_HEADS)]
    rows_of = lambda ci: slice(ci * c, (ci + 1) * c)
    lanes_of = lambda h: slice(h * hd, (h + 1) * hd)

    exps = []
    for ci in range(nc):
        lw_c = log_w[rows_of(ci)]
        lcum = _dot_01(tril_f, lw_c)
        l_last = lcum[c - 1:c, :]
        exps.append((jnp.exp(lcum), jnp.exp(-lcum), jnp.exp(l_last - lcum), jnp.exp(lcum - lw_c),
                     jnp.exp(l_last)))

    aqs, rqs, vs, lhss, rhss, bkes = [], [], [], [], [], []
    for ci, h in units:
        rows, hs = rows_of(ci), lanes_of(h)
        e_pos, e_neg, e_end, e_prev, _ = exps[ci]
        kk_h = kk[rows, hs]
        nrm = jnp.sqrt(jnp.sum(kk_h * kk_h, axis=-1, keepdims=True))
        kk_h = kk_h / jnp.maximum(nrm, RWKV_L2_EPS)
        k_h = k_mod[rows, hs]
        kka = kk_h * a[rows, hs]
        aq = -kk_h * e_prev[:, hs]
        rq = r[rows, hs] * e_pos[:, hs]
        aqs.append(aq)
        rqs.append(rq)
        vs.append(v[rows, hs])
        lhss.append(jnp.concatenate([aq, rq], axis=0))
        rhss.append(jnp.concatenate([kka * e_neg[:, hs], k_h * e_neg[:, hs]], axis=0))
        bkes.append(jnp.concatenate([kka * e_end[:, hs], k_h * e_end[:, hs]], axis=0))

    xss = [_dot_nt(lhs, rhs) for lhs, rhs in zip(lhss, rhss)]
    m1s = [jnp.where(tril_s, xs[:c, :c], 0.0) for xs in xss]
    m2s = [jnp.where(tril_s, xs[:c, c:], 0.0) for xs in xss]
    n12s = [jnp.concatenate([jnp.where(tril, xs[c:, :c], 0.0), jnp.where(tril, xs[c:, c:], 0.0)], axis=1)
            for xs in xss]
    tinvs = [eye + m1 for m1 in m1s]
    mps = m1s
    for _ in range(5):
        mps = [_dot(mp, mp) for mp in mps]
        tinvs = [tinv + _dot(tinv, mp) for tinv, mp in zip(tinvs, mps)]
    mvs = [_dot(m2, v_h) for m2, v_h in zip(m2s, vs)]
    pqs = [_dot(tinv, jnp.concatenate([mv, aq], axis=1)) for tinv, mv, aq in zip(tinvs, mvs, aqs)]
    pqvs = [jnp.concatenate([pq, jnp.concatenate([v_h, zeros_cc], axis=1)], axis=0)
            for pq, v_h in zip(pqs, vs)]
    yys = [_dot(n12, pqv) for n12, pqv in zip(n12s, pqvs)]
    ghs = [_dot_tn(bke, pqv) for bke, pqv in zip(bkes, pqvs)]

    states = [st_ref[h] for h in range(RWKV_HEADS)]
    ys = []
    for u, (ci, h) in enumerate(units):
        yq = rqs[u] + yys[u][:, hd:]
        gmat = eye * exps[ci][4][:, lanes_of(h)] + ghs[u][:, hd:]
        res = _dot(jnp.concatenate([yq, gmat], axis=0), states[h])
        ys.append(res[:c] + yys[u][:, :hd])
        states[h] = res[c:] + ghs[u][:, :hd]
    for h in range(RWKV_HEADS):
        st_ref[h] = states[h]

    out_rows = []
    for ci in range(nc):
        yns, bonuses = [], []
        for h in range(RWKV_HEADS):
            u = ci * RWKV_HEADS + h
            y = ys[u]
            cen = y - jnp.mean(y, axis=-1, keepdims=True)
            var = jnp.mean(cen * cen, axis=-1, keepdims=True)
            yns.append(cen * lax.rsqrt(var + RWKV_GN_EPS))
            bonuses.append(jnp.sum(rkr[rows_of(ci), lanes_of(h)], axis=-1, keepdims=True) * vs[u])
        yn = jnp.concatenate(yns, axis=1)
        bonus = jnp.concatenate(bonuses, axis=1)
        out_rows.append((yn * lg_ref[...] + lb_ref[...] + bonus) * g[rows_of(ci)])
    o_ref[0] = jnp.concatenate(out_rows, axis=0).astype(o_ref.dtype)


def _rwkv(h, norm_g, w_in, mu, w_w2, w0, w_a2, a0, w_g2, k_k, k_a, r_k, lnx_g, lnx_b, casts, zero_rows, tt=128):
    b, t, d = h.shape
    cols = w_in.shape[1]
    w = RWKV_HEADS * RWKV_HD
    nt = t // tt
    steps = b * nt
    vec = lambda a_: a_.reshape(1, -1)
    full = lambda shape: pl.BlockSpec(shape, lambda i, j: (0, 0))
    piece = lambda a_rows, a_cols: pl.BlockSpec((a_rows // steps, a_cols), lambda i, j: (i * nt + j, 0))
    cast_specs = [piece(*a.shape) for a in casts]
    outs = pl.pallas_call(
        functools.partial(_rwkv_kernel, tt=tt, n_cast=len(casts)),
        grid=(b, nt),
        in_specs=[pl.BlockSpec((1, tt, d), lambda i, j: (i, j, 0)), full((1, d)), full((d, cols)), full((1, cols)),
                  full(w_w2.shape), full((1, w)), full(w_a2.shape), full((1, w)), full(w_g2.shape),
                  full((1, w)), full((1, w)), full((1, w)), full((1, w)), full((1, w))] + cast_specs,
        out_specs=[pl.BlockSpec((1, tt, w), lambda i, j: (i, j, 0))] + cast_specs + [piece(zero_rows, d)],
        out_shape=[jax.ShapeDtypeStruct((b, t, w), BF16)] + [jax.ShapeDtypeStruct(a.shape, BF16) for a in casts]
        + [jax.ShapeDtypeStruct((zero_rows, d), F32)],
        scratch_shapes=[pltpu.VMEM((8 + tt, cols), F32),
                        pltpu.VMEM((RWKV_HEADS, RWKV_HD, RWKV_HD), F32)],
        compiler_params=_cparams("parallel", "arbitrary"),
        name="rwkv7",
    )(h, vec(norm_g), w_in, vec(mu), w_w2, vec(w0), w_a2, vec(a0), w_g2, vec(k_k), vec(k_a), vec(r_k),
      vec(lnx_g), vec(lnx_b), *casts)
    return outs[0], outs[1:-1], outs[-1]


def _fox_kernel(q_ref, k_ref, v_ref, c_ref, o_ref, kaug_ref, vaug_ref, *, tq, groups):
    qi = pl.program_id(2)
    d = FOX_HD
    per = LANE // d
    nh = groups * per
    head0 = pl.program_id(1) * nh

    def pack_parts(c_all):
        lane = lax.broadcasted_iota(jnp.int32, c_all.shape, 1)
        hi, mid, lo = _split3(c_all)
        packed = jnp.where(lane < FOX_HEADS, hi, jnp.where(
            lane < 2 * FOX_HEADS, pltpu.roll(mid, FOX_HEADS, axis=1), jnp.where(
                lane < 3 * FOX_HEADS, pltpu.roll(lo, 2 * FOX_HEADS, axis=1), 0.0)))
        return packed.astype(BF16)

    def augment(x_h, parts, head, first):
        r = lax.broadcasted_iota(jnp.int32, (LANE, LANE), 0)
        col = lax.broadcasted_iota(jnp.int32, (LANE, LANE), 1)
        base = d if first else d + 3
        hit = (r - head == (col - base) * FOX_HEADS) & (col >= base) & (col < base + 3)
        sel = jnp.where(hit, 1.0 if first else -1.0, 0.0).astype(BF16)
        lane = lax.broadcasted_iota(jnp.int32, x_h.shape, 1)
        ones_lo = d + 3 if first else d
        fixed = jnp.where(lane < d, x_h, jnp.where((lane >= ones_lo) & (lane < ones_lo + 3), 1.0, 0.0))
        return (fixed + jnp.dot(parts, sel, preferred_element_type=F32)).astype(BF16)

    def head_lanes(x, h):
        x_g = x[:, (h // per) * LANE:(h // per + 1) * LANE]
        return x_g if h % per == 0 else pltpu.roll(x_g, LANE - (h % per) * d, axis=1)

    @pl.when(qi == 0)
    def _():
        lane_t = lax.broadcasted_iota(jnp.int32, (k_ref.shape[1], LANE), 1)
        parts = pack_parts(c_ref[0])
        for h in range(nh):
            kaug_ref[h] = augment(head_lanes(k_ref[0].astype(F32), h), parts, head0 + h, first=False)
            v_h = jnp.where(lane_t < d, head_lanes(v_ref[0].astype(F32), h), jnp.where(lane_t == d, 1.0, 0.0))
            vaug_ref[h] = v_h.astype(BF16)

    row0 = pl.multiple_of(qi * tq, tq)
    q = q_ref[0].astype(F32) * (d ** -0.5)
    parts_q = pack_parts(c_ref[0, pl.ds(row0, tq), :])
    qs = [augment(head_lanes(q, h), parts_q, head0 + h, first=True) for h in range(nh)]
    causal = _tri(tq)
    lag = 2

    def step(j, carry, diagonal):
        start = pl.multiple_of(j * tq, tq)
        ss, new = {}, []
        for h in range(nh + lag):
            if h < nh:
                ss[h] = lax.dot_general(qs[h], kaug_ref[h, pl.ds(start, tq), :], (((1,), (1,)), ((), ())),
                                        preferred_element_type=F32)
            g = h - lag
            if g >= 0:
                m, acc = carry[g]
                s = ss.pop(g)
                s = jnp.where(causal, s, -jnp.inf) if diagonal else s
                m_new = jnp.maximum(m, jnp.max(s, axis=-1, keepdims=True))
                p = jnp.exp(s - m_new).astype(BF16)
                pv = jnp.dot(p, vaug_ref[g, pl.ds(start, tq), :], preferred_element_type=F32)
                new.append((m_new, acc * jnp.exp(m - m_new) + pv))
        return tuple(new)

    init = tuple((jnp.full((tq, 1), -jnp.inf, F32), jnp.zeros((tq, LANE), F32)) for _ in range(nh))
    carry = lax.fori_loop(0, qi, lambda j, cr: step(j, cr, False), init)
    carry = step(qi, carry, True)
    lane = lax.broadcasted_iota(jnp.int32, (tq, LANE), 1)
    for grp in range(groups):
        out = None
        for hh in range(per):
            acc = carry[grp * per + hh][1]
            o_h = acc / jnp.sum(jnp.where(lane == d, acc, 0.0), axis=-1, keepdims=True)
            o_h = o_h if hh == 0 else pltpu.roll(o_h, hh * d, axis=1)
            out = o_h if out is None else jnp.where(lane >= hh * d, o_h, out)
        o_ref[0, :, grp * LANE:(grp + 1) * LANE] = out.astype(o_ref.dtype)


def _fox(q, k, v, c_col, tq=256, groups=4):
    b, t, w = q.shape
    gw = groups * LANE
    return pl.pallas_call(
        functools.partial(_fox_kernel, tq=tq, groups=groups),
        grid=(b, w // gw, t // tq),
        in_specs=[pl.BlockSpec((1, tq, gw), lambda i, p, j: (i, j, p)),
                  pl.BlockSpec((1, t, gw), lambda i, p, j: (i, 0, p)),
                  pl.BlockSpec((1, t, gw), lambda i, p, j: (i, 0, p)),
                  pl.BlockSpec((1, t, LANE), lambda i, p, j: (i, 0, 0))],
        out_specs=pl.BlockSpec((1, tq, gw), lambda i, p, j: (i, j, p)),
        out_shape=jax.ShapeDtypeStruct((b, t, w), BF16),
        scratch_shapes=[pltpu.VMEM((groups * LANE // FOX_HD, t, LANE), BF16),
                        pltpu.VMEM((groups * LANE // FOX_HD, t, LANE), BF16)],
        compiler_params=_cparams("parallel", "arbitrary", "arbitrary"),
        name="fox_attention",
    )(q, k, v, c_col)


def _serpentine(i, f, nf):
    return jnp.where(i % 2 == 0, f, nf - 1 - f)


def _mix_residual(h_ref, ya_ref, yb_ref, wa_ref, wb_ref):
    return h_ref[...] + _dot(ya_ref[...], wa_ref[...]) + _dot(yb_ref[...], wb_ref[...])


def _ffn_kernel(h_ref, ya_ref, yb_ref, wa_ref, wb_ref, g_ref, w1_ref, w3_ref, w2_ref, o_ref, xn_ref, acc_ref):
    f = pl.program_id(1)

    @pl.when(f == 0)
    def _():
        hn = _mix_residual(h_ref, ya_ref, yb_ref, wa_ref, wb_ref)
        xn_ref[...] = _rmsnorm(hn, g_ref[...]).astype(BF16)
        acc_ref[...] = hn

    xn = xn_ref[...]
    mid = _silu(jnp.dot(xn, w1_ref[...], preferred_element_type=F32)) * jnp.dot(
        xn, w3_ref[...], preferred_element_type=F32)
    acc_ref[...] += _dot(mid, w2_ref[...])

    @pl.when(f == pl.num_programs(1) - 1)
    def _():
        o_ref[...] = acc_ref[...]


def _ffn(h, ya, yb, wa, wb, g, w1, w3, w2, tm=512, tf=1408):
    m, d = h.shape
    k = ya.shape[1]
    nf = w1.shape[1] // tf
    rows = lambda n: pl.BlockSpec((tm, n), lambda i, f: (i, 0))
    full = lambda shape: pl.BlockSpec(shape, lambda i, f: (0, 0))
    return pl.pallas_call(
        _ffn_kernel,
        grid=(m // tm, nf),
        in_specs=[rows(d), rows(k), rows(k), full((k, d)), full((k, d)), full((1, d)),
                  pl.BlockSpec((d, tf), lambda i, f: (0, _serpentine(i, f, nf))),
                  pl.BlockSpec((d, tf), lambda i, f: (0, _serpentine(i, f, nf))),
                  pl.BlockSpec((tf, d), lambda i, f: (_serpentine(i, f, nf), 0))],
        out_specs=rows(d),
        out_shape=jax.ShapeDtypeStruct((m, d), F32),
        scratch_shapes=[pltpu.VMEM((tm, d), BF16), pltpu.VMEM((tm, d), F32)],
        compiler_params=_cparams("parallel", "arbitrary"),
        name="ffn_dense",
    )(h, ya, yb, wa, wb, g.reshape(1, d), w1, w3, w2)


def _split2(x):
    hi = x.astype(BF16)
    return hi, (x - hi.astype(F32)).astype(BF16)


def _moe_route_kernel(h_ref, ya_ref, yb_ref, wa_ref, wb_ref, g_ref, r_ref, tri_ref, hn_ref, xn_ref, meta_ref,
                      cnt_ref, carry_ref):
    @pl.when(pl.program_id(0) == 0)
    def _():
        carry_ref[...] = jnp.zeros(carry_ref.shape, F32)

    hn = _mix_residual(h_ref, ya_ref, yb_ref, wa_ref, wb_ref)
    hn_ref[...] = hn
    xn = _rmsnorm(hn, g_ref[...])
    xn_ref[...] = xn
    tm = xn.shape[0]
    lane = lax.broadcasted_iota(jnp.int32, (tm, LANE), 1)
    x_hi, x_lo = _split2(xn)
    r_hi, r_lo = _split2(r_ref[...])
    dot = functools.partial(jnp.dot, preferred_element_type=F32)
    logits = jnp.where(lane < N_EXPERTS, dot(x_hi, r_hi) + (dot(x_hi, r_lo) + dot(x_lo, r_hi)), -jnp.inf)
    m1 = jnp.max(logits, axis=-1, keepdims=True)
    i1 = jnp.min(jnp.where(logits == m1, lane, LANE), axis=-1, keepdims=True)
    rest = jnp.where(lane == i1, -jnp.inf, logits)
    m2 = jnp.max(rest, axis=-1, keepdims=True)
    i2 = jnp.min(jnp.where(rest == m2, lane, LANE), axis=-1, keepdims=True)
    e2 = jnp.exp(m2 - m1)
    g1 = 1.0 / (1.0 + e2)
    g2 = e2 / (1.0 + e2)
    onehot = jnp.where(lane == i1, 1.0, jnp.where(lane == i2, 1.0, 0.0))
    before = dot(tri_ref[...], onehot.astype(BF16)) + carry_ref[0:1, :]
    r1 = jnp.sum(jnp.where(lane == i1, before, 0.0), axis=-1, keepdims=True)
    r2 = jnp.sum(jnp.where(lane == i2, before, 0.0), axis=-1, keepdims=True)
    meta = jnp.where(lane == 0, i1.astype(F32), jnp.where(lane == 1, i2.astype(F32), 0.0))
    meta = jnp.where(lane == 2, g1, jnp.where(lane == 3, g2, meta))
    meta_ref[...] = jnp.where(lane == 4, r1, jnp.where(lane == 5, r2, meta))
    total = carry_ref[0:1, :] + jnp.sum(onehot, axis=0, keepdims=True)
    carry_ref[...] = jnp.broadcast_to(total, carry_ref.shape)
    cnt_ref[...] = jnp.broadcast_to(total, cnt_ref.shape)


def _moe_route(h, ya, yb, wa, wb, g, router, tm=512):
    m, d = h.shape
    k = ya.shape[1]
    rows = lambda n: pl.BlockSpec((tm, n), lambda i: (i, 0))
    full = lambda shape: pl.BlockSpec(shape, lambda i: (0, 0))
    return pl.pallas_call(
        _moe_route_kernel,
        grid=(m // tm,),
        in_specs=[rows(d), rows(k), rows(k), full((k, d)), full((k, d)), full((1, d)), full((d, LANE)),
                  full((tm, tm))],
        out_specs=[rows(d), rows(d), rows(LANE), full((8, LANE))],
        out_shape=[jax.ShapeDtypeStruct((m, d), F32), jax.ShapeDtypeStruct((m, d), F32),
                   jax.ShapeDtypeStruct((m, LANE), F32), jax.ShapeDtypeStruct((8, LANE), F32)],
        scratch_shapes=[pltpu.VMEM((8, LANE), F32)],
        compiler_params=_cparams("arbitrary"),
        name="moe_route",
    )(h, ya, yb, wa, wb, g.reshape(1, d), router, jnp.tril(jnp.ones((tm, tm), BF16), -1))


def _row_copy(src_ref, src_group, src_sub, dst_ref, dst_group, dst_sub, sem):
    return pltpu.make_async_copy(src_ref.at[src_group, pl.ds(src_sub, 1)],
                                 dst_ref.at[dst_group, pl.ds(dst_sub, 1)], sem)


def _split_row(p):
    return lax.shift_right_logical(p, 3), lax.bitwise_and(p, _SUBLANES - 1)


def _moe_dispatch_kernel(pos_ref, xn_ref, xs_in_ref, xs_ref, sem, *, tm):
    del xs_in_ref

    def start(grp, carry):
        for u in range(_SUBLANES):
            for s in range(2):
                dst_group, dst_sub = _split_row(pos_ref[0, 0, 2 * (grp * _SUBLANES + u) + s])
                _row_copy(xn_ref, grp, u, xs_ref, dst_group, dst_sub, sem).start(priority=s)
        return carry

    def wait(grp, carry):
        for _ in range(2 * _SUBLANES):
            _row_copy(xn_ref, 0, 0, xs_ref, 0, 0, sem).wait()
        return carry

    lax.fori_loop(0, tm // _SUBLANES, start, 0)
    lax.fori_loop(0, tm // _SUBLANES, wait, 0)


def _moe_dispatch(xn, pos, zeros, tm=512):
    m, d = xn.shape
    n_rows = zeros.shape[0]
    xs = pl.pallas_call(
        functools.partial(_moe_dispatch_kernel, tm=tm),
        grid=(m // tm,),
        in_specs=[pl.BlockSpec((1, 1, 2 * tm), lambda i: (i, 0, 0), memory_space=pltpu.SMEM),
                  pl.BlockSpec((tm // _SUBLANES, _SUBLANES, d), lambda i: (i, 0, 0)),
                  pl.BlockSpec(memory_space=pl.ANY)],
        out_specs=pl.BlockSpec(memory_space=pl.ANY),
        out_shape=jax.ShapeDtypeStruct((n_rows // _SUBLANES, _SUBLANES, d), F32),
        scratch_shapes=[pltpu.SemaphoreType.DMA(())],
        input_output_aliases={2: 0},
        compiler_params=_cparams("arbitrary"),
        name="moe_dispatch",
    )(pos.reshape(m // tm, 1, 2 * tm), xn.reshape(m // _SUBLANES, _SUBLANES, d),
      zeros.reshape(n_rows // _SUBLANES, _SUBLANES, d))
    return xs.reshape(n_rows, d)


def _moe_expert_kernel(te_ref, nu_ref, x_ref, w1_ref, w3_ref, w2_ref, y_ref, xbf_ref, acc_ref):
    del te_ref
    f = pl.program_id(1)

    @pl.when(f == 0)
    def _():
        xbf_ref[...] = x_ref[...].astype(BF16)
        acc_ref[...] = jnp.zeros(acc_ref.shape, F32)

    @pl.when(pl.program_id(0) < nu_ref[0])
    def _():
        xb = xbf_ref[...]
        mid = _silu(jnp.dot(xb, w1_ref[0], preferred_element_type=F32)) * jnp.dot(
            xb, w3_ref[0], preferred_element_type=F32)
        acc_ref[...] += _dot(mid, w2_ref[0])

    @pl.when(f == pl.num_programs(1) - 1)
    def _():
        y_ref[...] = acc_ref[...]


def _moe_experts(xs, tile_expert, n_used, w1, w3, w2, tmx, tf=1792):
    n_rows, d = xs.shape
    nf = w1.shape[2] // tf

    def fidx(i, f, nu):
        last = i >= nu[0]
        return _serpentine(jnp.where(last, nu[0] - 1, i), jnp.where(last, nf - 1, f), nf)

    grid_spec = pltpu.PrefetchScalarGridSpec(
        num_scalar_prefetch=2,
        grid=(n_rows // tmx, nf),
        in_specs=[pl.BlockSpec((tmx, d), lambda i, f, te, nu: (i, 0)),
                  pl.BlockSpec((1, d, tf), lambda i, f, te, nu: (te[i], 0, fidx(i, f, nu))),
                  pl.BlockSpec((1, d, tf), lambda i, f, te, nu: (te[i], 0, fidx(i, f, nu))),
                  pl.BlockSpec((1, tf, d), lambda i, f, te, nu: (te[i], fidx(i, f, nu), 0))],
        out_specs=pl.BlockSpec((tmx, d), lambda i, f, te, nu: (i, 0)),
        scratch_shapes=[pltpu.VMEM((tmx, d), BF16), pltpu.VMEM((tmx, d), F32)],
    )
    return pl.pallas_call(
        _moe_expert_kernel,
        grid_spec=grid_spec,
        out_shape=jax.ShapeDtypeStruct((n_rows, d), F32),
        compiler_params=_cparams("arbitrary", "arbitrary"),
        name="moe_experts",
    )(tile_expert, n_used, xs, w1, w3, w2)


def _moe_combine_kernel(pos_ref, posn_ref, h_ref, meta_ref, fg_ref, ys_ref, o_ref, ybuf_ref, sem, *, tm):
    i = pl.program_id(0)
    slot = lax.rem(i, 2)

    groups = tm // _SUBLANES

    def gather(p_ref, dst_slot):
        def start(grp, carry):
            for u in range(_SUBLANES):
                for s in range(2):
                    src_group, src_sub = _split_row(p_ref[0, 0, 2 * (grp * _SUBLANES + u) + s])
                    _row_copy(ys_ref, src_group, src_sub, ybuf_ref.at[dst_slot], s * groups + grp, u,
                              sem.at[dst_slot]).start(priority=s)
            return carry
        lax.fori_loop(0, groups, start, 0)

    @pl.when(i == 0)
    def _():
        gather(pos_ref, 0)

    @pl.when(i + 1 < pl.num_programs(0))
    def _():
        gather(posn_ref, 1 - slot)

    def wait(grp, carry):
        for _ in range(2 * _SUBLANES):
            _row_copy(ys_ref, 0, 0, ybuf_ref.at[slot], 0, 0, sem.at[slot]).wait()
        return carry

    lax.fori_loop(0, groups, wait, 0)
    lane = lax.broadcasted_iota(jnp.int32, (tm, LANE), 1)
    meta = meta_ref[...]
    g1 = jnp.sum(jnp.where(lane == 2, meta, 0.0), axis=-1, keepdims=True)
    g2 = jnp.sum(jnp.where(lane == 3, meta, 0.0), axis=-1, keepdims=True)
    y = ybuf_ref[slot].reshape(2 * tm, -1)
    o_ref[...] = _rmsnorm(h_ref[...] + g1 * y[:tm] + g2 * y[tm:], fg_ref[...])


def _moe_combine(h, meta, pos, ys, final_g, tm=512):
    m, d = h.shape
    nt = m // tm
    pos3 = pos.reshape(nt, 1, 2 * tm)
    smem = lambda imap: pl.BlockSpec((1, 1, 2 * tm), imap, memory_space=pltpu.SMEM)
    return pl.pallas_call(
        functools.partial(_moe_combine_kernel, tm=tm),
        grid=(nt,),
        in_specs=[smem(lambda i: (i, 0, 0)), smem(lambda i: (jnp.minimum(i + 1, nt - 1), 0, 0)),
                  pl.BlockSpec((tm, d), lambda i: (i, 0)), pl.BlockSpec((tm, LANE), lambda i: (i, 0)),
                  pl.BlockSpec((1, d), lambda i: (0, 0)), pl.BlockSpec(memory_space=pl.ANY)],
        out_specs=pl.BlockSpec((tm, d), lambda i: (i, 0)),
        out_shape=jax.ShapeDtypeStruct((m, d), F32),
        scratch_shapes=[pltpu.VMEM((2, 2 * tm // _SUBLANES, _SUBLANES, d), F32), pltpu.SemaphoreType.DMA((2,))],
        compiler_params=_cparams("arbitrary"),
        name="moe_combine",
    )(pos3, pos3, h, meta, final_g.reshape(1, d), ys.reshape(-1, _SUBLANES, d))


_MOE_TILE = 512


def _moe_sorted_rows(m):
    return ((2 * m) // _MOE_TILE + N_EXPERTS) * _MOE_TILE


def _moe_routed(h, ya, yb, wa, wb, g, router, w1, w3, w2, final_g, zeros):
    m, d = h.shape
    tmx = _MOE_TILE
    h, xn, meta, cnt = _moe_route(h, ya, yb, wa, wb, g, router)
    experts = meta[:, 0:2].astype(jnp.int32)
    rank = meta[:, 4:6].astype(jnp.int32)
    counts = cnt[0, :N_EXPERTS].astype(jnp.int32)
    padded = (counts + tmx - 1) // tmx * tmx
    ends = jnp.cumsum(padded)
    pos = (ends - padded)[experts] + rank
    n_tiles = zeros.shape[0] // tmx
    n_used = ends[-1] // tmx
    tile_start = jnp.minimum(jnp.arange(n_tiles, dtype=jnp.int32), n_used - 1) * tmx
    tile_expert = jnp.sum(tile_start[:, None] >= ends[None, :], axis=1).astype(jnp.int32)
    xs = _moe_dispatch(xn, pos, zeros)
    ys = _moe_experts(xs, tile_expert, n_used.reshape(1).astype(jnp.int32), w1, w3, w2, tmx)
    return _moe_combine(h, meta, pos, ys, final_g)


def _pad_cols(w, n):
    return jnp.pad(w, ((0, 0), (0, n - w.shape[1])))


def kernel(x, norm_mix_g, norm_ffn_g, w_in0, conv_w, conv_b, conv_ln_g, conv_ln_b, gla_w_a2, gla_b_a, gla_norm_g, w_out0, ffn_w1, ffn_w3, ffn_w2, w_in1, rwkv_mu, rwkv_w2, rwkv_w0, rwkv_a2, rwkv_a0, rwkv_g2, rwkv_k_k, rwkv_k_a, rwkv_r_k, rwkv_lnx_g, rwkv_lnx_b, fox_b_f, w_out1, moe_router, moe_w1, moe_w3, moe_w2, final_norm_g):
    b, t, d = x.shape
    m = b * t
    half = d // 2
    bf = lambda a: a.astype(BF16)
    h = x.reshape(m, d)

    y_conv, y_gla = _layer0_mix(x, norm_mix_g[0], w_in0, conv_w, conv_b, conv_ln_g, conv_ln_b,
                                gla_w_a2, gla_b_a, gla_norm_g)
    h = _ffn(h, y_conv.reshape(m, half), y_gla.reshape(m, half), bf(w_out0[:half]), bf(w_out0[half:]),
             norm_ffn_g[0], bf(ffn_w1), bf(ffn_w3), bf(ffn_w2))

    rc = _RWKV_COLS
    flat = lambda w: w.reshape(-1, w.shape[-1])
    y_rwkv, (e_w1, e_w3, e_w2), moe_zeros = _rwkv(
        h.reshape(b, t, d), norm_mix_g[1], bf(w_in1[:, :rc]), rwkv_mu, rwkv_w2, rwkv_w0, rwkv_a2, rwkv_a0,
        rwkv_g2, rwkv_k_k, rwkv_k_a, rwkv_r_k, rwkv_lnx_g, rwkv_lnx_b,
        casts=[flat(moe_w1), flat(moe_w3), flat(moe_w2)], zero_rows=_moe_sorted_rows(m))
    b_f = jnp.pad(fox_b_f, (0, LANE - FOX_HEADS)).reshape(1, LANE)
    fq, fk, fv, c_col = _fox_proj(
        h.reshape(b, t, d), norm_mix_g[1], bf(w_in1[:, rc:rc + half]), bf(w_in1[:, rc + half:rc + 2 * half]),
        bf(w_in1[:, rc + 2 * half:rc + 3 * half]), bf(_pad_cols(w_in1[:, rc + 3 * half:], LANE)), b_f)
    y_fox = _fox(fq, fk, fv, c_col)
    out = _moe_routed(h, y_rwkv.reshape(m, half), y_fox.reshape(m, half), bf(w_out1[:half]), bf(w_out1[half:]),
                      norm_ffn_g[1], _pad_cols(moe_router, LANE), e_w1.reshape(moe_w1.shape),
                      e_w3.reshape(moe_w3.shape), e_w2.reshape(moe_w2.shape), final_norm_g, moe_zeros)
    return out.reshape(b, t, d)
```

```python
import functools

import jax
import jax.numpy as jnp
from jax import lax
from jax.experimental import pallas as pl
from jax.experimental.pallas import tpu as pltpu

F32 = jnp.float32
BF16 = jnp.bfloat16

NORM_EPS = 1e-6
CONV_WIDTH = 31
CONV_LN_EPS = 1e-5
GLA_HEADS = 4
GLA_DK = 64
GLA_DV = 128
GLA_TAU = 16.0
GLA_CHUNK = 64
RWKV_HEADS = 8
RWKV_HD = 64
RWKV_CHUNK = 64
RWKV_DECAY_SCALE = 0.606531
RWKV_GN_EPS = 64e-5
RWKV_L2_EPS = 1e-12
FOX_HEADS = 8
FOX_HD = 64
N_EXPERTS = 8
LANE = 128
_SUBLANES = 8
VMEM_LIMIT = 56 * 1024 * 1024


def _cparams(*sem):
    return pltpu.CompilerParams(dimension_semantics=sem, vmem_limit_bytes=VMEM_LIMIT)


def _dot(a, b):
    return jnp.dot(a.astype(BF16), b.astype(BF16), preferred_element_type=F32)


def _dot_nt(a, b):
    return lax.dot_general(a.astype(BF16), b.astype(BF16), (((1,), (1,)), ((), ())),
                           preferred_element_type=F32)


def _dot_tn(a, b):
    return lax.dot_general(a.astype(BF16), b.astype(BF16), (((0,), (0,)), ((), ())),
                           preferred_element_type=F32)


def _split3(x):
    hi = x.astype(BF16).astype(F32)
    mid = (x - hi).astype(BF16).astype(F32)
    lo = (x - hi - mid).astype(BF16).astype(F32)
    return hi, mid, lo


def _dot_01(sel, x):
    hi, mid, lo = _split3(x)
    return (_dot(sel, lo) + _dot(sel, mid)) + _dot(sel, hi)


def _dot_tn_01(x, sel):
    hi, mid, lo = _split3(x)
    return (_dot_tn(lo, sel) + _dot_tn(mid, sel)) + _dot_tn(hi, sel)


def _sigmoid(x):
    return 1.0 / (1.0 + jnp.exp(-x))


def _silu(x):
    return x * _sigmoid(x)


def _log_sigmoid(x):
    return jnp.minimum(x, 0.0) - jnp.log(1.0 + jnp.exp(-jnp.abs(x)))


def _rmsnorm(x, g):
    return x * lax.rsqrt(jnp.mean(x * x, axis=-1, keepdims=True) + NORM_EPS) * g


def _tri(n, strict=False):
    r = lax.broadcasted_iota(jnp.int32, (n, n), 0)
    c = lax.broadcasted_iota(jnp.int32, (n, n), 1)
    return (r > c) if strict else (r >= c)


_CUM_ROWS = 256


def _fox_proj_kernel(h_ref, g_ref, wq_ref, wk_ref, wv_ref, wf_ref, bf_ref, q_ref, k_ref, v_ref, c_ref, carry_ref):
    @pl.when(pl.program_id(1) == 0)
    def _():
        carry_ref[...] = jnp.zeros(carry_ref.shape, F32)

    xn = _rmsnorm(h_ref[0], g_ref[...]).astype(BF16)
    for w_ref, o_ref in ((wq_ref, q_ref), (wk_ref, k_ref), (wv_ref, v_ref)):
        o_ref[0] = jnp.dot(xn, w_ref[...], preferred_element_type=F32).astype(o_ref.dtype)
    log_f = _log_sigmoid(jnp.dot(xn, wf_ref[...], preferred_element_type=F32) + bf_ref[...])
    tri = _tri(_CUM_ROWS).astype(F32)
    last = carry_ref[0:1, :]
    for r0 in range(0, xn.shape[0], _CUM_ROWS):
        cum = _dot_01(tri, log_f[r0:r0 + _CUM_ROWS]) + last
        c_ref[0, r0:r0 + _CUM_ROWS, :] = cum
        last = cum[_CUM_ROWS - 1:_CUM_ROWS, :]
    carry_ref[...] = jnp.broadcast_to(last, carry_ref.shape)


def _fox_proj(h, g, wq, wk, wv, wf, b_f, tt=512):
    b, t, d = h.shape
    w = wq.shape[1]
    full = lambda a: pl.BlockSpec(a.shape, lambda i, j: (0, 0))
    tile = lambda n: pl.BlockSpec((1, tt, n), lambda i, j: (i, j, 0))
    g2 = g.reshape(1, d)
    return pl.pallas_call(
        _fox_proj_kernel,
        grid=(b, t // tt),
        in_specs=[tile(d), full(g2), full(wq), full(wk), full(wv), full(wf), full(b_f)],
        out_specs=[tile(w), tile(w), tile(w), tile(LANE)],
        out_shape=[jax.ShapeDtypeStruct((b, t, w), BF16)] * 3 + [jax.ShapeDtypeStruct((b, t, LANE), F32)],
        scratch_shapes=[pltpu.VMEM((8, LANE), F32)],
        compiler_params=_cparams("parallel", "arbitrary"),
        name="fox_proj",
    )(h, g2, wq, wk, wv, wf, b_f)


_CONV_HALO = 32
_CONV_ROWS = 32
_GLA_SPAN = 256


def _conv_body(val, gate, w_ref, b_ref, lg_ref, lb_ref, o_ref, u_ref, sh_ref):
    tt, c = val.shape
    rows = _CONV_ROWS
    u_ref[_CONV_HALO:_CONV_HALO + tt, :] = val * _sigmoid(gate)
    span = sh_ref.shape[1]
    for s in range(1, _SUBLANES):
        sh_ref[s - 1] = u_ref[s:s + span, :]
    base = _CONV_HALO - (CONV_WIDTH - 1)
    for r0 in range(0, tt, rows):
        acc = jnp.zeros((rows // _SUBLANES, _SUBLANES, c), F32) + b_ref[...]
        for j in range(CONV_WIDTH):
            phase = (base + j) % _SUBLANES
            row = r0 + base + j - phase
            win = u_ref[row:row + rows, :] if phase == 0 else sh_ref[phase - 1, row:row + rows, :]
            acc = acc + win.reshape(rows // _SUBLANES, _SUBLANES, c) * w_ref[j]
        acc = acc.reshape(rows, c)
        mu = jnp.mean(acc, axis=-1, keepdims=True)
        cen = acc - mu
        var = jnp.mean(cen * cen, axis=-1, keepdims=True)
        y = cen * lax.rsqrt(var + CONV_LN_EPS) * lg_ref[...] + lb_ref[...]
        o_ref[0, r0:r0 + rows, :] = _silu(y).astype(o_ref.dtype)
    u_ref[0:_CONV_HALO, :] = u_ref[tt:tt + _CONV_HALO, :]


def _gla_body(q, k, v, go, alr, wa2_ref, ba_ref, ng_ref, o_ref, s_ref, row0):
    c = GLA_CHUNK
    nc = q.shape[0] // c
    tril = _tri(c)
    tril_f = tril.astype(F32)
    ones_cv = jnp.ones((c, GLA_DV), F32)
    units = [(ci, h) for ci in range(nc) for h in range(GLA_HEADS)]
    rows_of = lambda ci: slice(ci * c, (ci + 1) * c)
    keys_of = lambda h: slice(h * GLA_DK, (h + 1) * GLA_DK)
    vals_of = lambda h: slice(h * GLA_DV, (h + 1) * GLA_DV)

    z = _dot(alr, wa2_ref[...]) + ba_ref[...]
    log_a = _log_sigmoid(z) * (1.0 / GLA_TAU)
    q = q * (GLA_DK ** -0.5)
    q_decs, k_decs, k_ends, decays = [], [], [], []
    for ci in range(nc):
        la = log_a[rows_of(ci)]
        bcum = _dot_01(tril_f, la)
        b_last = bcum[c - 1:c, :]
        q_decs.append(q[rows_of(ci)] * jnp.exp(bcum))
        k_decs.append(k[rows_of(ci)] * jnp.exp(-bcum))
        k_ends.append(k[rows_of(ci)] * jnp.exp(b_last - bcum))
        decays.append(jnp.exp(_dot_tn_01(la, ones_cv)))

    vs = [v[rows_of(ci), vals_of(h)] for ci, h in units]
    attns = [jnp.where(tril, _dot_nt(q_decs[ci][:, keys_of(h)], k_decs[ci][:, keys_of(h)]), 0.0)
             for ci, h in units]
    kvs = [_dot_tn(k_ends[ci][:, keys_of(h)], vs[u]) for u, (ci, h) in enumerate(units)]
    intras = [_dot(attns[u], vs[u]) for u in range(len(units))]
    states = [s_ref[h] for h in range(GLA_HEADS)]
    prevs = []
    for u, (ci, h) in enumerate(units):
        prevs.append(states[h])
        states[h] = states[h] * decays[ci][keys_of(h)] + kvs[u]
    for h in range(GLA_HEADS):
        s_ref[h] = states[h]
    inters = [_dot(q_decs[ci][:, keys_of(h)], prevs[u]) for u, (ci, h) in enumerate(units)]
    for ci in range(nc):
        outs = []
        for h in range(GLA_HEADS):
            u = ci * GLA_HEADS + h
            o = intras[u] + inters[u]
            outs.append(o * lax.rsqrt(jnp.mean(o * o, axis=-1, keepdims=True) + NORM_EPS))
        o_all = jnp.concatenate(outs, axis=-1)
        o_ref[0, row0 + ci * c:row0 + (ci + 1) * c, :] = (
            o_all * ng_ref[...] * _silu(go[rows_of(ci)])).astype(o_ref.dtype)


def _layer0_kernel(h_ref, g_ref, wcv_ref, wcg_ref, wq_ref, wk_ref, wv_ref, wgo_ref, walr_ref,
                   cw_ref, cb_ref, clg_ref, clb_ref, wa2_ref, ba_ref, ng_ref,
                   yc_ref, yg_ref, u_ref, sh_ref, s_ref):
    @pl.when(pl.program_id(1) == 0)
    def _():
        u_ref[0:_CONV_HALO, :] = jnp.zeros((_CONV_HALO, u_ref.shape[1]), F32)
        s_ref[...] = jnp.zeros(s_ref.shape, F32)

    xn = _rmsnorm(h_ref[0], g_ref[...]).astype(BF16)
    proj = lambda w_ref, rows=slice(None): jnp.dot(xn[rows], w_ref[...], preferred_element_type=F32)
    _conv_body(proj(wcv_ref), proj(wcg_ref), cw_ref, cb_ref, clg_ref, clb_ref, yc_ref, u_ref, sh_ref)
    for row0 in range(0, xn.shape[0], _GLA_SPAN):
        rows = slice(row0, row0 + _GLA_SPAN)
        _gla_body(proj(wq_ref, rows), proj(wk_ref, rows), proj(wv_ref, rows), proj(wgo_ref, rows),
                  proj(walr_ref, rows), wa2_ref, ba_ref, ng_ref, yg_ref, s_ref, row0)


def _layer0_mix(x, g, w_in, conv_w, conv_b, ln_g, ln_b, w_a2, b_a, norm_g, tt=512):
    b, t, d = x.shape
    c = d // 2
    gk = GLA_HEADS * GLA_DK
    o = [0, c, 2 * c, 2 * c + gk, 2 * c + 2 * gk, 3 * c + 2 * gk, 4 * c + 2 * gk]
    ws = [w_in[:, o[i]:o[i + 1]].astype(BF16) for i in range(6)] + [_pad_cols(w_in[:, o[6]:], LANE).astype(BF16)]
    wrep = jnp.broadcast_to(conv_w[:, None, :], (CONV_WIDTH, _SUBLANES, c))
    w_a2p = jnp.pad(w_a2, ((0, LANE - w_a2.shape[0]), (0, 0)))
    vec = lambda a: a.reshape(1, -1)
    full = lambda a: pl.BlockSpec(a.shape, lambda i, j: (0,) * a.ndim)
    tile = lambda n: pl.BlockSpec((1, tt, n), lambda i, j: (i, j, 0))
    consts = ws + [wrep, vec(conv_b), vec(ln_g), vec(ln_b), w_a2p, vec(b_a), vec(norm_g)]
    return pl.pallas_call(
        _layer0_kernel,
        grid=(b, t // tt),
        in_specs=[tile(d), full(vec(g))] + [full(a) for a in consts],
        out_specs=[tile(c), tile(c)],
        out_shape=[jax.ShapeDtypeStruct((b, t, c), BF16), jax.ShapeDtypeStruct((b, t, c), BF16)],
        scratch_shapes=[pltpu.VMEM((tt + _CONV_HALO, c), F32),
                        pltpu.VMEM((_SUBLANES - 1, tt + _CONV_HALO - _SUBLANES, c), F32),
                        pltpu.VMEM((GLA_HEADS, GLA_DK, GLA_DV), F32)],
        compiler_params=_cparams("parallel", "arbitrary"),
        name="layer0_mix",
    )(x, vec(g), *consts)


_RWKV_COLS = 3 * 512 + 64 + 64 + 128


def _rwkv_kernel(h_ref, ng_ref, win_ref, mu_ref, w2_ref, w0_ref, a2_ref, a0_ref, g2_ref, kk_ref, ka_ref, rk_ref,
                 lg_ref, lb_ref, *refs, tt, n_cast):
    cast_in, (o_ref, *cast_out, zero_ref), (xs_ref, st_ref) = (
        refs[:n_cast], refs[n_cast:2 * n_cast + 2], refs[2 * n_cast + 2:])
    c = RWKV_CHUNK
    hd = RWKV_HD
    w = RWKV_HEADS * hd
    for src, dst in zip(cast_in, cast_out):
        dst[...] = src[...].astype(BF16)
    zero_ref[...] = jnp.zeros(zero_ref.shape, F32)

    @pl.when(pl.program_id(1) == 0)
    def _():
        xs_ref[0:8, :] = jnp.zeros((8, xs_ref.shape[1]), F32)
        st_ref[...] = jnp.zeros(st_ref.shape, F32)

    x = jnp.dot(_rmsnorm(h_ref[0], ng_ref[...]).astype(BF16), win_ref[...], preferred_element_type=F32)
    xs_ref[8:8 + tt, :] = x
    prev = xs_ref[7:7 + tt, :]
    xs_ref[7:8, :] = x[tt - 1:tt, :]
    x = x + (prev - x) * mu_ref[...]

    r = x[:, 0:w]
    k = x[:, w:2 * w]
    v = x[:, 2 * w:3 * w]
    xw = x[:, 3 * w:3 * w + 64]
    xa = x[:, 3 * w + 64:3 * w + 128]
    xg = x[:, 3 * w + 128:3 * w + 256]
    log_w = -RWKV_DECAY_SCALE * _sigmoid(w0_ref[...] + _dot(jnp.tanh(xw), w2_ref[...]))
    a = _sigmoid(a0_ref[...] + _dot(xa, a2_ref[...]))
    g = _dot(_sigmoid(xg), g2_ref[...])
    kk = k * kk_ref[...]
    k_mod = k * (1.0 + (a - 1.0) * ka_ref[...])
    rkr = r * k_mod * rk_ref[...]

    tril = _tri(c)
    tril_s = _tri(c, strict=True)
    tril_f = tril.astype(F32)
    eye = (lax.broadcasted_iota(jnp.int32, (c, c), 0)
           == lax.broadcasted_iota(jnp.int32, (c, c), 1)).astype(F32)
    zeros_cc = jnp.zeros((c, hd), F32)

    nc = tt // c
    units = [(ci, h) for ci in range(nc) for h in range(RWKV_HEADS)]
    rows_of = lambda ci: slice(ci * c, (ci + 1) * c)
    lanes_of = lambda h: slice(h * hd, (h + 1) * hd)

    exps = []
    for ci in range(nc):
        lw_c = log_w[rows_of(ci)]
        lcum = _dot_01(tril_f, lw_c)
        l_last = lcum[c - 1:c, :]
        exps.append((jnp.exp(lcum), jnp.exp(-lcum), jnp.exp(l_last - lcum), jnp.exp(lcum - lw_c),
                     jnp.exp(l_last)))

    aqs, rqs, vs, lhss, rhss, bkes = [], [], [], [], [], []
    for ci, h in units:
        rows, hs = rows_of(ci), lanes_of(h)
        e_pos, e_neg, e_end, e_prev, _ = exps[ci]
        kk_h = kk[rows, hs]
        nrm = jnp.sqrt(jnp.sum(kk_h * kk_h, axis=-1, keepdims=True))
        kk_h = kk_h / jnp.maximum(nrm, RWKV_L2_EPS)
        k_h = k_mod[rows, hs]
        kka = kk_h * a[rows, hs]
        aq = -kk_h * e_prev[:, hs]
        rq = r[rows, hs] * e_pos[:, hs]
        aqs.append(aq)
        rqs.append(rq)
        vs.append(v[rows, hs])
        lhss.append(jnp.concatenate([aq, rq], axis=0))
        rhss.append(jnp.concatenate([kka * e_neg[:, hs], k_h * e_neg[:, hs]], axis=0))
        bkes.append(jnp.concatenate([kka * e_end[:, hs], k_h * e_end[:, hs]], axis=0))

    xss = [_dot_nt(lhs, rhs) for lhs, rhs in zip(lhss, rhss)]
    m1s = [jnp.where(tril_s, xs[:c, :c], 0.0) for xs in xss]
    m2s = [jnp.where(tril_s, xs[:c, c:], 0.0) for xs in xss]
    n12s = [jnp.concatenate([jnp.where(tril, xs[c:, :c], 0.0), jnp.where(tril, xs[c:, c:], 0.0)], axis=1)
            for xs in xss]
    tinvs = [eye + m1 for m1 in m1s]
    mps = m1s
    for _ in range(5):
        mps = [_dot(mp, mp) for mp in mps]
        tinvs = [tinv + _dot(tinv, mp) for tinv, mp in zip(tinvs, mps)]
    mvs = [_dot(m2, v_h) for m2, v_h in zip(m2s, vs)]
    pqs = [_dot(tinv, jnp.concatenate([mv, aq], axis=1)) for tinv, mv, aq in zip(tinvs, mvs, aqs)]
    pqvs = [jnp.concatenate([pq, jnp.concatenate([v_h, zeros_cc], axis=1)], axis=0)
            for pq, v_h in zip(pqs, vs)]
    yys = [_dot(n12, pqv) for n12, pqv in zip(n12s, pqvs)]
    ghs = [_dot_tn(bke, pqv) for bke, pqv in zip(bkes, pqvs)]

    states = [st_ref[h] for h in range(RWKV_HEADS)]
    ys = []
    for u, (ci, h) in enumerate(units):
        yq = rqs[u] + yys[u][:, hd:]
        gmat = eye * exps[ci][4][:, lanes_of(h)] + ghs[u][:, hd:]
        res = _dot(jnp.concatenate([yq, gmat], axis=0), states[h])
        ys.append(res[:c] + yys[u][:, :hd])
        states[h] = res[c:] + ghs[u][:, :hd]
    for h in range(RWKV_HEADS):
        st_ref[h] = states[h]

    out_rows = []
    for ci in range(nc):
        yns, bonuses = [], []
        for h in range(RWKV_HEADS):
            u = ci * RWKV_HEADS + h
            y = ys[u]
            cen = y - jnp.mean(y, axis=-1, keepdims=True)
            var = jnp.mean(cen * cen, axis=-1, keepdims=True)
            yns.append(cen * lax.rsqrt(var + RWKV_GN_EPS))
            bonuses.append(jnp.sum(rkr[rows_of(ci), lanes_of(h)], axis=-1, keepdims=True) * vs[u])
        yn = jnp.concatenate(yns, axis=1)
        bonus = jnp.concatenate(bonuses, axis=1)
        out_rows.append((yn * lg_ref[...] + lb_ref[...] + bonus) * g[rows_of(ci)])
    o_ref[0] = jnp.concatenate(out_rows, axis=0).astype(o_ref.dtype)


def _rwkv(h, norm_g, w_in, mu, w_w2, w0, w_a2, a0, w_g2, k_k, k_a, r_k, lnx_g, lnx_b, casts, zero_rows, tt=128):
    b, t, d = h.shape
    cols = w_in.shape[1]
    w = RWKV_HEADS * RWKV_HD
    nt = t // tt
    steps = b * nt
    vec = lambda a_: a_.reshape(1, -1)
    full = lambda shape: pl.BlockSpec(shape, lambda i, j: (0, 0))
    piece = lambda a_rows, a_cols: pl.BlockSpec((a_rows // steps, a_cols), lambda i, j: (i * nt + j, 0))
    cast_specs = [piece(*a.shape) for a in casts]
    outs = pl.pallas_call(
        functools.partial(_rwkv_kernel, tt=tt, n_cast=len(casts)),
        grid=(b, nt),
        in_specs=[pl.BlockSpec((1, tt, d), lambda i, j: (i, j, 0)), full((1, d)), full((d, cols)), full((1, cols)),
                  full(w_w2.shape), full((1, w)), full(w_a2.shape), full((1, w)), full(w_g2.shape),
                  full((1, w)), full((1, w)), full((1, w)), full((1, w)), full((1, w))] + cast_specs,
        out_specs=[pl.BlockSpec((1, tt, w), lambda i, j: (i, j, 0))] + cast_specs + [piece(zero_rows, d)],
        out_shape=[jax.ShapeDtypeStruct((b, t, w), BF16)] + [jax.ShapeDtypeStruct(a.shape, BF16) for a in casts]
        + [jax.ShapeDtypeStruct((zero_rows, d), F32)],
        scratch_shapes=[pltpu.VMEM((8 + tt, cols), F32),
                        pltpu.VMEM((RWKV_HEADS, RWKV_HD, RWKV_HD), F32)],
        compiler_params=_cparams("parallel", "arbitrary"),
        name="rwkv7",
    )(h, vec(norm_g), w_in, vec(mu), w_w2, vec(w0), w_a2, vec(a0), w_g2, vec(k_k), vec(k_a), vec(r_k),
      vec(lnx_g), vec(lnx_b), *casts)
    return outs[0], outs[1:-1], outs[-1]


def _fox_kernel(q_ref, k_ref, v_ref, c_ref, o_ref, kaug_ref, vaug_ref, *, tq, groups):
    qi = pl.program_id(2)
    d = FOX_HD
    per = LANE // d
    nh = groups * per
    head0 = pl.program_id(1) * nh

    def pack_parts(c_all):
        lane = lax.broadcasted_iota(jnp.int32, c_all.shape, 1)
        hi, mid, lo = _split3(c_all)
        packed = jnp.where(lane < FOX_HEADS, hi, jnp.where(
            lane < 2 * FOX_HEADS, pltpu.roll(mid, FOX_HEADS, axis=1), jnp.where(
                lane < 3 * FOX_HEADS, pltpu.roll(lo, 2 * FOX_HEADS, axis=1), 0.0)))
        return packed.astype(BF16)

    def augment(x_h, parts, head, first):
        r = lax.broadcasted_iota(jnp.int32, (LANE, LANE), 0)
        col = lax.broadcasted_iota(jnp.int32, (LANE, LANE), 1)
        base = d if first else d + 3
        hit = (r - head == (col - base) * FOX_HEADS) & (col >= base) & (col < base + 3)
        sel = jnp.where(hit, 1.0 if first else -1.0, 0.0).astype(BF16)
        lane = lax.broadcasted_iota(jnp.int32, x_h.shape, 1)
        ones_lo = d + 3 if first else d
        fixed = jnp.where(lane < d, x_h, jnp.where((lane >= ones_lo) & (lane < ones_lo + 3), 1.0, 0.0))
        return (fixed + jnp.dot(parts, sel, preferred_element_type=F32)).astype(BF16)

    def head_lanes(x, h):
        x_g = x[:, (h // per) * LANE:(h // per + 1) * LANE]
        return x_g if h % per == 0 else pltpu.roll(x_g, LANE - (h % per) * d, axis=1)

    @pl.when(qi == 0)
    def _():
        lane_t = lax.broadcasted_iota(jnp.int32, (k_ref.shape[1], LANE), 1)
        parts = pack_parts(c_ref[0])
        for h in range(nh):
            kaug_ref[h] = augment(head_lanes(k_ref[0].astype(F32), h), parts, head0 + h, first=False)
            v_h = jnp.where(lane_t < d, head_lanes(v_ref[0].astype(F32), h), jnp.where(lane_t == d, 1.0, 0.0))
            vaug_ref[h] = v_h.astype(BF16)

    row0 = pl.multiple_of(qi * tq, tq)
    q = q_ref[0].astype(F32) * (d ** -0.5)
    parts_q = pack_parts(c_ref[0, pl.ds(row0, tq), :])
    qs = [augment(head_lanes(q, h), parts_q, head0 + h, first=True) for h in range(nh)]
    causal = _tri(tq)
    lag = 2

    def step(j, carry, diagonal):
        start = pl.multiple_of(j * tq, tq)
        ss, new = {}, []
        for h in range(nh + lag):
            if h < nh:
                ss[h] = lax.dot_general(qs[h], kaug_ref[h, pl.ds(start, tq), :], (((1,), (1,)), ((), ())),
                                        preferred_element_type=F32)
            g = h - lag
            if g >= 0:
                m, acc = carry[g]
                s = ss.pop(g)
                s = jnp.where(causal, s, -jnp.inf) if diagonal else s
                m_new = jnp.maximum(m, jnp.max(s, axis=-1, keepdims=True))
                p = jnp.exp(s - m_new).astype(BF16)
                pv = jnp.dot(p, vaug_ref[g, pl.ds(start, tq), :], preferred_element_type=F32)
                new.append((m_new, acc * jnp.exp(m - m_new) + pv))
        return tuple(new)

    init = tuple((jnp.full((tq, 1), -jnp.inf, F32), jnp.zeros((tq, LANE), F32)) for _ in range(nh))
    carry = lax.fori_loop(0, qi, lambda j, cr: step(j, cr, False), init)
    carry = step(qi, carry, True)
    lane = lax.broadcasted_iota(jnp.int32, (tq, LANE), 1)
    for grp in range(groups):
        out = None
        for hh in range(per):
            acc = carry[grp * per + hh][1]
            o_h = acc / jnp.sum(jnp.where(lane == d, acc, 0.0), axis=-1, keepdims=True)
            o_h = o_h if hh == 0 else pltpu.roll(o_h, hh * d, axis=1)
            out = o_h if out is None else jnp.where(lane >= hh * d, o_h, out)
        o_ref[0, :, grp * LANE:(grp + 1) * LANE] = out.astype(o_ref.dtype)


def _fox(q, k, v, c_col, tq=256, groups=4):
    b, t, w = q.shape
    gw = groups * LANE
    return pl.pallas_call(
        functools.partial(_fox_kernel, tq=tq, groups=groups),
        grid=(b, w // gw, t // tq),
        in_specs=[pl.BlockSpec((1, tq, gw), lambda i, p, j: (i, j, p)),
                  pl.BlockSpec((1, t, gw), lambda i, p, j: (i, 0, p)),
                  pl.BlockSpec((1, t, gw), lambda i, p, j: (i, 0, p)),
                  pl.BlockSpec((1, t, LANE), lambda i, p, j: (i, 0, 0))],
        out_specs=pl.BlockSpec((1, tq, gw), lambda i, p, j: (i, j, p)),
        out_shape=jax.ShapeDtypeStruct((b, t, w), BF16),
        scratch_shapes=[pltpu.VMEM((groups * LANE // FOX_HD, t, LANE), BF16),
                        pltpu.VMEM((groups * LANE // FOX_HD, t, LANE), BF16)],
        compiler_params=_cparams("parallel", "arbitrary", "arbitrary"),
        name="fox_attention",
    )(q, k, v, c_col)


def _serpentine(i, f, nf):
    return jnp.where(i % 2 == 0, f, nf - 1 - f)


def _mix_residual(h_ref, ya_ref, yb_ref, wa_ref, wb_ref):
    return h_ref[...] + _dot(ya_ref[...], wa_ref[...]) + _dot(yb_ref[...], wb_ref[...])


def _ffn_kernel(h_ref, ya_ref, yb_ref, wa_ref, wb_ref, g_ref, w1_ref, w3_ref, w2_ref, o_ref, xn_ref, acc_ref):
    f = pl.program_id(1)

    @pl.when(f == 0)
    def _():
        hn = _mix_residual(h_ref, ya_ref, yb_ref, wa_ref, wb_ref)
        xn_ref[...] = _rmsnorm(hn, g_ref[...]).astype(BF16)
        acc_ref[...] = hn

    xn = xn_ref[...]
    mid = _silu(jnp.dot(xn, w1_ref[...], preferred_element_type=F32)) * jnp.dot(
        xn, w3_ref[...], preferred_element_type=F32)
    acc_ref[...] += _dot(mid, w2_ref[...])

    @pl.when(f == pl.num_programs(1) - 1)
    def _():
        o_ref[...] = acc_ref[...]


def _ffn(h, ya, yb, wa, wb, g, w1, w3, w2, tm=512, tf=1408):
    m, d = h.shape
    k = ya.shape[1]
    nf = w1.shape[1] // tf
    rows = lambda n: pl.BlockSpec((tm, n), lambda i, f: (i, 0))
    full = lambda shape: pl.BlockSpec(shape, lambda i, f: (0, 0))
    return pl.pallas_call(
        _ffn_kernel,
        grid=(m // tm, nf),
        in_specs=[rows(d), rows(k), rows(k), full((k, d)), full((k, d)), full((1, d)),
                  pl.BlockSpec((d, tf), lambda i, f: (0, _serpentine(i, f, nf))),
                  pl.BlockSpec((d, tf), lambda i, f: (0, _serpentine(i, f, nf))),
                  pl.BlockSpec((tf, d), lambda i, f: (_serpentine(i, f, nf), 0))],
        out_specs=rows(d),
        out_shape=jax.ShapeDtypeStruct((m, d), F32),
        scratch_shapes=[pltpu.VMEM((tm, d), BF16), pltpu.VMEM((tm, d), F32)],
        compiler_params=_cparams("parallel", "arbitrary"),
        name="ffn_dense",
    )(h, ya, yb, wa, wb, g.reshape(1, d), w1, w3, w2)


def _split2(x):
    hi = x.astype(BF16)
    return hi, (x - hi.astype(F32)).astype(BF16)


def _moe_route_kernel(h_ref, ya_ref, yb_ref, wa_ref, wb_ref, g_ref, r_ref, tri_ref, hn_ref, xn_ref, meta_ref,
                      idx_ref, cnt_ref, carry_ref):
    @pl.when(pl.program_id(0) == 0)
    def _():
        carry_ref[...] = jnp.zeros(carry_ref.shape, F32)

    hn = _mix_residual(h_ref, ya_ref, yb_ref, wa_ref, wb_ref)
    hn_ref[...] = hn
    xn = _rmsnorm(hn, g_ref[...])
    xn_ref[...] = xn
    tm = xn.shape[0]
    lane = lax.broadcasted_iota(jnp.int32, (tm, LANE), 1)
    x_hi, x_lo = _split2(xn)
    r_hi, r_lo = _split2(r_ref[...])
    dot = functools.partial(jnp.dot, preferred_element_type=F32)
    logits = jnp.where(lane < N_EXPERTS, dot(x_hi, r_hi) + (dot(x_hi, r_lo) + dot(x_lo, r_hi)), -jnp.inf)
    m1 = jnp.max(logits, axis=-1, keepdims=True)
    i1 = jnp.min(jnp.where(logits == m1, lane, LANE), axis=-1, keepdims=True)
    rest = jnp.where(lane == i1, -jnp.inf, logits)
    m2 = jnp.max(rest, axis=-1, keepdims=True)
    i2 = jnp.min(jnp.where(rest == m2, lane, LANE), axis=-1, keepdims=True)
    e2 = jnp.exp(m2 - m1)
    g1 = 1.0 / (1.0 + e2)
    g2 = e2 / (1.0 + e2)
    onehot = jnp.where(lane == i1, 1.0, jnp.where(lane == i2, 1.0, 0.0))
    before = dot(tri_ref[...], onehot.astype(BF16)) + carry_ref[0:1, :]
    r1 = jnp.sum(jnp.where(lane == i1, before, 0.0), axis=-1, keepdims=True)
    r2 = jnp.sum(jnp.where(lane == i2, before, 0.0), axis=-1, keepdims=True)
    meta = jnp.where(lane == 0, i1.astype(F32), jnp.where(lane == 1, i2.astype(F32), 0.0))
    meta = jnp.where(lane == 2, g1, jnp.where(lane == 3, g2, meta))
    meta = jnp.where(lane == 4, r1, jnp.where(lane == 5, r2, meta))
    meta_ref[...] = meta
    idx_ref[...] = jnp.transpose(meta)[0:_SUBLANES].astype(jnp.int32)
    total = carry_ref[0:1, :] + jnp.sum(onehot, axis=0, keepdims=True)
    carry_ref[...] = jnp.broadcast_to(total, carry_ref.shape)
    cnt_ref[...] = jnp.broadcast_to(total, cnt_ref.shape)


def _moe_route(h, ya, yb, wa, wb, g, router, tm=512):
    m, d = h.shape
    k = ya.shape[1]
    rows = lambda n: pl.BlockSpec((tm, n), lambda i: (i, 0))
    full = lambda shape: pl.BlockSpec(shape, lambda i: (0, 0))
    return pl.pallas_call(
        _moe_route_kernel,
        grid=(m // tm,),
        in_specs=[rows(d), rows(k), rows(k), full((k, d)), full((k, d)), full((1, d)), full((d, LANE)),
                  full((tm, tm))],
        out_specs=[rows(d), rows(d), rows(LANE), pl.BlockSpec((_SUBLANES, tm), lambda i: (0, i)),
                   full((8, LANE))],
        out_shape=[jax.ShapeDtypeStruct((m, d), F32), jax.ShapeDtypeStruct((m, d), F32),
                   jax.ShapeDtypeStruct((m, LANE), F32), jax.ShapeDtypeStruct((_SUBLANES, m), jnp.int32),
                   jax.ShapeDtypeStruct((8, LANE), F32)],
        scratch_shapes=[pltpu.VMEM((8, LANE), F32)],
        compiler_params=_cparams("arbitrary"),
        name="moe_route",
    )(h, ya, yb, wa, wb, g.reshape(1, d), router, jnp.tril(jnp.ones((tm, tm), BF16), -1))


def _row_copy(src_ref, src_group, src_sub, dst_ref, dst_group, dst_sub, sem):
    return pltpu.make_async_copy(src_ref.at[src_group, pl.ds(src_sub, 1)],
                                 dst_ref.at[dst_group, pl.ds(dst_sub, 1)], sem)


def _split_row(p):
    return lax.shift_right_logical(p, 3), lax.bitwise_and(p, _SUBLANES - 1)


def _moe_dispatch_kernel(pos_ref, xn_ref, xs_in_ref, xs_ref, sem, *, tm):
    del xs_in_ref

    def start(grp, carry):
        for u in range(_SUBLANES):
            for s in range(2):
                dst_group, dst_sub = _split_row(pos_ref[0, s, grp * _SUBLANES + u])
                _row_copy(xn_ref, grp, u, xs_ref, dst_group, dst_sub, sem).start(priority=s)
        return carry

    def wait(grp, carry):
        for _ in range(2 * _SUBLANES):
            _row_copy(xn_ref, 0, 0, xs_ref, 0, 0, sem).wait()
        return carry

    lax.fori_loop(0, tm // _SUBLANES, start, 0)
    lax.fori_loop(0, tm // _SUBLANES, wait, 0)


def _tile_positions(pos, tm):
    return pos.reshape(2, -1, tm).transpose(1, 0, 2)


def _moe_dispatch(xn, pos, zeros, tm=512):
    m, d = xn.shape
    n_rows = zeros.shape[0]
    xs = pl.pallas_call(
        functools.partial(_moe_dispatch_kernel, tm=tm),
        grid=(m // tm,),
        in_specs=[pl.BlockSpec((1, 2, tm), lambda i: (i, 0, 0), memory_space=pltpu.SMEM),
                  pl.BlockSpec((tm // _SUBLANES, _SUBLANES, d), lambda i: (i, 0, 0)),
                  pl.BlockSpec(memory_space=pl.ANY)],
        out_specs=pl.BlockSpec(memory_space=pl.ANY),
        out_shape=jax.ShapeDtypeStruct((n_rows // _SUBLANES, _SUBLANES, d), F32),
        scratch_shapes=[pltpu.SemaphoreType.DMA(())],
        input_output_aliases={2: 0},
        compiler_params=_cparams("arbitrary"),
        name="moe_dispatch",
    )(_tile_positions(pos, tm), xn.reshape(m // _SUBLANES, _SUBLANES, d),
      zeros.reshape(n_rows // _SUBLANES, _SUBLANES, d))
    return xs.reshape(n_rows, d)


def _moe_expert_kernel(te_ref, nu_ref, x_ref, w1_ref, w3_ref, w2_ref, y_ref, xbf_ref, acc_ref):
    del te_ref
    f = pl.program_id(1)

    @pl.when(f == 0)
    def _():
        xbf_ref[...] = x_ref[...].astype(BF16)
        acc_ref[...] = jnp.zeros(acc_ref.shape, F32)

    @pl.when(pl.program_id(0) < nu_ref[0])
    def _():
        xb = xbf_ref[...]
        mid = _silu(jnp.dot(xb, w1_ref[0], preferred_element_type=F32)) * jnp.dot(
            xb, w3_ref[0], preferred_element_type=F32)
        acc_ref[...] += _dot(mid, w2_ref[0])

    @pl.when(f == pl.num_programs(1) - 1)
    def _():
        y_ref[...] = acc_ref[...]


def _moe_experts(xs, tile_expert, n_used, w1, w3, w2, tmx, tf=1792):
    n_rows, d = xs.shape
    nf = w1.shape[2] // tf

    def fidx(i, f, nu):
        last = i >= nu[0]
        return _serpentine(jnp.where(last, nu[0] - 1, i), jnp.where(last, nf - 1, f), nf)

    grid_spec = pltpu.PrefetchScalarGridSpec(
        num_scalar_prefetch=2,
        grid=(n_rows // tmx, nf),
        in_specs=[pl.BlockSpec((tmx, d), lambda i, f, te, nu: (i, 0)),
                  pl.BlockSpec((1, d, tf), lambda i, f, te, nu: (te[i], 0, fidx(i, f, nu))),
                  pl.BlockSpec((1, d, tf), lambda i, f, te, nu: (te[i], 0, fidx(i, f, nu))),
                  pl.BlockSpec((1, tf, d), lambda i, f, te, nu: (te[i], fidx(i, f, nu), 0))],
        out_specs=pl.BlockSpec((tmx, d), lambda i, f, te, nu: (i, 0)),
        scratch_shapes=[pltpu.VMEM((tmx, d), BF16), pltpu.VMEM((tmx, d), F32)],
    )
    return pl.pallas_call(
        _moe_expert_kernel,
        grid_spec=grid_spec,
        out_shape=jax.ShapeDtypeStruct((n_rows, d), F32),
        compiler_params=_cparams("arbitrary", "arbitrary"),
        name="moe_experts",
    )(tile_expert, n_used, xs, w1, w3, w2)


def _moe_combine_kernel(pos_ref, posn_ref, h_ref, meta_ref, fg_ref, ys_ref, o_ref, ybuf_ref, sem, *, tm):
    i = pl.program_id(0)
    slot = lax.rem(i, 2)

    groups = tm // _SUBLANES

    def gather(p_ref, dst_slot):
        def start(grp, carry):
            for u in range(_SUBLANES):
                for s in range(2):
                    src_group, src_sub = _split_row(p_ref[0, s, grp * _SUBLANES + u])
                    _row_copy(ys_ref, src_group, src_sub, ybuf_ref.at[dst_slot], s * groups + grp, u,
                              sem.at[dst_slot]).start(priority=s)
            return carry
        lax.fori_loop(0, groups, start, 0)

    @pl.when(i == 0)
    def _():
        gather(pos_ref, 0)

    @pl.when(i + 1 < pl.num_programs(0))
    def _():
        gather(posn_ref, 1 - slot)

    def wait(grp, carry):
        for _ in range(2 * _SUBLANES):
            _row_copy(ys_ref, 0, 0, ybuf_ref.at[slot], 0, 0, sem.at[slot]).wait()
        return carry

    lax.fori_loop(0, groups, wait, 0)
    lane = lax.broadcasted_iota(jnp.int32, (tm, LANE), 1)
    meta = meta_ref[...]
    g1 = jnp.sum(jnp.where(lane == 2, meta, 0.0), axis=-1, keepdims=True)
    g2 = jnp.sum(jnp.where(lane == 3, meta, 0.0), axis=-1, keepdims=True)
    y = ybuf_ref[slot].reshape(2 * tm, -1)
    o_ref[...] = _rmsnorm(h_ref[...] + g1 * y[:tm] + g2 * y[tm:], fg_ref[...])


def _moe_combine(h, meta, pos, ys, final_g, tm=512):
    m, d = h.shape
    nt = m // tm
    pos3 = _tile_positions(pos, tm)
    smem = lambda imap: pl.BlockSpec((1, 2, tm), imap, memory_space=pltpu.SMEM)
    return pl.pallas_call(
        functools.partial(_moe_combine_kernel, tm=tm),
        grid=(nt,),
        in_specs=[smem(lambda i: (i, 0, 0)), smem(lambda i: (jnp.minimum(i + 1, nt - 1), 0, 0)),
                  pl.BlockSpec((tm, d), lambda i: (i, 0)), pl.BlockSpec((tm, LANE), lambda i: (i, 0)),
                  pl.BlockSpec((1, d), lambda i: (0, 0)), pl.BlockSpec(memory_space=pl.ANY)],
        out_specs=pl.BlockSpec((tm, d), lambda i: (i, 0)),
        out_shape=jax.ShapeDtypeStruct((m, d), F32),
        scratch_shapes=[pltpu.VMEM((2, 2 * tm // _SUBLANES, _SUBLANES, d), F32), pltpu.SemaphoreType.DMA((2,))],
        compiler_params=_cparams("arbitrary"),
        name="moe_combine",
    )(pos3, pos3, h, meta, final_g.reshape(1, d), ys.reshape(-1, _SUBLANES, d))


_MOE_TILE = 512


def _moe_sorted_rows(m):
    return ((2 * m) // _MOE_TILE + N_EXPERTS) * _MOE_TILE


def _moe_routed(h, ya, yb, wa, wb, g, router, w1, w3, w2, final_g, zeros):
    m, d = h.shape
    tmx = _MOE_TILE
    h, xn, meta, idx, cnt = _moe_route(h, ya, yb, wa, wb, g, router)
    counts = cnt[0, :N_EXPERTS].astype(jnp.int32)
    padded = (counts + tmx - 1) // tmx * tmx
    ends = jnp.cumsum(padded)
    pos = (ends - padded)[idx[0:2]] + idx[4:6]
    n_tiles = zeros.shape[0] // tmx
    n_used = ends[-1] // tmx
    tile_start = jnp.minimum(jnp.arange(n_tiles, dtype=jnp.int32), n_used - 1) * tmx
    tile_expert = jnp.sum(tile_start[:, None] >= ends[None, :], axis=1).astype(jnp.int32)
    xs = _moe_dispatch(xn, pos, zeros)
    ys = _moe_experts(xs, tile_expert, n_used.reshape(1).astype(jnp.int32), w1, w3, w2, tmx)
    return _moe_combine(h, meta, pos, ys, final_g)


def _pad_cols(w, n):
    return jnp.pad(w, ((0, 0), (0, n - w.shape[1])))


def kernel(x, norm_mix_g, norm_ffn_g, w_in0, conv_w, conv_b, conv_ln_g, conv_ln_b, gla_w_a2, gla_b_a, gla_norm_g, w_out0, ffn_w1, ffn_w3, ffn_w2, w_in1, rwkv_mu, rwkv_w2, rwkv_w0, rwkv_a2, rwkv_a0, rwkv_g2, rwkv_k_k, rwkv_k_a, rwkv_r_k, rwkv_lnx_g, rwkv_lnx_b, fox_b_f, w_out1, moe_router, moe_w1, moe_w3, moe_w2, final_norm_g):
    b, t, d = x.shape
    m = b * t
    half = d // 2
    bf = lambda a: a.astype(BF16)
    h = x.reshape(m, d)

    y_conv, y_gla = _layer0_mix(x, norm_mix_g[0], w_in0, conv_w, conv_b, conv_ln_g, conv_ln_b,
                                gla_w_a2, gla_b_a, gla_norm_g)
    h = _ffn(h, y_conv.reshape(m, half), y_gla.reshape(m, half), bf(w_out0[:half]), bf(w_out0[half:]),
             norm_ffn_g[0], bf(ffn_w1), bf(ffn_w3), bf(ffn_w2))

    rc = _RWKV_COLS
    flat = lambda w: w.reshape(-1, w.shape[-1])
    y_rwkv, (e_w1, e_w3, e_w2), moe_zeros = _rwkv(
        h.reshape(b, t, d), norm_mix_g[1], bf(w_in1[:, :rc]), rwkv_mu, rwkv_w2, rwkv_w0, rwkv_a2, rwkv_a0,
        rwkv_g2, rwkv_k_k, rwkv_k_a, rwkv_r_k, rwkv_lnx_g, rwkv_lnx_b,
        casts=[flat(moe_w1), flat(moe_w3), flat(moe_w2)], zero_rows=_moe_sorted_rows(m))
    b_f = jnp.pad(fox_b_f, (0, LANE - FOX_HEADS)).reshape(1, LANE)
    fq, fk, fv, c_col = _fox_proj(
        h.reshape(b, t, d), norm_mix_g[1], bf(w_in1[:, rc:rc + half]), bf(w_in1[:, rc + half:rc + 2 * half]),
        bf(w_in1[:, rc + 2 * half:rc + 3 * half]), bf(_pad_cols(w_in1[:, rc + 3 * half:], LANE)), b_f)
    y_fox = _fox(fq, fk, fv, c_col)
    out = _moe_routed(h, y_rwkv.reshape(m, half), y_fox.reshape(m, half), bf(w_out1[:half]), bf(w_out1[half:]),
                      norm_ffn_g[1], _pad_cols(moe_router, LANE), e_w1.reshape(moe_w1.shape),
                      e_w3.reshape(moe_w3.shape), e_w2.reshape(moe_w2.shape), final_norm_g, moe_zeros)
    return out.reshape(b, t, d)
```

```python
import functools

import jax
import jax.numpy as jnp
from jax import lax
from jax.experimental import pallas as pl
from jax.experimental.pallas import tpu as pltpu

F32 = jnp.float32
BF16 = jnp.bfloat16

NORM_EPS = 1e-6
CONV_WIDTH = 31
CONV_LN_EPS = 1e-5
GLA_HEADS = 4
GLA_DK = 64
GLA_DV = 128
GLA_TAU = 16.0
GLA_CHUNK = 64
RWKV_HEADS = 8
RWKV_HD = 64
RWKV_CHUNK = 64
RWKV_DECAY_SCALE = 0.606531
RWKV_GN_EPS = 64e-5
RWKV_L2_EPS = 1e-12
FOX_HEADS = 8
FOX_HD = 64
N_EXPERTS = 8
LANE = 128
_SUBLANES = 8
VMEM_LIMIT = 56 * 1024 * 1024


def _cparams(*sem):
    return pltpu.CompilerParams(dimension_semantics=sem, vmem_limit_bytes=VMEM_LIMIT)


def _dot(a, b):
    return jnp.dot(a.astype(BF16), b.astype(BF16), preferred_element_type=F32)


def _dot_nt(a, b):
    return lax.dot_general(a.astype(BF16), b.astype(BF16), (((1,), (1,)), ((), ())),
                           preferred_element_type=F32)


def _dot_tn(a, b):
    return lax.dot_general(a.astype(BF16), b.astype(BF16), (((0,), (0,)), ((), ())),
                           preferred_element_type=F32)


def _split3(x):
    hi = x.astype(BF16).astype(F32)
    mid = (x - hi).astype(BF16).astype(F32)
    lo = (x - hi - mid).astype(BF16).astype(F32)
    return hi, mid, lo


def _dot_01(sel, x):
    hi, mid, lo = _split3(x)
    return (_dot(sel, lo) + _dot(sel, mid)) + _dot(sel, hi)


def _dot_tn_01(x, sel):
    hi, mid, lo = _split3(x)
    return (_dot_tn(lo, sel) + _dot_tn(mid, sel)) + _dot_tn(hi, sel)


def _sigmoid(x):
    return 1.0 / (1.0 + jnp.exp(-x))


def _silu(x):
    return x * _sigmoid(x)


def _log_sigmoid(x):
    return jnp.minimum(x, 0.0) - jnp.log(1.0 + jnp.exp(-jnp.abs(x)))


def _rmsnorm(x, g):
    return x * lax.rsqrt(jnp.mean(x * x, axis=-1, keepdims=True) + NORM_EPS) * g


def _tri(n, strict=False):
    r = lax.broadcasted_iota(jnp.int32, (n, n), 0)
    c = lax.broadcasted_iota(jnp.int32, (n, n), 1)
    return (r > c) if strict else (r >= c)


_CUM_ROWS = 256


def _fox_pack_parts(c_all):
    lane = lax.broadcasted_iota(jnp.int32, c_all.shape, 1)
    hi, mid, lo = _split3(c_all)
    packed = jnp.where(lane < FOX_HEADS, hi, jnp.where(
        lane < 2 * FOX_HEADS, pltpu.roll(mid, FOX_HEADS, axis=1), jnp.where(
            lane < 3 * FOX_HEADS, pltpu.roll(lo, 2 * FOX_HEADS, axis=1), 0.0)))
    return packed.astype(BF16)


def _fox_augment(x_h, parts, head, first):
    d = FOX_HD
    r = lax.broadcasted_iota(jnp.int32, (LANE, LANE), 0)
    col = lax.broadcasted_iota(jnp.int32, (LANE, LANE), 1)
    base = d if first else d + 3
    hit = (r - head == (col - base) * FOX_HEADS) & (col >= base) & (col < base + 3)
    sel = jnp.where(hit, 1.0 if first else -1.0, 0.0).astype(BF16)
    lane = lax.broadcasted_iota(jnp.int32, x_h.shape, 1)
    ones_lo = d + 3 if first else d
    fixed = jnp.where(lane < d, x_h, jnp.where((lane >= ones_lo) & (lane < ones_lo + 3), 1.0, 0.0))
    return (fixed + jnp.dot(parts, sel, preferred_element_type=F32)).astype(BF16)


def _fox_head_lanes(x, h):
    per = LANE // FOX_HD
    x_g = x[:, (h // per) * LANE:(h // per + 1) * LANE]
    return x_g if h % per == 0 else pltpu.roll(x_g, LANE - (h % per) * FOX_HD, axis=1)


def _fox_proj_kernel(h_ref, g_ref, wq_ref, wk_ref, wv_ref, wf_ref, bf_ref, qa_ref, ka_ref, va_ref, carry_ref):
    @pl.when(pl.program_id(1) == 0)
    def _():
        carry_ref[...] = jnp.zeros(carry_ref.shape, F32)

    d = FOX_HD
    xn = _rmsnorm(h_ref[0], g_ref[...]).astype(BF16)
    proj = lambda w_ref: jnp.dot(xn, w_ref[...], preferred_element_type=F32)
    q = proj(wq_ref) * (d ** -0.5)
    k = proj(wk_ref)
    v = proj(wv_ref)
    log_f = _log_sigmoid(proj(wf_ref) + bf_ref[...])
    tri = _tri(_CUM_ROWS).astype(F32)
    last = carry_ref[0:1, :]
    cums = []
    for r0 in range(0, xn.shape[0], _CUM_ROWS):
        cums.append(_dot_01(tri, log_f[r0:r0 + _CUM_ROWS]) + last)
        last = cums[-1][_CUM_ROWS - 1:_CUM_ROWS, :]
    carry_ref[...] = jnp.broadcast_to(last, carry_ref.shape)
    parts = _fox_pack_parts(jnp.concatenate(cums, axis=0))
    lane = lax.broadcasted_iota(jnp.int32, (xn.shape[0], LANE), 1)
    for h in range(FOX_HEADS):
        qa_ref[0, h] = _fox_augment(_fox_head_lanes(q, h), parts, h, first=True)
        ka_ref[0, h] = _fox_augment(_fox_head_lanes(k, h), parts, h, first=False)
        va_ref[0, h] = jnp.where(lane < d, _fox_head_lanes(v, h), jnp.where(lane == d, 1.0, 0.0)).astype(BF16)


def _fox_proj(h, g, wq, wk, wv, wf, b_f, tt=512):
    b, t, d = h.shape
    full = lambda a: pl.BlockSpec(a.shape, lambda i, j: (0, 0))
    heads = pl.BlockSpec((1, FOX_HEADS, tt, LANE), lambda i, j: (i, 0, j, 0))
    g2 = g.reshape(1, d)
    return pl.pallas_call(
        _fox_proj_kernel,
        grid=(b, t // tt),
        in_specs=[pl.BlockSpec((1, tt, d), lambda i, j: (i, j, 0)), full(g2), full(wq), full(wk), full(wv),
                  full(wf), full(b_f)],
        out_specs=[heads] * 3,
        out_shape=[jax.ShapeDtypeStruct((b, FOX_HEADS, t, LANE), BF16)] * 3,
        scratch_shapes=[pltpu.VMEM((8, LANE), F32)],
        compiler_params=_cparams("parallel", "arbitrary"),
        name="fox_proj",
    )(h, g2, wq, wk, wv, wf, b_f)


_CONV_HALO = 32
_CONV_ROWS = 32
_GLA_SPAN = 256


def _conv_body(val, gate, w_ref, b_ref, lg_ref, lb_ref, o_ref, u_ref, sh_ref):
    tt, c = val.shape
    rows = _CONV_ROWS
    u_ref[_CONV_HALO:_CONV_HALO + tt, :] = val * _sigmoid(gate)
    span = sh_ref.shape[1]
    for s in range(1, _SUBLANES):
        sh_ref[s - 1] = u_ref[s:s + span, :]
    base = _CONV_HALO - (CONV_WIDTH - 1)
    for r0 in range(0, tt, rows):
        acc = jnp.zeros((rows // _SUBLANES, _SUBLANES, c), F32) + b_ref[...]
        for j in range(CONV_WIDTH):
            phase = (base + j) % _SUBLANES
            row = r0 + base + j - phase
            win = u_ref[row:row + rows, :] if phase == 0 else sh_ref[phase - 1, row:row + rows, :]
            acc = acc + win.reshape(rows // _SUBLANES, _SUBLANES, c) * w_ref[j]
        acc = acc.reshape(rows, c)
        mu = jnp.mean(acc, axis=-1, keepdims=True)
        cen = acc - mu
        var = jnp.mean(cen * cen, axis=-1, keepdims=True)
        y = cen * lax.rsqrt(var + CONV_LN_EPS) * lg_ref[...] + lb_ref[...]
        o_ref[0, r0:r0 + rows, :] = _silu(y).astype(o_ref.dtype)
    u_ref[0:_CONV_HALO, :] = u_ref[tt:tt + _CONV_HALO, :]


def _gla_body(q, k, v, go, alr, wa2_ref, ba_ref, ng_ref, o_ref, s_ref, row0):
    c = GLA_CHUNK
    nc = q.shape[0] // c
    tril = _tri(c)
    tril_f = tril.astype(F32)
    ones_cv = jnp.ones((c, GLA_DV), F32)
    units = [(ci, h) for ci in range(nc) for h in range(GLA_HEADS)]
    rows_of = lambda ci: slice(ci * c, (ci + 1) * c)
    keys_of = lambda h: slice(h * GLA_DK, (h + 1) * GLA_DK)
    vals_of = lambda h: slice(h * GLA_DV, (h + 1) * GLA_DV)

    z = _dot(alr, wa2_ref[...]) + ba_ref[...]
    log_a = _log_sigmoid(z) * (1.0 / GLA_TAU)
    q = q * (GLA_DK ** -0.5)
    q_decs, k_decs, k_ends, decays = [], [], [], []
    for ci in range(nc):
        la = log_a[rows_of(ci)]
        bcum = _dot_01(tril_f, la)
        b_last = bcum[c - 1:c, :]
        q_decs.append(q[rows_of(ci)] * jnp.exp(bcum))
        k_decs.append(k[rows_of(ci)] * jnp.exp(-bcum))
        k_ends.append(k[rows_of(ci)] * jnp.exp(b_last - bcum))
        decays.append(jnp.exp(_dot_tn_01(la, ones_cv)))

    vs = [v[rows_of(ci), vals_of(h)] for ci, h in units]
    attns = [jnp.where(tril, _dot_nt(q_decs[ci][:, keys_of(h)], k_decs[ci][:, keys_of(h)]), 0.0)
             for ci, h in units]
    kvs = [_dot_tn(k_ends[ci][:, keys_of(h)], vs[u]) for u, (ci, h) in enumerate(units)]
    intras = [_dot(attns[u], vs[u]) for u in range(len(units))]
    states = [s_ref[h] for h in range(GLA_HEADS)]
    prevs = []
    for u, (ci, h) in enumerate(units):
        prevs.append(states[h])
        states[h] = states[h] * decays[ci][keys_of(h)] + kvs[u]
    for h in range(GLA_HEADS):
        s_ref[h] = states[h]
    inters = [_dot(q_decs[ci][:, keys_of(h)], prevs[u]) for u, (ci, h) in enumerate(units)]
    for ci in range(nc):
        outs = []
        for h in range(GLA_HEADS):
            u = ci * GLA_HEADS + h
            o = intras[u] + inters[u]
            outs.append(o * lax.rsqrt(jnp.mean(o * o, axis=-1, keepdims=True) + NORM_EPS))
        o_all = jnp.concatenate(outs, axis=-1)
        o_ref[0, row0 + ci * c:row0 + (ci + 1) * c, :] = (
            o_all * ng_ref[...] * _silu(go[rows_of(ci)])).astype(o_ref.dtype)


def _layer0_kernel(h_ref, g_ref, wcv_ref, wcg_ref, wq_ref, wk_ref, wv_ref, wgo_ref, walr_ref,
                   cw_ref, cb_ref, clg_ref, clb_ref, wa2_ref, ba_ref, ng_ref,
                   yc_ref, yg_ref, u_ref, sh_ref, s_ref):
    @pl.when(pl.program_id(1) == 0)
    def _():
        u_ref[0:_CONV_HALO, :] = jnp.zeros((_CONV_HALO, u_ref.shape[1]), F32)
        s_ref[...] = jnp.zeros(s_ref.shape, F32)

    xn = _rmsnorm(h_ref[0], g_ref[...]).astype(BF16)
    proj = lambda w_ref, rows=slice(None): jnp.dot(xn[rows], w_ref[...], preferred_element_type=F32)
    _conv_body(proj(wcv_ref), proj(wcg_ref), cw_ref, cb_ref, clg_ref, clb_ref, yc_ref, u_ref, sh_ref)
    for row0 in range(0, xn.shape[0], _GLA_SPAN):
        rows = slice(row0, row0 + _GLA_SPAN)
        _gla_body(proj(wq_ref, rows), proj(wk_ref, rows), proj(wv_ref, rows), proj(wgo_ref, rows),
                  proj(walr_ref, rows), wa2_ref, ba_ref, ng_ref, yg_ref, s_ref, row0)


def _layer0_mix(x, g, w_in, conv_w, conv_b, ln_g, ln_b, w_a2, b_a, norm_g, tt=512):
    b, t, d = x.shape
    c = d // 2
    gk = GLA_HEADS * GLA_DK
    o = [0, c, 2 * c, 2 * c + gk, 2 * c + 2 * gk, 3 * c + 2 * gk, 4 * c + 2 * gk]
    ws = [w_in[:, o[i]:o[i + 1]].astype(BF16) for i in range(6)] + [_pad_cols(w_in[:, o[6]:], LANE).astype(BF16)]
    wrep = jnp.broadcast_to(conv_w[:, None, :], (CONV_WIDTH, _SUBLANES, c))
    w_a2p = jnp.pad(w_a2, ((0, LANE - w_a2.shape[0]), (0, 0)))
    vec = lambda a: a.reshape(1, -1)
    full = lambda a: pl.BlockSpec(a.shape, lambda i, j: (0,) * a.ndim)
    tile = lambda n: pl.BlockSpec((1, tt, n), lambda i, j: (i, j, 0))
    consts = ws + [wrep, vec(conv_b), vec(ln_g), vec(ln_b), w_a2p, vec(b_a), vec(norm_g)]
    return pl.pallas_call(
        _layer0_kernel,
        grid=(b, t // tt),
        in_specs=[tile(d), full(vec(g))] + [full(a) for a in consts],
        out_specs=[tile(c), tile(c)],
        out_shape=[jax.ShapeDtypeStruct((b, t, c), BF16), jax.ShapeDtypeStruct((b, t, c), BF16)],
        scratch_shapes=[pltpu.VMEM((tt + _CONV_HALO, c), F32),
                        pltpu.VMEM((_SUBLANES - 1, tt + _CONV_HALO - _SUBLANES, c), F32),
                        pltpu.VMEM((GLA_HEADS, GLA_DK, GLA_DV), F32)],
        compiler_params=_cparams("parallel", "arbitrary"),
        name="layer0_mix",
    )(x, vec(g), *consts)


_RWKV_COLS = 3 * 512 + 64 + 64 + 128


def _rwkv_kernel(h_ref, ng_ref, win_ref, mu_ref, w2_ref, w0_ref, a2_ref, a0_ref, g2_ref, kk_ref, ka_ref, rk_ref,
                 lg_ref, lb_ref, *refs, tt, n_cast):
    cast_in, (o_ref, *cast_out, zero_ref), (xs_ref, st_ref) = (
        refs[:n_cast], refs[n_cast:2 * n_cast + 2], refs[2 * n_cast + 2:])
    c = RWKV_CHUNK
    hd = RWKV_HD
    w = RWKV_HEADS * hd
    for src, dst in zip(cast_in, cast_out):
        dst[...] = src[...].astype(BF16)
    zero_ref[...] = jnp.zeros(zero_ref.shape, F32)

    @pl.when(pl.program_id(1) == 0)
    def _():
        xs_ref[0:8, :] = jnp.zeros((8, xs_ref.shape[1]), F32)
        st_ref[...] = jnp.zeros(st_ref.shape, F32)

    x = jnp.dot(_rmsnorm(h_ref[0], ng_ref[...]).astype(BF16), win_ref[...], preferred_element_type=F32)
    xs_ref[8:8 + tt, :] = x
    prev = xs_ref[7:7 + tt, :]
    xs_ref[7:8, :] = x[tt - 1:tt, :]
    x = x + (prev - x) * mu_ref[...]

    r = x[:, 0:w]
    k = x[:, w:2 * w]
    v = x[:, 2 * w:3 * w]
    xw = x[:, 3 * w:3 * w + 64]
    xa = x[:, 3 * w + 64:3 * w + 128]
    xg = x[:, 3 * w + 128:3 * w + 256]
    log_w = -RWKV_DECAY_SCALE * _sigmoid(w0_ref[...] + _dot(jnp.tanh(xw), w2_ref[...]))
    a = _sigmoid(a0_ref[...] + _dot(xa, a2_ref[...]))
    g = _dot(_sigmoid(xg), g2_ref[...])
    kk = k * kk_ref[...]
    k_mod = k * (1.0 + (a - 1.0) * ka_ref[...])
    rkr = r * k_mod * rk_ref[...]

    tril = _tri(c)
    tril_s = _tri(c, strict=True)
    tril_f = tril.astype(F32)
    eye = (lax.broadcasted_iota(jnp.int32, (c, c), 0)
           == lax.broadcasted_iota(jnp.int32, (c, c), 1)).astype(F32)
    zeros_cc = jnp.zeros((c, hd), F32)

    nc = tt // c
    units = [(ci, h) for ci in range(nc) for h in range(RWKV_HEADS)]
    rows_of = lambda ci: slice(ci * c, (ci + 1) * c)
    lanes_of = lambda h: slice(h * hd, (h + 1) * hd)

    exps = []
    for ci in range(nc):
        lw_c = log_w[rows_of(ci)]
        lcum = _dot_01(tril_f, lw_c)
        l_last = lcum[c - 1:c, :]
        exps.append((jnp.exp(lcum), jnp.exp(-lcum), jnp.exp(l_last - lcum), jnp.exp(lcum - lw_c),
                     jnp.exp(l_last)))

    aqs, rqs, vs, lhss, rhss, bkes = [], [], [], [], [], []
    for ci, h in units:
        rows, hs = rows_of(ci), lanes_of(h)
        e_pos, e_neg, e_end, e_prev, _ = exps[ci]
        kk_h = kk[rows, hs]
        nrm = jnp.sqrt(jnp.sum(kk_h * kk_h, axis=-1, keepdims=True))
        kk_h = kk_h / jnp.maximum(nrm, RWKV_L2_EPS)
        k_h = k_mod[rows, hs]
        kka = kk_h * a[rows, hs]
        aq = -kk_h * e_prev[:, hs]
        rq = r[rows, hs] * e_pos[:, hs]
        aqs.append(aq)
        rqs.append(rq)
        vs.append(v[rows, hs])
        lhss.append(jnp.concatenate([aq, rq], axis=0))
        rhss.append(jnp.concatenate([kka * e_neg[:, hs], k_h * e_neg[:, hs]], axis=0))
        bkes.append(jnp.concatenate([kka * e_end[:, hs], k_h * e_end[:, hs]], axis=0))

    xss = [_dot_nt(lhs, rhs) for lhs, rhs in zip(lhss, rhss)]
    m1s = [jnp.where(tril_s, xs[:c, :c], 0.0) for xs in xss]
    m2s = [jnp.where(tril_s, xs[:c, c:], 0.0) for xs in xss]
    n12s = [jnp.concatenate([jnp.where(tril, xs[c:, :c], 0.0), jnp.where(tril, xs[c:, c:], 0.0)], axis=1)
            for xs in xss]
    tinvs = [eye + m1 for m1 in m1s]
    mps = m1s
    for _ in range(5):
        mps = [_dot(mp, mp) for mp in mps]
        tinvs = [tinv + _dot(tinv, mp) for tinv, mp in zip(tinvs, mps)]
    mvs = [_dot(m2, v_h) for m2, v_h in zip(m2s, vs)]
    pqs = [_dot(tinv, jnp.concatenate([mv, aq], axis=1)) for tinv, mv, aq in zip(tinvs, mvs, aqs)]
    pqvs = [jnp.concatenate([pq, jnp.concatenate([v_h, zeros_cc], axis=1)], axis=0)
            for pq, v_h in zip(pqs, vs)]
    yys = [_dot(n12, pqv) for n12, pqv in zip(n12s, pqvs)]
    ghs = [_dot_tn(bke, pqv) for bke, pqv in zip(bkes, pqvs)]

    states = [st_ref[h] for h in range(RWKV_HEADS)]
    ys = []
    for u, (ci, h) in enumerate(units):
        yq = rqs[u] + yys[u][:, hd:]
        gmat = eye * exps[ci][4][:, lanes_of(h)] + ghs[u][:, hd:]
        res = _dot(jnp.concatenate([yq, gmat], axis=0), states[h])
        ys.append(res[:c] + yys[u][:, :hd])
        states[h] = res[c:] + ghs[u][:, :hd]
    for h in range(RWKV_HEADS):
        st_ref[h] = states[h]

    out_rows = []
    for ci in range(nc):
        yns, bonuses = [], []
        for h in range(RWKV_HEADS):
            u = ci * RWKV_HEADS + h
            y = ys[u]
            cen = y - jnp.mean(y, axis=-1, keepdims=True)
            var = jnp.mean(cen * cen, axis=-1, keepdims=True)
            yns.append(cen * lax.rsqrt(var + RWKV_GN_EPS))
            bonuses.append(jnp.sum(rkr[rows_of(ci), lanes_of(h)], axis=-1, keepdims=True) * vs[u])
        yn = jnp.concatenate(yns, axis=1)
        bonus = jnp.concatenate(bonuses, axis=1)
        out_rows.append((yn * lg_ref[...] + lb_ref[...] + bonus) * g[rows_of(ci)])
    o_ref[0] = jnp.concatenate(out_rows, axis=0).astype(o_ref.dtype)


def _rwkv(h, norm_g, w_in, mu, w_w2, w0, w_a2, a0, w_g2, k_k, k_a, r_k, lnx_g, lnx_b, casts, zero_rows, tt=128):
    b, t, d = h.shape
    cols = w_in.shape[1]
    w = RWKV_HEADS * RWKV_HD
    nt = t // tt
    steps = b * nt
    vec = lambda a_: a_.reshape(1, -1)
    full = lambda shape: pl.BlockSpec(shape, lambda i, j: (0, 0))
    piece = lambda a_rows, a_cols: pl.BlockSpec((a_rows // steps, a_cols), lambda i, j: (i * nt + j, 0))
    cast_specs = [piece(*a.shape) for a in casts]
    outs = pl.pallas_call(
        functools.partial(_rwkv_kernel, tt=tt, n_cast=len(casts)),
        grid=(b, nt),
        in_specs=[pl.BlockSpec((1, tt, d), lambda i, j: (i, j, 0)), full((1, d)), full((d, cols)), full((1, cols)),
                  full(w_w2.shape), full((1, w)), full(w_a2.shape), full((1, w)), full(w_g2.shape),
                  full((1, w)), full((1, w)), full((1, w)), full((1, w)), full((1, w))] + cast_specs,
        out_specs=[pl.BlockSpec((1, tt, w), lambda i, j: (i, j, 0))] + cast_specs + [piece(zero_rows, d)],
        out_shape=[jax.ShapeDtypeStruct((b, t, w), BF16)] + [jax.ShapeDtypeStruct(a.shape, BF16) for a in casts]
        + [jax.ShapeDtypeStruct((zero_rows, d), F32)],
        scratch_shapes=[pltpu.VMEM((8 + tt, cols), F32),
                        pltpu.VMEM((RWKV_HEADS, RWKV_HD, RWKV_HD), F32)],
        compiler_params=_cparams("parallel", "arbitrary"),
        name="rwkv7",
    )(h, vec(norm_g), w_in, vec(mu), w_w2, vec(w0), w_a2, vec(a0), w_g2, vec(k_k), vec(k_a), vec(r_k),
      vec(lnx_g), vec(lnx_b), *casts)
    return outs[0], outs[1:-1], outs[-1]


def _fox_kernel(qa_ref, ka_ref, va_ref, o_ref, *, tq):
    qi = pl.program_id(1)
    d = FOX_HD
    per = LANE // d
    nh = FOX_HEADS
    qs = [qa_ref[0, h] for h in range(nh)]
    causal = _tri(tq)
    lag = 2

    def step(j, carry, diagonal):
        start = pl.multiple_of(j * tq, tq)
        ss, new = {}, []
        for h in range(nh + lag):
            if h < nh:
                ss[h] = lax.dot_general(qs[h], ka_ref[0, h, pl.ds(start, tq), :], (((1,), (1,)), ((), ())),
                                        preferred_element_type=F32)
            g = h - lag
            if g >= 0:
                m, acc = carry[g]
                s = ss.pop(g)
                s = jnp.where(causal, s, -jnp.inf) if diagonal else s
                m_new = jnp.maximum(m, jnp.max(s, axis=-1, keepdims=True))
                p = jnp.exp(s - m_new).astype(BF16)
                pv = jnp.dot(p, va_ref[0, g, pl.ds(start, tq), :], preferred_element_type=F32)
                new.append((m_new, acc * jnp.exp(m - m_new) + pv))
        return tuple(new)

    init = tuple((jnp.full((tq, 1), -jnp.inf, F32), jnp.zeros((tq, LANE), F32)) for _ in range(nh))
    carry = lax.fori_loop(0, qi, lambda j, cr: step(j, cr, False), init)
    carry = step(qi, carry, True)
    lane = lax.broadcasted_iota(jnp.int32, (tq, LANE), 1)
    for grp in range(nh // per):
        out = None
        for hh in range(per):
            acc = carry[grp * per + hh][1]
            o_h = acc / jnp.sum(jnp.where(lane == d, acc, 0.0), axis=-1, keepdims=True)
            o_h = o_h if hh == 0 else pltpu.roll(o_h, hh * d, axis=1)
            out = o_h if out is None else jnp.where(lane >= hh * d, o_h, out)
        o_ref[0, :, grp * LANE:(grp + 1) * LANE] = out.astype(o_ref.dtype)


def _fox(qa, ka, va, tq=256):
    b, nh, t, _ = qa.shape
    w = nh * FOX_HD
    return pl.pallas_call(
        functools.partial(_fox_kernel, tq=tq),
        grid=(b, t // tq),
        in_specs=[pl.BlockSpec((1, nh, tq, LANE), lambda i, j: (i, 0, j, 0)),
                  pl.BlockSpec((1, nh, t, LANE), lambda i, j: (i, 0, 0, 0)),
                  pl.BlockSpec((1, nh, t, LANE), lambda i, j: (i, 0, 0, 0))],
        out_specs=pl.BlockSpec((1, tq, w), lambda i, j: (i, j, 0)),
        out_shape=jax.ShapeDtypeStruct((b, t, w), BF16),
        compiler_params=_cparams("parallel", "arbitrary"),
        name="fox_attention",
    )(qa, ka, va)


def _serpentine(i, f, nf):
    return jnp.where(i % 2 == 0, f, nf - 1 - f)


def _mix_residual(h_ref, ya_ref, yb_ref, wa_ref, wb_ref):
    return h_ref[...] + _dot(ya_ref[...], wa_ref[...]) + _dot(yb_ref[...], wb_ref[...])


def _ffn_kernel(h_ref, ya_ref, yb_ref, wa_ref, wb_ref, g_ref, w1_ref, w3_ref, w2_ref, o_ref, xn_ref, acc_ref):
    f = pl.program_id(1)

    @pl.when(f == 0)
    def _():
        hn = _mix_residual(h_ref, ya_ref, yb_ref, wa_ref, wb_ref)
        xn_ref[...] = _rmsnorm(hn, g_ref[...]).astype(BF16)
        acc_ref[...] = hn

    xn = xn_ref[...]
    mid = _silu(jnp.dot(xn, w1_ref[...], preferred_element_type=F32)) * jnp.dot(
        xn, w3_ref[...], preferred_element_type=F32)
    acc_ref[...] += _dot(mid, w2_ref[...])

    @pl.when(f == pl.num_programs(1) - 1)
    def _():
        o_ref[...] = acc_ref[...]


def _ffn(h, ya, yb, wa, wb, g, w1, w3, w2, tm=512, tf=1408):
    m, d = h.shape
    k = ya.shape[1]
    nf = w1.shape[1] // tf
    rows = lambda n: pl.BlockSpec((tm, n), lambda i, f: (i, 0))
    full = lambda shape: pl.BlockSpec(shape, lambda i, f: (0, 0))
    return pl.pallas_call(
        _ffn_kernel,
        grid=(m // tm, nf),
        in_specs=[rows(d), rows(k), rows(k), full((k, d)), full((k, d)), full((1, d)),
                  pl.BlockSpec((d, tf), lambda i, f: (0, _serpentine(i, f, nf))),
                  pl.BlockSpec((d, tf), lambda i, f: (0, _serpentine(i, f, nf))),
                  pl.BlockSpec((tf, d), lambda i, f: (_serpentine(i, f, nf), 0))],
        out_specs=rows(d),
        out_shape=jax.ShapeDtypeStruct((m, d), F32),
        scratch_shapes=[pltpu.VMEM((tm, d), BF16), pltpu.VMEM((tm, d), F32)],
        compiler_params=_cparams("parallel", "arbitrary"),
        name="ffn_dense",
    )(h, ya, yb, wa, wb, g.reshape(1, d), w1, w3, w2)


def _split2(x):
    hi = x.astype(BF16)
    return hi, (x - hi.astype(F32)).astype(BF16)


def _moe_route_kernel(h_ref, ya_ref, yb_ref, wa_ref, wb_ref, g_ref, r_ref, tri_ref, hn_ref, xn_ref, meta_ref,
                      cnt_ref, carry_ref):
    @pl.when(pl.program_id(0) == 0)
    def _():
        carry_ref[...] = jnp.zeros(carry_ref.shape, F32)

    hn = _mix_residual(h_ref, ya_ref, yb_ref, wa_ref, wb_ref)
    hn_ref[...] = hn
    xn = _rmsnorm(hn, g_ref[...])
    xn_ref[...] = xn
    tm = xn.shape[0]
    lane = lax.broadcasted_iota(jnp.int32, (tm, LANE), 1)
    x_hi, x_lo = _split2(xn)
    r_hi, r_lo = _split2(r_ref[...])
    dot = functools.partial(jnp.dot, preferred_element_type=F32)
    logits = jnp.where(lane < N_EXPERTS, dot(x_hi, r_hi) + (dot(x_hi, r_lo) + dot(x_lo, r_hi)), -jnp.inf)
    m1 = jnp.max(logits, axis=-1, keepdims=True)
    i1 = jnp.min(jnp.where(logits == m1, lane, LANE), axis=-1, keepdims=True)
    rest = jnp.where(lane == i1, -jnp.inf, logits)
    m2 = jnp.max(rest, axis=-1, keepdims=True)
    i2 = jnp.min(jnp.where(rest == m2, lane, LANE), axis=-1, keepdims=True)
    e2 = jnp.exp(m2 - m1)
    g1 = 1.0 / (1.0 + e2)
    g2 = e2 / (1.0 + e2)
    onehot = jnp.where(lane == i1, 1.0, jnp.where(lane == i2, 1.0, 0.0))
    before = dot(tri_ref[...], onehot.astype(BF16)) + carry_ref[0:1, :]
    r1 = jnp.sum(jnp.where(lane == i1, before, 0.0), axis=-1, keepdims=True)
    r2 = jnp.sum(jnp.where(lane == i2, before, 0.0), axis=-1, keepdims=True)
    meta = jnp.where(lane == 0, i1.astype(F32), jnp.where(lane == 1, i2.astype(F32), 0.0))
    meta = jnp.where(lane == 2, g1, jnp.where(lane == 3, g2, meta))
    meta_ref[...] = jnp.where(lane == 4, r1, jnp.where(lane == 5, r2, meta))
    total = carry_ref[0:1, :] + jnp.sum(onehot, axis=0, keepdims=True)
    carry_ref[...] = jnp.broadcast_to(total, carry_ref.shape)
    cnt_ref[...] = jnp.broadcast_to(total, cnt_ref.shape)


def _moe_route(h, ya, yb, wa, wb, g, router, tm=512):
    m, d = h.shape
    k = ya.shape[1]
    rows = lambda n: pl.BlockSpec((tm, n), lambda i: (i, 0))
    full = lambda shape: pl.BlockSpec(shape, lambda i: (0, 0))
    return pl.pallas_call(
        _moe_route_kernel,
        grid=(m // tm,),
        in_specs=[rows(d), rows(k), rows(k), full((k, d)), full((k, d)), full((1, d)), full((d, LANE)),
                  full((tm, tm))],
        out_specs=[rows(d), rows(d), rows(LANE), full((8, LANE))],
        out_shape=[jax.ShapeDtypeStruct((m, d), F32), jax.ShapeDtypeStruct((m, d), F32),
                   jax.ShapeDtypeStruct((m, LANE), F32), jax.ShapeDtypeStruct((8, LANE), F32)],
        scratch_shapes=[pltpu.VMEM((8, LANE), F32)],
        compiler_params=_cparams("arbitrary"),
        name="moe_route",
    )(h, ya, yb, wa, wb, g.reshape(1, d), router, jnp.tril(jnp.ones((tm, tm), BF16), -1))


def _row_copy(src_ref, src_group, src_sub, dst_ref, dst_group, dst_sub, sem):
    return pltpu.make_async_copy(src_ref.at[src_group, pl.ds(src_sub, 1)],
                                 dst_ref.at[dst_group, pl.ds(dst_sub, 1)], sem)


def _split_row(p):
    return lax.shift_right_logical(p, 3), lax.bitwise_and(p, _SUBLANES - 1)


def _moe_dispatch_kernel(pos_ref, xn_ref, xs_in_ref, xs_ref, sem, *, tm):
    del xs_in_ref

    def start(grp, carry):
        for u in range(_SUBLANES):
            for s in range(2):
                dst_group, dst_sub = _split_row(pos_ref[0, 0, 2 * (grp * _SUBLANES + u) + s])
                _row_copy(xn_ref, grp, u, xs_ref, dst_group, dst_sub, sem).start(priority=s)
        return carry

    def wait(grp, carry):
        for _ in range(2 * _SUBLANES):
            _row_copy(xn_ref, 0, 0, xs_ref, 0, 0, sem).wait()
        return carry

    lax.fori_loop(0, tm // _SUBLANES, start, 0)
    lax.fori_loop(0, tm // _SUBLANES, wait, 0)


def _moe_dispatch(xn, pos, zeros, tm=512):
    m, d = xn.shape
    n_rows = zeros.shape[0]
    xs = pl.pallas_call(
        functools.partial(_moe_dispatch_kernel, tm=tm),
        grid=(m // tm,),
        in_specs=[pl.BlockSpec((1, 1, 2 * tm), lambda i: (i, 0, 0), memory_space=pltpu.SMEM),
                  pl.BlockSpec((tm // _SUBLANES, _SUBLANES, d), lambda i: (i, 0, 0)),
                  pl.BlockSpec(memory_space=pl.ANY)],
        out_specs=pl.BlockSpec(memory_space=pl.ANY),
        out_shape=jax.ShapeDtypeStruct((n_rows // _SUBLANES, _SUBLANES, d), F32),
        scratch_shapes=[pltpu.SemaphoreType.DMA(())],
        input_output_aliases={2: 0},
        compiler_params=_cparams("arbitrary"),
        name="moe_dispatch",
    )(pos.reshape(m // tm, 1, 2 * tm), xn.reshape(m // _SUBLANES, _SUBLANES, d),
      zeros.reshape(n_rows // _SUBLANES, _SUBLANES, d))
    return xs.reshape(n_rows, d)


def _moe_expert_kernel(te_ref, nu_ref, x_ref, w1_ref, w3_ref, w2_ref, y_ref, xbf_ref, acc_ref):
    del te_ref
    f = pl.program_id(1)

    @pl.when(f == 0)
    def _():
        xbf_ref[...] = x_ref[...].astype(BF16)
        acc_ref[...] = jnp.zeros(acc_ref.shape, F32)

    @pl.when(pl.program_id(0) < nu_ref[0])
    def _():
        xb = xbf_ref[...]
        mid = _silu(jnp.dot(xb, w1_ref[0], preferred_element_type=F32)) * jnp.dot(
            xb, w3_ref[0], preferred_element_type=F32)
        acc_ref[...] += _dot(mid, w2_ref[0])

    @pl.when(f == pl.num_programs(1) - 1)
    def _():
        y_ref[...] = acc_ref[...]


def _moe_experts(xs, tile_expert, n_used, w1, w3, w2, tmx, tf=1792):
    n_rows, d = xs.shape
    nf = w1.shape[2] // tf

    def fidx(i, f, nu):
        last = i >= nu[0]
        return _serpentine(jnp.where(last, nu[0] - 1, i), jnp.where(last, nf - 1, f), nf)

    grid_spec = pltpu.PrefetchScalarGridSpec(
        num_scalar_prefetch=2,
        grid=(n_rows // tmx, nf),
        in_specs=[pl.BlockSpec((tmx, d), lambda i, f, te, nu: (i, 0)),
                  pl.BlockSpec((1, d, tf), lambda i, f, te, nu: (te[i], 0, fidx(i, f, nu))),
                  pl.BlockSpec((1, d, tf), lambda i, f, te, nu: (te[i], 0, fidx(i, f, nu))),
                  pl.BlockSpec((1, tf, d), lambda i, f, te, nu: (te[i], fidx(i, f, nu), 0))],
        out_specs=pl.BlockSpec((tmx, d), lambda i, f, te, nu: (i, 0)),
        scratch_shapes=[pltpu.VMEM((tmx, d), BF16), pltpu.VMEM((tmx, d), F32)],
    )
    return pl.pallas_call(
        _moe_expert_kernel,
        grid_spec=grid_spec,
        out_shape=jax.ShapeDtypeStruct((n_rows, d), F32),
        compiler_params=_cparams("arbitrary", "arbitrary"),
        name="moe_experts",
    )(tile_expert, n_used, xs, w1, w3, w2)


def _moe_combine_kernel(pos_ref, posn_ref, h_ref, meta_ref, fg_ref, ys_ref, o_ref, ybuf_ref, sem, *, tm):
    i = pl.program_id(0)
    slot = lax.rem(i, 2)

    groups = tm // _SUBLANES

    def gather(p_ref, dst_slot):
        def start(grp, carry):
            for u in range(_SUBLANES):
                for s in range(2):
                    src_group, src_sub = _split_row(p_ref[0, 0, 2 * (grp * _SUBLANES + u) + s])
                    _row_copy(ys_ref, src_group, src_sub, ybuf_ref.at[dst_slot], s * groups + grp, u,
                              sem.at[dst_slot]).start(priority=s)
            return carry
        lax.fori_loop(0, groups, start, 0)

    @pl.when(i == 0)
    def _():
        gather(pos_ref, 0)

    @pl.when(i + 1 < pl.num_programs(0))
    def _():
        gather(posn_ref, 1 - slot)

    def wait(grp, carry):
        for _ in range(2 * _SUBLANES):
            _row_copy(ys_ref, 0, 0, ybuf_ref.at[slot], 0, 0, sem.at[slot]).wait()
        return carry

    lax.fori_loop(0, groups, wait, 0)
    lane = lax.broadcasted_iota(jnp.int32, (tm, LANE), 1)
    meta = meta_ref[...]
    g1 = jnp.sum(jnp.where(lane == 2, meta, 0.0), axis=-1, keepdims=True)
    g2 = jnp.sum(jnp.where(lane == 3, meta, 0.0), axis=-1, keepdims=True)
    y = ybuf_ref[slot].reshape(2 * tm, -1)
    o_ref[...] = _rmsnorm(h_ref[...] + g1 * y[:tm] + g2 * y[tm:], fg_ref[...])


def _moe_combine(h, meta, pos, ys, final_g, tm=512):
    m, d = h.shape
    nt = m // tm
    pos3 = pos.reshape(nt, 1, 2 * tm)
    smem = lambda imap: pl.BlockSpec((1, 1, 2 * tm), imap, memory_space=pltpu.SMEM)
    return pl.pallas_call(
        functools.partial(_moe_combine_kernel, tm=tm),
        grid=(nt,),
        in_specs=[smem(lambda i: (i, 0, 0)), smem(lambda i: (jnp.minimum(i + 1, nt - 1), 0, 0)),
                  pl.BlockSpec((tm, d), lambda i: (i, 0)), pl.BlockSpec((tm, LANE), lambda i: (i, 0)),
                  pl.BlockSpec((1, d), lambda i: (0, 0)), pl.BlockSpec(memory_space=pl.ANY)],
        out_specs=pl.BlockSpec((tm, d), lambda i: (i, 0)),
        out_shape=jax.ShapeDtypeStruct((m, d), F32),
        scratch_shapes=[pltpu.VMEM((2, 2 * tm // _SUBLANES, _SUBLANES, d), F32), pltpu.SemaphoreType.DMA((2,))],
        compiler_params=_cparams("arbitrary"),
        name="moe_combine",
    )(pos3, pos3, h, meta, final_g.reshape(1, d), ys.reshape(-1, _SUBLANES, d))


_MOE_TILE = 512


def _moe_sorted_rows(m):
    return ((2 * m) // _MOE_TILE + N_EXPERTS) * _MOE_TILE


def _moe_routed(h, ya, yb, wa, wb, g, router, w1, w3, w2, final_g, zeros):
    m, d = h.shape
    tmx = _MOE_TILE
    h, xn, meta, cnt = _moe_route(h, ya, yb, wa, wb, g, router)
    experts = meta[:, 0:2].astype(jnp.int32)
    rank = meta[:, 4:6].astype(jnp.int32)
    counts = cnt[0, :N_EXPERTS].astype(jnp.int32)
    padded = (counts + tmx - 1) // tmx * tmx
    ends = jnp.cumsum(padded)
    pos = (ends - padded)[experts] + rank
    n_tiles = zeros.shape[0] // tmx
    n_used = ends[-1] // tmx
    tile_start = jnp.minimum(jnp.arange(n_tiles, dtype=jnp.int32), n_used - 1) * tmx
    tile_expert = jnp.sum(tile_start[:, None] >= ends[None, :], axis=1).astype(jnp.int32)
    xs = _moe_dispatch(xn, pos, zeros)
    ys = _moe_experts(xs, tile_expert, n_used.reshape(1).astype(jnp.int32), w1, w3, w2, tmx)
    return _moe_combine(h, meta, pos, ys, final_g)


def _pad_cols(w, n):
    return jnp.pad(w, ((0, 0), (0, n - w.shape[1])))


def kernel(x, norm_mix_g, norm_ffn_g, w_in0, conv_w, conv_b, conv_ln_g, conv_ln_b, gla_w_a2, gla_b_a, gla_norm_g, w_out0, ffn_w1, ffn_w3, ffn_w2, w_in1, rwkv_mu, rwkv_w2, rwkv_w0, rwkv_a2, rwkv_a0, rwkv_g2, rwkv_k_k, rwkv_k_a, rwkv_r_k, rwkv_lnx_g, rwkv_lnx_b, fox_b_f, w_out1, moe_router, moe_w1, moe_w3, moe_w2, final_norm_g):
    b, t, d = x.shape
    m = b * t
    half = d // 2
    bf = lambda a: a.astype(BF16)
    h = x.reshape(m, d)

    y_conv, y_gla = _layer0_mix(x, norm_mix_g[0], w_in0, conv_w, conv_b, conv_ln_g, conv_ln_b,
                                gla_w_a2, gla_b_a, gla_norm_g)
    h = _ffn(h, y_conv.reshape(m, half), y_gla.reshape(m, half), bf(w_out0[:half]), bf(w_out0[half:]),
             norm_ffn_g[0], bf(ffn_w1), bf(ffn_w3), bf(ffn_w2))

    rc = _RWKV_COLS
    flat = lambda w: w.reshape(-1, w.shape[-1])
    y_rwkv, (e_w1, e_w3, e_w2), moe_zeros = _rwkv(
        h.reshape(b, t, d), norm_mix_g[1], bf(w_in1[:, :rc]), rwkv_mu, rwkv_w2, rwkv_w0, rwkv_a2, rwkv_a0,
        rwkv_g2, rwkv_k_k, rwkv_k_a, rwkv_r_k, rwkv_lnx_g, rwkv_lnx_b,
        casts=[flat(moe_w1), flat(moe_w3), flat(moe_w2)], zero_rows=_moe_sorted_rows(m))
    b_f = jnp.pad(fox_b_f, (0, LANE - FOX_HEADS)).reshape(1, LANE)
    y_fox = _fox(*_fox_proj(
        h.reshape(b, t, d), norm_mix_g[1], bf(w_in1[:, rc:rc + half]), bf(w_in1[:, rc + half:rc + 2 * half]),
        bf(w_in1[:, rc + 2 * half:rc + 3 * half]), bf(_pad_cols(w_in1[:, rc + 3 * half:], LANE)), b_f))
    out = _moe_routed(h, y_rwkv.reshape(m, half), y_fox.reshape(m, half), bf(w_out1[:half]), bf(w_out1[half:]),
                      norm_ffn_g[1], _pad_cols(moe_router, LANE), e_w1.reshape(moe_w1.shape),
                      e_w3.reshape(moe_w3.shape), e_w2.reshape(moe_w2.shape), final_norm_g, moe_zeros)
    return out.reshape(b, t, d)
```

```python
import functools

import jax
import jax.numpy as jnp
from jax import lax
from jax.experimental import pallas as pl
from jax.experimental.pallas import tpu as pltpu

F32 = jnp.float32
BF16 = jnp.bfloat16

NORM_EPS = 1e-6
CONV_WIDTH = 31
CONV_LN_EPS = 1e-5
GLA_HEADS = 4
GLA_DK = 64
GLA_DV = 128
GLA_TAU = 16.0
GLA_CHUNK = 64
RWKV_HEADS = 8
RWKV_HD = 64
RWKV_CHUNK = 64
RWKV_DECAY_SCALE = 0.606531
RWKV_GN_EPS = 64e-5
RWKV_L2_EPS = 1e-12
FOX_HEADS = 8
FOX_HD = 64
N_EXPERTS = 8
LANE = 128
_SUBLANES = 8
VMEM_LIMIT = 56 * 1024 * 1024


def _cparams(*sem):
    return pltpu.CompilerParams(dimension_semantics=sem, vmem_limit_bytes=VMEM_LIMIT)


def _dot(a, b):
    return jnp.dot(a.astype(BF16), b.astype(BF16), preferred_element_type=F32)


def _dot_nt(a, b):
    return lax.dot_general(a.astype(BF16), b.astype(BF16), (((1,), (1,)), ((), ())),
                           preferred_element_type=F32)


def _dot_tn(a, b):
    return lax.dot_general(a.astype(BF16), b.astype(BF16), (((0,), (0,)), ((), ())),
                           preferred_element_type=F32)


def _split3(x):
    hi = x.astype(BF16).astype(F32)
    mid = (x - hi).astype(BF16).astype(F32)
    lo = (x - hi - mid).astype(BF16).astype(F32)
    return hi, mid, lo


def _dot_01(sel, x):
    hi, mid, lo = _split3(x)
    return (_dot(sel, lo) + _dot(sel, mid)) + _dot(sel, hi)


def _dot_tn_01(x, sel):
    hi, mid, lo = _split3(x)
    return (_dot_tn(lo, sel) + _dot_tn(mid, sel)) + _dot_tn(hi, sel)


def _sigmoid(x):
    return 1.0 / (1.0 + jnp.exp(-x))


def _silu(x):
    return x * _sigmoid(x)


def _log_sigmoid(x):
    return jnp.minimum(x, 0.0) - jnp.log(1.0 + jnp.exp(-jnp.abs(x)))


def _rmsnorm(x, g):
    return x * lax.rsqrt(jnp.mean(x * x, axis=-1, keepdims=True) + NORM_EPS) * g


def _tri(n, strict=False):
    r = lax.broadcasted_iota(jnp.int32, (n, n), 0)
    c = lax.broadcasted_iota(jnp.int32, (n, n), 1)
    return (r > c) if strict else (r >= c)


_CUM_ROWS = 256


def _fox_pack_parts(c_all):
    lane = lax.broadcasted_iota(jnp.int32, c_all.shape, 1)
    hi, mid, lo = _split3(c_all)
    packed = jnp.where(lane < FOX_HEADS, hi, jnp.where(
        lane < 2 * FOX_HEADS, pltpu.roll(mid, FOX_HEADS, axis=1), jnp.where(
            lane < 3 * FOX_HEADS, pltpu.roll(lo, 2 * FOX_HEADS, axis=1), 0.0)))
    return packed.astype(BF16)


def _fox_augment(x_h, parts, head, first):
    d = FOX_HD
    r = lax.broadcasted_iota(jnp.int32, (LANE, LANE), 0)
    col = lax.broadcasted_iota(jnp.int32, (LANE, LANE), 1)
    base = d if first else d + 3
    hit = (r - head == (col - base) * FOX_HEADS) & (col >= base) & (col < base + 3)
    sel = jnp.where(hit, 1.0 if first else -1.0, 0.0).astype(BF16)
    lane = lax.broadcasted_iota(jnp.int32, x_h.shape, 1)
    ones_lo = d + 3 if first else d
    fixed = jnp.where(lane < d, x_h, jnp.where((lane >= ones_lo) & (lane < ones_lo + 3), 1.0, 0.0))
    return (fixed + jnp.dot(parts, sel, preferred_element_type=F32)).astype(BF16)


def _fox_head_lanes(x, h):
    per = LANE // FOX_HD
    x_g = x[:, (h // per) * LANE:(h // per + 1) * LANE]
    return x_g if h % per == 0 else pltpu.roll(x_g, LANE - (h % per) * FOX_HD, axis=1)


def _fox_proj_kernel(h_ref, g_ref, wq_ref, wk_ref, wv_ref, wf_ref, bf_ref, qa_ref, ka_ref, va_ref, carry_ref):
    @pl.when(pl.program_id(1) == 0)
    def _():
        carry_ref[...] = jnp.zeros(carry_ref.shape, F32)

    d = FOX_HD
    xn = _rmsnorm(h_ref[0], g_ref[...]).astype(BF16)
    proj = lambda w_ref: jnp.dot(xn, w_ref[...], preferred_element_type=F32)
    q = proj(wq_ref) * (d ** -0.5)
    k = proj(wk_ref)
    v = proj(wv_ref)
    log_f = _log_sigmoid(proj(wf_ref) + bf_ref[...])
    tri = _tri(_CUM_ROWS).astype(F32)
    last = carry_ref[0:1, :]
    cums = []
    for r0 in range(0, xn.shape[0], _CUM_ROWS):
        cums.append(_dot_01(tri, log_f[r0:r0 + _CUM_ROWS]) + last)
        last = cums[-1][_CUM_ROWS - 1:_CUM_ROWS, :]
    carry_ref[...] = jnp.broadcast_to(last, carry_ref.shape)
    parts = _fox_pack_parts(jnp.concatenate(cums, axis=0))
    lane = lax.broadcasted_iota(jnp.int32, (xn.shape[0], LANE), 1)
    for h in range(FOX_HEADS):
        qa_ref[0, h] = _fox_augment(_fox_head_lanes(q, h), parts, h, first=True)
        ka_ref[0, h] = _fox_augment(_fox_head_lanes(k, h), parts, h, first=False)
        va_ref[0, h] = jnp.where(lane < d, _fox_head_lanes(v, h), jnp.where(lane == d, 1.0, 0.0)).astype(BF16)


def _fox_proj(h, g, wq, wk, wv, wf, b_f, tt=512):
    b, t, d = h.shape
    full = lambda a: pl.BlockSpec(a.shape, lambda i, j: (0, 0))
    heads = pl.BlockSpec((1, FOX_HEADS, tt, LANE), lambda i, j: (i, 0, j, 0))
    g2 = g.reshape(1, d)
    return pl.pallas_call(
        _fox_proj_kernel,
        grid=(b, t // tt),
        in_specs=[pl.BlockSpec((1, tt, d), lambda i, j: (i, j, 0)), full(g2), full(wq), full(wk), full(wv),
                  full(wf), full(b_f)],
        out_specs=[heads] * 3,
        out_shape=[jax.ShapeDtypeStruct((b, FOX_HEADS, t, LANE), BF16)] * 3,
        scratch_shapes=[pltpu.VMEM((8, LANE), F32)],
        compiler_params=_cparams("parallel", "arbitrary"),
        name="fox_proj",
    )(h, g2, wq, wk, wv, wf, b_f)


_CONV_HALO = 32
_CONV_ROWS = 32
_GLA_SPAN = 256


def _conv_body(val, gate, w_ref, b_ref, lg_ref, lb_ref, o_ref, u_ref, sh_ref):
    tt, c = val.shape
    rows = _CONV_ROWS
    u_ref[_CONV_HALO:_CONV_HALO + tt, :] = val * _sigmoid(gate)
    span = sh_ref.shape[1]
    for s in range(1, _SUBLANES):
        sh_ref[s - 1] = u_ref[s:s + span, :]
    base = _CONV_HALO - (CONV_WIDTH - 1)
    for r0 in range(0, tt, rows):
        acc = jnp.zeros((rows // _SUBLANES, _SUBLANES, c), F32) + b_ref[...]
        for j in range(CONV_WIDTH):
            phase = (base + j) % _SUBLANES
            row = r0 + base + j - phase
            win = u_ref[row:row + rows, :] if phase == 0 else sh_ref[phase - 1, row:row + rows, :]
            acc = acc + win.reshape(rows // _SUBLANES, _SUBLANES, c) * w_ref[j]
        acc = acc.reshape(rows, c)
        mu = jnp.mean(acc, axis=-1, keepdims=True)
        cen = acc - mu
        var = jnp.mean(cen * cen, axis=-1, keepdims=True)
        y = cen * lax.rsqrt(var + CONV_LN_EPS) * lg_ref[...] + lb_ref[...]
        o_ref[0, r0:r0 + rows, :] = _silu(y).astype(o_ref.dtype)
    u_ref[0:_CONV_HALO, :] = u_ref[tt:tt + _CONV_HALO, :]


def _gla_body(q, k, v, go, alr, wa2_ref, ba_ref, ng_ref, o_ref, s_ref, row0):
    c = GLA_CHUNK
    nc = q.shape[0] // c
    tril = _tri(c)
    tril_f = tril.astype(F32)
    ones_cv = jnp.ones((c, GLA_DV), F32)
    units = [(ci, h) for ci in range(nc) for h in range(GLA_HEADS)]
    rows_of = lambda ci: slice(ci * c, (ci + 1) * c)
    keys_of = lambda h: slice(h * GLA_DK, (h + 1) * GLA_DK)
    vals_of = lambda h: slice(h * GLA_DV, (h + 1) * GLA_DV)

    z = _dot(alr, wa2_ref[...]) + ba_ref[...]
    log_a = _log_sigmoid(z) * (1.0 / GLA_TAU)
    q = q * (GLA_DK ** -0.5)
    q_decs, k_decs, k_ends, decays = [], [], [], []
    for ci in range(nc):
        la = log_a[rows_of(ci)]
        bcum = _dot_01(tril_f, la)
        b_last = bcum[c - 1:c, :]
        q_decs.append(q[rows_of(ci)] * jnp.exp(bcum))
        k_decs.append(k[rows_of(ci)] * jnp.exp(-bcum))
        k_ends.append(k[rows_of(ci)] * jnp.exp(b_last - bcum))
        decays.append(jnp.exp(_dot_tn_01(la, ones_cv)))

    vs = [v[rows_of(ci), vals_of(h)] for ci, h in units]
    attns = [jnp.where(tril, _dot_nt(q_decs[ci][:, keys_of(h)], k_decs[ci][:, keys_of(h)]), 0.0)
             for ci, h in units]
    kvs = [_dot_tn(k_ends[ci][:, keys_of(h)], vs[u]) for u, (ci, h) in enumerate(units)]
    intras = [_dot(attns[u], vs[u]) for u in range(len(units))]
    states = [s_ref[h] for h in range(GLA_HEADS)]
    prevs = []
    for u, (ci, h) in enumerate(units):
        prevs.append(states[h])
        states[h] = states[h] * decays[ci][keys_of(h)] + kvs[u]
    for h in range(GLA_HEADS):
        s_ref[h] = states[h]
    inters = [_dot(q_decs[ci][:, keys_of(h)], prevs[u]) for u, (ci, h) in enumerate(units)]
    for ci in range(nc):
        outs = []
        for h in range(GLA_HEADS):
            u = ci * GLA_HEADS + h
            o = intras[u] + inters[u]
            outs.append(o * lax.rsqrt(jnp.mean(o * o, axis=-1, keepdims=True) + NORM_EPS))
        o_all = jnp.concatenate(outs, axis=-1)
        o_ref[0, row0 + ci * c:row0 + (ci + 1) * c, :] = (
            o_all * ng_ref[...] * _silu(go[rows_of(ci)])).astype(o_ref.dtype)


def _layer0_kernel(h_ref, g_ref, wcv_ref, wcg_ref, wq_ref, wk_ref, wv_ref, wgo_ref, walr_ref,
                   cw_ref, cb_ref, clg_ref, clb_ref, wa2_ref, ba_ref, ng_ref,
                   yc_ref, yg_ref, u_ref, sh_ref, s_ref):
    @pl.when(pl.program_id(1) == 0)
    def _():
        u_ref[0:_CONV_HALO, :] = jnp.zeros((_CONV_HALO, u_ref.shape[1]), F32)
        s_ref[...] = jnp.zeros(s_ref.shape, F32)

    xn = _rmsnorm(h_ref[0], g_ref[...]).astype(BF16)
    proj = lambda w_ref, rows=slice(None): jnp.dot(xn[rows], w_ref[...], preferred_element_type=F32)
    _conv_body(proj(wcv_ref), proj(wcg_ref), cw_ref, cb_ref, clg_ref, clb_ref, yc_ref, u_ref, sh_ref)
    for row0 in range(0, xn.shape[0], _GLA_SPAN):
        rows = slice(row0, row0 + _GLA_SPAN)
        _gla_body(proj(wq_ref, rows), proj(wk_ref, rows), proj(wv_ref, rows), proj(wgo_ref, rows),
                  proj(walr_ref, rows), wa2_ref, ba_ref, ng_ref, yg_ref, s_ref, row0)


def _layer0_mix(x, g, w_in, conv_w, conv_b, ln_g, ln_b, w_a2, b_a, norm_g, tt=512):
    b, t, d = x.shape
    c = d // 2
    gk = GLA_HEADS * GLA_DK
    o = [0, c, 2 * c, 2 * c + gk, 2 * c + 2 * gk, 3 * c + 2 * gk, 4 * c + 2 * gk]
    ws = [w_in[:, o[i]:o[i + 1]].astype(BF16) for i in range(6)] + [_pad_cols(w_in[:, o[6]:], LANE).astype(BF16)]
    wrep = jnp.broadcast_to(conv_w[:, None, :], (CONV_WIDTH, _SUBLANES, c))
    w_a2p = jnp.pad(w_a2, ((0, LANE - w_a2.shape[0]), (0, 0)))
    vec = lambda a: a.reshape(1, -1)
    full = lambda a: pl.BlockSpec(a.shape, lambda i, j: (0,) * a.ndim)
    tile = lambda n: pl.BlockSpec((1, tt, n), lambda i, j: (i, j, 0))
    consts = ws + [wrep, vec(conv_b), vec(ln_g), vec(ln_b), w_a2p, vec(b_a), vec(norm_g)]
    return pl.pallas_call(
        _layer0_kernel,
        grid=(b, t // tt),
        in_specs=[tile(d), full(vec(g))] + [full(a) for a in consts],
        out_specs=[tile(c), tile(c)],
        out_shape=[jax.ShapeDtypeStruct((b, t, c), BF16), jax.ShapeDtypeStruct((b, t, c), BF16)],
        scratch_shapes=[pltpu.VMEM((tt + _CONV_HALO, c), F32),
                        pltpu.VMEM((_SUBLANES - 1, tt + _CONV_HALO - _SUBLANES, c), F32),
                        pltpu.VMEM((GLA_HEADS, GLA_DK, GLA_DV), F32)],
        compiler_params=_cparams("parallel", "arbitrary"),
        name="layer0_mix",
    )(x, vec(g), *consts)


_RWKV_COLS = 3 * 512 + 64 + 64 + 128


def _rwkv_kernel(h_ref, ng_ref, win_ref, mu_ref, w2_ref, w0_ref, a2_ref, a0_ref, g2_ref, kk_ref, ka_ref, rk_ref,
                 lg_ref, lb_ref, *refs, tt, n_cast):
    cast_in, (o_ref, *cast_out, zero_ref), (xs_ref, st_ref) = (
        refs[:n_cast], refs[n_cast:2 * n_cast + 2], refs[2 * n_cast + 2:])
    c = RWKV_CHUNK
    hd = RWKV_HD
    w = RWKV_HEADS * hd
    for src, dst in zip(cast_in, cast_out):
        dst[...] = src[...].astype(BF16)
    zero_ref[...] = jnp.zeros(zero_ref.shape, F32)

    @pl.when(pl.program_id(1) == 0)
    def _():
        xs_ref[0:8, :] = jnp.zeros((8, xs_ref.shape[1]), F32)
        st_ref[...] = jnp.zeros(st_ref.shape, F32)

    x = jnp.dot(_rmsnorm(h_ref[0], ng_ref[...]).astype(BF16), win_ref[...], preferred_element_type=F32)
    xs_ref[8:8 + tt, :] = x
    prev = xs_ref[7:7 + tt, :]
    xs_ref[7:8, :] = x[tt - 1:tt, :]
    x = x + (prev - x) * mu_ref[...]

    r = x[:, 0:w]
    k = x[:, w:2 * w]
    v = x[:, 2 * w:3 * w]
    xw = x[:, 3 * w:3 * w + 64]
    xa = x[:, 3 * w + 64:3 * w + 128]
    xg = x[:, 3 * w + 128:3 * w + 256]
    log_w = -RWKV_DECAY_SCALE * _sigmoid(w0_ref[...] + _dot(jnp.tanh(xw), w2_ref[...]))
    a = _sigmoid(a0_ref[...] + _dot(xa, a2_ref[...]))
    g = _dot(_sigmoid(xg), g2_ref[...])
    kk = k * kk_ref[...]
    k_mod = k * (1.0 + (a - 1.0) * ka_ref[...])
    rkr = r * k_mod * rk_ref[...]

    tril = _tri(c)
    tril_s = _tri(c, strict=True)
    tril_f = tril.astype(F32)
    eye = (lax.broadcasted_iota(jnp.int32, (c, c), 0)
           == lax.broadcasted_iota(jnp.int32, (c, c), 1)).astype(F32)
    zeros_cc = jnp.zeros((c, hd), F32)

    nc = tt // c
    units = [(ci, h) for ci in range(nc) for h in range(RWKV_HEADS)]
    rows_of = lambda ci: slice(ci * c, (ci + 1) * c)
    lanes_of = lambda h: slice(h * hd, (h + 1) * hd)

    exps = []
    for ci in range(nc):
        lw_c = log_w[rows_of(ci)]
        lcum = _dot_01(tril_f, lw_c)
        l_last = lcum[c - 1:c, :]
        exps.append((jnp.exp(lcum), jnp.exp(-lcum), jnp.exp(l_last - lcum), jnp.exp(lcum - lw_c),
                     jnp.exp(l_last)))

    aqs, rqs, vs, lhss, rhss, bkes = [], [], [], [], [], []
    for ci, h in units:
        rows, hs = rows_of(ci), lanes_of(h)
        e_pos, e_neg, e_end, e_prev, _ = exps[ci]
        kk_h = kk[rows, hs]
        nrm = jnp.sqrt(jnp.sum(kk_h * kk_h, axis=-1, keepdims=True))
        kk_h = kk_h / jnp.maximum(nrm, RWKV_L2_EPS)
        k_h = k_mod[rows, hs]
        kka = kk_h * a[rows, hs]
        aq = -kk_h * e_prev[:, hs]
        rq = r[rows, hs] * e_pos[:, hs]
        aqs.append(aq)
        rqs.append(rq)
        vs.append(v[rows, hs])
        lhss.append(jnp.concatenate([aq, rq], axis=0))
        rhss.append(jnp.concatenate([kka * e_neg[:, hs], k_h * e_neg[:, hs]], axis=0))
        bkes.append(jnp.concatenate([kka * e_end[:, hs], k_h * e_end[:, hs]], axis=0))

    xss = [_dot_nt(lhs, rhs) for lhs, rhs in zip(lhss, rhss)]
    m1s = [jnp.where(tril_s, xs[:c, :c], 0.0) for xs in xss]
    m2s = [jnp.where(tril_s, xs[:c, c:], 0.0) for xs in xss]
    n12s = [jnp.concatenate([jnp.where(tril, xs[c:, :c], 0.0), jnp.where(tril, xs[c:, c:], 0.0)], axis=1)
            for xs in xss]
    tinvs = [eye + m1 for m1 in m1s]
    mps = m1s
    for _ in range(5):
        mps = [_dot(mp, mp) for mp in mps]
        tinvs = [tinv + _dot(tinv, mp) for tinv, mp in zip(tinvs, mps)]
    mvs = [_dot(m2, v_h) for m2, v_h in zip(m2s, vs)]
    pqs = [_dot(tinv, jnp.concatenate([mv, aq], axis=1)) for tinv, mv, aq in zip(tinvs, mvs, aqs)]
    pqvs = [jnp.concatenate([pq, jnp.concatenate([v_h, zeros_cc], axis=1)], axis=0)
            for pq, v_h in zip(pqs, vs)]
    yys = [_dot(n12, pqv) for n12, pqv in zip(n12s, pqvs)]
    ghs = [_dot_tn(bke, pqv) for bke, pqv in zip(bkes, pqvs)]

    states = [st_ref[h] for h in range(RWKV_HEADS)]
    ys = []
    for u, (ci, h) in enumerate(units):
        yq = rqs[u] + yys[u][:, hd:]
        gmat = eye * exps[ci][4][:, lanes_of(h)] + ghs[u][:, hd:]
        res = _dot(jnp.concatenate([yq, gmat], axis=0), states[h])
        ys.append(res[:c] + yys[u][:, :hd])
        states[h] = res[c:] + ghs[u][:, :hd]
    for h in range(RWKV_HEADS):
        st_ref[h] = states[h]

    out_rows = []
    for ci in range(nc):
        yns, bonuses = [], []
        for h in range(RWKV_HEADS):
            u = ci * RWKV_HEADS + h
            y = ys[u]
            cen = y - jnp.mean(y, axis=-1, keepdims=True)
            var = jnp.mean(cen * cen, axis=-1, keepdims=True)
            yns.append(cen * lax.rsqrt(var + RWKV_GN_EPS))
            bonuses.append(jnp.sum(rkr[rows_of(ci), lanes_of(h)], axis=-1, keepdims=True) * vs[u])
        yn = jnp.concatenate(yns, axis=1)
        bonus = jnp.concatenate(bonuses, axis=1)
        out_rows.append((yn * lg_ref[...] + lb_ref[...] + bonus) * g[rows_of(ci)])
    o_ref[0] = jnp.concatenate(out_rows, axis=0).astype(o_ref.dtype)


def _rwkv(h, norm_g, w_in, mu, w_w2, w0, w_a2, a0, w_g2, k_k, k_a, r_k, lnx_g, lnx_b, casts, zero_rows, tt=128):
    b, t, d = h.shape
    cols = w_in.shape[1]
    w = RWKV_HEADS * RWKV_HD
    nt = t // tt
    steps = b * nt
    vec = lambda a_: a_.reshape(1, -1)
    full = lambda shape: pl.BlockSpec(shape, lambda i, j: (0, 0))
    piece = lambda a_rows, a_cols: pl.BlockSpec((a_rows // steps, a_cols), lambda i, j: (i * nt + j, 0))
    cast_specs = [piece(*a.shape) for a in casts]
    outs = pl.pallas_call(
        functools.partial(_rwkv_kernel, tt=tt, n_cast=len(casts)),
        grid=(b, nt),
        in_specs=[pl.BlockSpec((1, tt, d), lambda i, j: (i, j, 0)), full((1, d)), full((d, cols)), full((1, cols)),
                  full(w_w2.shape), full((1, w)), full(w_a2.shape), full((1, w)), full(w_g2.shape),
                  full((1, w)), full((1, w)), full((1, w)), full((1, w)), full((1, w))] + cast_specs,
        out_specs=[pl.BlockSpec((1, tt, w), lambda i, j: (i, j, 0))] + cast_specs + [piece(zero_rows, d)],
        out_shape=[jax.ShapeDtypeStruct((b, t, w), BF16)] + [jax.ShapeDtypeStruct(a.shape, BF16) for a in casts]
        + [jax.ShapeDtypeStruct((zero_rows, d), F32)],
        scratch_shapes=[pltpu.VMEM((8 + tt, cols), F32),
                        pltpu.VMEM((RWKV_HEADS, RWKV_HD, RWKV_HD), F32)],
        compiler_params=_cparams("parallel", "arbitrary"),
        name="rwkv7",
    )(h, vec(norm_g), w_in, vec(mu), w_w2, vec(w0), w_a2, vec(a0), w_g2, vec(k_k), vec(k_a), vec(r_k),
      vec(lnx_g), vec(lnx_b), *casts)
    return outs[0], outs[1:-1], outs[-1]


def _fox_kernel(qa_ref, ka_ref, va_ref, o_ref, *, tq):
    qi = pl.program_id(1)
    d = FOX_HD
    per = LANE // d
    nh = FOX_HEADS
    qs = [qa_ref[0, h] for h in range(nh)]
    causal = _tri(tq)
    lag = 2

    def step(j, carry, diagonal):
        start = pl.multiple_of(j * tq, tq)
        ss, new = {}, []
        for h in range(nh + lag):
            if h < nh:
                ss[h] = lax.dot_general(qs[h], ka_ref[0, h, pl.ds(start, tq), :], (((1,), (1,)), ((), ())),
                                        preferred_element_type=F32)
            g = h - lag
            if g >= 0:
                m, acc = carry[g]
                s = ss.pop(g)
                s = jnp.where(causal, s, -jnp.inf) if diagonal else s
                m_new = jnp.maximum(m, jnp.max(s, axis=-1, keepdims=True))
                p = jnp.exp(s - m_new).astype(BF16)
                pv = jnp.dot(p, va_ref[0, g, pl.ds(start, tq), :], preferred_element_type=F32)
                new.append((m_new, acc * jnp.exp(m - m_new) + pv))
        return tuple(new)

    init = tuple((jnp.full((tq, 1), -jnp.inf, F32), jnp.zeros((tq, LANE), F32)) for _ in range(nh))
    carry = lax.fori_loop(0, qi, lambda j, cr: step(j, cr, False), init)
    carry = step(qi, carry, True)
    lane = lax.broadcasted_iota(jnp.int32, (tq, LANE), 1)
    for grp in range(nh // per):
        out = None
        for hh in range(per):
            acc = carry[grp * per + hh][1]
            o_h = acc / jnp.sum(jnp.where(lane == d, acc, 0.0), axis=-1, keepdims=True)
            o_h = o_h if hh == 0 else pltpu.roll(o_h, hh * d, axis=1)
            out = o_h if out is None else jnp.where(lane >= hh * d, o_h, out)
        o_ref[0, :, grp * LANE:(grp + 1) * LANE] = out.astype(o_ref.dtype)


def _fox(qa, ka, va, tq=256):
    b, nh, t, _ = qa.shape
    w = nh * FOX_HD
    return pl.pallas_call(
        functools.partial(_fox_kernel, tq=tq),
        grid=(b, t // tq),
        in_specs=[pl.BlockSpec((1, nh, tq, LANE), lambda i, j: (i, 0, j, 0)),
                  pl.BlockSpec((1, nh, t, LANE), lambda i, j: (i, 0, 0, 0)),
                  pl.BlockSpec((1, nh, t, LANE), lambda i, j: (i, 0, 0, 0))],
        out_specs=pl.BlockSpec((1, tq, w), lambda i, j: (i, j, 0)),
        out_shape=jax.ShapeDtypeStruct((b, t, w), BF16),
        compiler_params=_cparams("parallel", "arbitrary"),
        name="fox_attention",
    )(qa, ka, va)


def _serpentine(i, f, nf):
    return jnp.where(i % 2 == 0, f, nf - 1 - f)


def _mix_residual(h_ref, ya_ref, yb_ref, wa_ref, wb_ref):
    return h_ref[...] + _dot(ya_ref[...], wa_ref[...]) + _dot(yb_ref[...], wb_ref[...])


def _ffn_kernel(h_ref, ya_ref, yb_ref, wa_ref, wb_ref, g_ref, w1_ref, w3_ref, w2_ref, o_ref, xn_ref, acc_ref):
    f = pl.program_id(1)

    @pl.when(f == 0)
    def _():
        hn = _mix_residual(h_ref, ya_ref, yb_ref, wa_ref, wb_ref)
        xn_ref[...] = _rmsnorm(hn, g_ref[...]).astype(BF16)
        acc_ref[...] = hn

    xn = xn_ref[...]
    mid = _silu(jnp.dot(xn, w1_ref[...], preferred_element_type=F32)) * jnp.dot(
        xn, w3_ref[...], preferred_element_type=F32)
    acc_ref[...] += _dot(mid, w2_ref[...])

    @pl.when(f == pl.num_programs(1) - 1)
    def _():
        o_ref[...] = acc_ref[...]


def _ffn(h, ya, yb, wa, wb, g, w1, w3, w2, tm=512, tf=1408):
    m, d = h.shape
    k = ya.shape[1]
    nf = w1.shape[1] // tf
    rows = lambda n: pl.BlockSpec((tm, n), lambda i, f: (i, 0))
    full = lambda shape: pl.BlockSpec(shape, lambda i, f: (0, 0))
    return pl.pallas_call(
        _ffn_kernel,
        grid=(m // tm, nf),
        in_specs=[rows(d), rows(k), rows(k), full((k, d)), full((k, d)), full((1, d)),
                  pl.BlockSpec((d, tf), lambda i, f: (0, _serpentine(i, f, nf))),
                  pl.BlockSpec((d, tf), lambda i, f: (0, _serpentine(i, f, nf))),
                  pl.BlockSpec((tf, d), lambda i, f: (_serpentine(i, f, nf), 0))],
        out_specs=rows(d),
        out_shape=jax.ShapeDtypeStruct((m, d), F32),
        scratch_shapes=[pltpu.VMEM((tm, d), BF16), pltpu.VMEM((tm, d), F32)],
        compiler_params=_cparams("parallel", "arbitrary"),
        name="ffn_dense",
    )(h, ya, yb, wa, wb, g.reshape(1, d), w1, w3, w2)


def _split2(x):
    hi = x.astype(BF16)
    return hi, (x - hi.astype(F32)).astype(BF16)


def _moe_route_kernel(h_ref, ya_ref, yb_ref, wa_ref, wb_ref, g_ref, r_ref, tri_ref, hn_ref, xn_ref, meta_ref,
                      cnt_ref, carry_ref):
    @pl.when(pl.program_id(0) == 0)
    def _():
        carry_ref[...] = jnp.zeros(carry_ref.shape, F32)

    hn = _mix_residual(h_ref, ya_ref, yb_ref, wa_ref, wb_ref)
    hn_ref[...] = hn
    xn = _rmsnorm(hn, g_ref[...])
    xn_ref[...] = xn
    tm = xn.shape[0]
    lane = lax.broadcasted_iota(jnp.int32, (tm, LANE), 1)
    x_hi, x_lo = _split2(xn)
    r_hi, r_lo = _split2(r_ref[...])
    dot = functools.partial(jnp.dot, preferred_element_type=F32)
    logits = jnp.where(lane < N_EXPERTS, dot(x_hi, r_hi) + (dot(x_hi, r_lo) + dot(x_lo, r_hi)), -jnp.inf)
    m1 = jnp.max(logits, axis=-1, keepdims=True)
    i1 = jnp.min(jnp.where(logits == m1, lane, LANE), axis=-1, keepdims=True)
    rest = jnp.where(lane == i1, -jnp.inf, logits)
    m2 = jnp.max(rest, axis=-1, keepdims=True)
    i2 = jnp.min(jnp.where(rest == m2, lane, LANE), axis=-1, keepdims=True)
    e2 = jnp.exp(m2 - m1)
    g1 = 1.0 / (1.0 + e2)
    g2 = e2 / (1.0 + e2)
    onehot = jnp.where(lane == i1, 1.0, jnp.where(lane == i2, 1.0, 0.0))
    before = dot(tri_ref[...], onehot.astype(BF16)) + carry_ref[0:1, :]
    r1 = jnp.sum(jnp.where(lane == i1, before, 0.0), axis=-1, keepdims=True)
    r2 = jnp.sum(jnp.where(lane == i2, before, 0.0), axis=-1, keepdims=True)
    meta = jnp.where(lane == 0, i1.astype(F32), jnp.where(lane == 1, i2.astype(F32), 0.0))
    meta = jnp.where(lane == 2, g1, jnp.where(lane == 3, g2, meta))
    meta_ref[...] = jnp.where(lane == 4, r1, jnp.where(lane == 5, r2, meta))
    total = carry_ref[0:1, :] + jnp.sum(onehot, axis=0, keepdims=True)
    carry_ref[...] = jnp.broadcast_to(total, carry_ref.shape)
    cnt_ref[...] = jnp.broadcast_to(total, cnt_ref.shape)


def _moe_route(h, ya, yb, wa, wb, g, router, tm=512):
    m, d = h.shape
    k = ya.shape[1]
    rows = lambda n: pl.BlockSpec((tm, n), lambda i: (i, 0))
    full = lambda shape: pl.BlockSpec(shape, lambda i: (0, 0))
    return pl.pallas_call(
        _moe_route_kernel,
        grid=(m // tm,),
        in_specs=[rows(d), rows(k), rows(k), full((k, d)), full((k, d)), full((1, d)), full((d, LANE)),
                  full((tm, tm))],
        out_specs=[rows(d), rows(d), rows(LANE), full((8, LANE))],
        out_shape=[jax.ShapeDtypeStruct((m, d), F32), jax.ShapeDtypeStruct((m, d), F32),
                   jax.ShapeDtypeStruct((m, LANE), F32), jax.ShapeDtypeStruct((8, LANE), F32)],
        scratch_shapes=[pltpu.VMEM((8, LANE), F32)],
        compiler_params=_cparams("arbitrary"),
        name="moe_route",
    )(h, ya, yb, wa, wb, g.reshape(1, d), router, jnp.tril(jnp.ones((tm, tm), BF16), -1))


def _row_copy(src_ref, src_group, src_sub, dst_ref, dst_group, dst_sub, sem):
    return pltpu.make_async_copy(src_ref.at[src_group, pl.ds(src_sub, 1)],
                                 dst_ref.at[dst_group, pl.ds(dst_sub, 1)], sem)


def _split_row(p):
    return lax.shift_right_logical(p, 3), lax.bitwise_and(p, _SUBLANES - 1)


def _moe_dispatch_kernel(pos_ref, xn_ref, xs_in_ref, xs_ref, sem, *, tm):
    del xs_in_ref

    def start(grp, carry):
        for u in range(_SUBLANES):
            for s in range(2):
                dst_group, dst_sub = _split_row(pos_ref[0, 0, 2 * (grp * _SUBLANES + u) + s])
                _row_copy(xn_ref, grp, u, xs_ref, dst_group, dst_sub, sem).start(priority=s)
        return carry

    def wait(grp, carry):
        for _ in range(2 * _SUBLANES):
            _row_copy(xn_ref, 0, 0, xs_ref, 0, 0, sem).wait()
        return carry

    lax.fori_loop(0, tm // _SUBLANES, start, 0)
    lax.fori_loop(0, tm // _SUBLANES, wait, 0)


def _moe_dispatch(xn, pos, zeros, tm=1024):
    m, d = xn.shape
    n_rows = zeros.shape[0]
    xs = pl.pallas_call(
        functools.partial(_moe_dispatch_kernel, tm=tm),
        grid=(m // tm,),
        in_specs=[pl.BlockSpec((1, 1, 2 * tm), lambda i: (i, 0, 0), memory_space=pltpu.SMEM),
                  pl.BlockSpec((tm // _SUBLANES, _SUBLANES, d), lambda i: (i, 0, 0)),
                  pl.BlockSpec(memory_space=pl.ANY)],
        out_specs=pl.BlockSpec(memory_space=pl.ANY),
        out_shape=jax.ShapeDtypeStruct((n_rows // _SUBLANES, _SUBLANES, d), F32),
        scratch_shapes=[pltpu.SemaphoreType.DMA(())],
        input_output_aliases={2: 0},
        compiler_params=_cparams("arbitrary"),
        name="moe_dispatch",
    )(pos.reshape(m // tm, 1, 2 * tm), xn.reshape(m // _SUBLANES, _SUBLANES, d),
      zeros.reshape(n_rows // _SUBLANES, _SUBLANES, d))
    return xs.reshape(n_rows, d)


def _moe_expert_kernel(te_ref, nu_ref, x_ref, w1_ref, w3_ref, w2_ref, y_ref, xbf_ref, acc_ref):
    del te_ref
    f = pl.program_id(1)

    @pl.when(f == 0)
    def _():
        xbf_ref[...] = x_ref[...].astype(BF16)
        acc_ref[...] = jnp.zeros(acc_ref.shape, F32)

    @pl.when(pl.program_id(0) < nu_ref[0])
    def _():
        xb = xbf_ref[...]
        mid = _silu(jnp.dot(xb, w1_ref[0], preferred_element_type=F32)) * jnp.dot(
            xb, w3_ref[0], preferred_element_type=F32)
        acc_ref[...] += _dot(mid, w2_ref[0])

    @pl.when(f == pl.num_programs(1) - 1)
    def _():
        y_ref[...] = acc_ref[...]


def _moe_experts(xs, tile_expert, n_used, w1, w3, w2, tmx, tf=1792):
    n_rows, d = xs.shape
    nf = w1.shape[2] // tf

    def fidx(i, f, nu):
        last = i >= nu[0]
        return _serpentine(jnp.where(last, nu[0] - 1, i), jnp.where(last, nf - 1, f), nf)

    grid_spec = pltpu.PrefetchScalarGridSpec(
        num_scalar_prefetch=2,
        grid=(n_rows // tmx, nf),
        in_specs=[pl.BlockSpec((tmx, d), lambda i, f, te, nu: (i, 0)),
                  pl.BlockSpec((1, d, tf), lambda i, f, te, nu: (te[i], 0, fidx(i, f, nu))),
                  pl.BlockSpec((1, d, tf), lambda i, f, te, nu: (te[i], 0, fidx(i, f, nu))),
                  pl.BlockSpec((1, tf, d), lambda i, f, te, nu: (te[i], fidx(i, f, nu), 0))],
        out_specs=pl.BlockSpec((tmx, d), lambda i, f, te, nu: (i, 0)),
        scratch_shapes=[pltpu.VMEM((tmx, d), BF16), pltpu.VMEM((tmx, d), F32)],
    )
    return pl.pallas_call(
        _moe_expert_kernel,
        grid_spec=grid_spec,
        out_shape=jax.ShapeDtypeStruct((n_rows, d), F32),
        compiler_params=_cparams("arbitrary", "arbitrary"),
        name="moe_experts",
    )(tile_expert, n_used, xs, w1, w3, w2)


def _moe_combine_kernel(pos_ref, posn_ref, h_ref, meta_ref, fg_ref, ys_ref, o_ref, ybuf_ref, sem, *, tm):
    i = pl.program_id(0)
    slot = lax.rem(i, 2)

    groups = tm // _SUBLANES

    def gather(p_ref, dst_slot):
        def start(grp, carry):
            for u in range(_SUBLANES):
                for s in range(2):
                    src_group, src_sub = _split_row(p_ref[0, 0, 2 * (grp * _SUBLANES + u) + s])
                    _row_copy(ys_ref, src_group, src_sub, ybuf_ref.at[dst_slot], s * groups + grp, u,
                              sem.at[dst_slot]).start(priority=s)
            return carry
        lax.fori_loop(0, groups, start, 0)

    @pl.when(i == 0)
    def _():
        gather(pos_ref, 0)

    @pl.when(i + 1 < pl.num_programs(0))
    def _():
        gather(posn_ref, 1 - slot)

    def wait(grp, carry):
        for _ in range(2 * _SUBLANES):
            _row_copy(ys_ref, 0, 0, ybuf_ref.at[slot], 0, 0, sem.at[slot]).wait()
        return carry

    lax.fori_loop(0, groups, wait, 0)
    lane = lax.broadcasted_iota(jnp.int32, (tm, LANE), 1)
    meta = meta_ref[...]
    g1 = jnp.sum(jnp.where(lane == 2, meta, 0.0), axis=-1, keepdims=True)
    g2 = jnp.sum(jnp.where(lane == 3, meta, 0.0), axis=-1, keepdims=True)
    y = ybuf_ref[slot].reshape(2 * tm, -1)
    o_ref[...] = _rmsnorm(h_ref[...] + g1 * y[:tm] + g2 * y[tm:], fg_ref[...])


def _moe_combine(h, meta, pos, ys, final_g, tm=1024):
    m, d = h.shape
    nt = m // tm
    pos3 = pos.reshape(nt, 1, 2 * tm)
    smem = lambda imap: pl.BlockSpec((1, 1, 2 * tm), imap, memory_space=pltpu.SMEM)
    return pl.pallas_call(
        functools.partial(_moe_combine_kernel, tm=tm),
        grid=(nt,),
        in_specs=[smem(lambda i: (i, 0, 0)), smem(lambda i: (jnp.minimum(i + 1, nt - 1), 0, 0)),
                  pl.BlockSpec((tm, d), lambda i: (i, 0)), pl.BlockSpec((tm, LANE), lambda i: (i, 0)),
                  pl.BlockSpec((1, d), lambda i: (0, 0)), pl.BlockSpec(memory_space=pl.ANY)],
        out_specs=pl.BlockSpec((tm, d), lambda i: (i, 0)),
        out_shape=jax.ShapeDtypeStruct((m, d), F32),
        scratch_shapes=[pltpu.VMEM((2, 2 * tm // _SUBLANES, _SUBLANES, d), F32), pltpu.SemaphoreType.DMA((2,))],
        compiler_params=_cparams("arbitrary"),
        name="moe_combine",
    )(pos3, pos3, h, meta, final_g.reshape(1, d), ys.reshape(-1, _SUBLANES, d))


_MOE_TILE = 512


def _moe_sorted_rows(m):
    return ((2 * m) // _MOE_TILE + N_EXPERTS) * _MOE_TILE


def _moe_routed(h, ya, yb, wa, wb, g, router, w1, w3, w2, final_g, zeros):
    m, d = h.shape
    tmx = _MOE_TILE
    h, xn, meta, cnt = _moe_route(h, ya, yb, wa, wb, g, router)
    experts = meta[:, 0:2].astype(jnp.int32)
    rank = meta[:, 4:6].astype(jnp.int32)
    counts = cnt[0, :N_EXPERTS].astype(jnp.int32)
    padded = (counts + tmx - 1) // tmx * tmx
    ends = jnp.cumsum(padded)
    pos = (ends - padded)[experts] + rank
    n_tiles = zeros.shape[0] // tmx
    n_used = ends[-1] // tmx
    tile_start = jnp.minimum(jnp.arange(n_tiles, dtype=jnp.int32), n_used - 1) * tmx
    tile_expert = jnp.sum(tile_start[:, None] >= ends[None, :], axis=1).astype(jnp.int32)
    xs = _moe_dispatch(xn, pos, zeros)
    ys = _moe_experts(xs, tile_expert, n_used.reshape(1).astype(jnp.int32), w1, w3, w2, tmx)
    return _moe_combine(h, meta, pos, ys, final_g)


def _pad_cols(w, n):
    return jnp.pad(w, ((0, 0), (0, n - w.shape[1])))


def kernel(x, norm_mix_g, norm_ffn_g, w_in0, conv_w, conv_b, conv_ln_g, conv_ln_b, gla_w_a2, gla_b_a, gla_norm_g, w_out0, ffn_w1, ffn_w3, ffn_w2, w_in1, rwkv_mu, rwkv_w2, rwkv_w0, rwkv_a2, rwkv_a0, rwkv_g2, rwkv_k_k, rwkv_k_a, rwkv_r_k, rwkv_lnx_g, rwkv_lnx_b, fox_b_f, w_out1, moe_router, moe_w1, moe_w3, moe_w2, final_norm_g):
    b, t, d = x.shape
    m = b * t
    half = d // 2
    bf = lambda a: a.astype(BF16)
    h = x.reshape(m, d)

    y_conv, y_gla = _layer0_mix(x, norm_mix_g[0], w_in0, conv_w, conv_b, conv_ln_g, conv_ln_b,
                                gla_w_a2, gla_b_a, gla_norm_g)
    h = _ffn(h, y_conv.reshape(m, half), y_gla.reshape(m, half), bf(w_out0[:half]), bf(w_out0[half:]),
             norm_ffn_g[0], bf(ffn_w1), bf(ffn_w3), bf(ffn_w2))

    rc = _RWKV_COLS
    flat = lambda w: w.reshape(-1, w.shape[-1])
    y_rwkv, (e_w1, e_w3, e_w2), moe_zeros = _rwkv(
        h.reshape(b, t, d), norm_mix_g[1], bf(w_in1[:, :rc]), rwkv_mu, rwkv_w2, rwkv_w0, rwkv_a2, rwkv_a0,
        rwkv_g2, rwkv_k_k, rwkv_k_a, rwkv_r_k, rwkv_lnx_g, rwkv_lnx_b,
        casts=[flat(moe_w1), flat(moe_w3), flat(moe_w2)], zero_rows=_moe_sorted_rows(m))
    b_f = jnp.pad(fox_b_f, (0, LANE - FOX_HEADS)).reshape(1, LANE)
    y_fox = _fox(*_fox_proj(
        h.reshape(b, t, d), norm_mix_g[1], bf(w_in1[:, rc:rc + half]), bf(w_in1[:, rc + half:rc + 2 * half]),
        bf(w_in1[:, rc + 2 * half:rc + 3 * half]), bf(_pad_cols(w_in1[:, rc + 3 * half:], LANE)), b_f))
    out = _moe_routed(h, y_rwkv.reshape(m, half), y_fox.reshape(m, half), bf(w_out1[:half]), bf(w_out1[half:]),
                      norm_ffn_g[1], _pad_cols(moe_router, LANE), e_w1.reshape(moe_w1.shape),
                      e_w3.reshape(moe_w3.shape), e_w2.reshape(moe_w2.shape), final_norm_g, moe_zeros)
    return out.reshape(b, t, d)
```
